```python
import jax, jax.numpy as jnp
from jax import lax
import numpy as np

D_MODEL = 1024
BATCH = 4
SEQ = 4096
DEPTH = 1

ATTN_HEAD_DIM = 64
ATTN_HEADS = D_MODEL // ATTN_HEAD_DIM
ATTN_KV_HEADS = ATTN_HEADS // 4
ATTN_GROUP = ATTN_HEADS // ATTN_KV_HEADS
WINDOW = 128

MLSTM_HEADS = 8
MLSTM_V_DIM = D_MODEL // MLSTM_HEADS
MLSTM_QK_DIM = MLSTM_V_DIM // 2
MLSTM_CHUNK = 64
CONV_WIDTH = 4
GATE_SOFTCAP = 15.0

D_FF = ((8 * D_MODEL // 3 + 127) // 128) * 128
FFN_RESIDUAL_WEIGHT = 0.5
NORM_EPS = 1e-6

SZ_AQ = ATTN_HEADS * ATTN_HEAD_DIM
SZ_AK = ATTN_KV_HEADS * ATTN_HEAD_DIM
SZ_AV = ATTN_KV_HEADS * ATTN_HEAD_DIM
SZ_MQ = MLSTM_HEADS * MLSTM_QK_DIM
SZ_MK = MLSTM_HEADS * MLSTM_QK_DIM
SZ_MV = MLSTM_HEADS * MLSTM_V_DIM
SZ_MO = MLSTM_HEADS * MLSTM_V_DIM
SZ_MI = MLSTM_HEADS
SZ_MF = MLSTM_HEADS
SZ_GATES = 2 * D_MODEL
SPLIT_POINTS = (
    SZ_AQ,
    SZ_AQ + SZ_AK,
    SZ_AQ + SZ_AK + SZ_AV,
    SZ_AQ + SZ_AK + SZ_AV + SZ_MQ,
    SZ_AQ + SZ_AK + SZ_AV + SZ_MQ + SZ_MK,
    SZ_AQ + SZ_AK + SZ_AV + SZ_MQ + SZ_MK + SZ_MV,
    SZ_AQ + SZ_AK + SZ_AV + SZ_MQ + SZ_MK + SZ_MV + SZ_MO,
    SZ_AQ + SZ_AK + SZ_AV + SZ_MQ + SZ_MK + SZ_MV + SZ_MO + SZ_MI,
    SZ_AQ + SZ_AK + SZ_AV + SZ_MQ + SZ_MK + SZ_MV + SZ_MO + SZ_MI + SZ_MF,
)
F_OFFSET = SPLIT_POINTS[7]
IN_WIDTH = SPLIT_POINTS[8] + SZ_GATES

kernel_name = "hybrid_swa_mlstm_macaron_block"


def rms_norm(x, g):
    xf = x.astype(jnp.float32)
    y = xf * lax.rsqrt(jnp.mean(xf * xf, axis=-1, keepdims=True) + NORM_EPS)
    return (y * g.astype(jnp.float32)).astype(x.dtype)


def swiglu(x, w_gate, w_up, w_down):
    return (jax.nn.silu(x @ w_gate) * (x @ w_up)) @ w_down


def causal_depthwise_conv(x, w):
    s = x.shape[1]
    xp = jnp.pad(x, ((0, 0), (CONV_WIDTH - 1, 0), (0, 0)))
    out = xp[:, 0:s] * w[0]
    for j in range(1, CONV_WIDTH):
        out = out + xp[:, j:j + s] * w[j]
    return out


def sliding_window_attention(q, k, v, sinks):
    b, s = q.shape[0], q.shape[1]
    nb = s // WINDOW
    qb = q.reshape(b, nb, WINDOW, ATTN_KV_HEADS, ATTN_GROUP, ATTN_HEAD_DIM)

    def band(t):
        tp = jnp.pad(t, ((0, 0), (WINDOW, 0), (0, 0), (0, 0)))
        tp = tp.reshape(b, nb + 1, WINDOW, ATTN_KV_HEADS, ATTN_HEAD_DIM)
        return jnp.concatenate([tp[:, :-1], tp[:, 1:]], axis=2)

    kb, vb = band(k), band(v)
    scale = ATTN_HEAD_DIM ** -0.5
    scores = jnp.einsum('bnqhgd,bnkhd->bhgnqk', qb, kb).astype(jnp.float32) * scale
    qi = jnp.arange(WINDOW)[:, None]
    kj = jnp.arange(2 * WINDOW)[None, :]
    in_window = (kj > qi) & (kj <= qi + WINDOW)
    is_pad = (jnp.arange(nb)[:, None, None] == 0) & (kj < WINDOW)[None]
    valid = in_window[None] & ~is_pad
    scores = jnp.where(valid, scores, -jnp.inf)
    sink = jnp.broadcast_to(
        sinks.astype(jnp.float32).reshape(ATTN_KV_HEADS, ATTN_GROUP)[None, :, :, None, None, None],
        scores.shape[:-1] + (1,))
    probs = jax.nn.softmax(jnp.concatenate([scores, sink], axis=-1), axis=-1)[..., :-1]
    out = jnp.einsum('bhgnqk,bnkhd->bnqhgd', probs.astype(v.dtype), vb)
    return out.reshape(b, s, ATTN_HEADS * ATTN_HEAD_DIM)


def mlstm_chunkwise(q, k, v, i_pre, f_pre):
    b, s = q.shape[0], q.shape[1]
    nc = s // MLSTM_CHUNK
    f32 = jnp.float32

    def to_chunks(t):
        return t.astype(f32).reshape(b, nc, MLSTM_CHUNK, MLSTM_HEADS, t.shape[-1]).transpose(1, 0, 3, 2, 4)

    def gate_chunks(t):
        return t.astype(f32).reshape(b, nc, MLSTM_CHUNK, MLSTM_HEADS).transpose(1, 0, 3, 2)

    qc, kc, vc = to_chunks(q), to_chunks(k), to_chunks(v)
    ic = gate_chunks(i_pre)
    lfc = jax.nn.log_sigmoid(gate_chunks(f_pre))
    causal = jnp.tril(jnp.ones((MLSTM_CHUNK, MLSTM_CHUNK), dtype=bool))

    def step(carry, inp):
        c_prev, n_prev, m_prev = carry
        q_, k_, v_, i_, lf_ = inp
        cum = jnp.cumsum(lf_, axis=-1)
        logw = cum[..., :, None] - cum[..., None, :] + i_[..., None, :]
        logw = jnp.where(causal, logw, -jnp.inf)
        log_inter = cum + m_prev[..., None]
        m_t = jnp.maximum(log_inter, jnp.max(logw, axis=-1))
        w_intra = jnp.exp(logw - m_t[..., None])
        w_inter = jnp.exp(log_inter - m_t)
        sc = jnp.einsum('bhtd,bhsd->bhts', q_, k_) * w_intra
        num = jnp.einsum('bhts,bhsv->bhtv', sc, v_) + w_inter[..., None] * jnp.einsum('bhtd,bhdv->bhtv', q_, c_prev)
        den = jnp.sum(sc, axis=-1) + w_inter * jnp.einsum('bhtd,bhd->bht', q_, n_prev)
        h = num / jnp.maximum(jnp.abs(den), jnp.exp(-m_t))[..., None]
        total = cum[..., -1]
        log_k = total[..., None] - cum + i_
        m_new = jnp.maximum(total + m_prev, jnp.max(log_k, axis=-1))
        wk = jnp.exp(log_k - m_new[..., None])
        decay = jnp.exp(total + m_prev - m_new)
        c_new = decay[..., None, None] * c_prev + jnp.einsum('bhs,bhsd,bhsv->bhdv', wk, k_, v_)
        n_new = decay[..., None] * n_prev + jnp.einsum('bhs,bhsd->bhd', wk, k_)
        return (c_new, n_new, m_new), h

    init = (jnp.zeros((b, MLSTM_HEADS, MLSTM_QK_DIM, MLSTM_V_DIM), f32),
            jnp.zeros((b, MLSTM_HEADS, MLSTM_QK_DIM), f32),
            jnp.zeros((b, MLSTM_HEADS), f32))
    _, hs = lax.scan(step, init, (qc, kc, vc, ic, lfc))
    return hs.transpose(1, 0, 3, 2, 4).reshape(b, s, MLSTM_HEADS, MLSTM_V_DIM)


def softcap(t):
    return GATE_SOFTCAP * jnp.tanh(t / GATE_SOFTCAP)


def setup_inputs(seed: int = 0) -> dict:
    key = jax.random.key(seed)
    ks = jax.random.split(key, 24)
    f32 = jnp.float32

    def dense(k, fan_in, fan_out):
        return jax.random.normal(k, (DEPTH, fan_in, fan_out), f32) * fan_in ** -0.5

    def gain(k, shape):
        return 1.0 + 0.05 * jax.random.normal(k, shape, f32)

    b_in = 0.02 * jax.random.normal(ks[7], (DEPTH, IN_WIDTH), f32)
    f_bias = jnp.linspace(3.0, 6.0, MLSTM_HEADS, dtype=f32)
    b_in = b_in.at[:, F_OFFSET:F_OFFSET + SZ_MF].add(f_bias)
    return {
        "x": jax.random.normal(ks[0], (BATCH, SEQ, D_MODEL), f32),
        "ffn1_norm": gain(ks[1], (DEPTH, D_MODEL)),
        "ffn1_w_gate": dense(ks[2], D_MODEL, D_FF),
        "ffn1_w_up": dense(ks[3], D_MODEL, D_FF),
        "ffn1_w_down": dense(ks[4], D_FF, D_MODEL),
        "mix_norm": gain(ks[5], (DEPTH, D_MODEL)),
        "w_in": dense(ks[6], D_MODEL, IN_WIDTH),
        "b_in": b_in,
        "attn_sinks": 0.5 * jax.random.normal(ks[8], (DEPTH, ATTN_HEADS), f32),
        "mlstm_conv": jax.random.normal(ks[9], (DEPTH, CONV_WIDTH, SZ_MQ + SZ_MK), f32) * CONV_WIDTH ** -0.5,
        "mlstm_head_norm": gain(ks[10], (DEPTH, MLSTM_HEADS, MLSTM_V_DIM)),
        "w_proj_attn": dense(ks[11], SZ_AQ, D_MODEL),
        "w_proj_mlstm": dense(ks[12], SZ_MV, D_MODEL),
        "w_out": dense(ks[13], D_MODEL, D_MODEL),
        "ffn2_norm": gain(ks[14], (DEPTH, D_MODEL)),
        "ffn2_w_gate": dense(ks[15], D_MODEL, D_FF),
        "ffn2_w_up": dense(ks[16], D_MODEL, D_FF),
        "ffn2_w_down": dense(ks[17], D_FF, D_MODEL),
        "final_norm": gain(ks[18], (D_MODEL,)),
    }


def reference(x, ffn1_norm, ffn1_w_gate, ffn1_w_up, ffn1_w_down, mix_norm, w_in, b_in,
              attn_sinks, mlstm_conv, mlstm_head_norm, w_proj_attn, w_proj_mlstm, w_out,
              ffn2_norm, ffn2_w_gate, ffn2_w_up, ffn2_w_down, final_norm):
    b, s, _ = x.shape
    for l in range(DEPTH):
        h = rms_norm(x, ffn1_norm[l])
        x = x + FFN_RESIDUAL_WEIGHT * swiglu(h, ffn1_w_gate[l], ffn1_w_up[l], ffn1_w_down[l])

        h = rms_norm(x, mix_norm[l])
        z = h @ w_in[l] + b_in[l]
        a_q, a_k, a_v, m_q, m_k, m_v, m_o, m_i, m_f, g_pre = jnp.split(z, SPLIT_POINTS, axis=-1)

        y_attn = sliding_window_attention(
            a_q.reshape(b, s, ATTN_HEADS, ATTN_HEAD_DIM),
            a_k.reshape(b, s, ATTN_KV_HEADS, ATTN_HEAD_DIM),
            a_v.reshape(b, s, ATTN_KV_HEADS, ATTN_HEAD_DIM),
            attn_sinks[l]) @ w_proj_attn[l]

        qk = jax.nn.silu(causal_depthwise_conv(jnp.concatenate([m_q, m_k], axis=-1), mlstm_conv[l]))
        mq = qk[..., :SZ_MQ].reshape(b, s, MLSTM_HEADS, MLSTM_QK_DIM)
        mk = qk[..., SZ_MQ:].reshape(b, s, MLSTM_HEADS, MLSTM_QK_DIM) * MLSTM_QK_DIM ** -0.5
        mv = m_v.reshape(b, s, MLSTM_HEADS, MLSTM_V_DIM)
        hm = mlstm_chunkwise(mq, mk, mv,
                             softcap(m_i.astype(jnp.float32)), softcap(m_f.astype(jnp.float32)))
        hm = rms_norm(hm, mlstm_head_norm[l]).reshape(b, s, SZ_MV)
        hm = (jax.nn.sigmoid(m_o.astype(jnp.float32)) * hm).astype(x.dtype)
        y_mlstm = hm @ w_proj_mlstm[l]

        gates = jax.nn.sigmoid(g_pre.astype(jnp.float32)).astype(x.dtype).reshape(b, s, 2, D_MODEL)
        merged = gates[:, :, 0] * y_attn + gates[:, :, 1] * y_mlstm
        x = x + merged @ w_out[l]

        h = rms_norm(x, ffn2_norm[l])
        x = x + FFN_RESIDUAL_WEIGHT * swiglu(h, ffn2_w_gate[l], ffn2_w_up[l], ffn2_w_down[l])
    return rms_norm(x, final_norm)
```

```python
import functools

import jax
import jax.numpy as jnp
from jax import lax
from jax.experimental import pallas as pl
from jax.experimental.pallas import tpu as pltpu

F32 = jnp.float32
BF16 = jnp.bfloat16

D_MODEL = 1024
ATTN_HEAD_DIM = 64
ATTN_HEADS = 16
ATTN_KV_HEADS = 4
ATTN_GROUP = 4
WINDOW = 128
MLSTM_HEADS = 8
MLSTM_V_DIM = 128
MLSTM_QK_DIM = 64
CONV_WIDTH = 4
GATE_SOFTCAP = 15.0
D_FF = 2816
FFN_RESIDUAL_WEIGHT = 0.5
NORM_EPS = 1e-6

SZ_AQ, SZ_AK, SZ_AV = 1024, 256, 256
SZ_MQ, SZ_MK, SZ_MV, SZ_MO = 512, 512, 1024, 1024
SZ_MI, SZ_MF = 8, 8

LANES = 128
SUBLANES = 8
MXU_DIM = 256
VMEM_LIMIT_BYTES = 56 * 1024 * 1024

MLSTM_CHUNK = 128
FF_CHUNK = MXU_DIM
TM_FFN = 1024
TM_INPROJ = 512
TM_MERGE = 1024

COL_Q = 0
COL_KV = 1024
COL_MQK = 2048
COL_MV = 3072
COL_MO = 4096
COL_G = 5120
PACKED_WIDTH = 7168


def _rms(x, g):
    return x * lax.rsqrt(jnp.mean(x * x, axis=-1, keepdims=True) + NORM_EPS) * g


def _const_spec(shape):
    zeros = (0,) * len(shape)
    return pl.BlockSpec(shape, lambda *_: zeros, pipeline_mode=pl.Buffered(1))


def _ffn_kernel(x_ref, g_ref, wg_ref, wu_ref, wd_ref, fin_ref, o_ref, acc_ref, *, apply_final):
    x = x_ref[...]
    h = _rms(x, g_ref[...]).astype(BF16)
    for c in range(D_FF // FF_CHUNK):
        sl = slice(c * FF_CHUNK, (c + 1) * FF_CHUNK)
        a = jnp.dot(h, wg_ref[:, sl], preferred_element_type=F32)
        u = jnp.dot(h, wu_ref[:, sl], preferred_element_type=F32)
        act = (a * jax.nn.sigmoid(a) * u).astype(BF16)
        d = jnp.dot(act, wd_ref[sl, :], preferred_element_type=F32)
        if c == 0:
            acc_ref[...] = d
        else:
            acc_ref[...] += d
    y = x + FFN_RESIDUAL_WEIGHT * acc_ref[...]
    if apply_final:
        y = _rms(y, fin_ref[...])
    o_ref[...] = y


def _ffn(x, g, wg, wu, wd, fin, apply_final):
    n = x.shape[0]
    tm = TM_FFN
    return pl.pallas_call(
        functools.partial(_ffn_kernel, apply_final=apply_final),
        grid=(n // tm,),
        in_specs=[
            pl.BlockSpec((tm, D_MODEL), lambda i: (i, 0)),
            _const_spec((1, D_MODEL)),
            _const_spec((D_MODEL, D_FF)),
            _const_spec((D_MODEL, D_FF)),
            _const_spec((D_FF, D_MODEL)),
            _const_spec((1, D_MODEL)),
        ],
        out_specs=pl.BlockSpec((tm, D_MODEL), lambda i: (i, 0)),
        out_shape=jax.ShapeDtypeStruct((n, D_MODEL), F32),
        scratch_shapes=[pltpu.VMEM((tm, D_MODEL), F32)],
        compiler_params=pltpu.CompilerParams(
            dimension_semantics=("arbitrary",), vmem_limit_bytes=VMEM_LIMIT_BYTES),
        name="ffn_final" if apply_final else "ffn",
    )(x, g, wg, wu, wd, fin)


def _inproj_kernel(x_ref, g_ref, w_ref, b_ref, wif_ref, bif_ref, conv_ref,
                   q_ref, kv_ref, mqk_ref, mv_ref, mo_ref, gate_ref, gt_ref,
                   conv_scr, *, tiles_per_seq):
    i = pl.program_id(0)
    tm = x_ref.shape[0]
    h = _rms(x_ref[...], g_ref[...]).astype(BF16)

    def seg(c0, width):
        return (jnp.dot(h, w_ref[:, c0:c0 + width], preferred_element_type=F32)
                + b_ref[:, c0:c0 + width])

    cw = 512
    for c in range(0, 1024, cw):
        q_ref[:, c:c + cw] = seg(COL_Q + c, cw).astype(BF16)
        kv_ref[:, c:c + cw] = seg(COL_KV + c, cw).astype(BF16)
        mv_ref[:, c:c + cw] = seg(COL_MV + c, cw).astype(BF16)
        mo_ref[:, c:c + cw] = seg(COL_MO + c, cw).astype(BF16)
    for c in range(0, 2048, cw):
        gate_ref[:, c:c + cw] = seg(COL_G + c, cw).astype(BF16)

    gt_ref[...] = lax.dot_general(wif_ref[...], h, (((1,), (1,)), ((), ())),
                                  preferred_element_type=F32) + bif_ref[...]

    @pl.when(i % tiles_per_seq == 0)
    def _():
        conv_scr[0:SUBLANES, :] = jnp.zeros((SUBLANES, 1024), F32)

    @pl.when(i % tiles_per_seq != 0)
    def _():
        conv_scr[0:SUBLANES, :] = conv_scr[tm:tm + SUBLANES, :]

    for c in range(0, 1024, cw):
        conv_scr[SUBLANES:SUBLANES + tm, c:c + cw] = seg(COL_MQK + c, cw)
    for c in range(0, 1024, cw):
        acc = None
        for j in range(CONV_WIDTH):
            r0 = SUBLANES - (CONV_WIDTH - 1) + j
            term = conv_scr[r0:r0 + tm, c:c + cw] * conv_ref[j:j + 1, c:c + cw]
            acc = term if acc is None else acc + term
        y = acc * jax.nn.sigmoid(acc)
        if c >= SZ_MQ:
            y = y * (MLSTM_QK_DIM ** -0.5)
        mqk_ref[:, c:c + cw] = y.astype(BF16)


def _inproj(x, g, w, b, wif, bif, conv, seq):
    n = x.shape[0]
    tm = TM_INPROJ
    row = lambda width: pl.BlockSpec((tm, width), lambda i: (i, 0))
    bf = lambda width: jax.ShapeDtypeStruct((n, width), BF16)
    return pl.pallas_call(
        functools.partial(_inproj_kernel, tiles_per_seq=seq // tm),
        grid=(n // tm,),
        in_specs=[
            row(D_MODEL),
            _const_spec((1, D_MODEL)),
            _const_spec((D_MODEL, PACKED_WIDTH)),
            _const_spec((1, PACKED_WIDTH)),
            _const_spec((2 * MLSTM_HEADS, D_MODEL)),
            _const_spec((2 * MLSTM_HEADS, 1)),
            _const_spec((CONV_WIDTH, 1024)),
        ],
        out_specs=[row(1024), row(1024), row(1024), row(1024), row(1024), row(2048),
                   pl.BlockSpec((2 * MLSTM_HEADS, tm), lambda i: (0, i))],
        out_shape=[bf(1024), bf(1024), bf(1024), bf(1024), bf(1024), bf(2048),
                   jax.ShapeDtypeStruct((2 * MLSTM_HEADS, n), F32)],
        scratch_shapes=[pltpu.VMEM((tm + 2 * SUBLANES, 1024), F32)],
        compiler_params=pltpu.CompilerParams(
            dimension_semantics=("arbitrary",), vmem_limit_bytes=VMEM_LIMIT_BYTES),
        name="inproj",
    )(x, g, w, b, wif, bif, conv)


def _attn_kernel(sink_ref, q_ref, kvc_ref, kvp_ref, o_ref):
    n = pl.program_id(1)
    w = WINDOW
    qi = lax.broadcasted_iota(jnp.int32, (w, 2 * w), 0)
    kj = lax.broadcasted_iota(jnp.int32, (w, 2 * w), 1)
    valid = (kj > qi) & (kj <= qi + w) & ((kj >= w) | (n > 0))
    lo_kv = lax.broadcasted_iota(jnp.int32, (2 * w, LANES), 1) < ATTN_HEAD_DIM
    lo_out = lax.broadcasted_iota(jnp.int32, (w, LANES), 1) < ATTN_HEAD_DIM
    scale = ATTN_HEAD_DIM ** -0.5
    zero = jnp.zeros((), BF16)
    for h in range(ATTN_KV_HEADS):
        ksl = slice(h * LANES, (h + 1) * LANES)
        vsl = slice(512 + h * LANES, 512 + (h + 1) * LANES)
        kk = jnp.concatenate([kvp_ref[:, ksl], kvc_ref[:, ksl]], axis=0)
        vv = jnp.concatenate([kvp_ref[:, vsl], kvc_ref[:, vsl]], axis=0)
        kk_par = (jnp.where(lo_kv, kk, zero), jnp.where(lo_kv, zero, kk))
        for p in range(ATTN_GROUP // 2):
            c0 = h * ATTN_GROUP * ATTN_HEAD_DIM + p * LANES
            q2 = q_ref[:, c0:c0 + LANES]
            outs = []
            for par in range(2):
                head = h * ATTN_GROUP + 2 * p + par
                s = lax.dot_general(q2, kk_par[par], (((1,), (1,)), ((), ())),
                                    preferred_element_type=F32) * scale
                s = jnp.where(valid, s, -jnp.inf)
                sink = sink_ref[head]
                mx = jnp.maximum(jnp.max(s, axis=-1, keepdims=True), sink)
                pr = jnp.exp(s - mx)
                denom = jnp.sum(pr, axis=-1, keepdims=True) + jnp.exp(sink - mx)
                o = jnp.dot(pr.astype(BF16), vv, preferred_element_type=F32)
                outs.append(o / denom)
            o_ref[:, c0:c0 + LANES] = jnp.where(lo_out, outs[0], outs[1]).astype(BF16)


def _attention(sinks, q, kv, batch, seq):
    n = q.shape[0]
    nb = seq // WINDOW
    return pl.pallas_call(
        _attn_kernel,
        grid=(batch, nb),
        in_specs=[
            pl.BlockSpec(memory_space=pltpu.SMEM),
            pl.BlockSpec((WINDOW, 1024), lambda b, j: (b * nb + j, 0)),
            pl.BlockSpec((WINDOW, 1024), lambda b, j: (b * nb + j, 0)),
            pl.BlockSpec((WINDOW, 1024), lambda b, j: (b * nb + jnp.maximum(j - 1, 0), 0)),
        ],
        out_specs=pl.BlockSpec((WINDOW, 1024), lambda b, j: (b * nb + j, 0)),
        out_shape=jax.ShapeDtypeStruct((n, 1024), BF16),
        compiler_params=pltpu.CompilerParams(
            dimension_semantics=("arbitrary", "arbitrary"), vmem_limit_bytes=VMEM_LIMIT_BYTES),
        name="attention",
    )(sinks, q, kv, kv)


def _log_sigmoid(x):
    return -(jnp.maximum(-x, 0.0) + jnp.log(1.0 + jnp.exp(-jnp.abs(x))))


def _mlstm_kernel(g_ref, qk_ref, v_ref, og_ref, hn_ref, o_ref, c_scr, n_scr, m_scr):
    L = MLSTM_CHUNK
    nh = MLSTM_HEADS

    @pl.when(pl.program_id(1) == 0)
    def _():
        c_scr[...] = jnp.zeros_like(c_scr)
        n_scr[...] = jnp.zeros_like(n_scr)
        m_scr[...] = jnp.zeros_like(m_scr)

    g = g_ref[0]
    ig = GATE_SOFTCAP * jnp.tanh(g[0:nh] / GATE_SOFTCAP)
    fg = GATE_SOFTCAP * jnp.tanh(g[nh:2 * nh] / GATE_SOFTCAP)
    lf = _log_sigmoid(fg)
    lane = lax.broadcasted_iota(jnp.int32, (nh, L), 1)
    a = lf
    sh = 1
    while sh < L:
        a = a + jnp.where(lane >= sh, pltpu.roll(a, sh, 1), 0.0)
        sh *= 2
    bvec = ig - a
    pm = bvec
    sh = 1
    while sh < L:
        pm = jnp.maximum(pm, jnp.where(lane >= sh, pltpu.roll(pm, sh, 1), -jnp.inf))
        sh *= 2
    mprev = m_scr[...]
    mrow = jnp.maximum(mprev, pm)
    total = jnp.broadcast_to(a[:, L - 1:L], (nh, L))
    mlast = jnp.broadcast_to(mrow[:, L - 1:L], (nh, L))
    m_scr[...] = total + mlast
    w_inter = jnp.exp(mprev - mrow)
    e_neg = jnp.exp(-(a + mrow))
    wk = jnp.exp(bvec - mlast)
    decay = jnp.exp(mprev - mlast)

    stacked = jnp.concatenate(
        [mrow, w_inter, e_neg, wk, jnp.zeros((L - 4 * nh, L), F32)], axis=0)
    cols = stacked.T

    def col(j):
        return cols[:, j:j + 1]

    row_i = lax.broadcasted_iota(jnp.int32, (L, L), 0)
    col_i = lax.broadcasted_iota(jnp.int32, (L, L), 1)
    causal = col_i <= row_i
    lo_lane = col_i < MLSTM_QK_DIM
    lo_row = row_i < MLSTM_QK_DIM
    zero = jnp.zeros((), BF16)

    for p in range(nh // 2):
        h0, h1 = 2 * p, 2 * p + 1
        q2 = qk_ref[:, p * LANES:(p + 1) * LANES]
        k2 = qk_ref[:, SZ_MQ + p * LANES:SZ_MQ + (p + 1) * LANES]
        q_par = (jnp.where(lo_lane, q2, zero), jnp.where(lo_lane, zero, q2))
        wk2 = jnp.where(lo_lane, col(3 * nh + h0), col(3 * nh + h1))
        kw_f = k2.astype(F32) * wk2
        kw = kw_f.astype(BF16)
        c2 = c_scr[p]
        c2b = c2.astype(BF16)
        n2 = n_scr[p]
        kv_par = []
        for par, hh in ((0, h0), (1, h1)):
            vsl = slice(hh * MLSTM_V_DIM, (hh + 1) * MLSTM_V_DIM)
            v = v_ref[:, vsl]
            qm = q_par[par]
            s = lax.dot_general(qm, k2, (((1,), (1,)), ((), ())), preferred_element_type=F32)
            dmat = jnp.where(causal, jnp.exp(bvec[hh:hh + 1, :] - col(hh)), 0.0)
            sc = s * dmat
            wi = col(nh + hh)
            qn = jnp.sum(qm.astype(F32) * n2, axis=-1, keepdims=True)
            den = jnp.sum(sc, axis=-1, keepdims=True) + wi * qn
            num = (jnp.dot(sc.astype(BF16), v, preferred_element_type=F32)
                   + wi * jnp.dot(qm, c2b, preferred_element_type=F32))
            hv = num / jnp.maximum(jnp.abs(den), col(2 * nh + hh))
            hv = _rms(hv, hn_ref[:, vsl])
            o_ref[:, vsl] = (jax.nn.sigmoid(og_ref[:, vsl].astype(F32)) * hv).astype(BF16)
            kv_par.append(lax.dot_general(kw, v, (((0,), (0,)), ((), ())),
                                          preferred_element_type=F32))
        dec_rows = jnp.where(lo_row, decay[h0:h0 + 1, :], decay[h1:h1 + 1, :])
        c_scr[p] = dec_rows * c2 + jnp.where(lo_row, kv_par[0], kv_par[1])
        dec_lanes = jnp.where(lo_lane[0:1, :], decay[h0:h0 + 1, :], decay[h1:h1 + 1, :])
        n_scr[p] = dec_lanes * n2 + jnp.sum(kw_f, axis=0, keepdims=True)


def _mlstm(g3, qk, v, og, hn, batch, seq):
    n = qk.shape[0]
    L = MLSTM_CHUNK
    nc = seq // L
    row = pl.BlockSpec((L, 1024), lambda b, c: (b * nc + c, 0))
    return pl.pallas_call(
        _mlstm_kernel,
        grid=(batch, nc),
        in_specs=[
            pl.BlockSpec((1, 2 * MLSTM_HEADS, L), lambda b, c: (b * nc + c, 0, 0)),
            row, row, row,
            _const_spec((1, 1024)),
        ],
        out_specs=row,
        out_shape=jax.ShapeDtypeStruct((n, 1024), BF16),
        scratch_shapes=[
            pltpu.VMEM((MLSTM_HEADS // 2, 2 * MLSTM_QK_DIM, MLSTM_V_DIM), F32),
            pltpu.VMEM((MLSTM_HEADS // 2, 1, 2 * MLSTM_QK_DIM), F32),
            pltpu.VMEM((MLSTM_HEADS, L), F32),
        ],
        compiler_params=pltpu.CompilerParams(
            dimension_semantics=("arbitrary", "arbitrary"), vmem_limit_bytes=VMEM_LIMIT_BYTES),
        name="mlstm",
    )(g3, qk, v, og, hn)


def _merge_kernel(x_ref, a_ref, hm_ref, gate_ref, wpa_ref, wpm_ref, wo_ref, o_ref):
    ya = jnp.dot(a_ref[...], wpa_ref[...], preferred_element_type=F32)
    ym = jnp.dot(hm_ref[...], wpm_ref[...], preferred_element_type=F32)
    ga = jax.nn.sigmoid(gate_ref[:, 0:D_MODEL].astype(F32))
    gm = jax.nn.sigmoid(gate_ref[:, D_MODEL:2 * D_MODEL].astype(F32))
    merged = (ga * ya + gm * ym).astype(BF16)
    o_ref[...] = x_ref[...] + jnp.dot(merged, wo_ref[...], preferred_element_type=F32)


def _merge(x, a, hm, gates, wpa, wpm, wo):
    n = x.shape[0]
    tm = TM_MERGE
    row = lambda width: pl.BlockSpec((tm, width), lambda i: (i, 0))
    return pl.pallas_call(
        _merge_kernel,
        grid=(n // tm,),
        in_specs=[row(D_MODEL), row(1024), row(1024), row(2048),
                  _const_spec((1024, D_MODEL)), _const_spec((1024, D_MODEL)),
                  _const_spec((D_MODEL, D_MODEL))],
        out_specs=row(D_MODEL),
        out_shape=jax.ShapeDtypeStruct((n, D_MODEL), F32),
        compiler_params=pltpu.CompilerParams(
            dimension_semantics=("arbitrary",), vmem_limit_bytes=VMEM_LIMIT_BYTES),
        name="merge",
    )(x, a, hm, gates, wpa, wpm, wo)


def _pack_inproj(w_in, b_in):
    o = 0
    parts = {}
    for name, size in (("aq", SZ_AQ), ("ak", SZ_AK), ("av", SZ_AV), ("mq", SZ_MQ), ("mk", SZ_MK),
                       ("mv", SZ_MV), ("mo", SZ_MO), ("mi", SZ_MI), ("mf", SZ_MF),
                       ("g", 2 * D_MODEL)):
        parts[name] = (w_in[:, o:o + size], b_in[o:o + size])
        o += size

    def dup(t):
        lead = t.shape[:-1]
        t = t.reshape(lead + (ATTN_KV_HEADS, 1, ATTN_HEAD_DIM))
        t = jnp.broadcast_to(t, lead + (ATTN_KV_HEADS, 2, ATTN_HEAD_DIM))
        return t.reshape(lead + (2 * SZ_AK,))

    order = [parts["aq"], tuple(dup(t) for t in parts["ak"]), tuple(dup(t) for t in parts["av"]),
             parts["mq"], parts["mk"], parts["mv"], parts["mo"], parts["g"]]
    w = jnp.concatenate([t[0] for t in order], axis=1).astype(BF16)
    b = jnp.concatenate([t[1] for t in order], axis=0).reshape(1, PACKED_WIDTH)
    wif = jnp.concatenate([parts["mi"][0], parts["mf"][0]], axis=1).T.astype(BF16)
    bif = jnp.concatenate([parts["mi"][1], parts["mf"][1]], axis=0).reshape(2 * MLSTM_HEADS, 1)
    return w, b, wif, bif


def kernel(x, ffn1_norm, ffn1_w_gate, ffn1_w_up, ffn1_w_down, mix_norm, w_in, b_in, attn_sinks,
           mlstm_conv, mlstm_head_norm, w_proj_attn, w_proj_mlstm, w_out, ffn2_norm, ffn2_w_gate,
           ffn2_w_up, ffn2_w_down, final_norm):
    batch, seq, d = x.shape
    assert d == D_MODEL and seq % TM_INPROJ == 0 and (batch * seq) % TM_FFN == 0
    assert ffn1_norm.shape[0] == 1, "one layer"
    n = batch * seq
    xf = x.reshape(n, d)
    fin = final_norm.reshape(1, d)

    x1 = _ffn(xf, ffn1_norm[0].reshape(1, d), ffn1_w_gate[0].astype(BF16), ffn1_w_up[0].astype(BF16),
              ffn1_w_down[0].astype(BF16), fin, apply_final=False)

    w, b, wif, bif = _pack_inproj(w_in[0], b_in[0])
    q, kv, mqk, mv, mo, gates, gt = _inproj(x1, mix_norm[0].reshape(1, d), w, b, wif, bif,
                                           mlstm_conv[0], seq)

    ya = _attention(attn_sinks[0], q, kv, batch, seq)

    nchunks = n // MLSTM_CHUNK
    g3 = gt.reshape(2 * MLSTM_HEADS, nchunks, MLSTM_CHUNK).transpose(1, 0, 2)
    hm = _mlstm(g3, mqk, mv, mo, mlstm_head_norm[0].reshape(1, SZ_MV), batch, seq)

    x2 = _merge(x1, ya, hm, gates, w_proj_attn[0].astype(BF16), w_proj_mlstm[0].astype(BF16),
                w_out[0].astype(BF16))

    out = _ffn(x2, ffn2_norm[0].reshape(1, d), ffn2_w_gate[0].astype(BF16), ffn2_w_up[0].astype(BF16),
               ffn2_w_down[0].astype(BF16), fin, apply_final=True)
    return out.reshape(batch, seq, d)
```

```python
import functools

import jax
import jax.numpy as jnp
from jax import lax
from jax.experimental import pallas as pl
from jax.experimental.pallas import tpu as pltpu

F32 = jnp.float32
BF16 = jnp.bfloat16

D_MODEL = 1024
ATTN_HEAD_DIM = 64
ATTN_HEADS = 16
ATTN_KV_HEADS = 4
ATTN_GROUP = 4
WINDOW = 128
MLSTM_HEADS = 8
MLSTM_V_DIM = 128
MLSTM_QK_DIM = 64
CONV_WIDTH = 4
GATE_SOFTCAP = 15.0
D_FF = 2816
FFN_RESIDUAL_WEIGHT = 0.5
NORM_EPS = 1e-6

SZ_AQ, SZ_AK, SZ_AV = 1024, 256, 256
SZ_MQ, SZ_MK, SZ_MV, SZ_MO = 512, 512, 1024, 1024
SZ_MI, SZ_MF = 8, 8

LANES = 128
SUBLANES = 8
MXU_DIM = 256
VMEM_LIMIT_BYTES = 56 * 1024 * 1024

MLSTM_CHUNK = 128
FF_CHUNK = MXU_DIM
TM_FFN = 1024
TM_INPROJ = 512
TM_MERGE = 1024
TM_MLSTM = 2 * MLSTM_CHUNK

LOG2E = 1.4426950408889634
ROW_M2, ROW_WINTER, ROW_ENEG, ROW_WK, ROW_DECAY = 0, 8, 16, 24, 32
GATE_ROWS = 40

COL_Q = 0
COL_KV = 1024
COL_MQK = 2048
COL_MO = 3072
COL_G = 4096
PACKED_WIDTH = 6144


def _rms(x, g):
    return x * lax.rsqrt(jnp.mean(x * x, axis=-1, keepdims=True) + NORM_EPS) * g


def _const_spec(shape):
    zeros = (0,) * len(shape)
    return pl.BlockSpec(shape, lambda *_: zeros, pipeline_mode=pl.Buffered(1))


def _ffn_kernel(x_ref, g_ref, wg_ref, wu_ref, wd_ref, fin_ref, o_ref, acc_ref, *, apply_final):
    x = x_ref[...]
    h = _rms(x, g_ref[...]).astype(BF16)
    for c in range(D_FF // FF_CHUNK):
        sl = slice(c * FF_CHUNK, (c + 1) * FF_CHUNK)
        a = jnp.dot(h, wg_ref[:, sl], preferred_element_type=F32)
        u = jnp.dot(h, wu_ref[:, sl], preferred_element_type=F32)
        act = (a * jax.nn.sigmoid(a) * u).astype(BF16)
        d = jnp.dot(act, wd_ref[sl, :], preferred_element_type=F32)
        if c == 0:
            acc_ref[...] = d
        else:
            acc_ref[...] += d
    y = x + FFN_RESIDUAL_WEIGHT * acc_ref[...]
    if apply_final:
        y = _rms(y, fin_ref[...])
    o_ref[...] = y


def _ffn(x, g, wg, wu, wd, fin, apply_final):
    n = x.shape[0]
    tm = TM_FFN
    return pl.pallas_call(
        functools.partial(_ffn_kernel, apply_final=apply_final),
        grid=(n // tm,),
        in_specs=[
            pl.BlockSpec((tm, D_MODEL), lambda i: (i, 0)),
            _const_spec((1, D_MODEL)),
            _const_spec((D_MODEL, D_FF)),
            _const_spec((D_MODEL, D_FF)),
            _const_spec((D_FF, D_MODEL)),
            _const_spec((1, D_MODEL)),
        ],
        out_specs=pl.BlockSpec((tm, D_MODEL), lambda i: (i, 0)),
        out_shape=jax.ShapeDtypeStruct((n, D_MODEL), F32),
        scratch_shapes=[pltpu.VMEM((tm, D_MODEL), F32)],
        compiler_params=pltpu.CompilerParams(
            dimension_semantics=("arbitrary",), vmem_limit_bytes=VMEM_LIMIT_BYTES),
        name="ffn_final" if apply_final else "ffn",
    )(x, g, wg, wu, wd, fin)


def _inproj_kernel(x_ref, g_ref, w_ref, b_ref, wif_ref, bif_ref, wvt_ref, bvt_ref, conv_ref,
                   q_ref, kv_ref, mqk_ref, mvt_ref, mo_ref, gate_ref, gt_ref,
                   conv_scr, *, tiles_per_seq):
    i = pl.program_id(0)
    tm = x_ref.shape[0]
    h = _rms(x_ref[...], g_ref[...]).astype(BF16)

    def seg(c0, width):
        return (jnp.dot(h, w_ref[:, c0:c0 + width], preferred_element_type=F32)
                + b_ref[:, c0:c0 + width])

    def seg_t(wt_ref, bt_ref, r0, rows):
        return (lax.dot_general(wt_ref[r0:r0 + rows, :], h, (((1,), (1,)), ((), ())),
                                preferred_element_type=F32) + bt_ref[r0:r0 + rows, :])

    cw = 512
    for c in range(0, 1024, cw):
        q_ref[:, c:c + cw] = seg(COL_Q + c, cw).astype(BF16)
        kv_ref[:, c:c + cw] = seg(COL_KV + c, cw).astype(BF16)
        mvt_ref[c:c + cw, :] = seg_t(wvt_ref, bvt_ref, c, cw).astype(BF16)
        mo_ref[:, c:c + cw] = seg(COL_MO + c, cw).astype(BF16)
    for c in range(0, 2048, cw):
        gate_ref[:, c:c + cw] = seg(COL_G + c, cw).astype(BF16)

    gt_ref[...] = seg_t(wif_ref, bif_ref, 0, 2 * MLSTM_HEADS)

    @pl.when(i % tiles_per_seq == 0)
    def _():
        conv_scr[0:SUBLANES, :] = jnp.zeros((SUBLANES, 1024), F32)

    @pl.when(i % tiles_per_seq != 0)
    def _():
        conv_scr[0:SUBLANES, :] = conv_scr[tm:tm + SUBLANES, :]

    for c in range(0, 1024, cw):
        conv_scr[SUBLANES:SUBLANES + tm, c:c + cw] = seg(COL_MQK + c, cw)
    for c in range(0, 1024, cw):
        acc = None
        for j in range(CONV_WIDTH):
            r0 = SUBLANES - (CONV_WIDTH - 1) + j
            term = conv_scr[r0:r0 + tm, c:c + cw] * conv_ref[j:j + 1, c:c + cw]
            acc = term if acc is None else acc + term
        y = acc * jax.nn.sigmoid(acc)
        if c >= SZ_MQ:
            y = y * (MLSTM_QK_DIM ** -0.5)
        mqk_ref[:, c:c + cw] = y.astype(BF16)


def _inproj(x, g, w, b, wif, bif, wvt, bvt, conv, seq):
    n = x.shape[0]
    tm = TM_INPROJ
    row = lambda width: pl.BlockSpec((tm, width), lambda i: (i, 0))
    col = lambda height: pl.BlockSpec((height, tm), lambda i: (0, i))
    bf = lambda width: jax.ShapeDtypeStruct((n, width), BF16)
    return pl.pallas_call(
        functools.partial(_inproj_kernel, tiles_per_seq=seq // tm),
        grid=(n // tm,),
        in_specs=[
            row(D_MODEL),
            _const_spec((1, D_MODEL)),
            _const_spec((D_MODEL, PACKED_WIDTH)),
            _const_spec((1, PACKED_WIDTH)),
            _const_spec((2 * MLSTM_HEADS, D_MODEL)),
            _const_spec((2 * MLSTM_HEADS, 1)),
            _const_spec((SZ_MV, D_MODEL)),
            _const_spec((SZ_MV, 1)),
            _const_spec((CONV_WIDTH, 1024)),
        ],
        out_specs=[row(1024), row(1024), row(1024), col(SZ_MV), row(1024), row(2048),
                   col(2 * MLSTM_HEADS)],
        out_shape=[bf(1024), bf(1024), bf(1024), jax.ShapeDtypeStruct((SZ_MV, n), BF16),
                   bf(1024), bf(2048), jax.ShapeDtypeStruct((2 * MLSTM_HEADS, n), F32)],
        scratch_shapes=[pltpu.VMEM((tm + 2 * SUBLANES, 1024), F32)],
        compiler_params=pltpu.CompilerParams(
            dimension_semantics=("arbitrary",), vmem_limit_bytes=VMEM_LIMIT_BYTES),
        name="inproj",
    )(x, g, w, b, wif, bif, wvt, bvt, conv)


def _attn_kernel(sink_ref, q_ref, kvc_ref, kvp_ref, o_ref):
    n = pl.program_id(1)
    w = WINDOW
    qi = lax.broadcasted_iota(jnp.int32, (w, 2 * w), 0)
    kj = lax.broadcasted_iota(jnp.int32, (w, 2 * w), 1)
    valid = (kj > qi) & (kj <= qi + w) & ((kj >= w) | (n > 0))
    lo_kv = lax.broadcasted_iota(jnp.int32, (2 * w, LANES), 1) < ATTN_HEAD_DIM
    lo_out = lax.broadcasted_iota(jnp.int32, (w, LANES), 1) < ATTN_HEAD_DIM
    scale = ATTN_HEAD_DIM ** -0.5
    zero = jnp.zeros((), BF16)
    for h in range(ATTN_KV_HEADS):
        ksl = slice(h * LANES, (h + 1) * LANES)
        vsl = slice(512 + h * LANES, 512 + (h + 1) * LANES)
        kk = jnp.concatenate([kvp_ref[:, ksl], kvc_ref[:, ksl]], axis=0)
        vv = jnp.concatenate([kvp_ref[:, vsl], kvc_ref[:, vsl]], axis=0)
        kk_par = (jnp.where(lo_kv, kk, zero), jnp.where(lo_kv, zero, kk))
        for p in range(ATTN_GROUP // 2):
            c0 = h * ATTN_GROUP * ATTN_HEAD_DIM + p * LANES
            q2 = q_ref[:, c0:c0 + LANES]
            outs = []
            for par in range(2):
                head = h * ATTN_GROUP + 2 * p + par
                s = lax.dot_general(q2, kk_par[par], (((1,), (1,)), ((), ())),
                                    preferred_element_type=F32) * scale
                s = jnp.where(valid, s, -jnp.inf)
                sink = sink_ref[head]
                mx = jnp.maximum(jnp.max(s, axis=-1, keepdims=True), sink)
                pr = jnp.exp(s - mx)
                denom = jnp.sum(pr, axis=-1, keepdims=True) + jnp.exp(sink - mx)
                o = jnp.dot(pr.astype(BF16), vv, preferred_element_type=F32)
                outs.append(o / denom)
            o_ref[:, c0:c0 + LANES] = jnp.where(lo_out, outs[0], outs[1]).astype(BF16)


def _attention(sinks, q, kv, batch, seq):
    n = q.shape[0]
    nb = seq // WINDOW
    return pl.pallas_call(
        _attn_kernel,
        grid=(batch, nb),
        in_specs=[
            pl.BlockSpec(memory_space=pltpu.SMEM),
            pl.BlockSpec((WINDOW, 1024), lambda b, j: (b * nb + j, 0)),
            pl.BlockSpec((WINDOW, 1024), lambda b, j: (b * nb + j, 0)),
            pl.BlockSpec((WINDOW, 1024), lambda b, j: (b * nb + jnp.maximum(j - 1, 0), 0)),
        ],
        out_specs=pl.BlockSpec((WINDOW, 1024), lambda b, j: (b * nb + j, 0)),
        out_shape=jax.ShapeDtypeStruct((n, 1024), BF16),
        compiler_params=pltpu.CompilerParams(
            dimension_semantics=("arbitrary", "arbitrary"), vmem_limit_bytes=VMEM_LIMIT_BYTES),
        name="attention",
    )(sinks, q, kv, kv)


def _log_sigmoid(x):
    return -(jnp.maximum(-x, 0.0) + jnp.log(1.0 + jnp.exp(-jnp.abs(x))))


def _mlstm_gate_kernel(g_ref, rows_ref, cols_ref):
    L = MLSTM_CHUNK
    nh = MLSTM_HEADS
    seq = g_ref.shape[1]
    g = g_ref[...]
    ig = GATE_SOFTCAP * jnp.tanh(g[0:nh] / GATE_SOFTCAP)
    fg = GATE_SOFTCAP * jnp.tanh(g[nh:2 * nh] / GATE_SOFTCAP)
    lf = _log_sigmoid(fg)
    pos = lax.broadcasted_iota(jnp.int32, (nh, seq), 1) & (L - 1)
    a = lf
    sh = 1
    while sh < L:
        a = a + jnp.where(pos >= sh, pltpu.roll(a, sh, 1), 0.0)
        sh *= 2
    bvec = ig - a
    pm = bvec
    sh = 1
    while sh < L:
        pm = jnp.maximum(pm, jnp.where(pos >= sh, pltpu.roll(pm, sh, 1), -jnp.inf))
        sh *= 2
    m = jnp.zeros((nh, L), F32)
    pad = jnp.zeros((L - nh, L), F32)
    for c in range(seq // L):
        sl = slice(c * L, (c + 1) * L)
        a_c, b_c = a[:, sl], bvec[:, sl]
        mrow = jnp.maximum(m, pm[:, sl])
        total = jnp.broadcast_to(a_c[:, L - 1:L], (nh, L))
        mlast = jnp.broadcast_to(mrow[:, L - 1:L], (nh, L))
        rows_ref[ROW_M2:ROW_M2 + nh, sl] = mrow * LOG2E
        rows_ref[ROW_WINTER:ROW_WINTER + nh, sl] = jnp.exp(m - mrow)
        rows_ref[ROW_ENEG:ROW_ENEG + nh, sl] = jnp.exp(-(a_c + mrow))
        rows_ref[ROW_WK:ROW_WK + nh, sl] = jnp.exp(b_c - mlast)
        rows_ref[ROW_DECAY:ROW_DECAY + nh, sl] = jnp.exp(m - mlast)
        m = total + mlast
        cols_ref[sl, :] = jnp.concatenate([b_c * LOG2E, pad], axis=0).T


def _mlstm_gates(gt, batch, seq):
    n = gt.shape[1]
    return pl.pallas_call(
        _mlstm_gate_kernel,
        grid=(batch,),
        in_specs=[pl.BlockSpec((2 * MLSTM_HEADS, seq), lambda b: (0, b))],
        out_specs=[pl.BlockSpec((GATE_ROWS, seq), lambda b: (0, b)),
                   pl.BlockSpec((seq, LANES), lambda b: (b, 0))],
        out_shape=[jax.ShapeDtypeStruct((GATE_ROWS, n), F32),
                   jax.ShapeDtypeStruct((n, LANES), F32)],
        compiler_params=pltpu.CompilerParams(
            dimension_semantics=("arbitrary",), vmem_limit_bytes=VMEM_LIMIT_BYTES),
        name="mlstm_gates",
    )(gt)


def _mlstm_kernel(rows_ref, cols_ref, qk_ref, vt_ref, og_ref, hn_ref, o_ref, ct_scr, n_scr):
    L = MLSTM_CHUNK
    nh = MLSTM_HEADS
    contract_lanes = (((1,), (1,)), ((), ()))

    @pl.when(pl.program_id(1) == 0)
    def _():
        ct_scr[...] = jnp.zeros_like(ct_scr)
        n_scr[...] = jnp.zeros_like(n_scr)

    row_i = lax.broadcasted_iota(jnp.int32, (L, L), 0)
    col_i = lax.broadcasted_iota(jnp.int32, (L, L), 1)
    causal_t = row_i <= col_i
    lo_lane = col_i < MLSTM_QK_DIM
    zero = jnp.zeros((), BF16)

    for ci in range(qk_ref.shape[0] // L):
        rs = slice(ci * L, (ci + 1) * L)
        m2 = rows_ref[ROW_M2:ROW_M2 + nh, rs]
        w_inter = rows_ref[ROW_WINTER:ROW_WINTER + nh, rs]
        e_neg = rows_ref[ROW_ENEG:ROW_ENEG + nh, rs]
        wk = rows_ref[ROW_WK:ROW_WK + nh, rs]
        decay = rows_ref[ROW_DECAY:ROW_DECAY + nh, rs]
        b2cols = cols_ref[rs, :]
        wk_b = jnp.concatenate([wk, wk], axis=0).astype(BF16)

        for p in range(nh // 2):
            h0, h1 = 2 * p, 2 * p + 1
            q2 = qk_ref[rs, p * LANES:(p + 1) * LANES]
            k2 = qk_ref[rs, SZ_MQ + p * LANES:SZ_MQ + (p + 1) * LANES]
            q_par = (jnp.where(lo_lane, q2, zero), jnp.where(lo_lane, zero, q2))
            ct2 = ct_scr[p]
            ct2b = ct2.astype(BF16)
            n2 = n_scr[p]
            n2b = jnp.broadcast_to(n2, (2 * SUBLANES, LANES)).astype(BF16)
            upd = []
            for par, hh in ((0, h0), (1, h1)):
                vsl = slice(hh * MLSTM_V_DIM, (hh + 1) * MLSTM_V_DIM)
                vt = vt_ref[vsl, rs]
                qm = q_par[par]
                st = lax.dot_general(k2, qm, contract_lanes, preferred_element_type=F32)
                dt = jnp.where(causal_t, jnp.exp2(b2cols[:, hh:hh + 1] - m2[hh:hh + 1, :]), 0.0)
                sct = st * dt
                wi = w_inter[hh:hh + 1, :]
                qn = lax.dot_general(n2b, qm, contract_lanes, preferred_element_type=F32)[0:1, :]
                den = jnp.sum(sct, axis=0, keepdims=True) + wi * qn
                numt = (jnp.dot(vt, sct.astype(BF16), preferred_element_type=F32)
                        + wi * lax.dot_general(ct2b, qm, contract_lanes,
                                               preferred_element_type=F32))
                rden = 1.0 / jnp.maximum(jnp.abs(den), e_neg[hh:hh + 1, :])
                ms = jnp.mean(numt * numt, axis=0, keepdims=True)
                scale = rden * lax.rsqrt(rden * rden * ms + NORM_EPS)
                hv = (numt * scale).T
                gate = jax.nn.sigmoid(og_ref[rs, vsl].astype(F32))
                o_ref[rs, vsl] = (hv * hn_ref[:, vsl] * gate).astype(BF16)
                vw = (vt.astype(F32) * wk[hh:hh + 1, :]).astype(BF16)
                upd.append(jnp.dot(vw, k2, preferred_element_type=F32))
            dec_lanes = jnp.where(lo_lane[0:1, :], decay[h0:h0 + 1, :], decay[h1:h1 + 1, :])
            ct_scr[p] = dec_lanes * ct2 + jnp.where(lo_lane, upd[0], upd[1])
            n_inc = jnp.dot(wk_b, k2, preferred_element_type=F32)
            n_scr[p] = dec_lanes * n2 + jnp.where(lo_lane[0:1, :], n_inc[h0:h0 + 1, :],
                                                  n_inc[h1:h1 + 1, :])


def _mlstm(rows, cols, qk, v, og, hn, batch, seq):
    n = qk.shape[0]
    tm = TM_MLSTM
    nt = seq // tm
    row = pl.BlockSpec((tm, 1024), lambda b, c: (b * nt + c, 0))
    return pl.pallas_call(
        _mlstm_kernel,
        grid=(batch, nt),
        in_specs=[
            pl.BlockSpec((GATE_ROWS, tm), lambda b, c: (0, b * nt + c)),
            pl.BlockSpec((tm, LANES), lambda b, c: (b * nt + c, 0)),
            row,
            pl.BlockSpec((SZ_MV, tm), lambda b, c: (0, b * nt + c)),
            row,
            _const_spec((1, 1024)),
        ],
        out_specs=row,
        out_shape=jax.ShapeDtypeStruct((n, 1024), BF16),
        scratch_shapes=[
            pltpu.VMEM((MLSTM_HEADS // 2, 2 * MLSTM_QK_DIM, MLSTM_V_DIM), F32),
            pltpu.VMEM((MLSTM_HEADS // 2, 1, 2 * MLSTM_QK_DIM), F32),
        ],
        compiler_params=pltpu.CompilerParams(
            dimension_semantics=("arbitrary", "arbitrary"), vmem_limit_bytes=VMEM_LIMIT_BYTES),
        name="mlstm",
    )(rows, cols, qk, v, og, hn)


def _merge_kernel(x_ref, a_ref, hm_ref, gate_ref, wpa_ref, wpm_ref, wo_ref, o_ref):
    ya = jnp.dot(a_ref[...], wpa_ref[...], preferred_element_type=F32)
    ym = jnp.dot(hm_ref[...], wpm_ref[...], preferred_element_type=F32)
    ga = jax.nn.sigmoid(gate_ref[:, 0:D_MODEL].astype(F32))
    gm = jax.nn.sigmoid(gate_ref[:, D_MODEL:2 * D_MODEL].astype(F32))
    merged = (ga * ya + gm * ym).astype(BF16)
    o_ref[...] = x_ref[...] + jnp.dot(merged, wo_ref[...], preferred_element_type=F32)


def _merge(x, a, hm, gates, wpa, wpm, wo):
    n = x.shape[0]
    tm = TM_MERGE
    row = lambda width: pl.BlockSpec((tm, width), lambda i: (i, 0))
    return pl.pallas_call(
        _merge_kernel,
        grid=(n // tm,),
        in_specs=[row(D_MODEL), row(1024), row(1024), row(2048),
                  _const_spec((1024, D_MODEL)), _const_spec((1024, D_MODEL)),
                  _const_spec((D_MODEL, D_MODEL))],
        out_specs=row(D_MODEL),
        out_shape=jax.ShapeDtypeStruct((n, D_MODEL), F32),
        compiler_params=pltpu.CompilerParams(
            dimension_semantics=("arbitrary",), vmem_limit_bytes=VMEM_LIMIT_BYTES),
        name="merge",
    )(x, a, hm, gates, wpa, wpm, wo)


def _pack_inproj(w_in, b_in):
    o = 0
    parts = {}
    for name, size in (("aq", SZ_AQ), ("ak", SZ_AK), ("av", SZ_AV), ("mq", SZ_MQ), ("mk", SZ_MK),
                       ("mv", SZ_MV), ("mo", SZ_MO), ("mi", SZ_MI), ("mf", SZ_MF),
                       ("g", 2 * D_MODEL)):
        parts[name] = (w_in[:, o:o + size], b_in[o:o + size])
        o += size

    def dup(t):
        lead = t.shape[:-1]
        t = t.reshape(lead + (ATTN_KV_HEADS, 1, ATTN_HEAD_DIM))
        t = jnp.broadcast_to(t, lead + (ATTN_KV_HEADS, 2, ATTN_HEAD_DIM))
        return t.reshape(lead + (2 * SZ_AK,))

    order = [parts["aq"], tuple(dup(t) for t in parts["ak"]), tuple(dup(t) for t in parts["av"]),
             parts["mq"], parts["mk"], parts["mo"], parts["g"]]
    w = jnp.concatenate([t[0] for t in order], axis=1).astype(BF16)
    b = jnp.concatenate([t[1] for t in order], axis=0).reshape(1, PACKED_WIDTH)
    wif = jnp.concatenate([parts["mi"][0], parts["mf"][0]], axis=1).T.astype(BF16)
    bif = jnp.concatenate([parts["mi"][1], parts["mf"][1]], axis=0).reshape(2 * MLSTM_HEADS, 1)
    wvt = parts["mv"][0].T.astype(BF16)
    bvt = parts["mv"][1].reshape(SZ_MV, 1)
    return w, b, wif, bif, wvt, bvt


def kernel(x, ffn1_norm, ffn1_w_gate, ffn1_w_up, ffn1_w_down, mix_norm, w_in, b_in, attn_sinks,
           mlstm_conv, mlstm_head_norm, w_proj_attn, w_proj_mlstm, w_out, ffn2_norm, ffn2_w_gate,
           ffn2_w_up, ffn2_w_down, final_norm):
    batch, seq, d = x.shape
    assert d == D_MODEL and seq % TM_INPROJ == 0 and (batch * seq) % TM_FFN == 0
    assert ffn1_norm.shape[0] == 1, "one layer"
    n = batch * seq
    xf = x.reshape(n, d)
    fin = final_norm.reshape(1, d)

    x1 = _ffn(xf, ffn1_norm[0].reshape(1, d), ffn1_w_gate[0].astype(BF16), ffn1_w_up[0].astype(BF16),
              ffn1_w_down[0].astype(BF16), fin, apply_final=False)

    w, b, wif, bif, wvt, bvt = _pack_inproj(w_in[0], b_in[0])
    q, kv, mqk, mvt, mo, gates, gt = _inproj(x1, mix_norm[0].reshape(1, d), w, b, wif, bif,
                                            wvt, bvt, mlstm_conv[0], seq)

    ya = _attention(attn_sinks[0], q, kv, batch, seq)

    rows, cols = _mlstm_gates(gt, batch, seq)
    hm = _mlstm(rows, cols, mqk, mvt, mo, mlstm_head_norm[0].reshape(1, SZ_MV), batch, seq)

    x2 = _merge(x1, ya, hm, gates, w_proj_attn[0].astype(BF16), w_proj_mlstm[0].astype(BF16),
                w_out[0].astype(BF16))

    out = _ffn(x2, ffn2_norm[0].reshape(1, d), ffn2_w_gate[0].astype(BF16), ffn2_w_up[0].astype(BF16),
               ffn2_w_down[0].astype(BF16), fin, apply_final=True)
    return out.reshape(batch, seq, d)
```

```python
import functools

import jax
import jax.numpy as jnp
from jax import lax
from jax.experimental import pallas as pl
from jax.experimental.pallas import tpu as pltpu

F32 = jnp.float32
BF16 = jnp.bfloat16

D_MODEL = 1024
ATTN_HEAD_DIM = 64
ATTN_HEADS = 16
ATTN_KV_HEADS = 4
ATTN_GROUP = 4
WINDOW = 128
MLSTM_HEADS = 8
MLSTM_V_DIM = 128
MLSTM_QK_DIM = 64
CONV_WIDTH = 4
GATE_SOFTCAP = 15.0
D_FF = 2816
FFN_RESIDUAL_WEIGHT = 0.5
NORM_EPS = 1e-6

SZ_AQ, SZ_AK, SZ_AV = 1024, 256, 256
SZ_MQ, SZ_MK, SZ_MV, SZ_MO = 512, 512, 1024, 1024
SZ_MI, SZ_MF = 8, 8

LANES = 128
SUBLANES = 8
MXU_DIM = 256
VMEM_LIMIT_BYTES = 56 * 1024 * 1024

MLSTM_CHUNK = 128
FF_CHUNK = MXU_DIM
TM_FFN = 1024
TM_INPROJ = 512
TM_MERGE = 1024
TM_MLSTM = 2 * MLSTM_CHUNK
ATTN_LOOKAHEAD = 8

LOG2E = 1.4426950408889634
ROW_M2, ROW_WINTER, ROW_ENEG, ROW_WK, ROW_DECAY = 0, 8, 16, 24, 32
GATE_ROWS = 40

COL_Q = 0
COL_KK = 1024
COL_MQK = 1536
COL_MO = 2560
COL_G = 3584
PACKED_WIDTH = 5632


def _rms(x, g):
    return x * lax.rsqrt(jnp.mean(x * x, axis=-1, keepdims=True) + NORM_EPS) * g


def _const_spec(shape):
    zeros = (0,) * len(shape)
    return pl.BlockSpec(shape, lambda *_: zeros, pipeline_mode=pl.Buffered(1))


def _ffn_kernel(x_ref, g_ref, wg_ref, wu_ref, wd_ref, fin_ref, o_ref, acc_ref, *, apply_final):
    x = x_ref[...]
    h = _rms(x, g_ref[...]).astype(BF16)
    for c in range(D_FF // FF_CHUNK):
        sl = slice(c * FF_CHUNK, (c + 1) * FF_CHUNK)
        a = jnp.dot(h, wg_ref[:, sl], preferred_element_type=F32)
        u = jnp.dot(h, wu_ref[:, sl], preferred_element_type=F32)
        act = (a * jax.nn.sigmoid(a) * u).astype(BF16)
        d = jnp.dot(act, wd_ref[sl, :], preferred_element_type=F32)
        if c == 0:
            acc_ref[...] = d
        else:
            acc_ref[...] += d
    y = x + FFN_RESIDUAL_WEIGHT * acc_ref[...]
    if apply_final:
        y = _rms(y, fin_ref[...])
    o_ref[...] = y


def _ffn(x, g, wg, wu, wd, fin, apply_final):
    n = x.shape[0]
    tm = TM_FFN
    return pl.pallas_call(
        functools.partial(_ffn_kernel, apply_final=apply_final),
        grid=(n // tm,),
        in_specs=[
            pl.BlockSpec((tm, D_MODEL), lambda i: (i, 0)),
            _const_spec((1, D_MODEL)),
            _const_spec((D_MODEL, D_FF)),
            _const_spec((D_MODEL, D_FF)),
            _const_spec((D_FF, D_MODEL)),
            _const_spec((1, D_MODEL)),
        ],
        out_specs=pl.BlockSpec((tm, D_MODEL), lambda i: (i, 0)),
        out_shape=jax.ShapeDtypeStruct((n, D_MODEL), F32),
        scratch_shapes=[pltpu.VMEM((tm, D_MODEL), F32)],
        compiler_params=pltpu.CompilerParams(
            dimension_semantics=("arbitrary",), vmem_limit_bytes=VMEM_LIMIT_BYTES),
        name="ffn_final" if apply_final else "ffn",
    )(x, g, wg, wu, wd, fin)


def _inproj_kernel(x_ref, g_ref, w_ref, b_ref, wif_ref, bif_ref, wvt_ref, bvt_ref, conv_ref,
                   q_ref, kk_ref, avt_ref, mqk_ref, mvt_ref, mo_ref, gate_ref, gt_ref,
                   conv_scr, *, tiles_per_seq):
    i = pl.program_id(0)
    tm = x_ref.shape[0]
    h = _rms(x_ref[...], g_ref[...]).astype(BF16)

    def seg(c0, width):
        return (jnp.dot(h, w_ref[:, c0:c0 + width], preferred_element_type=F32)
                + b_ref[:, c0:c0 + width])

    def seg_t(wt_ref, bt_ref, r0, rows):
        return (lax.dot_general(wt_ref[r0:r0 + rows, :], h, (((1,), (1,)), ((), ())),
                                preferred_element_type=F32) + bt_ref[r0:r0 + rows, :])

    cw = 512

    @pl.when(i == 0)
    def _():
        conv_scr[...] = jnp.zeros_like(conv_scr)

    carry = jnp.where(i % tiles_per_seq == 0, 0.0, conv_scr[...])

    def conv_slab(c, width):
        z = seg(COL_MQK + c, width)
        conv_scr[:, c:c + width] = z[tm - SUBLANES:tm, :]
        zc = jnp.concatenate([carry[:, c:c + width], z], axis=0)
        acc = z * conv_ref[CONV_WIDTH - 1:CONV_WIDTH, c:c + width]
        for k in range(1, CONV_WIDTH):
            zk = pltpu.roll(zc, k, 0)[SUBLANES:, :]
            acc = acc + zk * conv_ref[CONV_WIDTH - 1 - k:CONV_WIDTH - k, c:c + width]
        y = acc * jax.nn.sigmoid(acc)
        if c >= SZ_MQ:
            y = y * (MLSTM_QK_DIM ** -0.5)
        mqk_ref[:, c:c + width] = y.astype(BF16)

    dw = MXU_DIM
    for k in range(1024 // dw):
        sl = slice(k * dw, (k + 1) * dw)
        conv_slab(k * dw, dw)
        q_ref[:, sl] = (seg(COL_Q + k * dw, dw) * (LOG2E * ATTN_HEAD_DIM ** -0.5)).astype(BF16)
        if k < 2 * SZ_AK // dw:
            kk_ref[:, sl] = seg(COL_KK + k * dw, dw).astype(BF16)
        elif k == 2 * SZ_AK // dw:
            avt_ref[...] = seg_t(wvt_ref, bvt_ref, SZ_MV, SZ_AV).astype(BF16)
        mvt_ref[sl, :] = seg_t(wvt_ref, bvt_ref, k * dw, dw).astype(BF16)
        mo_ref[:, sl] = seg(COL_MO + k * dw, dw).astype(BF16)
        for c in (2 * k * dw, (2 * k + 1) * dw):
            gate_ref[:, c:c + dw] = seg(COL_G + c, dw).astype(BF16)

    gt_ref[...] = seg_t(wif_ref, bif_ref, 0, 2 * MLSTM_HEADS)


def _inproj(x, g, w, b, wif, bif, wvt, bvt, conv, seq):
    n = x.shape[0]
    tm = TM_INPROJ
    row = lambda width: pl.BlockSpec((tm, width), lambda i: (i, 0))
    col = lambda height: pl.BlockSpec((height, tm), lambda i: (0, i))
    bf = lambda width: jax.ShapeDtypeStruct((n, width), BF16)
    return pl.pallas_call(
        functools.partial(_inproj_kernel, tiles_per_seq=seq // tm),
        grid=(n // tm,),
        in_specs=[
            row(D_MODEL),
            _const_spec((1, D_MODEL)),
            _const_spec((D_MODEL, PACKED_WIDTH)),
            _const_spec((1, PACKED_WIDTH)),
            _const_spec((2 * MLSTM_HEADS, D_MODEL)),
            _const_spec((2 * MLSTM_HEADS, 1)),
            _const_spec((SZ_MV + SZ_AV, D_MODEL)),
            _const_spec((SZ_MV + SZ_AV, 1)),
            _const_spec((CONV_WIDTH, 1024)),
        ],
        out_specs=[row(1024), row(2 * SZ_AK), col(SZ_AV), row(1024), col(SZ_MV), row(1024),
                   row(2048), col(2 * MLSTM_HEADS)],
        out_shape=[bf(1024), bf(2 * SZ_AK), jax.ShapeDtypeStruct((SZ_AV, n), BF16), bf(1024),
                   jax.ShapeDtypeStruct((SZ_MV, n), BF16), bf(1024), bf(2048),
                   jax.ShapeDtypeStruct((2 * MLSTM_HEADS, n), F32)],
        scratch_shapes=[pltpu.VMEM((SUBLANES, 1024), F32)],
        compiler_params=pltpu.CompilerParams(
            dimension_semantics=("arbitrary",), vmem_limit_bytes=VMEM_LIMIT_BYTES),
        name="inproj",
    )(x, g, w, b, wif, bif, wvt, bvt, conv)


def _attn_kernel(sink_ref, q_ref, kc_ref, kp_ref, vtc_ref, vtp_ref, o_ref):
    n = pl.program_id(1)
    w = WINDOW
    contract_lanes = (((1,), (1,)), ((), ()))
    kj = lax.broadcasted_iota(jnp.int32, (w, w), 0)
    qi = lax.broadcasted_iota(jnp.int32, (w, w), 1)
    valid_prev = (kj > qi) & (n > 0)
    valid_cur = kj <= qi
    lo = lax.broadcasted_iota(jnp.int32, (w, LANES), 1) < ATTN_HEAD_DIM
    zero = jnp.zeros((), BF16)
    units = [(h, p, par) for h in range(ATTN_KV_HEADS) for p in range(ATTN_GROUP // 2)
             for par in range(2)]

    def scores(h, p, par):
        c0 = h * ATTN_GROUP * ATTN_HEAD_DIM + p * LANES
        q2 = q_ref[:, c0:c0 + LANES]
        qm = jnp.where(lo, q2, zero) if par == 0 else jnp.where(lo, zero, q2)
        ksl = slice(h * LANES, (h + 1) * LANES)
        sp = lax.dot_general(kp_ref[:, ksl], qm, contract_lanes, preferred_element_type=F32)
        sc = lax.dot_general(kc_ref[:, ksl], qm, contract_lanes, preferred_element_type=F32)
        return sp, sc

    def head_out(h, p, par, sp, sc):
        sp = jnp.where(valid_prev, sp, -jnp.inf)
        sc = jnp.where(valid_cur, sc, -jnp.inf)
        sink = sink_ref[h * ATTN_GROUP + 2 * p + par] * LOG2E
        mx = jnp.maximum(jnp.max(jnp.maximum(sp, sc), axis=0, keepdims=True), sink)
        pp = jnp.exp2(sp - mx)
        pc = jnp.exp2(sc - mx)
        denom = jnp.sum(pp + pc, axis=0, keepdims=True) + jnp.exp2(sink - mx)
        vsl = slice(h * ATTN_HEAD_DIM, (h + 1) * ATTN_HEAD_DIM)
        ot = (jnp.dot(vtp_ref[vsl, :], pp.astype(BF16), preferred_element_type=F32)
              + jnp.dot(vtc_ref[vsl, :], pc.astype(BF16), preferred_element_type=F32))
        return ot * (1.0 / denom)

    pending = [scores(*u) for u in units[:ATTN_LOOKAHEAD]]
    outs = []
    for idx, (h, p, par) in enumerate(units):
        outs.append(head_out(h, p, par, *pending[idx]))
        if idx + ATTN_LOOKAHEAD < len(units):
            pending.append(scores(*units[idx + ATTN_LOOKAHEAD]))
        if par == 1:
            c0 = h * ATTN_GROUP * ATTN_HEAD_DIM + p * LANES
            pair = jnp.concatenate(outs[-2:], axis=0)
            o_ref[:, c0:c0 + LANES] = pair.T.astype(BF16)


def _attention(sinks, q, kk, vt, batch, seq):
    n = q.shape[0]
    nb = seq // WINDOW
    cur = lambda b, j: b * nb + j
    prev = lambda b, j: b * nb + jnp.maximum(j - 1, 0)
    return pl.pallas_call(
        _attn_kernel,
        grid=(batch, nb),
        in_specs=[
            pl.BlockSpec(memory_space=pltpu.SMEM),
            pl.BlockSpec((WINDOW, 1024), lambda b, j: (cur(b, j), 0)),
            pl.BlockSpec((WINDOW, 2 * SZ_AK), lambda b, j: (cur(b, j), 0)),
            pl.BlockSpec((WINDOW, 2 * SZ_AK), lambda b, j: (prev(b, j), 0)),
            pl.BlockSpec((SZ_AV, WINDOW), lambda b, j: (0, cur(b, j))),
            pl.BlockSpec((SZ_AV, WINDOW), lambda b, j: (0, prev(b, j))),
        ],
        out_specs=pl.BlockSpec((WINDOW, 1024), lambda b, j: (cur(b, j), 0)),
        out_shape=jax.ShapeDtypeStruct((n, 1024), BF16),
        compiler_params=pltpu.CompilerParams(
            dimension_semantics=("arbitrary", "arbitrary"), vmem_limit_bytes=VMEM_LIMIT_BYTES),
        name="attention",
    )(sinks, q, kk, kk, vt, vt)


def _log_sigmoid(x):
    return -(jnp.maximum(-x, 0.0) + jnp.log(1.0 + jnp.exp(-jnp.abs(x))))


def _mlstm_gate_kernel(g_ref, rows_ref, cols_ref):
    L = MLSTM_CHUNK
    nh = MLSTM_HEADS
    seq = g_ref.shape[1]
    g = g_ref[...]
    ig = GATE_SOFTCAP * jnp.tanh(g[0:nh] / GATE_SOFTCAP)
    fg = GATE_SOFTCAP * jnp.tanh(g[nh:2 * nh] / GATE_SOFTCAP)
    lf = _log_sigmoid(fg)
    pos = lax.broadcasted_iota(jnp.int32, (nh, seq), 1) & (L - 1)
    a = lf
    sh = 1
    while sh < L:
        a = a + jnp.where(pos >= sh, pltpu.roll(a, sh, 1), 0.0)
        sh *= 2
    bvec = ig - a
    pm = bvec
    sh = 1
    while sh < L:
        pm = jnp.maximum(pm, jnp.where(pos >= sh, pltpu.roll(pm, sh, 1), -jnp.inf))
        sh *= 2
    m = jnp.zeros((nh, L), F32)
    pad = jnp.zeros((L - nh, L), F32)
    for c in range(seq // L):
        sl = slice(c * L, (c + 1) * L)
        a_c, b_c = a[:, sl], bvec[:, sl]
        mrow = jnp.maximum(m, pm[:, sl])
        total = jnp.broadcast_to(a_c[:, L - 1:L], (nh, L))
        mlast = jnp.broadcast_to(mrow[:, L - 1:L], (nh, L))
        rows_ref[ROW_M2:ROW_M2 + nh, sl] = mrow * LOG2E
        rows_ref[ROW_WINTER:ROW_WINTER + nh, sl] = jnp.exp(m - mrow)
        rows_ref[ROW_ENEG:ROW_ENEG + nh, sl] = jnp.exp(-(a_c + mrow))
        rows_ref[ROW_WK:ROW_WK + nh, sl] = jnp.exp(b_c - mlast)
        rows_ref[ROW_DECAY:ROW_DECAY + nh, sl] = jnp.exp(m - mlast)
        m = total + mlast
        cols_ref[sl, :] = jnp.concatenate([b_c * LOG2E, pad], axis=0).T


def _mlstm_gates(gt, batch, seq):
    n = gt.shape[1]
    return pl.pallas_call(
        _mlstm_gate_kernel,
        grid=(batch,),
        in_specs=[pl.BlockSpec((2 * MLSTM_HEADS, seq), lambda b: (0, b))],
        out_specs=[pl.BlockSpec((GATE_ROWS, seq), lambda b: (0, b)),
                   pl.BlockSpec((seq, LANES), lambda b: (b, 0))],
        out_shape=[jax.ShapeDtypeStruct((GATE_ROWS, n), F32),
                   jax.ShapeDtypeStruct((n, LANES), F32)],
        compiler_params=pltpu.CompilerParams(
            dimension_semantics=("arbitrary",), vmem_limit_bytes=VMEM_LIMIT_BYTES),
        name="mlstm_gates",
    )(gt)


def _mlstm_kernel(rows_ref, cols_ref, qk_ref, vt_ref, og_ref, hn_ref, o_ref, ct_scr, n_scr):
    L = MLSTM_CHUNK
    nh = MLSTM_HEADS
    contract_lanes = (((1,), (1,)), ((), ()))

    @pl.when(pl.program_id(1) == 0)
    def _():
        ct_scr[...] = jnp.zeros_like(ct_scr)
        n_scr[...] = jnp.zeros_like(n_scr)

    row_i = lax.broadcasted_iota(jnp.int32, (L, L), 0)
    col_i = lax.broadcasted_iota(jnp.int32, (L, L), 1)
    causal_t = row_i <= col_i
    lo_lane = col_i < MLSTM_QK_DIM
    zero = jnp.zeros((), BF16)

    for ci in range(qk_ref.shape[0] // L):
        rs = slice(ci * L, (ci + 1) * L)
        m2 = rows_ref[ROW_M2:ROW_M2 + nh, rs]
        w_inter = rows_ref[ROW_WINTER:ROW_WINTER + nh, rs]
        e_neg = rows_ref[ROW_ENEG:ROW_ENEG + nh, rs]
        wk = rows_ref[ROW_WK:ROW_WK + nh, rs]
        decay = rows_ref[ROW_DECAY:ROW_DECAY + nh, rs]
        b2cols = cols_ref[rs, :]
        wk_b = jnp.concatenate([wk, wk], axis=0).astype(BF16)

        for p in range(nh // 2):
            h0, h1 = 2 * p, 2 * p + 1
            q2 = qk_ref[rs, p * LANES:(p + 1) * LANES]
            k2 = qk_ref[rs, SZ_MQ + p * LANES:SZ_MQ + (p + 1) * LANES]
            q_par = (jnp.where(lo_lane, q2, zero), jnp.where(lo_lane, zero, q2))
            ct2 = ct_scr[p]
            ct2b = ct2.astype(BF16)
            n2 = n_scr[p]
            n2b = jnp.broadcast_to(n2, (2 * SUBLANES, LANES)).astype(BF16)
            upd = []
            for par, hh in ((0, h0), (1, h1)):
                vsl = slice(hh * MLSTM_V_DIM, (hh + 1) * MLSTM_V_DIM)
                vt = vt_ref[vsl, rs]
                qm = q_par[par]
                st = lax.dot_general(k2, qm, contract_lanes, preferred_element_type=F32)
                dt = jnp.where(causal_t, jnp.exp2(b2cols[:, hh:hh + 1] - m2[hh:hh + 1, :]), 0.0)
                sct = st * dt
                wi = w_inter[hh:hh + 1, :]
                qn = lax.dot_general(n2b, qm, contract_lanes, preferred_element_type=F32)[0:1, :]
                den = jnp.sum(sct, axis=0, keepdims=True) + wi * qn
                numt = (jnp.dot(vt, sct.astype(BF16), preferred_element_type=F32)
                        + wi * lax.dot_general(ct2b, qm, contract_lanes,
                                               preferred_element_type=F32))
                rden = 1.0 / jnp.maximum(jnp.abs(den), e_neg[hh:hh + 1, :])
                ms = jnp.mean(numt * numt, axis=0, keepdims=True)
                scale = rden * lax.rsqrt(rden * rden * ms + NORM_EPS)
                hv = (numt * scale).T
                gate = jax.nn.sigmoid(og_ref[rs, vsl].astype(F32))
                o_ref[rs, vsl] = (hv * hn_ref[:, vsl] * gate).astype(BF16)
                vw = (vt.astype(F32) * wk[hh:hh + 1, :]).astype(BF16)
                upd.append(jnp.dot(vw, k2, preferred_element_type=F32))
            dec_lanes = jnp.where(lo_lane[0:1, :], decay[h0:h0 + 1, :], decay[h1:h1 + 1, :])
            ct_scr[p] = dec_lanes * ct2 + jnp.where(lo_lane, upd[0], upd[1])
            n_inc = jnp.dot(wk_b, k2, preferred_element_type=F32)
            n_scr[p] = dec_lanes * n2 + jnp.where(lo_lane[0:1, :], n_inc[h0:h0 + 1, :],
                                                  n_inc[h1:h1 + 1, :])


def _mlstm(rows, cols, qk, v, og, hn, batch, seq):
    n = qk.shape[0]
    tm = TM_MLSTM
    nt = seq // tm
    row = pl.BlockSpec((tm, 1024), lambda b, c: (b * nt + c, 0))
    return pl.pallas_call(
        _mlstm_kernel,
        grid=(batch, nt),
        in_specs=[
            pl.BlockSpec((GATE_ROWS, tm), lambda b, c: (0, b * nt + c)),
            pl.BlockSpec((tm, LANES), lambda b, c: (b * nt + c, 0)),
            row,
            pl.BlockSpec((SZ_MV, tm), lambda b, c: (0, b * nt + c)),
            row,
            _const_spec((1, 1024)),
        ],
        out_specs=row,
        out_shape=jax.ShapeDtypeStruct((n, 1024), BF16),
        scratch_shapes=[
            pltpu.VMEM((MLSTM_HEADS // 2, 2 * MLSTM_QK_DIM, MLSTM_V_DIM), F32),
            pltpu.VMEM((MLSTM_HEADS // 2, 1, 2 * MLSTM_QK_DIM), F32),
        ],
        compiler_params=pltpu.CompilerParams(
            dimension_semantics=("arbitrary", "arbitrary"), vmem_limit_bytes=VMEM_LIMIT_BYTES),
        name="mlstm",
    )(rows, cols, qk, v, og, hn)


def _merge_kernel(x_ref, a_ref, hm_ref, gate_ref, wpa_ref, wpm_ref, wo_ref, o_ref):
    ya = jnp.dot(a_ref[...], wpa_ref[...], preferred_element_type=F32)
    ym = jnp.dot(hm_ref[...], wpm_ref[...], preferred_element_type=F32)
    ga = jax.nn.sigmoid(gate_ref[:, 0:D_MODEL].astype(F32))
    gm = jax.nn.sigmoid(gate_ref[:, D_MODEL:2 * D_MODEL].astype(F32))
    merged = (ga * ya + gm * ym).astype(BF16)
    o_ref[...] = x_ref[...] + jnp.dot(merged, wo_ref[...], preferred_element_type=F32)


def _merge(x, a, hm, gates, wpa, wpm, wo):
    n = x.shape[0]
    tm = TM_MERGE
    row = lambda width: pl.BlockSpec((tm, width), lambda i: (i, 0))
    return pl.pallas_call(
        _merge_kernel,
        grid=(n // tm,),
        in_specs=[row(D_MODEL), row(1024), row(1024), row(2048),
                  _const_spec((1024, D_MODEL)), _const_spec((1024, D_MODEL)),
                  _const_spec((D_MODEL, D_MODEL))],
        out_specs=row(D_MODEL),
        out_shape=jax.ShapeDtypeStruct((n, D_MODEL), F32),
        compiler_params=pltpu.CompilerParams(
            dimension_semantics=("arbitrary",), vmem_limit_bytes=VMEM_LIMIT_BYTES),
        name="merge",
    )(x, a, hm, gates, wpa, wpm, wo)


def _pack_inproj(w_in, b_in):
    o = 0
    parts = {}
    for name, size in (("aq", SZ_AQ), ("ak", SZ_AK), ("av", SZ_AV), ("mq", SZ_MQ), ("mk", SZ_MK),
                       ("mv", SZ_MV), ("mo", SZ_MO), ("mi", SZ_MI), ("mf", SZ_MF),
                       ("g", 2 * D_MODEL)):
        parts[name] = (w_in[:, o:o + size], b_in[o:o + size])
        o += size

    def dup(t):
        lead = t.shape[:-1]
        t = t.reshape(lead + (ATTN_KV_HEADS, 1, ATTN_HEAD_DIM))
        t = jnp.broadcast_to(t, lead + (ATTN_KV_HEADS, 2, ATTN_HEAD_DIM))
        return t.reshape(lead + (2 * SZ_AK,))

    order = [parts["aq"], tuple(dup(t) for t in parts["ak"]),
             parts["mq"], parts["mk"], parts["mo"], parts["g"]]
    w = jnp.concatenate([t[0] for t in order], axis=1).astype(BF16)
    b = jnp.concatenate([t[1] for t in order], axis=0).reshape(1, PACKED_WIDTH)
    wif = jnp.concatenate([parts["mi"][0], parts["mf"][0]], axis=1).T.astype(BF16)
    bif = jnp.concatenate([parts["mi"][1], parts["mf"][1]], axis=0).reshape(2 * MLSTM_HEADS, 1)
    wvt = jnp.concatenate([parts["mv"][0], parts["av"][0]], axis=1).T.astype(BF16)
    bvt = jnp.concatenate([parts["mv"][1], parts["av"][1]], axis=0).reshape(SZ_MV + SZ_AV, 1)
    return w, b, wif, bif, wvt, bvt


def kernel(x, ffn1_norm, ffn1_w_gate, ffn1_w_up, ffn1_w_down, mix_norm, w_in, b_in, attn_sinks,
           mlstm_conv, mlstm_head_norm, w_proj_attn, w_proj_mlstm, w_out, ffn2_norm, ffn2_w_gate,
           ffn2_w_up, ffn2_w_down, final_norm):
    batch, seq, d = x.shape
    assert d == D_MODEL and seq % TM_INPROJ == 0 and (batch * seq) % TM_FFN == 0
    assert ffn1_norm.shape[0] == 1, "one layer"
    n = batch * seq
    xf = x.reshape(n, d)
    fin = final_norm.reshape(1, d)

    x1 = _ffn(xf, ffn1_norm[0].reshape(1, d), ffn1_w_gate[0].astype(BF16), ffn1_w_up[0].astype(BF16),
              ffn1_w_down[0].astype(BF16), fin, apply_final=False)

    w, b, wif, bif, wvt, bvt = _pack_inproj(w_in[0], b_in[0])
    q, kk, avt, mqk, mvt, mo, gates, gt = _inproj(x1, mix_norm[0].reshape(1, d), w, b, wif, bif,
                                                  wvt, bvt, mlstm_conv[0], seq)

    ya = _attention(attn_sinks[0], q, kk, avt, batch, seq)

    rows, cols = _mlstm_gates(gt, batch, seq)
    hm = _mlstm(rows, cols, mqk, mvt, mo, mlstm_head_norm[0].reshape(1, SZ_MV), batch, seq)

    x2 = _merge(x1, ya, hm, gates, w_proj_attn[0].astype(BF16), w_proj_mlstm[0].astype(BF16),
                w_out[0].astype(BF16))

    out = _ffn(x2, ffn2_norm[0].reshape(1, d), ffn2_w_gate[0].astype(BF16), ffn2_w_up[0].astype(BF16),
               ffn2_w_down[0].astype(BF16), fin, apply_final=True)
    return out.reshape(batch, seq, d)
```

```python
import functools

import jax
import jax.numpy as jnp
from jax import lax
from jax.experimental import pallas as pl
from jax.experimental.pallas import tpu as pltpu

F32 = jnp.float32
BF16 = jnp.bfloat16

D_MODEL = 1024
ATTN_HEAD_DIM = 64
ATTN_HEADS = 16
ATTN_KV_HEADS = 4
ATTN_GROUP = 4
WINDOW = 128
MLSTM_HEADS = 8
MLSTM_V_DIM = 128
MLSTM_QK_DIM = 64
CONV_WIDTH = 4
GATE_SOFTCAP = 15.0
D_FF = 2816
FFN_RESIDUAL_WEIGHT = 0.5
NORM_EPS = 1e-6

SZ_AQ, SZ_AK, SZ_AV = 1024, 256, 256
SZ_MQ, SZ_MK, SZ_MV, SZ_MO = 512, 512, 1024, 1024
SZ_MI, SZ_MF = 8, 8

LANES = 128
SUBLANES = 8
MXU_DIM = 256
VMEM_LIMIT_BYTES = 56 * 1024 * 1024

MLSTM_CHUNK = 128
FF_CHUNK = MXU_DIM
TM_FFN = 1024
TM_INPROJ = 512
TM_MERGE = 1024
TM_MLSTM = 4 * MLSTM_CHUNK
ATTN_LOOKAHEAD = 8
ATTN_BLOCKS = 2
MLSTM_LOOKAHEAD = 8

LOG2E = 1.4426950408889634
ROW_M2, ROW_WINTER, ROW_ENEG, ROW_WK, ROW_DECAY = 0, 8, 16, 24, 32
GATE_ROWS = 40

COL_Q = 0
COL_KK = 1024
COL_MQK = 1536
COL_MO = 2560
COL_G = 3584
PACKED_WIDTH = 5632


def _rms(x, g):
    return x * lax.rsqrt(jnp.mean(x * x, axis=-1, keepdims=True) + NORM_EPS) * g


def _const_spec(shape):
    zeros = (0,) * len(shape)
    return pl.BlockSpec(shape, lambda *_: zeros, pipeline_mode=pl.Buffered(1))


def _ffn_kernel(x_ref, g_ref, wg_ref, wu_ref, wd_ref, fin_ref, o_ref, acc_ref, *, apply_final):
    x = x_ref[...]
    h = _rms(x, g_ref[...]).astype(BF16)
    for c in range(D_FF // FF_CHUNK):
        sl = slice(c * FF_CHUNK, (c + 1) * FF_CHUNK)
        a = jnp.dot(h, wg_ref[:, sl], preferred_element_type=F32)
        u = jnp.dot(h, wu_ref[:, sl], preferred_element_type=F32)
        act = (a * jax.nn.sigmoid(a) * u).astype(BF16)
        d = jnp.dot(act, wd_ref[sl, :], preferred_element_type=F32)
        if c == 0:
            acc_ref[...] = d
        else:
            acc_ref[...] += d
    y = x + FFN_RESIDUAL_WEIGHT * acc_ref[...]
    if apply_final:
        y = _rms(y, fin_ref[...])
    o_ref[...] = y


def _ffn(x, g, wg, wu, wd, fin, apply_final):
    n = x.shape[0]
    tm = TM_FFN
    return pl.pallas_call(
        functools.partial(_ffn_kernel, apply_final=apply_final),
        grid=(n // tm,),
        in_specs=[
            pl.BlockSpec((tm, D_MODEL), lambda i: (i, 0)),
            _const_spec((1, D_MODEL)),
            _const_spec((D_MODEL, D_FF)),
            _const_spec((D_MODEL, D_FF)),
            _const_spec((D_FF, D_MODEL)),
            _const_spec((1, D_MODEL)),
        ],
        out_specs=pl.BlockSpec((tm, D_MODEL), lambda i: (i, 0)),
        out_shape=jax.ShapeDtypeStruct((n, D_MODEL), F32),
        scratch_shapes=[pltpu.VMEM((tm, D_MODEL), F32)],
        compiler_params=pltpu.CompilerParams(
            dimension_semantics=("arbitrary",), vmem_limit_bytes=VMEM_LIMIT_BYTES),
        name="ffn_final" if apply_final else "ffn",
    )(x, g, wg, wu, wd, fin)


def _inproj_kernel(x_ref, g_ref, w_ref, b_ref, wif_ref, bif_ref, wvt_ref, bvt_ref, conv_ref,
                   q_ref, kk_ref, avt_ref, mqk_ref, mvt_ref, mo_ref, gate_ref, gt_ref,
                   conv_scr, *, tiles_per_seq):
    i = pl.program_id(0)
    tm = x_ref.shape[0]
    h = _rms(x_ref[...], g_ref[...]).astype(BF16)

    def seg(c0, width):
        return (jnp.dot(h, w_ref[:, c0:c0 + width], preferred_element_type=F32)
                + b_ref[:, c0:c0 + width])

    def seg_t(wt_ref, bt_ref, r0, rows):
        return (lax.dot_general(wt_ref[r0:r0 + rows, :], h, (((1,), (1,)), ((), ())),
                                preferred_element_type=F32) + bt_ref[r0:r0 + rows, :])

    cw = 512

    @pl.when(i == 0)
    def _():
        conv_scr[...] = jnp.zeros_like(conv_scr)

    carry = jnp.where(i % tiles_per_seq == 0, 0.0, conv_scr[...])

    def conv_slab(c, width):
        z = seg(COL_MQK + c, width)
        conv_scr[:, c:c + width] = z[tm - SUBLANES:tm, :]
        zc = jnp.concatenate([carry[:, c:c + width], z], axis=0)
        acc = z * conv_ref[CONV_WIDTH - 1:CONV_WIDTH, c:c + width]
        for k in range(1, CONV_WIDTH):
            zk = pltpu.roll(zc, k, 0)[SUBLANES:, :]
            acc = acc + zk * conv_ref[CONV_WIDTH - 1 - k:CONV_WIDTH - k, c:c + width]
        y = acc * jax.nn.sigmoid(acc)
        if c >= SZ_MQ:
            y = y * (MLSTM_QK_DIM ** -0.5)
        mqk_ref[:, c:c + width] = y.astype(BF16)

    dw = MXU_DIM
    for k in range(1024 // dw):
        sl = slice(k * dw, (k + 1) * dw)
        conv_slab(k * dw, dw)
        q_ref[:, sl] = (seg(COL_Q + k * dw, dw) * (LOG2E * ATTN_HEAD_DIM ** -0.5)).astype(BF16)
        if k < 2 * SZ_AK // dw:
            kk_ref[:, sl] = seg(COL_KK + k * dw, dw).astype(BF16)
        elif k == 2 * SZ_AK // dw:
            avt_ref[...] = seg_t(wvt_ref, bvt_ref, SZ_MV, SZ_AV).astype(BF16)
        mvt_ref[sl, :] = seg_t(wvt_ref, bvt_ref, k * dw, dw).astype(BF16)
        mo_ref[:, sl] = seg(COL_MO + k * dw, dw).astype(BF16)
        for c in (2 * k * dw, (2 * k + 1) * dw):
            gate_ref[:, c:c + dw] = seg(COL_G + c, dw).astype(BF16)

    gt_ref[...] = seg_t(wif_ref, bif_ref, 0, 2 * MLSTM_HEADS)


def _inproj(x, g, w, b, wif, bif, wvt, bvt, conv, seq):
    n = x.shape[0]
    tm = TM_INPROJ
    row = lambda width: pl.BlockSpec((tm, width), lambda i: (i, 0))
    col = lambda height: pl.BlockSpec((height, tm), lambda i: (0, i))
    bf = lambda width: jax.ShapeDtypeStruct((n, width), BF16)
    return pl.pallas_call(
        functools.partial(_inproj_kernel, tiles_per_seq=seq // tm),
        grid=(n // tm,),
        in_specs=[
            row(D_MODEL),
            _const_spec((1, D_MODEL)),
            _const_spec((D_MODEL, PACKED_WIDTH)),
            _const_spec((1, PACKED_WIDTH)),
            _const_spec((2 * MLSTM_HEADS, D_MODEL)),
            _const_spec((2 * MLSTM_HEADS, 1)),
            _const_spec((SZ_MV + SZ_AV, D_MODEL)),
            _const_spec((SZ_MV + SZ_AV, 1)),
            _const_spec((CONV_WIDTH, 1024)),
        ],
        out_specs=[row(1024), row(2 * SZ_AK), col(SZ_AV), row(1024), col(SZ_MV), row(1024),
                   row(2048), col(2 * MLSTM_HEADS)],
        out_shape=[bf(1024), bf(2 * SZ_AK), jax.ShapeDtypeStruct((SZ_AV, n), BF16), bf(1024),
                   jax.ShapeDtypeStruct((SZ_MV, n), BF16), bf(1024), bf(2048),
                   jax.ShapeDtypeStruct((2 * MLSTM_HEADS, n), F32)],
        scratch_shapes=[pltpu.VMEM((SUBLANES, 1024), F32)],
        compiler_params=pltpu.CompilerParams(
            dimension_semantics=("arbitrary",), vmem_limit_bytes=VMEM_LIMIT_BYTES),
        name="inproj",
    )(x, g, w, b, wif, bif, wvt, bvt, conv)


def _attn_kernel(sink_ref, q_ref, kc_ref, kp_ref, vtc_ref, vtp_ref, o_ref):
    w = WINDOW
    contract_lanes = (((1,), (1,)), ((), ()))
    kj = lax.broadcasted_iota(jnp.int32, (w, w), 0)
    qi = lax.broadcasted_iota(jnp.int32, (w, w), 1)
    above = kj > qi
    valid_cur = kj <= qi
    lo = lax.broadcasted_iota(jnp.int32, (w, LANES), 1) < ATTN_HEAD_DIM
    zero = jnp.zeros((), BF16)
    units = [(jb, h, p, par) for jb in range(q_ref.shape[0] // w) for h in range(ATTN_KV_HEADS)
             for p in range(ATTN_GROUP // 2) for par in range(2)]

    def blk(jb):
        return slice(jb * w, (jb + 1) * w)

    def scores(jb, h, p, par):
        c0 = h * ATTN_GROUP * ATTN_HEAD_DIM + p * LANES
        q2 = q_ref[blk(jb), c0:c0 + LANES]
        qm = jnp.where(lo, q2, zero) if par == 0 else jnp.where(lo, zero, q2)
        ksl = slice(h * LANES, (h + 1) * LANES)
        kp = kp_ref[:, ksl] if jb == 0 else kc_ref[blk(jb - 1), ksl]
        sp = lax.dot_general(kp, qm, contract_lanes, preferred_element_type=F32)
        sc = lax.dot_general(kc_ref[blk(jb), ksl], qm, contract_lanes, preferred_element_type=F32)
        return sp, sc

    def head_out(jb, h, p, par, sp, sc):
        valid_prev = (above & (pl.program_id(1) > 0)) if jb == 0 else above
        sp = jnp.where(valid_prev, sp, -jnp.inf)
        sc = jnp.where(valid_cur, sc, -jnp.inf)
        sink = sink_ref[h * ATTN_GROUP + 2 * p + par] * LOG2E
        mx = jnp.maximum(jnp.max(jnp.maximum(sp, sc), axis=0, keepdims=True), sink)
        pp = jnp.exp2(sp - mx)
        pc = jnp.exp2(sc - mx)
        denom = jnp.sum(pp + pc, axis=0, keepdims=True) + jnp.exp2(sink - mx)
        vsl = slice(h * ATTN_HEAD_DIM, (h + 1) * ATTN_HEAD_DIM)
        vtp = vtp_ref[vsl, :] if jb == 0 else vtc_ref[vsl, blk(jb - 1)]
        ot = (jnp.dot(vtp, pp.astype(BF16), preferred_element_type=F32)
              + jnp.dot(vtc_ref[vsl, blk(jb)], pc.astype(BF16), preferred_element_type=F32))
        return ot * (1.0 / denom)

    pending = [scores(*u) for u in units[:ATTN_LOOKAHEAD]]
    outs = []
    for idx, (jb, h, p, par) in enumerate(units):
        outs.append(head_out(jb, h, p, par, *pending[idx]))
        pending[idx] = None
        if idx + ATTN_LOOKAHEAD < len(units):
            pending.append(scores(*units[idx + ATTN_LOOKAHEAD]))
        if par == 1:
            c0 = h * ATTN_GROUP * ATTN_HEAD_DIM + p * LANES
            pair = jnp.concatenate(outs, axis=0)
            o_ref[blk(jb), c0:c0 + LANES] = pair.T.astype(BF16)
            outs.clear()


def _attention(sinks, q, kk, vt, batch, seq):
    n = q.shape[0]
    tq = ATTN_BLOCKS * WINDOW
    nt = seq // tq
    cur = lambda b, j: b * nt + j
    prev = lambda b, j: (b * nt + j) * ATTN_BLOCKS - jnp.minimum(j, 1)
    return pl.pallas_call(
        _attn_kernel,
        grid=(batch, nt),
        in_specs=[
            pl.BlockSpec(memory_space=pltpu.SMEM),
            pl.BlockSpec((tq, 1024), lambda b, j: (cur(b, j), 0)),
            pl.BlockSpec((tq, 2 * SZ_AK), lambda b, j: (cur(b, j), 0)),
            pl.BlockSpec((WINDOW, 2 * SZ_AK), lambda b, j: (prev(b, j), 0)),
            pl.BlockSpec((SZ_AV, tq), lambda b, j: (0, cur(b, j))),
            pl.BlockSpec((SZ_AV, WINDOW), lambda b, j: (0, prev(b, j))),
        ],
        out_specs=pl.BlockSpec((tq, 1024), lambda b, j: (cur(b, j), 0)),
        out_shape=jax.ShapeDtypeStruct((n, 1024), BF16),
        compiler_params=pltpu.CompilerParams(
            dimension_semantics=("arbitrary", "arbitrary"), vmem_limit_bytes=VMEM_LIMIT_BYTES),
        name="attention",
    )(sinks, q, kk, kk, vt, vt)


def _log_sigmoid(x):
    return -(jnp.maximum(-x, 0.0) + jnp.log(1.0 + jnp.exp(-jnp.abs(x))))


def _mlstm_gate_kernel(g_ref, rows_ref, cols_ref):
    L = MLSTM_CHUNK
    nh = MLSTM_HEADS
    seq = g_ref.shape[1]
    g = g_ref[...]
    ig = GATE_SOFTCAP * jnp.tanh(g[0:nh] / GATE_SOFTCAP)
    fg = GATE_SOFTCAP * jnp.tanh(g[nh:2 * nh] / GATE_SOFTCAP)
    lf = _log_sigmoid(fg)
    pos = lax.broadcasted_iota(jnp.int32, (nh, seq), 1) & (L - 1)
    a = lf
    sh = 1
    while sh < L:
        a = a + jnp.where(pos >= sh, pltpu.roll(a, sh, 1), 0.0)
        sh *= 2
    bvec = ig - a
    pm = bvec
    sh = 1
    while sh < L:
        pm = jnp.maximum(pm, jnp.where(pos >= sh, pltpu.roll(pm, sh, 1), -jnp.inf))
        sh *= 2
    m = jnp.zeros((nh, L), F32)
    pad = jnp.zeros((L - nh, L), F32)
    for c in range(seq // L):
        sl = slice(c * L, (c + 1) * L)
        a_c, b_c = a[:, sl], bvec[:, sl]
        mrow = jnp.maximum(m, pm[:, sl])
        total = jnp.broadcast_to(a_c[:, L - 1:L], (nh, L))
        mlast = jnp.broadcast_to(mrow[:, L - 1:L], (nh, L))
        rows_ref[ROW_M2:ROW_M2 + nh, sl] = mrow * LOG2E
        rows_ref[ROW_WINTER:ROW_WINTER + nh, sl] = jnp.exp(m - mrow)
        rows_ref[ROW_ENEG:ROW_ENEG + nh, sl] = jnp.exp(-(a_c + mrow))
        rows_ref[ROW_WK:ROW_WK + nh, sl] = jnp.exp(b_c - mlast)
        rows_ref[ROW_DECAY:ROW_DECAY + nh, sl] = jnp.exp(m - mlast)
        m = total + mlast
        cols_ref[sl, :] = jnp.concatenate([b_c * LOG2E, pad], axis=0).T


def _mlstm_gates(gt, batch, seq):
    n = gt.shape[1]
    return pl.pallas_call(
        _mlstm_gate_kernel,
        grid=(batch,),
        in_specs=[pl.BlockSpec((2 * MLSTM_HEADS, seq), lambda b: (0, b))],
        out_specs=[pl.BlockSpec((GATE_ROWS, seq), lambda b: (0, b)),
                   pl.BlockSpec((seq, LANES), lambda b: (b, 0))],
        out_shape=[jax.ShapeDtypeStruct((GATE_ROWS, n), F32),
                   jax.ShapeDtypeStruct((n, LANES), F32)],
        compiler_params=pltpu.CompilerParams(
            dimension_semantics=("arbitrary",), vmem_limit_bytes=VMEM_LIMIT_BYTES),
        name="mlstm_gates",
    )(gt)


def _mlstm_kernel(rows_ref, cols_ref, qk_ref, vt_ref, og_ref, hn_ref, o_ref, ct_scr, n_scr):
    L = MLSTM_CHUNK
    nh = MLSTM_HEADS
    contract_lanes = (((1,), (1,)), ((), ()))

    @pl.when(pl.program_id(1) == 0)
    def _():
        ct_scr[...] = jnp.zeros_like(ct_scr)
        n_scr[...] = jnp.zeros_like(n_scr)

    row_i = lax.broadcasted_iota(jnp.int32, (L, L), 0)
    col_i = lax.broadcasted_iota(jnp.int32, (L, L), 1)
    causal_t = row_i <= col_i
    lo_lane = col_i < MLSTM_QK_DIM
    zero = jnp.zeros((), BF16)

    n_chunks = qk_ref.shape[0] // L
    units = [(ci, p, par) for ci in range(n_chunks) for p in range(nh // 2) for par in range(2)]
    ct = [ct_scr[p] for p in range(nh // 2)]
    nn = [n_scr[p] for p in range(nh // 2)]
    gate_rows = {}
    upd = {}

    def chunk_gates(ci):
        if ci not in gate_rows:
            rs = slice(ci * L, (ci + 1) * L)
            wk = rows_ref[ROW_WK:ROW_WK + nh, rs]
            gate_rows[ci] = dict(
                m2=rows_ref[ROW_M2:ROW_M2 + nh, rs], w_inter=rows_ref[ROW_WINTER:ROW_WINTER + nh, rs],
                e_neg=rows_ref[ROW_ENEG:ROW_ENEG + nh, rs], wk=wk,
                decay=rows_ref[ROW_DECAY:ROW_DECAY + nh, rs], b2cols=cols_ref[rs, :],
                wk_b=jnp.concatenate([wk, wk], axis=0).astype(BF16))
        return gate_rows[ci]

    def front(ci, p, par):
        g = chunk_gates(ci)
        rs = slice(ci * L, (ci + 1) * L)
        hh = 2 * p + par
        q2 = qk_ref[rs, p * LANES:(p + 1) * LANES]
        k2 = qk_ref[rs, SZ_MQ + p * LANES:SZ_MQ + (p + 1) * LANES]
        qm = jnp.where(lo_lane, q2, zero) if par == 0 else jnp.where(lo_lane, zero, q2)
        vt = vt_ref[hh * MLSTM_V_DIM:(hh + 1) * MLSTM_V_DIM, rs]
        st = lax.dot_general(k2, qm, contract_lanes, preferred_element_type=F32)
        n2b = jnp.broadcast_to(nn[p], (2 * SUBLANES, LANES)).astype(BF16)
        qn = lax.dot_general(n2b, qm, contract_lanes, preferred_element_type=F32)[0:1, :]
        inter = lax.dot_general(ct[p].astype(BF16), qm, contract_lanes,
                                preferred_element_type=F32)
        vw = (vt.astype(F32) * g["wk"][hh:hh + 1, :]).astype(BF16)
        upd[(ci, p, par)] = jnp.dot(vw, k2, preferred_element_type=F32)
        if par == 1:
            h0, h1 = 2 * p, 2 * p + 1
            dec = jnp.where(lo_lane[0:1, :], g["decay"][h0:h0 + 1, :], g["decay"][h1:h1 + 1, :])
            n_inc = jnp.dot(g["wk_b"], k2, preferred_element_type=F32)
            ct[p] = dec * ct[p] + jnp.where(lo_lane, upd.pop((ci, p, 0)), upd.pop((ci, p, 1)))
            nn[p] = dec * nn[p] + jnp.where(lo_lane[0:1, :], n_inc[h0:h0 + 1, :],
                                            n_inc[h1:h1 + 1, :])
        return st, qn, inter

    def back(ci, p, par, st, qn, inter):
        g = chunk_gates(ci)
        rs = slice(ci * L, (ci + 1) * L)
        hh = 2 * p + par
        vsl = slice(hh * MLSTM_V_DIM, (hh + 1) * MLSTM_V_DIM)
        dt = jnp.where(causal_t,
                       jnp.exp2(g["b2cols"][:, hh:hh + 1] - g["m2"][hh:hh + 1, :]), 0.0)
        sct = st * dt
        wi = g["w_inter"][hh:hh + 1, :]
        den = jnp.sum(sct, axis=0, keepdims=True) + wi * qn
        numt = jnp.dot(vt_ref[vsl, rs], sct.astype(BF16), preferred_element_type=F32) + wi * inter
        rden = 1.0 / jnp.maximum(jnp.abs(den), g["e_neg"][hh:hh + 1, :])
        ms = jnp.mean(numt * numt, axis=0, keepdims=True)
        scale = rden * lax.rsqrt(rden * rden * ms + NORM_EPS)
        hv = (numt * scale).T
        gate = jax.nn.sigmoid(og_ref[rs, vsl].astype(F32))
        o_ref[rs, vsl] = (hv * hn_ref[:, vsl] * gate).astype(BF16)

    pending = [front(*u) for u in units[:MLSTM_LOOKAHEAD]]
    for idx, u in enumerate(units):
        back(*u, *pending[idx])
        pending[idx] = None
        if idx + MLSTM_LOOKAHEAD < len(units):
            pending.append(front(*units[idx + MLSTM_LOOKAHEAD]))
    for p in range(nh // 2):
        ct_scr[p] = ct[p]
        n_scr[p] = nn[p]


def _mlstm(rows, cols, qk, v, og, hn, batch, seq):
    n = qk.shape[0]
    tm = TM_MLSTM
    nt = seq // tm
    row = pl.BlockSpec((tm, 1024), lambda b, c: (b * nt + c, 0))
    return pl.pallas_call(
        _mlstm_kernel,
        grid=(batch, nt),
        in_specs=[
            pl.BlockSpec((GATE_ROWS, tm), lambda b, c: (0, b * nt + c)),
            pl.BlockSpec((tm, LANES), lambda b, c: (b * nt + c, 0)),
            row,
            pl.BlockSpec((SZ_MV, tm), lambda b, c: (0, b * nt + c)),
            row,
            _const_spec((1, 1024)),
        ],
        out_specs=row,
        out_shape=jax.ShapeDtypeStruct((n, 1024), BF16),
        scratch_shapes=[
            pltpu.VMEM((MLSTM_HEADS // 2, 2 * MLSTM_QK_DIM, MLSTM_V_DIM), F32),
            pltpu.VMEM((MLSTM_HEADS // 2, 1, 2 * MLSTM_QK_DIM), F32),
        ],
        compiler_params=pltpu.CompilerParams(
            dimension_semantics=("arbitrary", "arbitrary"), vmem_limit_bytes=VMEM_LIMIT_BYTES),
        name="mlstm",
    )(rows, cols, qk, v, og, hn)


def _merge_kernel(x_ref, a_ref, hm_ref, gate_ref, wpa_ref, wpm_ref, wo_ref, o_ref):
    ya = jnp.dot(a_ref[...], wpa_ref[...], preferred_element_type=F32)
    ym = jnp.dot(hm_ref[...], wpm_ref[...], preferred_element_type=F32)
    ga = jax.nn.sigmoid(gate_ref[:, 0:D_MODEL].astype(F32))
    gm = jax.nn.sigmoid(gate_ref[:, D_MODEL:2 * D_MODEL].astype(F32))
    merged = (ga * ya + gm * ym).astype(BF16)
    o_ref[...] = x_ref[...] + jnp.dot(merged, wo_ref[...], preferred_element_type=F32)


def _merge(x, a, hm, gates, wpa, wpm, wo):
    n = x.shape[0]
    tm = TM_MERGE
    row = lambda width: pl.BlockSpec((tm, width), lambda i: (i, 0))
    return pl.pallas_call(
        _merge_kernel,
        grid=(n // tm,),
        in_specs=[row(D_MODEL), row(1024), row(1024), row(2048),
                  _const_spec((1024, D_MODEL)), _const_spec((1024, D_MODEL)),
                  _const_spec((D_MODEL, D_MODEL))],
        out_specs=row(D_MODEL),
        out_shape=jax.ShapeDtypeStruct((n, D_MODEL), F32),
        compiler_params=pltpu.CompilerParams(
            dimension_semantics=("arbitrary",), vmem_limit_bytes=VMEM_LIMIT_BYTES),
        name="merge",
    )(x, a, hm, gates, wpa, wpm, wo)


def _pack_inproj(w_in, b_in):
    o = 0
    parts = {}
    for name, size in (("aq", SZ_AQ), ("ak", SZ_AK), ("av", SZ_AV), ("mq", SZ_MQ), ("mk", SZ_MK),
                       ("mv", SZ_MV), ("mo", SZ_MO), ("mi", SZ_MI), ("mf", SZ_MF),
                       ("g", 2 * D_MODEL)):
        parts[name] = (w_in[:, o:o + size], b_in[o:o + size])
        o += size

    def dup(t):
        lead = t.shape[:-1]
        t = t.reshape(lead + (ATTN_KV_HEADS, 1, ATTN_HEAD_DIM))
        t = jnp.broadcast_to(t, lead + (ATTN_KV_HEADS, 2, ATTN_HEAD_DIM))
        return t.reshape(lead + (2 * SZ_AK,))

    order = [parts["aq"], tuple(dup(t) for t in parts["ak"]),
             parts["mq"], parts["mk"], parts["mo"], parts["g"]]
    w = jnp.concatenate([t[0] for t in order], axis=1).astype(BF16)
    b = jnp.concatenate([t[1] for t in order], axis=0).reshape(1, PACKED_WIDTH)
    wif = jnp.concatenate([parts["mi"][0], parts["mf"][0]], axis=1).T.astype(BF16)
    bif = jnp.concatenate([parts["mi"][1], parts["mf"][1]], axis=0).reshape(2 * MLSTM_HEADS, 1)
    wvt = jnp.concatenate([parts["mv"][0], parts["av"][0]], axis=1).T.astype(BF16)
    bvt = jnp.concatenate([parts["mv"][1], parts["av"][1]], axis=0).reshape(SZ_MV + SZ_AV, 1)
    return w, b, wif, bif, wvt, bvt


def kernel(x, ffn1_norm, ffn1_w_gate, ffn1_w_up, ffn1_w_down, mix_norm, w_in, b_in, attn_sinks,
           mlstm_conv, mlstm_head_norm, w_proj_attn, w_proj_mlstm, w_out, ffn2_norm, ffn2_w_gate,
           ffn2_w_up, ffn2_w_down, final_norm):
    batch, seq, d = x.shape
    assert d == D_MODEL and seq % TM_INPROJ == 0 and (batch * seq) % TM_FFN == 0
    assert ffn1_norm.shape[0] == 1, "one layer"
    n = batch * seq
    xf = x.reshape(n, d)
    fin = final_norm.reshape(1, d)

    x1 = _ffn(xf, ffn1_norm[0].reshape(1, d), ffn1_w_gate[0].astype(BF16), ffn1_w_up[0].astype(BF16),
              ffn1_w_down[0].astype(BF16), fin, apply_final=False)

    w, b, wif, bif, wvt, bvt = _pack_inproj(w_in[0], b_in[0])
    q, kk, avt, mqk, mvt, mo, gates, gt = _inproj(x1, mix_norm[0].reshape(1, d), w, b, wif, bif,
                                                  wvt, bvt, mlstm_conv[0], seq)

    ya = _attention(attn_sinks[0], q, kk, avt, batch, seq)

    rows, cols = _mlstm_gates(gt, batch, seq)
    hm = _mlstm(rows, cols, mqk, mvt, mo, mlstm_head_norm[0].reshape(1, SZ_MV), batch, seq)

    x2 = _merge(x1, ya, hm, gates, w_proj_attn[0].astype(BF16), w_proj_mlstm[0].astype(BF16),
                w_out[0].astype(BF16))

    out = _ffn(x2, ffn2_norm[0].reshape(1, d), ffn2_w_gate[0].astype(BF16), ffn2_w_up[0].astype(BF16),
               ffn2_w_down[0].astype(BF16), fin, apply_final=True)
    return out.reshape(batch, seq, d)
```

```python
import functools

import jax
import jax.numpy as jnp
from jax import lax
from jax.experimental import pallas as pl
from jax.experimental.pallas import tpu as pltpu

F32 = jnp.float32
BF16 = jnp.bfloat16

D_MODEL = 1024
ATTN_HEAD_DIM = 64
ATTN_HEADS = 16
ATTN_KV_HEADS = 4
ATTN_GROUP = 4
WINDOW = 128
MLSTM_HEADS = 8
MLSTM_V_DIM = 128
MLSTM_QK_DIM = 64
CONV_WIDTH = 4
GATE_SOFTCAP = 15.0
D_FF = 2816
FFN_RESIDUAL_WEIGHT = 0.5
NORM_EPS = 1e-6

SZ_AQ, SZ_AK, SZ_AV = 1024, 256, 256
SZ_MQ, SZ_MK, SZ_MV, SZ_MO = 512, 512, 1024, 1024
SZ_MI, SZ_MF = 8, 8

LANES = 128
SUBLANES = 8
MXU_DIM = 256
VMEM_LIMIT_BYTES = 56 * 1024 * 1024

MLSTM_CHUNK = 128
FF_CHUNK = MXU_DIM
TM_FFN = 1024
TM_INPROJ = 512
TM_MERGE = 1024
TM_MLSTM = 4 * MLSTM_CHUNK
ATTN_LOOKAHEAD = 4
ATTN_BLOCKS = 4
MLSTM_LOOKAHEAD = 8

LOG2E = 1.4426950408889634
ROW_M2, ROW_WINTER, ROW_ENEG, ROW_WK, ROW_DECAY = 0, 8, 16, 24, 32
GATE_ROWS = 40

COL_Q = 0
COL_KK = 1024
COL_MQK = 1536
COL_MO = 2560
COL_G = 3584
PACKED_WIDTH = 5632


def _rms(x, g):
    return x * lax.rsqrt(jnp.mean(x * x, axis=-1, keepdims=True) + NORM_EPS) * g


def _const_spec(shape):
    zeros = (0,) * len(shape)
    return pl.BlockSpec(shape, lambda *_: zeros, pipeline_mode=pl.Buffered(1))


def _ffn_kernel(x_ref, g_ref, wg_ref, wu_ref, wd_ref, fin_ref, o_ref, acc_ref, *, apply_final):
    x = x_ref[...]
    h = _rms(x, g_ref[...]).astype(BF16)
    for c in range(D_FF // FF_CHUNK):
        sl = slice(c * FF_CHUNK, (c + 1) * FF_CHUNK)
        a = jnp.dot(h, wg_ref[:, sl], preferred_element_type=F32)
        u = jnp.dot(h, wu_ref[:, sl], preferred_element_type=F32)
        act = (a * jax.nn.sigmoid(a) * u).astype(BF16)
        d = jnp.dot(act, wd_ref[sl, :], preferred_element_type=F32)
        if c == 0:
            acc_ref[...] = d
        else:
            acc_ref[...] += d
    y = x + FFN_RESIDUAL_WEIGHT * acc_ref[...]
    if apply_final:
        y = _rms(y, fin_ref[...])
    o_ref[...] = y


def _ffn(x, g, wg, wu, wd, fin, apply_final):
    n = x.shape[0]
    tm = TM_FFN
    return pl.pallas_call(
        functools.partial(_ffn_kernel, apply_final=apply_final),
        grid=(n // tm,),
        in_specs=[
            pl.BlockSpec((tm, D_MODEL), lambda i: (i, 0)),
            _const_spec((1, D_MODEL)),
            _const_spec((D_MODEL, D_FF)),
            _const_spec((D_MODEL, D_FF)),
            _const_spec((D_FF, D_MODEL)),
            _const_spec((1, D_MODEL)),
        ],
        out_specs=pl.BlockSpec((tm, D_MODEL), lambda i: (i, 0)),
        out_shape=jax.ShapeDtypeStruct((n, D_MODEL), F32),
        scratch_shapes=[pltpu.VMEM((tm, D_MODEL), F32)],
        compiler_params=pltpu.CompilerParams(
            dimension_semantics=("arbitrary",), vmem_limit_bytes=VMEM_LIMIT_BYTES),
        name="ffn_final" if apply_final else "ffn",
    )(x, g, wg, wu, wd, fin)


def _inproj_kernel(x_ref, g_ref, w_ref, b_ref, wif_ref, bif_ref, wvt_ref, bvt_ref, conv_ref,
                   q_ref, kk_ref, avt_ref, mqk_ref, mvt_ref, mo_ref, gate_ref, gt_ref,
                   conv_scr, *, tiles_per_seq):
    i = pl.program_id(0)
    tm = x_ref.shape[0]
    h = _rms(x_ref[...], g_ref[...]).astype(BF16)

    def seg(c0, width):
        return (jnp.dot(h, w_ref[:, c0:c0 + width], preferred_element_type=F32)
                + b_ref[:, c0:c0 + width])

    def seg_t(wt_ref, bt_ref, r0, rows):
        return (lax.dot_general(wt_ref[r0:r0 + rows, :], h, (((1,), (1,)), ((), ())),
                                preferred_element_type=F32) + bt_ref[r0:r0 + rows, :])

    cw = 512

    @pl.when(i == 0)
    def _():
        conv_scr[...] = jnp.zeros_like(conv_scr)

    carry = jnp.where(i % tiles_per_seq == 0, 0.0, conv_scr[...])

    def conv_slab(c, width):
        z = seg(COL_MQK + c, width)
        conv_scr[:, c:c + width] = z[tm - SUBLANES:tm, :]
        zc = jnp.concatenate([carry[:, c:c + width], z], axis=0)
        acc = z * conv_ref[CONV_WIDTH - 1:CONV_WIDTH, c:c + width]
        for k in range(1, CONV_WIDTH):
            zk = pltpu.roll(zc, k, 0)[SUBLANES:, :]
            acc = acc + zk * conv_ref[CONV_WIDTH - 1 - k:CONV_WIDTH - k, c:c + width]
        y = acc * jax.nn.sigmoid(acc)
        if c >= SZ_MQ:
            y = y * (MLSTM_QK_DIM ** -0.5)
        mqk_ref[:, c:c + width] = y.astype(BF16)

    dw = MXU_DIM
    for k in range(1024 // dw):
        sl = slice(k * dw, (k + 1) * dw)
        conv_slab(k * dw, dw)
        q_ref[:, sl] = (seg(COL_Q + k * dw, dw) * (LOG2E * ATTN_HEAD_DIM ** -0.5)).astype(BF16)
        if k < 2 * SZ_AK // dw:
            kk_ref[:, sl] = seg(COL_KK + k * dw, dw).astype(BF16)
        elif k == 2 * SZ_AK // dw:
            avt_ref[...] = seg_t(wvt_ref, bvt_ref, SZ_MV, SZ_AV).astype(BF16)
        mvt_ref[sl, :] = seg_t(wvt_ref, bvt_ref, k * dw, dw).astype(BF16)
        mo_ref[:, sl] = seg(COL_MO + k * dw, dw).astype(BF16)
        for c in (2 * k * dw, (2 * k + 1) * dw):
            gate_ref[:, c:c + dw] = seg(COL_G + c, dw).astype(BF16)

    gt_ref[...] = seg_t(wif_ref, bif_ref, 0, 2 * MLSTM_HEADS)


def _inproj(x, g, w, b, wif, bif, wvt, bvt, conv, seq):
    n = x.shape[0]
    tm = TM_INPROJ
    row = lambda width: pl.BlockSpec((tm, width), lambda i: (i, 0))
    col = lambda height: pl.BlockSpec((height, tm), lambda i: (0, i))
    bf = lambda width: jax.ShapeDtypeStruct((n, width), BF16)
    return pl.pallas_call(
        functools.partial(_inproj_kernel, tiles_per_seq=seq // tm),
        grid=(n // tm,),
        in_specs=[
            row(D_MODEL),
            _const_spec((1, D_MODEL)),
            _const_spec((D_MODEL, PACKED_WIDTH)),
            _const_spec((1, PACKED_WIDTH)),
            _const_spec((2 * MLSTM_HEADS, D_MODEL)),
            _const_spec((2 * MLSTM_HEADS, 1)),
            _const_spec((SZ_MV + SZ_AV, D_MODEL)),
            _const_spec((SZ_MV + SZ_AV, 1)),
            _const_spec((CONV_WIDTH, 1024)),
        ],
        out_specs=[row(1024), row(2 * SZ_AK), col(SZ_AV), row(1024), col(SZ_MV), row(1024),
                   row(2048), col(2 * MLSTM_HEADS)],
        out_shape=[bf(1024), bf(2 * SZ_AK), jax.ShapeDtypeStruct((SZ_AV, n), BF16), bf(1024),
                   jax.ShapeDtypeStruct((SZ_MV, n), BF16), bf(1024), bf(2048),
                   jax.ShapeDtypeStruct((2 * MLSTM_HEADS, n), F32)],
        scratch_shapes=[pltpu.VMEM((SUBLANES, 1024), F32)],
        compiler_params=pltpu.CompilerParams(
            dimension_semantics=("arbitrary",), vmem_limit_bytes=VMEM_LIMIT_BYTES),
        name="inproj",
    )(x, g, w, b, wif, bif, wvt, bvt, conv)


def _attn_kernel(sink_ref, q_ref, kc_ref, kp_ref, vtc_ref, vtp_ref, o_ref, k_scr, vt_scr):
    w = WINDOW
    contract_lanes = (((1,), (1,)), ((), ()))
    k_scr[0:w, :] = kp_ref[...]
    k_scr[w:, :] = kc_ref[...]
    vt_scr[:, 0:w] = vtp_ref[...]
    vt_scr[:, w:] = vtc_ref[...]
    kj = lax.broadcasted_iota(jnp.int32, (2 * w, w), 0)
    qi = lax.broadcasted_iota(jnp.int32, (2 * w, w), 1)
    band = (kj > qi) & (kj <= qi + w)
    band_first = band & ((kj >= w) | (pl.program_id(1) > 0))
    lo = lax.broadcasted_iota(jnp.int32, (w, LANES), 1) < ATTN_HEAD_DIM
    zero = jnp.zeros((), BF16)
    pairs = [(jb, h, p) for jb in range(q_ref.shape[0] // w) for h in range(ATTN_KV_HEADS)
             for p in range(ATTN_GROUP // 2)]

    def scores(jb, h, p):
        c0 = h * ATTN_GROUP * ATTN_HEAD_DIM + p * LANES
        q2 = q_ref[jb * w:(jb + 1) * w, c0:c0 + LANES]
        qm2 = jnp.concatenate([jnp.where(lo, q2, zero), jnp.where(lo, zero, q2)], axis=0)
        kk = k_scr[jb * w:(jb + 2) * w, h * LANES:(h + 1) * LANES]
        return lax.dot_general(kk, qm2, contract_lanes, preferred_element_type=F32)

    def head_out(jb, h, p, par, s):
        s = jnp.where(band_first if jb == 0 else band, s, -jnp.inf)
        sink = sink_ref[h * ATTN_GROUP + 2 * p + par] * LOG2E
        mx = jnp.maximum(jnp.max(s, axis=0, keepdims=True), sink)
        pr = jnp.exp2(s - mx)
        denom = jnp.sum(pr, axis=0, keepdims=True) + jnp.exp2(sink - mx)
        vt = vt_scr[h * ATTN_HEAD_DIM:(h + 1) * ATTN_HEAD_DIM, jb * w:(jb + 2) * w]
        return jnp.dot(vt, pr.astype(BF16), preferred_element_type=F32) * (1.0 / denom)

    pending = [scores(*u) for u in pairs[:ATTN_LOOKAHEAD]]
    for idx, (jb, h, p) in enumerate(pairs):
        s2 = pending[idx]
        pending[idx] = None
        outs = [head_out(jb, h, p, par, s2[:, par * w:(par + 1) * w]) for par in range(2)]
        if idx + ATTN_LOOKAHEAD < len(pairs):
            pending.append(scores(*pairs[idx + ATTN_LOOKAHEAD]))
        c0 = h * ATTN_GROUP * ATTN_HEAD_DIM + p * LANES
        pair = jnp.concatenate(outs, axis=0)
        o_ref[jb * w:(jb + 1) * w, c0:c0 + LANES] = pair.T.astype(BF16)


def _attention(sinks, q, kk, vt, batch, seq):
    n = q.shape[0]
    tq = ATTN_BLOCKS * WINDOW
    nt = seq // tq
    cur = lambda b, j: b * nt + j
    prev = lambda b, j: (b * nt + j) * ATTN_BLOCKS - jnp.minimum(j, 1)
    return pl.pallas_call(
        _attn_kernel,
        grid=(batch, nt),
        in_specs=[
            pl.BlockSpec(memory_space=pltpu.SMEM),
            pl.BlockSpec((tq, 1024), lambda b, j: (cur(b, j), 0)),
            pl.BlockSpec((tq, 2 * SZ_AK), lambda b, j: (cur(b, j), 0)),
            pl.BlockSpec((WINDOW, 2 * SZ_AK), lambda b, j: (prev(b, j), 0)),
            pl.BlockSpec((SZ_AV, tq), lambda b, j: (0, cur(b, j))),
            pl.BlockSpec((SZ_AV, WINDOW), lambda b, j: (0, prev(b, j))),
        ],
        out_specs=pl.BlockSpec((tq, 1024), lambda b, j: (cur(b, j), 0)),
        out_shape=jax.ShapeDtypeStruct((n, 1024), BF16),
        scratch_shapes=[pltpu.VMEM((WINDOW + tq, 2 * SZ_AK), BF16),
                        pltpu.VMEM((SZ_AV, WINDOW + tq), BF16)],
        compiler_params=pltpu.CompilerParams(
            dimension_semantics=("arbitrary", "arbitrary"), vmem_limit_bytes=VMEM_LIMIT_BYTES),
        name="attention",
    )(sinks, q, kk, kk, vt, vt)


def _log_sigmoid(x):
    return -(jnp.maximum(-x, 0.0) + jnp.log(1.0 + jnp.exp(-jnp.abs(x))))


def _mlstm_gate_kernel(g_ref, rows_ref, cols_ref):
    L = MLSTM_CHUNK
    nh = MLSTM_HEADS
    seq = g_ref.shape[1]
    g = g_ref[...]
    ig = GATE_SOFTCAP * jnp.tanh(g[0:nh] / GATE_SOFTCAP)
    fg = GATE_SOFTCAP * jnp.tanh(g[nh:2 * nh] / GATE_SOFTCAP)
    lf = _log_sigmoid(fg)
    pos = lax.broadcasted_iota(jnp.int32, (nh, seq), 1) & (L - 1)
    a = lf
    sh = 1
    while sh < L:
        a = a + jnp.where(pos >= sh, pltpu.roll(a, sh, 1), 0.0)
        sh *= 2
    bvec = ig - a
    pm = bvec
    sh = 1
    while sh < L:
        pm = jnp.maximum(pm, jnp.where(pos >= sh, pltpu.roll(pm, sh, 1), -jnp.inf))
        sh *= 2
    m = jnp.zeros((nh, L), F32)
    pad = jnp.zeros((L - nh, L), F32)
    for c in range(seq // L):
        sl = slice(c * L, (c + 1) * L)
        a_c, b_c = a[:, sl], bvec[:, sl]
        mrow = jnp.maximum(m, pm[:, sl])
        total = jnp.broadcast_to(a_c[:, L - 1:L], (nh, L))
        mlast = jnp.broadcast_to(mrow[:, L - 1:L], (nh, L))
        rows_ref[ROW_M2:ROW_M2 + nh, sl] = mrow * LOG2E
        rows_ref[ROW_WINTER:ROW_WINTER + nh, sl] = jnp.exp(m - mrow)
        rows_ref[ROW_ENEG:ROW_ENEG + nh, sl] = jnp.exp(-(a_c + mrow))
        rows_ref[ROW_WK:ROW_WK + nh, sl] = jnp.exp(b_c - mlast)
        rows_ref[ROW_DECAY:ROW_DECAY + nh, sl] = jnp.exp(m - mlast)
        m = total + mlast
        cols_ref[sl, :] = jnp.concatenate([b_c * LOG2E, pad], axis=0).T


def _mlstm_gates(gt, batch, seq):
    n = gt.shape[1]
    return pl.pallas_call(
        _mlstm_gate_kernel,
        grid=(batch,),
        in_specs=[pl.BlockSpec((2 * MLSTM_HEADS, seq), lambda b: (0, b))],
        out_specs=[pl.BlockSpec((GATE_ROWS, seq), lambda b: (0, b)),
                   pl.BlockSpec((seq, LANES), lambda b: (b, 0))],
        out_shape=[jax.ShapeDtypeStruct((GATE_ROWS, n), F32),
                   jax.ShapeDtypeStruct((n, LANES), F32)],
        compiler_params=pltpu.CompilerParams(
            dimension_semantics=("arbitrary",), vmem_limit_bytes=VMEM_LIMIT_BYTES),
        name="mlstm_gates",
    )(gt)


def _mlstm_kernel(rows_ref, cols_ref, qk_ref, vt_ref, og_ref, hn_ref, o_ref, ct_scr, n_scr):
    L = MLSTM_CHUNK
    nh = MLSTM_HEADS
    contract_lanes = (((1,), (1,)), ((), ()))

    @pl.when(pl.program_id(1) == 0)
    def _():
        ct_scr[...] = jnp.zeros_like(ct_scr)
        n_scr[...] = jnp.zeros_like(n_scr)

    row_i = lax.broadcasted_iota(jnp.int32, (L, L), 0)
    col_i = lax.broadcasted_iota(jnp.int32, (L, L), 1)
    causal_t = row_i <= col_i
    lo_lane = col_i < MLSTM_QK_DIM
    zero = jnp.zeros((), BF16)

    n_chunks = qk_ref.shape[0] // L
    units = [(ci, p, par) for ci in range(n_chunks) for p in range(nh // 2) for par in range(2)]
    ct = [ct_scr[p] for p in range(nh // 2)]
    nn = [n_scr[p] for p in range(nh // 2)]
    gate_rows = {}
    upd = {}

    def chunk_gates(ci):
        if ci not in gate_rows:
            rs = slice(ci * L, (ci + 1) * L)
            wk = rows_ref[ROW_WK:ROW_WK + nh, rs]
            gate_rows[ci] = dict(
                m2=rows_ref[ROW_M2:ROW_M2 + nh, rs], w_inter=rows_ref[ROW_WINTER:ROW_WINTER + nh, rs],
                e_neg=rows_ref[ROW_ENEG:ROW_ENEG + nh, rs], wk=wk,
                decay=rows_ref[ROW_DECAY:ROW_DECAY + nh, rs], b2cols=cols_ref[rs, :],
                wk_b=jnp.concatenate([wk, wk], axis=0).astype(BF16))
        return gate_rows[ci]

    def front(ci, p, par):
        g = chunk_gates(ci)
        rs = slice(ci * L, (ci + 1) * L)
        hh = 2 * p + par
        q2 = qk_ref[rs, p * LANES:(p + 1) * LANES]
        k2 = qk_ref[rs, SZ_MQ + p * LANES:SZ_MQ + (p + 1) * LANES]
        qm = jnp.where(lo_lane, q2, zero) if par == 0 else jnp.where(lo_lane, zero, q2)
        vt = vt_ref[hh * MLSTM_V_DIM:(hh + 1) * MLSTM_V_DIM, rs]
        st = lax.dot_general(k2, qm, contract_lanes, preferred_element_type=F32)
        n2b = jnp.broadcast_to(nn[p], (2 * SUBLANES, LANES)).astype(BF16)
        qn = lax.dot_general(n2b, qm, contract_lanes, preferred_element_type=F32)[0:1, :]
        inter = lax.dot_general(ct[p].astype(BF16), qm, contract_lanes,
                                preferred_element_type=F32)
        vw = (vt.astype(F32) * g["wk"][hh:hh + 1, :]).astype(BF16)
        upd[(ci, p, par)] = jnp.dot(vw, k2, preferred_element_type=F32)
        if par == 1:
            h0, h1 = 2 * p, 2 * p + 1
            dec = jnp.where(lo_lane[0:1, :], g["decay"][h0:h0 + 1, :], g["decay"][h1:h1 + 1, :])
            n_inc = jnp.dot(g["wk_b"], k2, preferred_element_type=F32)
            ct[p] = dec * ct[p] + jnp.where(lo_lane, upd.pop((ci, p, 0)), upd.pop((ci, p, 1)))
            nn[p] = dec * nn[p] + jnp.where(lo_lane[0:1, :], n_inc[h0:h0 + 1, :],
                                            n_inc[h1:h1 + 1, :])
        return st, qn, inter

    def back(ci, p, par, st, qn, inter):
        g = chunk_gates(ci)
        rs = slice(ci * L, (ci + 1) * L)
        hh = 2 * p + par
        vsl = slice(hh * MLSTM_V_DIM, (hh + 1) * MLSTM_V_DIM)
        dt = jnp.where(causal_t,
                       jnp.exp2(g["b2cols"][:, hh:hh + 1] - g["m2"][hh:hh + 1, :]), 0.0)
        sct = st * dt
        wi = g["w_inter"][hh:hh + 1, :]
        den = jnp.sum(sct, axis=0, keepdims=True) + wi * qn
        numt = jnp.dot(vt_ref[vsl, rs], sct.astype(BF16), preferred_element_type=F32) + wi * inter
        rden = 1.0 / jnp.maximum(jnp.abs(den), g["e_neg"][hh:hh + 1, :])
        ms = jnp.mean(numt * numt, axis=0, keepdims=True)
        scale = rden * lax.rsqrt(rden * rden * ms + NORM_EPS)
        hv = (numt * scale).T
        gate = jax.nn.sigmoid(og_ref[rs, vsl].astype(F32))
        o_ref[rs, vsl] = (hv * hn_ref[:, vsl] * gate).astype(BF16)

    pending = [front(*u) for u in units[:MLSTM_LOOKAHEAD]]
    for idx, u in enumerate(units):
        back(*u, *pending[idx])
        pending[idx] = None
        if idx + MLSTM_LOOKAHEAD < len(units):
            pending.append(front(*units[idx + MLSTM_LOOKAHEAD]))
    for p in range(nh // 2):
        ct_scr[p] = ct[p]
        n_scr[p] = nn[p]


def _mlstm(rows, cols, qk, v, og, hn, batch, seq):
    n = qk.shape[0]
    tm = TM_MLSTM
    nt = seq // tm
    row = pl.BlockSpec((tm, 1024), lambda b, c: (b * nt + c, 0))
    return pl.pallas_call(
        _mlstm_kernel,
        grid=(batch, nt),
        in_specs=[
            pl.BlockSpec((GATE_ROWS, tm), lambda b, c: (0, b * nt + c)),
            pl.BlockSpec((tm, LANES), lambda b, c: (b * nt + c, 0)),
            row,
            pl.BlockSpec((SZ_MV, tm), lambda b, c: (0, b * nt + c)),
            row,
            _const_spec((1, 1024)),
        ],
        out_specs=row,
        out_shape=jax.ShapeDtypeStruct((n, 1024), BF16),
        scratch_shapes=[
            pltpu.VMEM((MLSTM_HEADS // 2, 2 * MLSTM_QK_DIM, MLSTM_V_DIM), F32),
            pltpu.VMEM((MLSTM_HEADS // 2, 1, 2 * MLSTM_QK_DIM), F32),
        ],
        compiler_params=pltpu.CompilerParams(
            dimension_semantics=("arbitrary", "arbitrary"), vmem_limit_bytes=VMEM_LIMIT_BYTES),
        name="mlstm",
    )(rows, cols, qk, v, og, hn)


def _merge_kernel(x_ref, a_ref, hm_ref, gate_ref, wpa_ref, wpm_ref, wo_ref, o_ref):
    ya = jnp.dot(a_ref[...], wpa_ref[...], preferred_element_type=F32)
    ym = jnp.dot(hm_ref[...], wpm_ref[...], preferred_element_type=F32)
    ga = jax.nn.sigmoid(gate_ref[:, 0:D_MODEL].astype(F32))
    gm = jax.nn.sigmoid(gate_ref[:, D_MODEL:2 * D_MODEL].astype(F32))
    merged = (ga * ya + gm * ym).astype(BF16)
    o_ref[...] = x_ref[...] + jnp.dot(merged, wo_ref[...], preferred_element_type=F32)


def _merge(x, a, hm, gates, wpa, wpm, wo):
    n = x.shape[0]
    tm = TM_MERGE
    row = lambda width: pl.BlockSpec((tm, width), lambda i: (i, 0))
    return pl.pallas_call(
        _merge_kernel,
        grid=(n // tm,),
        in_specs=[row(D_MODEL), row(1024), row(1024), row(2048),
                  _const_spec((1024, D_MODEL)), _const_spec((1024, D_MODEL)),
                  _const_spec((D_MODEL, D_MODEL))],
        out_specs=row(D_MODEL),
        out_shape=jax.ShapeDtypeStruct((n, D_MODEL), F32),
        compiler_params=pltpu.CompilerParams(
            dimension_semantics=("arbitrary",), vmem_limit_bytes=VMEM_LIMIT_BYTES),
        name="merge",
    )(x, a, hm, gates, wpa, wpm, wo)


def _pack_inproj(w_in, b_in):
    o = 0
    parts = {}
    for name, size in (("aq", SZ_AQ), ("ak", SZ_AK), ("av", SZ_AV), ("mq", SZ_MQ), ("mk", SZ_MK),
                       ("mv", SZ_MV), ("mo", SZ_MO), ("mi", SZ_MI), ("mf", SZ_MF),
                       ("g", 2 * D_MODEL)):
        parts[name] = (w_in[:, o:o + size], b_in[o:o + size])
        o += size

    def dup(t):
        lead = t.shape[:-1]
        t = t.reshape(lead + (ATTN_KV_HEADS, 1, ATTN_HEAD_DIM))
        t = jnp.broadcast_to(t, lead + (ATTN_KV_HEADS, 2, ATTN_HEAD_DIM))
        return t.reshape(lead + (2 * SZ_AK,))

    order = [parts["aq"], tuple(dup(t) for t in parts["ak"]),
             parts["mq"], parts["mk"], parts["mo"], parts["g"]]
    w = jnp.concatenate([t[0] for t in order], axis=1).astype(BF16)
    b = jnp.concatenate([t[1] for t in order], axis=0).reshape(1, PACKED_WIDTH)
    wif = jnp.concatenate([parts["mi"][0], parts["mf"][0]], axis=1).T.astype(BF16)
    bif = jnp.concatenate([parts["mi"][1], parts["mf"][1]], axis=0).reshape(2 * MLSTM_HEADS, 1)
    wvt = jnp.concatenate([parts["mv"][0], parts["av"][0]], axis=1).T.astype(BF16)
    bvt = jnp.concatenate([parts["mv"][1], parts["av"][1]], axis=0).reshape(SZ_MV + SZ_AV, 1)
    return w, b, wif, bif, wvt, bvt


def kernel(x, ffn1_norm, ffn1_w_gate, ffn1_w_up, ffn1_w_down, mix_norm, w_in, b_in, attn_sinks,
           mlstm_conv, mlstm_head_norm, w_proj_attn, w_proj_mlstm, w_out, ffn2_norm, ffn2_w_gate,
           ffn2_w_up, ffn2_w_down, final_norm):
    batch, seq, d = x.shape
    assert d == D_MODEL and seq % TM_INPROJ == 0 and (batch * seq) % TM_FFN == 0
    assert ffn1_norm.shape[0] == 1, "one layer"
    n = batch * seq
    xf = x.reshape(n, d)
    fin = final_norm.reshape(1, d)

    x1 = _ffn(xf, ffn1_norm[0].reshape(1, d), ffn1_w_gate[0].astype(BF16), ffn1_w_up[0].astype(BF16),
              ffn1_w_down[0].astype(BF16), fin, apply_final=False)

    w, b, wif, bif, wvt, bvt = _pack_inproj(w_in[0], b_in[0])
    q, kk, avt, mqk, mvt, mo, gates, gt = _inproj(x1, mix_norm[0].reshape(1, d), w, b, wif, bif,
                                                  wvt, bvt, mlstm_conv[0], seq)

    ya = _attention(attn_sinks[0], q, kk, avt, batch, seq)

    rows, cols = _mlstm_gates(gt, batch, seq)
    hm = _mlstm(rows, cols, mqk, mvt, mo, mlstm_head_norm[0].reshape(1, SZ_MV), batch, seq)

    x2 = _merge(x1, ya, hm, gates, w_proj_attn[0].astype(BF16), w_proj_mlstm[0].astype(BF16),
                w_out[0].astype(BF16))

    out = _ffn(x2, ffn2_norm[0].reshape(1, d), ffn2_w_gate[0].astype(BF16), ffn2_w_up[0].astype(BF16),
               ffn2_w_down[0].astype(BF16), fin, apply_final=True)
    return out.reshape(batch, seq, d)
```

```python
import functools

import jax
import jax.numpy as jnp
from jax import lax
from jax.experimental import pallas as pl
from jax.experimental.pallas import tpu as pltpu

F32 = jnp.float32
BF16 = jnp.bfloat16

D_MODEL = 1024
ATTN_HEAD_DIM = 64
ATTN_HEADS = 16
ATTN_KV_HEADS = 4
ATTN_GROUP = 4
WINDOW = 128
MLSTM_HEADS = 8
MLSTM_V_DIM = 128
MLSTM_QK_DIM = 64
CONV_WIDTH = 4
GATE_SOFTCAP = 15.0
D_FF = 2816
FFN_RESIDUAL_WEIGHT = 0.5
NORM_EPS = 1e-6

SZ_AQ, SZ_AK, SZ_AV = 1024, 256, 256
SZ_MQ, SZ_MK, SZ_MV, SZ_MO = 512, 512, 1024, 1024
SZ_MI, SZ_MF = 8, 8

LANES = 128
SUBLANES = 8
MXU_DIM = 256
VMEM_LIMIT_BYTES = 56 * 1024 * 1024

MLSTM_CHUNK = 128
FF_CHUNK = MXU_DIM
TM_FFN = 1024
TM_INPROJ = 512
TM_MERGE = 1024
TM_MLSTM = 4 * MLSTM_CHUNK
ATTN_LOOKAHEAD = 4
ATTN_BLOCKS = 4
MLSTM_LOOKAHEAD = 8

LOG2E = 1.4426950408889634
K_SCALE_LOG2 = 3.0
ROW_M2, ROW_WINTER, ROW_ENEG, ROW_WK, ROW_DECAY = 0, 8, 16, 24, 32
GATE_ROWS = 40

COL_Q = 0
COL_KK = 1024
COL_MQK = 1280
COL_MO = 2304
COL_G = 3328
PACKED_WIDTH = 5376

ATTN_HEAD_ORDER = tuple((2 * (t // 4) + par) * ATTN_GROUP + t % 4
                        for t in range(ATTN_HEADS // 2) for par in range(2))


def _rms(x, g):
    return x * lax.rsqrt(jnp.mean(x * x, axis=-1, keepdims=True) + NORM_EPS) * g


def _const_spec(shape):
    zeros = (0,) * len(shape)
    return pl.BlockSpec(shape, lambda *_: zeros, pipeline_mode=pl.Buffered(1))


def _ffn_kernel(x_ref, g_ref, wg_ref, wu_ref, wd_ref, fin_ref, o_ref, acc_ref, *, apply_final):
    x = x_ref[...]
    h = _rms(x, g_ref[...]).astype(BF16)
    for c in range(D_FF // FF_CHUNK):
        sl = slice(c * FF_CHUNK, (c + 1) * FF_CHUNK)
        a = jnp.dot(h, wg_ref[:, sl], preferred_element_type=F32)
        u = jnp.dot(h, wu_ref[:, sl], preferred_element_type=F32)
        act = (a * jax.nn.sigmoid(a) * u).astype(BF16)
        d = jnp.dot(act, wd_ref[sl, :], preferred_element_type=F32)
        if c == 0:
            acc_ref[...] = d
        else:
            acc_ref[...] += d
    y = x + FFN_RESIDUAL_WEIGHT * acc_ref[...]
    if apply_final:
        y = _rms(y, fin_ref[...])
    o_ref[...] = y


def _ffn(x, g, wg, wu, wd, fin, apply_final):
    n = x.shape[0]
    tm = TM_FFN
    return pl.pallas_call(
        functools.partial(_ffn_kernel, apply_final=apply_final),
        grid=(n // tm,),
        in_specs=[
            pl.BlockSpec((tm, D_MODEL), lambda i: (i, 0)),
            _const_spec((1, D_MODEL)),
            _const_spec((D_MODEL, D_FF)),
            _const_spec((D_MODEL, D_FF)),
            _const_spec((D_FF, D_MODEL)),
            _const_spec((1, D_MODEL)),
        ],
        out_specs=pl.BlockSpec((tm, D_MODEL), lambda i: (i, 0)),
        out_shape=jax.ShapeDtypeStruct((n, D_MODEL), F32),
        scratch_shapes=[pltpu.VMEM((tm, D_MODEL), F32)],
        compiler_params=pltpu.CompilerParams(
            dimension_semantics=("arbitrary",), vmem_limit_bytes=VMEM_LIMIT_BYTES),
        name="ffn_final" if apply_final else "ffn",
    )(x, g, wg, wu, wd, fin)


def _inproj_kernel(x_ref, g_ref, w_ref, b_ref, wif_ref, bif_ref, wvt_ref, bvt_ref, conv_ref,
                   q_ref, kk_ref, avt_ref, mqk_ref, mvt_ref, mo_ref, gate_ref, gt_ref,
                   conv_scr, *, tiles_per_seq):
    i = pl.program_id(0)
    tm = x_ref.shape[0]
    h = _rms(x_ref[...], g_ref[...]).astype(BF16)

    def seg(c0, width):
        return (jnp.dot(h, w_ref[:, c0:c0 + width], preferred_element_type=F32)
                + b_ref[:, c0:c0 + width])

    def seg_t(wt_ref, bt_ref, r0, rows):
        return (lax.dot_general(wt_ref[r0:r0 + rows, :], h, (((1,), (1,)), ((), ())),
                                preferred_element_type=F32) + bt_ref[r0:r0 + rows, :])

    cw = 512

    @pl.when(i == 0)
    def _():
        conv_scr[...] = jnp.zeros_like(conv_scr)

    carry = jnp.where(i % tiles_per_seq == 0, 0.0, conv_scr[...])

    def conv_slab(c, width):
        z = seg(COL_MQK + c, width)
        conv_scr[:, c:c + width] = z[tm - SUBLANES:tm, :]
        zc = jnp.concatenate([carry[:, c:c + width], z], axis=0)
        acc = z * conv_ref[CONV_WIDTH - 1:CONV_WIDTH, c:c + width]
        for k in range(1, CONV_WIDTH):
            zk = pltpu.roll(zc, k, 0)[SUBLANES:, :]
            acc = acc + zk * conv_ref[CONV_WIDTH - 1 - k:CONV_WIDTH - k, c:c + width]
        mqk_ref[:, c:c + width] = (acc * jax.nn.sigmoid(acc)).astype(BF16)

    dw = MXU_DIM
    for k in range(1024 // dw):
        sl = slice(k * dw, (k + 1) * dw)
        conv_slab(k * dw, dw)
        q_ref[:, sl] = (seg(COL_Q + k * dw, dw) * (LOG2E * ATTN_HEAD_DIM ** -0.5)).astype(BF16)
        if k == 0:
            kk_ref[...] = seg(COL_KK, SZ_AK).astype(BF16)
        elif k == 1:
            avt_ref[...] = seg_t(wvt_ref, bvt_ref, SZ_MV, SZ_AV).astype(BF16)
        mvt_ref[sl, :] = seg_t(wvt_ref, bvt_ref, k * dw, dw).astype(BF16)
        mo_ref[:, sl] = seg(COL_MO + k * dw, dw).astype(BF16)
        for c in (2 * k * dw, (2 * k + 1) * dw):
            gate_ref[:, c:c + dw] = seg(COL_G + c, dw).astype(BF16)

    gt_ref[...] = seg_t(wif_ref, bif_ref, 0, 2 * MLSTM_HEADS)


def _inproj(x, g, w, b, wif, bif, wvt, bvt, conv, seq):
    n = x.shape[0]
    tm = TM_INPROJ
    row = lambda width: pl.BlockSpec((tm, width), lambda i: (i, 0))
    col = lambda height: pl.BlockSpec((height, tm), lambda i: (0, i))
    bf = lambda width: jax.ShapeDtypeStruct((n, width), BF16)
    return pl.pallas_call(
        functools.partial(_inproj_kernel, tiles_per_seq=seq // tm),
        grid=(n // tm,),
        in_specs=[
            row(D_MODEL),
            _const_spec((1, D_MODEL)),
            _const_spec((D_MODEL, PACKED_WIDTH)),
            _const_spec((1, PACKED_WIDTH)),
            _const_spec((2 * MLSTM_HEADS, D_MODEL)),
            _const_spec((2 * MLSTM_HEADS, 1)),
            _const_spec((SZ_MV + SZ_AV, D_MODEL)),
            _const_spec((SZ_MV + SZ_AV, 1)),
            _const_spec((CONV_WIDTH, 1024)),
        ],
        out_specs=[row(1024), row(SZ_AK), col(SZ_AV), row(1024), col(SZ_MV), row(1024),
                   row(2048), col(2 * MLSTM_HEADS)],
        out_shape=[bf(1024), bf(SZ_AK), jax.ShapeDtypeStruct((SZ_AV, n), BF16), bf(1024),
                   jax.ShapeDtypeStruct((SZ_MV, n), BF16), bf(1024), bf(2048),
                   jax.ShapeDtypeStruct((2 * MLSTM_HEADS, n), F32)],
        scratch_shapes=[pltpu.VMEM((SUBLANES, 1024), F32)],
        compiler_params=pltpu.CompilerParams(
            dimension_semantics=("arbitrary",), vmem_limit_bytes=VMEM_LIMIT_BYTES),
        name="inproj",
    )(x, g, w, b, wif, bif, wvt, bvt, conv)


def _attn_kernel(sink_ref, q_ref, kc_ref, kp_ref, vtc_ref, vtp_ref, o_ref, k_scr, vt_scr):
    w = WINDOW
    contract_lanes = (((1,), (1,)), ((), ()))
    k_scr[0:w, :] = kp_ref[...]
    k_scr[w:, :] = kc_ref[...]
    vt_scr[:, 0:w] = vtp_ref[...]
    vt_scr[:, w:] = vtc_ref[...]
    kj = lax.broadcasted_iota(jnp.int32, (2 * w, w), 0)
    qi = lax.broadcasted_iota(jnp.int32, (2 * w, w), 1)
    band = (kj > qi) & (kj <= qi + w)
    band_first = band & ((kj >= w) | (pl.program_id(1) > 0))
    lo = lax.broadcasted_iota(jnp.int32, (w, LANES), 1) < ATTN_HEAD_DIM
    zero = jnp.zeros((), BF16)
    pairs = [(jb, t) for jb in range(q_ref.shape[0] // w) for t in range(ATTN_HEADS // 2)]

    def scores(jb, t):
        q2 = q_ref[jb * w:(jb + 1) * w, t * LANES:(t + 1) * LANES]
        qm2 = jnp.concatenate([jnp.where(lo, q2, zero), jnp.where(lo, zero, q2)], axis=0)
        kk = k_scr[jb * w:(jb + 2) * w, (t // 4) * LANES:(t // 4 + 1) * LANES]
        return lax.dot_general(kk, qm2, contract_lanes, preferred_element_type=F32)

    def head_out(jb, t, par, s):
        s = jnp.where(band_first if jb == 0 else band, s, -jnp.inf)
        sink = sink_ref[ATTN_HEAD_ORDER[2 * t + par]] * LOG2E
        mx = jnp.maximum(jnp.max(s, axis=0, keepdims=True), sink)
        pr = jnp.exp2(s - mx)
        denom = jnp.sum(pr, axis=0, keepdims=True) + jnp.exp2(sink - mx)
        kvh = 2 * (t // 4) + par
        vt = vt_scr[kvh * ATTN_HEAD_DIM:(kvh + 1) * ATTN_HEAD_DIM, jb * w:(jb + 2) * w]
        return jnp.dot(vt, pr.astype(BF16), preferred_element_type=F32) * (1.0 / denom)

    pending = [scores(*u) for u in pairs[:ATTN_LOOKAHEAD]]
    for idx, (jb, t) in enumerate(pairs):
        s2 = pending[idx]
        pending[idx] = None
        outs = [head_out(jb, t, par, s2[:, par * w:(par + 1) * w]) for par in range(2)]
        if idx + ATTN_LOOKAHEAD < len(pairs):
            pending.append(scores(*pairs[idx + ATTN_LOOKAHEAD]))
        pair = jnp.concatenate(outs, axis=0)
        o_ref[jb * w:(jb + 1) * w, t * LANES:(t + 1) * LANES] = pair.T.astype(BF16)


def _attention(sinks, q, kk, vt, batch, seq):
    n = q.shape[0]
    tq = ATTN_BLOCKS * WINDOW
    nt = seq // tq
    cur = lambda b, j: b * nt + j
    prev = lambda b, j: (b * nt + j) * ATTN_BLOCKS - jnp.minimum(j, 1)
    return pl.pallas_call(
        _attn_kernel,
        grid=(batch, nt),
        in_specs=[
            pl.BlockSpec(memory_space=pltpu.SMEM),
            pl.BlockSpec((tq, 1024), lambda b, j: (cur(b, j), 0)),
            pl.BlockSpec((tq, SZ_AK), lambda b, j: (cur(b, j), 0)),
            pl.BlockSpec((WINDOW, SZ_AK), lambda b, j: (prev(b, j), 0)),
            pl.BlockSpec((SZ_AV, tq), lambda b, j: (0, cur(b, j))),
            pl.BlockSpec((SZ_AV, WINDOW), lambda b, j: (0, prev(b, j))),
        ],
        out_specs=pl.BlockSpec((tq, 1024), lambda b, j: (cur(b, j), 0)),
        out_shape=jax.ShapeDtypeStruct((n, 1024), BF16),
        scratch_shapes=[pltpu.VMEM((WINDOW + tq, SZ_AK), BF16),
                        pltpu.VMEM((SZ_AV, WINDOW + tq), BF16)],
        compiler_params=pltpu.CompilerParams(
            dimension_semantics=("arbitrary", "arbitrary"), vmem_limit_bytes=VMEM_LIMIT_BYTES),
        name="attention",
    )(sinks, q, kk, kk, vt, vt)


def _log_sigmoid(x):
    return -(jnp.maximum(-x, 0.0) + jnp.log(1.0 + jnp.exp(-jnp.abs(x))))


def _mlstm_gate_kernel(g_ref, rows_ref, cols_ref, *, seq):
    L = MLSTM_CHUNK
    nh = MLSTM_HEADS
    n = g_ref.shape[1]
    g = g_ref[...]
    ig = GATE_SOFTCAP * jnp.tanh(g[0:nh] / GATE_SOFTCAP)
    fg = GATE_SOFTCAP * jnp.tanh(g[nh:2 * nh] / GATE_SOFTCAP)
    lf = _log_sigmoid(fg)
    pos = lax.broadcasted_iota(jnp.int32, (nh, n), 1) & (L - 1)
    a = lf
    sh = 1
    while sh < L:
        a = a + jnp.where(pos >= sh, pltpu.roll(a, sh, 1), 0.0)
        sh *= 2
    bvec = ig - a
    pm = bvec
    sh = 1
    while sh < L:
        pm = jnp.maximum(pm, jnp.where(pos >= sh, pltpu.roll(pm, sh, 1), -jnp.inf))
        sh *= 2
    pad = jnp.zeros((L - nh, L), F32)
    m = [jnp.zeros((nh, L), F32) for _ in range(n // seq)]
    for c in range(seq // L):
        for s in range(n // seq):
            sl = slice(s * seq + c * L, s * seq + (c + 1) * L)
            a_c, b_c = a[:, sl], bvec[:, sl]
            mrow = jnp.maximum(m[s], pm[:, sl])
            total = jnp.broadcast_to(a_c[:, L - 1:L], (nh, L))
            mlast = jnp.broadcast_to(mrow[:, L - 1:L], (nh, L))
            rows_ref[ROW_M2:ROW_M2 + nh, sl] = mrow * LOG2E
            rows_ref[ROW_WINTER:ROW_WINTER + nh, sl] = jnp.exp(m[s] - mrow)
            rows_ref[ROW_ENEG:ROW_ENEG + nh, sl] = jnp.exp(-(a_c + mrow))
            rows_ref[ROW_WK:ROW_WK + nh, sl] = jnp.exp(b_c - mlast) * MLSTM_QK_DIM ** -0.5
            rows_ref[ROW_DECAY:ROW_DECAY + nh, sl] = jnp.exp(m[s] - mlast)
            m[s] = total + mlast
            cols_ref[sl, :] = jnp.concatenate([b_c * LOG2E - K_SCALE_LOG2, pad], axis=0).T


def _mlstm_gates(gt, seq):
    n = gt.shape[1]
    return pl.pallas_call(
        functools.partial(_mlstm_gate_kernel, seq=seq),
        grid=(1,),
        in_specs=[pl.BlockSpec((2 * MLSTM_HEADS, n), lambda i: (0, 0))],
        out_specs=[pl.BlockSpec((GATE_ROWS, n), lambda i: (0, 0)),
                   pl.BlockSpec((n, LANES), lambda i: (0, 0))],
        out_shape=[jax.ShapeDtypeStruct((GATE_ROWS, n), F32),
                   jax.ShapeDtypeStruct((n, LANES), F32)],
        compiler_params=pltpu.CompilerParams(
            dimension_semantics=("arbitrary",), vmem_limit_bytes=VMEM_LIMIT_BYTES),
        name="mlstm_gates",
    )(gt)


def _mlstm_kernel(rows_ref, cols_ref, qk_ref, vt_ref, og_ref, hn_ref, o_ref, ct_scr, n_scr):
    L = MLSTM_CHUNK
    nh = MLSTM_HEADS
    contract_lanes = (((1,), (1,)), ((), ()))

    @pl.when(pl.program_id(1) == 0)
    def _():
        ct_scr[...] = jnp.zeros_like(ct_scr)
        n_scr[...] = jnp.zeros_like(n_scr)

    row_i = lax.broadcasted_iota(jnp.int32, (L, L), 0)
    col_i = lax.broadcasted_iota(jnp.int32, (L, L), 1)
    causal_t = row_i <= col_i
    lo_lane = col_i < MLSTM_QK_DIM
    zero = jnp.zeros((), BF16)

    n_chunks = qk_ref.shape[0] // L
    units = [(ci, p, par) for ci in range(n_chunks) for p in range(nh // 2) for par in range(2)]
    ct = [ct_scr[p] for p in range(nh // 2)]
    nn = [n_scr[p] for p in range(nh // 2)]
    gate_rows = {}
    upd = {}

    def chunk_gates(ci):
        if ci not in gate_rows:
            rs = slice(ci * L, (ci + 1) * L)
            wk = rows_ref[ROW_WK:ROW_WK + nh, rs]
            gate_rows[ci] = dict(
                m2=rows_ref[ROW_M2:ROW_M2 + nh, rs], w_inter=rows_ref[ROW_WINTER:ROW_WINTER + nh, rs],
                e_neg=rows_ref[ROW_ENEG:ROW_ENEG + nh, rs], wk=wk,
                decay=rows_ref[ROW_DECAY:ROW_DECAY + nh, rs], b2cols=cols_ref[rs, :],
                wk_b=jnp.concatenate([wk, wk], axis=0).astype(BF16))
        return gate_rows[ci]

    def front(ci, p, par):
        g = chunk_gates(ci)
        rs = slice(ci * L, (ci + 1) * L)
        hh = 2 * p + par
        q2 = qk_ref[rs, p * LANES:(p + 1) * LANES]
        k2 = qk_ref[rs, SZ_MQ + p * LANES:SZ_MQ + (p + 1) * LANES]
        qm = jnp.where(lo_lane, q2, zero) if par == 0 else jnp.where(lo_lane, zero, q2)
        vt = vt_ref[hh * MLSTM_V_DIM:(hh + 1) * MLSTM_V_DIM, rs]
        st = lax.dot_general(k2, qm, contract_lanes, preferred_element_type=F32)
        n2b = jnp.broadcast_to(nn[p], (2 * SUBLANES, LANES)).astype(BF16)
        qn = lax.dot_general(n2b, qm, contract_lanes, preferred_element_type=F32)[0:1, :]
        inter = lax.dot_general(ct[p].astype(BF16), qm, contract_lanes,
                                preferred_element_type=F32)
        vw = (vt.astype(F32) * g["wk"][hh:hh + 1, :]).astype(BF16)
        upd[(ci, p, par)] = jnp.dot(vw, k2, preferred_element_type=F32)
        if par == 1:
            h0, h1 = 2 * p, 2 * p + 1
            dec = jnp.where(lo_lane[0:1, :], g["decay"][h0:h0 + 1, :], g["decay"][h1:h1 + 1, :])
            n_inc = jnp.dot(g["wk_b"], k2, preferred_element_type=F32)
            ct[p] = dec * ct[p] + jnp.where(lo_lane, upd.pop((ci, p, 0)), upd.pop((ci, p, 1)))
            nn[p] = dec * nn[p] + jnp.where(lo_lane[0:1, :], n_inc[h0:h0 + 1, :],
                                            n_inc[h1:h1 + 1, :])
        return st, qn, inter

    def back(ci, p, par, st, qn, inter):
        g = chunk_gates(ci)
        rs = slice(ci * L, (ci + 1) * L)
        hh = 2 * p + par
        vsl = slice(hh * MLSTM_V_DIM, (hh + 1) * MLSTM_V_DIM)
        dt = jnp.where(causal_t,
                       jnp.exp2(g["b2cols"][:, hh:hh + 1] - g["m2"][hh:hh + 1, :]), 0.0)
        sct = st * dt
        wi = g["w_inter"][hh:hh + 1, :]
        den = jnp.sum(sct, axis=0, keepdims=True) + wi * qn
        numt = jnp.dot(vt_ref[vsl, rs], sct.astype(BF16), preferred_element_type=F32) + wi * inter
        rden = 1.0 / jnp.maximum(jnp.abs(den), g["e_neg"][hh:hh + 1, :])
        ms = jnp.mean(numt * numt, axis=0, keepdims=True)
        scale = rden * lax.rsqrt(rden * rden * ms + NORM_EPS)
        hv = (numt * scale).T
        gate = jax.nn.sigmoid(og_ref[rs, vsl].astype(F32))
        o_ref[rs, vsl] = (hv * hn_ref[:, vsl] * gate).astype(BF16)

    pending = [front(*u) for u in units[:MLSTM_LOOKAHEAD]]
    for idx, u in enumerate(units):
        back(*u, *pending[idx])
        pending[idx] = None
        if idx + MLSTM_LOOKAHEAD < len(units):
            pending.append(front(*units[idx + MLSTM_LOOKAHEAD]))
    for p in range(nh // 2):
        ct_scr[p] = ct[p]
        n_scr[p] = nn[p]


def _mlstm(rows, cols, qk, v, og, hn, batch, seq):
    n = qk.shape[0]
    tm = TM_MLSTM
    nt = seq // tm
    row = pl.BlockSpec((tm, 1024), lambda b, c: (b * nt + c, 0))
    return pl.pallas_call(
        _mlstm_kernel,
        grid=(batch, nt),
        in_specs=[
            pl.BlockSpec((GATE_ROWS, tm), lambda b, c: (0, b * nt + c)),
            pl.BlockSpec((tm, LANES), lambda b, c: (b * nt + c, 0)),
            row,
            pl.BlockSpec((SZ_MV, tm), lambda b, c: (0, b * nt + c)),
            row,
            _const_spec((1, 1024)),
        ],
        out_specs=row,
        out_shape=jax.ShapeDtypeStruct((n, 1024), BF16),
        scratch_shapes=[
            pltpu.VMEM((MLSTM_HEADS // 2, 2 * MLSTM_QK_DIM, MLSTM_V_DIM), F32),
            pltpu.VMEM((MLSTM_HEADS // 2, 1, 2 * MLSTM_QK_DIM), F32),
        ],
        compiler_params=pltpu.CompilerParams(
            dimension_semantics=("arbitrary", "arbitrary"), vmem_limit_bytes=VMEM_LIMIT_BYTES),
        name="mlstm",
    )(rows, cols, qk, v, og, hn)


def _merge_kernel(x_ref, a_ref, hm_ref, gate_ref, wpa_ref, wpm_ref, wo_ref, o_ref):
    ya = jnp.dot(a_ref[...], wpa_ref[...], preferred_element_type=F32)
    ym = jnp.dot(hm_ref[...], wpm_ref[...], preferred_element_type=F32)
    ga = jax.nn.sigmoid(gate_ref[:, 0:D_MODEL].astype(F32))
    gm = jax.nn.sigmoid(gate_ref[:, D_MODEL:2 * D_MODEL].astype(F32))
    merged = (ga * ya + gm * ym).astype(BF16)
    o_ref[...] = x_ref[...] + jnp.dot(merged, wo_ref[...], preferred_element_type=F32)


def _merge(x, a, hm, gates, wpa, wpm, wo):
    n = x.shape[0]
    tm = TM_MERGE
    row = lambda width: pl.BlockSpec((tm, width), lambda i: (i, 0))
    return pl.pallas_call(
        _merge_kernel,
        grid=(n // tm,),
        in_specs=[row(D_MODEL), row(1024), row(1024), row(2048),
                  _const_spec((1024, D_MODEL)), _const_spec((1024, D_MODEL)),
                  _const_spec((D_MODEL, D_MODEL))],
        out_specs=row(D_MODEL),
        out_shape=jax.ShapeDtypeStruct((n, D_MODEL), F32),
        compiler_params=pltpu.CompilerParams(
            dimension_semantics=("arbitrary",), vmem_limit_bytes=VMEM_LIMIT_BYTES),
        name="merge",
    )(x, a, hm, gates, wpa, wpm, wo)


def _pack_inproj(w_in, b_in):
    o = 0
    parts = {}
    for name, size in (("aq", SZ_AQ), ("ak", SZ_AK), ("av", SZ_AV), ("mq", SZ_MQ), ("mk", SZ_MK),
                       ("mv", SZ_MV), ("mo", SZ_MO), ("mi", SZ_MI), ("mf", SZ_MF),
                       ("g", 2 * D_MODEL)):
        parts[name] = (w_in[:, o:o + size], b_in[o:o + size])
        o += size

    def pair_heads(t):
        lead = t.shape[:-1]
        t = t.reshape(lead + (ATTN_HEADS, ATTN_HEAD_DIM))
        return jnp.take(t, jnp.array(ATTN_HEAD_ORDER), axis=-2).reshape(lead + (SZ_AQ,))

    order = [tuple(pair_heads(t) for t in parts["aq"]), parts["ak"],
             parts["mq"], parts["mk"], parts["mo"], parts["g"]]
    w = jnp.concatenate([t[0] for t in order], axis=1).astype(BF16)
    b = jnp.concatenate([t[1] for t in order], axis=0).reshape(1, PACKED_WIDTH)
    wif = jnp.concatenate([parts["mi"][0], parts["mf"][0]], axis=1).T.astype(BF16)
    bif = jnp.concatenate([parts["mi"][1], parts["mf"][1]], axis=0).reshape(2 * MLSTM_HEADS, 1)
    wvt = jnp.concatenate([parts["mv"][0], parts["av"][0]], axis=1).T.astype(BF16)
    bvt = jnp.concatenate([parts["mv"][1], parts["av"][1]], axis=0).reshape(SZ_MV + SZ_AV, 1)
    return w, b, wif, bif, wvt, bvt


def kernel(x, ffn1_norm, ffn1_w_gate, ffn1_w_up, ffn1_w_down, mix_norm, w_in, b_in, attn_sinks,
           mlstm_conv, mlstm_head_norm, w_proj_attn, w_proj_mlstm, w_out, ffn2_norm, ffn2_w_gate,
           ffn2_w_up, ffn2_w_down, final_norm):
    batch, seq, d = x.shape
    assert d == D_MODEL and seq % TM_INPROJ == 0 and (batch * seq) % TM_FFN == 0
    assert ffn1_norm.shape[0] == 1, "one layer"
    n = batch * seq
    xf = x.reshape(n, d)
    fin = final_norm.reshape(1, d)

    x1 = _ffn(xf, ffn1_norm[0].reshape(1, d), ffn1_w_gate[0].astype(BF16), ffn1_w_up[0].astype(BF16),
              ffn1_w_down[0].astype(BF16), fin, apply_final=False)

    w, b, wif, bif, wvt, bvt = _pack_inproj(w_in[0], b_in[0])
    q, kk, avt, mqk, mvt, mo, gates, gt = _inproj(x1, mix_norm[0].reshape(1, d), w, b, wif, bif,
                                                  wvt, bvt, mlstm_conv[0], seq)

    ya = _attention(attn_sinks[0], q, kk, avt, batch, seq)

    rows, cols = _mlstm_gates(gt, seq)
    hm = _mlstm(rows, cols, mqk, mvt, mo, mlstm_head_norm[0].reshape(1, SZ_MV), batch, seq)

    wpa = jnp.take(w_proj_attn[0].reshape(ATTN_HEADS, ATTN_HEAD_DIM, d), jnp.array(ATTN_HEAD_ORDER),
                   axis=0).reshape(SZ_AQ, d)
    x2 = _merge(x1, ya, hm, gates, wpa.astype(BF16), w_proj_mlstm[0].astype(BF16),
                w_out[0].astype(BF16))

    out = _ffn(x2, ffn2_norm[0].reshape(1, d), ffn2_w_gate[0].astype(BF16), ffn2_w_up[0].astype(BF16),
               ffn2_w_down[0].astype(BF16), fin, apply_final=True)
    return out.reshape(batch, seq, d)
```

```python
import functools

import jax
import jax.numpy as jnp
from jax import lax
from jax.experimental import pallas as pl
from jax.experimental.pallas import tpu as pltpu

F32 = jnp.float32
BF16 = jnp.bfloat16

D_MODEL = 1024
ATTN_HEAD_DIM = 64
ATTN_HEADS = 16
ATTN_KV_HEADS = 4
ATTN_GROUP = 4
WINDOW = 128
MLSTM_HEADS = 8
MLSTM_V_DIM = 128
MLSTM_QK_DIM = 64
CONV_WIDTH = 4
GATE_SOFTCAP = 15.0
D_FF = 2816
FFN_RESIDUAL_WEIGHT = 0.5
NORM_EPS = 1e-6

SZ_AQ, SZ_AK, SZ_AV = 1024, 256, 256
SZ_MQ, SZ_MK, SZ_MV, SZ_MO = 512, 512, 1024, 1024
SZ_MI, SZ_MF = 8, 8

LANES = 128
SUBLANES = 8
MXU_DIM = 256
VMEM_LIMIT_BYTES = 56 * 1024 * 1024

MLSTM_CHUNK = 128
FF_CHUNK = MXU_DIM
TM_FFN = 1024
TM_INPROJ = 512
TM_MERGE = 1024
TM_MLSTM = 4 * MLSTM_CHUNK
ATTN_LOOKAHEAD = 4
ATTN_BLOCKS = 4
MLSTM_LOOKAHEAD = 8

LOG2E = 1.4426950408889634
K_SCALE_LOG2 = 3.0
ROW_M2, ROW_WINTER, ROW_ENEG, ROW_WK, ROW_DECAY = 0, 8, 16, 24, 32
GATE_ROWS = 40

COL_Q = 0
COL_KK = 1024
COL_MQK = 1280
COL_MO = 2304
COL_G = 3328
PACKED_WIDTH = 5376

ATTN_HEAD_ORDER = tuple((2 * (t // 4) + par) * ATTN_GROUP + t % 4
                        for t in range(ATTN_HEADS // 2) for par in range(2))


def _rms(x, g):
    return x * lax.rsqrt(jnp.mean(x * x, axis=-1, keepdims=True) + NORM_EPS) * g


def _const_spec(shape):
    zeros = (0,) * len(shape)
    return pl.BlockSpec(shape, lambda *_: zeros, pipeline_mode=pl.Buffered(1))


def _ffn_stage_weights(wg_hbm, wu_hbm, wd_hbm, wg_ref, wu_ref, wd_ref, gu_stage, d_stage, sem):
    n_chunks = D_FF // FF_CHUNK

    def copies(c, slot):
        cols = pl.ds(c * FF_CHUNK, FF_CHUNK)
        return (pltpu.make_async_copy(wg_hbm.at[:, cols], gu_stage.at[slot, 0], sem.at[slot, 0]),
                pltpu.make_async_copy(wu_hbm.at[:, cols], gu_stage.at[slot, 1], sem.at[slot, 1]),
                pltpu.make_async_copy(wd_hbm.at[cols, :], d_stage.at[slot], sem.at[slot, 2]))

    for cp in copies(0, 0):
        cp.start()
    for c in range(n_chunks):
        slot = c % 2
        if c + 1 < n_chunks:
            for cp in copies(c + 1, 1 - slot):
                cp.start()
        for cp in copies(c, slot):
            cp.wait()
        sl = slice(c * FF_CHUNK, (c + 1) * FF_CHUNK)
        wg_ref[:, sl] = gu_stage[slot, 0].astype(BF16)
        wu_ref[:, sl] = gu_stage[slot, 1].astype(BF16)
        wd_ref[sl, :] = d_stage[slot].astype(BF16)


def _ffn_kernel(x_ref, g_ref, wg_hbm, wu_hbm, wd_hbm, fin_ref, o_ref,
                wg_ref, wu_ref, wd_ref, gu_stage, d_stage, sem, acc_ref, *, apply_final):
    @pl.when(pl.program_id(0) == 0)
    def _():
        _ffn_stage_weights(wg_hbm, wu_hbm, wd_hbm, wg_ref, wu_ref, wd_ref, gu_stage, d_stage, sem)

    x = x_ref[...]
    h = _rms(x, g_ref[...]).astype(BF16)
    for c in range(D_FF // FF_CHUNK):
        sl = slice(c * FF_CHUNK, (c + 1) * FF_CHUNK)
        a = jnp.dot(h, wg_ref[:, sl], preferred_element_type=F32)
        u = jnp.dot(h, wu_ref[:, sl], preferred_element_type=F32)
        act = (a * jax.nn.sigmoid(a) * u).astype(BF16)
        d = jnp.dot(act, wd_ref[sl, :], preferred_element_type=F32)
        if c == 0:
            acc_ref[...] = d
        else:
            acc_ref[...] += d
    y = x + FFN_RESIDUAL_WEIGHT * acc_ref[...]
    if apply_final:
        y = _rms(y, fin_ref[...])
    o_ref[...] = y


def _ffn(x, g, wg, wu, wd, fin, apply_final):
    n = x.shape[0]
    tm = TM_FFN
    return pl.pallas_call(
        functools.partial(_ffn_kernel, apply_final=apply_final),
        grid=(n // tm,),
        in_specs=[
            pl.BlockSpec((tm, D_MODEL), lambda i: (i, 0)),
            _const_spec((1, D_MODEL)),
            pl.BlockSpec(memory_space=pl.ANY),
            pl.BlockSpec(memory_space=pl.ANY),
            pl.BlockSpec(memory_space=pl.ANY),
            _const_spec((1, D_MODEL)),
        ],
        out_specs=pl.BlockSpec((tm, D_MODEL), lambda i: (i, 0)),
        out_shape=jax.ShapeDtypeStruct((n, D_MODEL), F32),
        scratch_shapes=[
            pltpu.VMEM((D_MODEL, D_FF), BF16),
            pltpu.VMEM((D_MODEL, D_FF), BF16),
            pltpu.VMEM((D_FF, D_MODEL), BF16),
            pltpu.VMEM((2, 2, D_MODEL, FF_CHUNK), F32),
            pltpu.VMEM((2, FF_CHUNK, D_MODEL), F32),
            pltpu.SemaphoreType.DMA((2, 3)),
            pltpu.VMEM((tm, D_MODEL), F32),
        ],
        compiler_params=pltpu.CompilerParams(
            dimension_semantics=("arbitrary",), vmem_limit_bytes=VMEM_LIMIT_BYTES),
        name="ffn_final" if apply_final else "ffn",
    )(x, g, wg, wu, wd, fin)


def _inproj_kernel(x_ref, g_ref, w_ref, b_ref, wif_ref, bif_ref, wvt_ref, bvt_ref, conv_ref,
                   q_ref, kk_ref, avt_ref, mqk_ref, mvt_ref, mo_ref, gate_ref, gt_ref,
                   conv_scr, *, tiles_per_seq):
    i = pl.program_id(0)
    tm = x_ref.shape[0]
    h = _rms(x_ref[...], g_ref[...]).astype(BF16)

    def seg(c0, width):
        return (jnp.dot(h, w_ref[:, c0:c0 + width], preferred_element_type=F32)
                + b_ref[:, c0:c0 + width])

    def seg_t(wt_ref, bt_ref, r0, rows):
        return (lax.dot_general(wt_ref[r0:r0 + rows, :], h, (((1,), (1,)), ((), ())),
                                preferred_element_type=F32) + bt_ref[r0:r0 + rows, :])

    cw = 512

    @pl.when(i == 0)
    def _():
        conv_scr[...] = jnp.zeros_like(conv_scr)

    carry = jnp.where(i % tiles_per_seq == 0, 0.0, conv_scr[...])

    def conv_slab(c, width):
        z = seg(COL_MQK + c, width)
        conv_scr[:, c:c + width] = z[tm - SUBLANES:tm, :]
        zc = jnp.concatenate([carry[:, c:c + width], z], axis=0)
        acc = z * conv_ref[CONV_WIDTH - 1:CONV_WIDTH, c:c + width]
        for k in range(1, CONV_WIDTH):
            zk = pltpu.roll(zc, k, 0)[SUBLANES:, :]
            acc = acc + zk * conv_ref[CONV_WIDTH - 1 - k:CONV_WIDTH - k, c:c + width]
        mqk_ref[:, c:c + width] = (acc * jax.nn.sigmoid(acc)).astype(BF16)

    dw = MXU_DIM
    for k in range(1024 // dw):
        sl = slice(k * dw, (k + 1) * dw)
        conv_slab(k * dw, dw)
        q_ref[:, sl] = (seg(COL_Q + k * dw, dw) * (LOG2E * ATTN_HEAD_DIM ** -0.5)).astype(BF16)
        if k == 0:
            kk_ref[...] = seg(COL_KK, SZ_AK).astype(BF16)
        elif k == 1:
            avt_ref[...] = seg_t(wvt_ref, bvt_ref, SZ_MV, SZ_AV).astype(BF16)
        mvt_ref[sl, :] = seg_t(wvt_ref, bvt_ref, k * dw, dw).astype(BF16)
        mo_ref[:, sl] = seg(COL_MO + k * dw, dw).astype(BF16)
        for c in (2 * k * dw, (2 * k + 1) * dw):
            gate_ref[:, c:c + dw] = seg(COL_G + c, dw).astype(BF16)

    gt_ref[...] = seg_t(wif_ref, bif_ref, 0, 2 * MLSTM_HEADS)


def _inproj(x, g, w, b, wif, bif, wvt, bvt, conv, seq):
    n = x.shape[0]
    tm = TM_INPROJ
    row = lambda width: pl.BlockSpec((tm, width), lambda i: (i, 0))
    col = lambda height: pl.BlockSpec((height, tm), lambda i: (0, i))
    bf = lambda width: jax.ShapeDtypeStruct((n, width), BF16)
    return pl.pallas_call(
        functools.partial(_inproj_kernel, tiles_per_seq=seq // tm),
        grid=(n // tm,),
        in_specs=[
            row(D_MODEL),
            _const_spec((1, D_MODEL)),
            _const_spec((D_MODEL, PACKED_WIDTH)),
            _const_spec((1, PACKED_WIDTH)),
            _const_spec((2 * MLSTM_HEADS, D_MODEL)),
            _const_spec((2 * MLSTM_HEADS, 1)),
            _const_spec((SZ_MV + SZ_AV, D_MODEL)),
            _const_spec((SZ_MV + SZ_AV, 1)),
            _const_spec((CONV_WIDTH, 1024)),
        ],
        out_specs=[row(1024), row(SZ_AK), col(SZ_AV), row(1024), col(SZ_MV), row(1024),
                   row(2048), col(2 * MLSTM_HEADS)],
        out_shape=[bf(1024), bf(SZ_AK), jax.ShapeDtypeStruct((SZ_AV, n), BF16), bf(1024),
                   jax.ShapeDtypeStruct((SZ_MV, n), BF16), bf(1024), bf(2048),
                   jax.ShapeDtypeStruct((2 * MLSTM_HEADS, n), F32)],
        scratch_shapes=[pltpu.VMEM((SUBLANES, 1024), F32)],
        compiler_params=pltpu.CompilerParams(
            dimension_semantics=("arbitrary",), vmem_limit_bytes=VMEM_LIMIT_BYTES),
        name="inproj",
    )(x, g, w, b, wif, bif, wvt, bvt, conv)


def _attn_kernel(sink_ref, q_ref, kc_ref, kp_ref, vtc_ref, vtp_ref, o_ref, k_scr, vt_scr):
    w = WINDOW
    contract_lanes = (((1,), (1,)), ((), ()))
    k_scr[0:w, :] = kp_ref[...]
    k_scr[w:, :] = kc_ref[...]
    vt_scr[:, 0:w] = vtp_ref[...]
    vt_scr[:, w:] = vtc_ref[...]
    kj = lax.broadcasted_iota(jnp.int32, (2 * w, w), 0)
    qi = lax.broadcasted_iota(jnp.int32, (2 * w, w), 1)
    band = (kj > qi) & (kj <= qi + w)
    band_first = band & ((kj >= w) | (pl.program_id(1) > 0))
    lo = lax.broadcasted_iota(jnp.int32, (w, LANES), 1) < ATTN_HEAD_DIM
    zero = jnp.zeros((), BF16)
    pairs = [(jb, t) for jb in range(q_ref.shape[0] // w) for t in range(ATTN_HEADS // 2)]

    def scores(jb, t):
        q2 = q_ref[jb * w:(jb + 1) * w, t * LANES:(t + 1) * LANES]
        qm2 = jnp.concatenate([jnp.where(lo, q2, zero), jnp.where(lo, zero, q2)], axis=0)
        kk = k_scr[jb * w:(jb + 2) * w, (t // 4) * LANES:(t // 4 + 1) * LANES]
        return lax.dot_general(kk, qm2, contract_lanes, preferred_element_type=F32)

    def head_out(jb, t, par, s):
        s = jnp.where(band_first if jb == 0 else band, s, -jnp.inf)
        sink = sink_ref[ATTN_HEAD_ORDER[2 * t + par]] * LOG2E
        mx = jnp.maximum(jnp.max(s, axis=0, keepdims=True), sink)
        pr = jnp.exp2(s - mx)
        denom = jnp.sum(pr, axis=0, keepdims=True) + jnp.exp2(sink - mx)
        kvh = 2 * (t // 4) + par
        vt = vt_scr[kvh * ATTN_HEAD_DIM:(kvh + 1) * ATTN_HEAD_DIM, jb * w:(jb + 2) * w]
        return jnp.dot(vt, pr.astype(BF16), preferred_element_type=F32) * (1.0 / denom)

    pending = [scores(*u) for u in pairs[:ATTN_LOOKAHEAD]]
    for idx, (jb, t) in enumerate(pairs):
        s2 = pending[idx]
        pending[idx] = None
        outs = [head_out(jb, t, par, s2[:, par * w:(par + 1) * w]) for par in range(2)]
        if idx + ATTN_LOOKAHEAD < len(pairs):
            pending.append(scores(*pairs[idx + ATTN_LOOKAHEAD]))
        pair = jnp.concatenate(outs, axis=0)
        o_ref[jb * w:(jb + 1) * w, t * LANES:(t + 1) * LANES] = pair.T.astype(BF16)


def _attention(sinks, q, kk, vt, batch, seq):
    n = q.shape[0]
    tq = ATTN_BLOCKS * WINDOW
    nt = seq // tq
    cur = lambda b, j: b * nt + j
    prev = lambda b, j: (b * nt + j) * ATTN_BLOCKS - jnp.minimum(j, 1)
    return pl.pallas_call(
        _attn_kernel,
        grid=(batch, nt),
        in_specs=[
            pl.BlockSpec(memory_space=pltpu.SMEM),
            pl.BlockSpec((tq, 1024), lambda b, j: (cur(b, j), 0)),
            pl.BlockSpec((tq, SZ_AK), lambda b, j: (cur(b, j), 0)),
            pl.BlockSpec((WINDOW, SZ_AK), lambda b, j: (prev(b, j), 0)),
            pl.BlockSpec((SZ_AV, tq), lambda b, j: (0, cur(b, j))),
            pl.BlockSpec((SZ_AV, WINDOW), lambda b, j: (0, prev(b, j))),
        ],
        out_specs=pl.BlockSpec((tq, 1024), lambda b, j: (cur(b, j), 0)),
        out_shape=jax.ShapeDtypeStruct((n, 1024), BF16),
        scratch_shapes=[pltpu.VMEM((WINDOW + tq, SZ_AK), BF16),
                        pltpu.VMEM((SZ_AV, WINDOW + tq), BF16)],
        compiler_params=pltpu.CompilerParams(
            dimension_semantics=("arbitrary", "arbitrary"), vmem_limit_bytes=VMEM_LIMIT_BYTES),
        name="attention",
    )(sinks, q, kk, kk, vt, vt)


def _log_sigmoid(x):
    return -(jnp.maximum(-x, 0.0) + jnp.log(1.0 + jnp.exp(-jnp.abs(x))))


def _mlstm_gate_kernel(g_ref, rows_ref, cols_ref, *, seq):
    L = MLSTM_CHUNK
    nh = MLSTM_HEADS
    n = g_ref.shape[1]
    g = g_ref[...]
    ig = GATE_SOFTCAP * jnp.tanh(g[0:nh] / GATE_SOFTCAP)
    fg = GATE_SOFTCAP * jnp.tanh(g[nh:2 * nh] / GATE_SOFTCAP)
    lf = _log_sigmoid(fg)
    pos = lax.broadcasted_iota(jnp.int32, (nh, n), 1) & (L - 1)
    a = lf
    sh = 1
    while sh < L:
        a = a + jnp.where(pos >= sh, pltpu.roll(a, sh, 1), 0.0)
        sh *= 2
    bvec = ig - a
    pm = bvec
    sh = 1
    while sh < L:
        pm = jnp.maximum(pm, jnp.where(pos >= sh, pltpu.roll(pm, sh, 1), -jnp.inf))
        sh *= 2
    pad = jnp.zeros((L - nh, L), F32)
    m = [jnp.zeros((nh, L), F32) for _ in range(n // seq)]
    for c in range(seq // L):
        for s in range(n // seq):
            sl = slice(s * seq + c * L, s * seq + (c + 1) * L)
            a_c, b_c = a[:, sl], bvec[:, sl]
            mrow = jnp.maximum(m[s], pm[:, sl])
            total = jnp.broadcast_to(a_c[:, L - 1:L], (nh, L))
            mlast = jnp.broadcast_to(mrow[:, L - 1:L], (nh, L))
            rows_ref[ROW_M2:ROW_M2 + nh, sl] = mrow * LOG2E
            rows_ref[ROW_WINTER:ROW_WINTER + nh, sl] = jnp.exp(m[s] - mrow)
            rows_ref[ROW_ENEG:ROW_ENEG + nh, sl] = jnp.exp(-(a_c + mrow))
            rows_ref[ROW_WK:ROW_WK + nh, sl] = jnp.exp(b_c - mlast) * MLSTM_QK_DIM ** -0.5
            rows_ref[ROW_DECAY:ROW_DECAY + nh, sl] = jnp.exp(m[s] - mlast)
            m[s] = total + mlast
            cols_ref[sl, :] = jnp.concatenate([b_c * LOG2E - K_SCALE_LOG2, pad], axis=0).T


def _mlstm_gates(gt, seq):
    n = gt.shape[1]
    return pl.pallas_call(
        functools.partial(_mlstm_gate_kernel, seq=seq),
        grid=(1,),
        in_specs=[pl.BlockSpec((2 * MLSTM_HEADS, n), lambda i: (0, 0))],
        out_specs=[pl.BlockSpec((GATE_ROWS, n), lambda i: (0, 0)),
                   pl.BlockSpec((n, LANES), lambda i: (0, 0))],
        out_shape=[jax.ShapeDtypeStruct((GATE_ROWS, n), F32),
                   jax.ShapeDtypeStruct((n, LANES), F32)],
        compiler_params=pltpu.CompilerParams(
            dimension_semantics=("arbitrary",), vmem_limit_bytes=VMEM_LIMIT_BYTES),
        name="mlstm_gates",
    )(gt)


def _mlstm_kernel(rows_ref, cols_ref, qk_ref, vt_ref, og_ref, hn_ref, o_ref, ct_scr, n_scr):
    L = MLSTM_CHUNK
    nh = MLSTM_HEADS
    contract_lanes = (((1,), (1,)), ((), ()))

    @pl.when(pl.program_id(1) == 0)
    def _():
        ct_scr[...] = jnp.zeros_like(ct_scr)
        n_scr[...] = jnp.zeros_like(n_scr)

    row_i = lax.broadcasted_iota(jnp.int32, (L, L), 0)
    col_i = lax.broadcasted_iota(jnp.int32, (L, L), 1)
    causal_t = row_i <= col_i
    lo_lane = col_i < MLSTM_QK_DIM
    zero = jnp.zeros((), BF16)

    n_chunks = qk_ref.shape[0] // L
    units = [(ci, p, par) for ci in range(n_chunks) for p in range(nh // 2) for par in range(2)]
    ct = [ct_scr[p] for p in range(nh // 2)]
    nn = [n_scr[p] for p in range(nh // 2)]
    gate_rows = {}
    upd = {}

    def chunk_gates(ci):
        if ci not in gate_rows:
            rs = slice(ci * L, (ci + 1) * L)
            wk = rows_ref[ROW_WK:ROW_WK + nh, rs]
            gate_rows[ci] = dict(
                m2=rows_ref[ROW_M2:ROW_M2 + nh, rs], w_inter=rows_ref[ROW_WINTER:ROW_WINTER + nh, rs],
                e_neg=rows_ref[ROW_ENEG:ROW_ENEG + nh, rs], wk=wk,
                decay=rows_ref[ROW_DECAY:ROW_DECAY + nh, rs], b2cols=cols_ref[rs, :],
                wk_b=jnp.concatenate([wk, wk], axis=0).astype(BF16))
        return gate_rows[ci]

    def front(ci, p, par):
        g = chunk_gates(ci)
        rs = slice(ci * L, (ci + 1) * L)
        hh = 2 * p + par
        q2 = qk_ref[rs, p * LANES:(p + 1) * LANES]
        k2 = qk_ref[rs, SZ_MQ + p * LANES:SZ_MQ + (p + 1) * LANES]
        qm = jnp.where(lo_lane, q2, zero) if par == 0 else jnp.where(lo_lane, zero, q2)
        vt = vt_ref[hh * MLSTM_V_DIM:(hh + 1) * MLSTM_V_DIM, rs]
        st = lax.dot_general(k2, qm, contract_lanes, preferred_element_type=F32)
        n2b = jnp.broadcast_to(nn[p], (2 * SUBLANES, LANES)).astype(BF16)
        qn = lax.dot_general(n2b, qm, contract_lanes, preferred_element_type=F32)[0:1, :]
        inter = lax.dot_general(ct[p].astype(BF16), qm, contract_lanes,
                                preferred_element_type=F32)
        vw = (vt.astype(F32) * g["wk"][hh:hh + 1, :]).astype(BF16)
        upd[(ci, p, par)] = jnp.dot(vw, k2, preferred_element_type=F32)
        if par == 1:
            h0, h1 = 2 * p, 2 * p + 1
            dec = jnp.where(lo_lane[0:1, :], g["decay"][h0:h0 + 1, :], g["decay"][h1:h1 + 1, :])
            n_inc = jnp.dot(g["wk_b"], k2, preferred_element_type=F32)
            ct[p] = dec * ct[p] + jnp.where(lo_lane, upd.pop((ci, p, 0)), upd.pop((ci, p, 1)))
            nn[p] = dec * nn[p] + jnp.where(lo_lane[0:1, :], n_inc[h0:h0 + 1, :],
                                            n_inc[h1:h1 + 1, :])
        return st, qn, inter

    def back(ci, p, par, st, qn, inter):
        g = chunk_gates(ci)
        rs = slice(ci * L, (ci + 1) * L)
        hh = 2 * p + par
        vsl = slice(hh * MLSTM_V_DIM, (hh + 1) * MLSTM_V_DIM)
        dt = jnp.where(causal_t,
                       jnp.exp2(g["b2cols"][:, hh:hh + 1] - g["m2"][hh:hh + 1, :]), 0.0)
        sct = st * dt
        wi = g["w_inter"][hh:hh + 1, :]
        den = jnp.sum(sct, axis=0, keepdims=True) + wi * qn
        numt = jnp.dot(vt_ref[vsl, rs], sct.astype(BF16), preferred_element_type=F32) + wi * inter
        rden = 1.0 / jnp.maximum(jnp.abs(den), g["e_neg"][hh:hh + 1, :])
        ms = jnp.mean(numt * numt, axis=0, keepdims=True)
        scale = rden * lax.rsqrt(rden * rden * ms + NORM_EPS)
        hv = (numt * scale).T
        gate = jax.nn.sigmoid(og_ref[rs, vsl].astype(F32))
        o_ref[rs, vsl] = (hv * hn_ref[:, vsl] * gate).astype(BF16)

    pending = [front(*u) for u in units[:MLSTM_LOOKAHEAD]]
    for idx, u in enumerate(units):
        back(*u, *pending[idx])
        pending[idx] = None
        if idx + MLSTM_LOOKAHEAD < len(units):
            pending.append(front(*units[idx + MLSTM_LOOKAHEAD]))
    for p in range(nh // 2):
        ct_scr[p] = ct[p]
        n_scr[p] = nn[p]


def _mlstm(rows, cols, qk, v, og, hn, batch, seq):
    n = qk.shape[0]
    tm = TM_MLSTM
    nt = seq // tm
    row = pl.BlockSpec((tm, 1024), lambda b, c: (b * nt + c, 0))
    return pl.pallas_call(
        _mlstm_kernel,
        grid=(batch, nt),
        in_specs=[
            pl.BlockSpec((GATE_ROWS, tm), lambda b, c: (0, b * nt + c)),
            pl.BlockSpec((tm, LANES), lambda b, c: (b * nt + c, 0)),
            row,
            pl.BlockSpec((SZ_MV, tm), lambda b, c: (0, b * nt + c)),
            row,
            _const_spec((1, 1024)),
        ],
        out_specs=row,
        out_shape=jax.ShapeDtypeStruct((n, 1024), BF16),
        scratch_shapes=[
            pltpu.VMEM((MLSTM_HEADS // 2, 2 * MLSTM_QK_DIM, MLSTM_V_DIM), F32),
            pltpu.VMEM((MLSTM_HEADS // 2, 1, 2 * MLSTM_QK_DIM), F32),
        ],
        compiler_params=pltpu.CompilerParams(
            dimension_semantics=("arbitrary", "arbitrary"), vmem_limit_bytes=VMEM_LIMIT_BYTES),
        name="mlstm",
    )(rows, cols, qk, v, og, hn)


def _merge_kernel(x_ref, a_ref, hm_ref, gate_ref, wpa_ref, wpm_ref, wo_ref, o_ref):
    ya = jnp.dot(a_ref[...], wpa_ref[...], preferred_element_type=F32)
    ym = jnp.dot(hm_ref[...], wpm_ref[...], preferred_element_type=F32)
    ga = jax.nn.sigmoid(gate_ref[:, 0:D_MODEL].astype(F32))
    gm = jax.nn.sigmoid(gate_ref[:, D_MODEL:2 * D_MODEL].astype(F32))
    merged = (ga * ya + gm * ym).astype(BF16)
    o_ref[...] = x_ref[...] + jnp.dot(merged, wo_ref[...], preferred_element_type=F32)


def _merge(x, a, hm, gates, wpa, wpm, wo):
    n = x.shape[0]
    tm = TM_MERGE
    row = lambda width: pl.BlockSpec((tm, width), lambda i: (i, 0))
    return pl.pallas_call(
        _merge_kernel,
        grid=(n // tm,),
        in_specs=[row(D_MODEL), row(1024), row(1024), row(2048),
                  _const_spec((1024, D_MODEL)), _const_spec((1024, D_MODEL)),
                  _const_spec((D_MODEL, D_MODEL))],
        out_specs=row(D_MODEL),
        out_shape=jax.ShapeDtypeStruct((n, D_MODEL), F32),
        compiler_params=pltpu.CompilerParams(
            dimension_semantics=("arbitrary",), vmem_limit_bytes=VMEM_LIMIT_BYTES),
        name="merge",
    )(x, a, hm, gates, wpa, wpm, wo)


def _to_head_order(t, axis):
    shape = t.shape
    t = t.reshape(shape[:axis] + (2, 2, ATTN_GROUP) + shape[axis + 1:])
    t = jnp.swapaxes(t, axis + 1, axis + 2)
    return t.reshape(shape)


def _pack_inproj(w_in, b_in):
    o = 0
    parts = {}
    for name, size in (("aq", SZ_AQ), ("ak", SZ_AK), ("av", SZ_AV), ("mq", SZ_MQ), ("mk", SZ_MK),
                       ("mv", SZ_MV), ("mo", SZ_MO), ("mi", SZ_MI), ("mf", SZ_MF),
                       ("g", 2 * D_MODEL)):
        parts[name] = (w_in[:, o:o + size], b_in[o:o + size])
        o += size

    def pair_heads(t):
        lead = t.shape[:-1]
        return _to_head_order(t.reshape(lead + (ATTN_HEADS, ATTN_HEAD_DIM)), len(lead)).reshape(
            lead + (SZ_AQ,))

    order = [tuple(pair_heads(t) for t in parts["aq"]), parts["ak"],
             parts["mq"], parts["mk"], parts["mo"], parts["g"]]
    w = jnp.concatenate([t[0] for t in order], axis=1).astype(BF16)
    b = jnp.concatenate([t[1] for t in order], axis=0).reshape(1, PACKED_WIDTH)
    wif = jnp.concatenate([parts["mi"][0], parts["mf"][0]], axis=1).T.astype(BF16)
    bif = jnp.concatenate([parts["mi"][1], parts["mf"][1]], axis=0).reshape(2 * MLSTM_HEADS, 1)
    wvt = jnp.concatenate([parts["mv"][0], parts["av"][0]], axis=1).T.astype(BF16)
    bvt = jnp.concatenate([parts["mv"][1], parts["av"][1]], axis=0).reshape(SZ_MV + SZ_AV, 1)
    return w, b, wif, bif, wvt, bvt


def kernel(x, ffn1_norm, ffn1_w_gate, ffn1_w_up, ffn1_w_down, mix_norm, w_in, b_in, attn_sinks,
           mlstm_conv, mlstm_head_norm, w_proj_attn, w_proj_mlstm, w_out, ffn2_norm, ffn2_w_gate,
           ffn2_w_up, ffn2_w_down, final_norm):
    batch, seq, d = x.shape
    assert d == D_MODEL and seq % TM_INPROJ == 0 and (batch * seq) % TM_FFN == 0
    assert ffn1_norm.shape[0] == 1, "one layer"
    n = batch * seq
    xf = x.reshape(n, d)
    fin = final_norm.reshape(1, d)

    x1 = _ffn(xf, ffn1_norm[0].reshape(1, d), ffn1_w_gate[0], ffn1_w_up[0], ffn1_w_down[0], fin,
              apply_final=False)

    w, b, wif, bif, wvt, bvt = _pack_inproj(w_in[0], b_in[0])
    q, kk, avt, mqk, mvt, mo, gates, gt = _inproj(x1, mix_norm[0].reshape(1, d), w, b, wif, bif,
                                                  wvt, bvt, mlstm_conv[0], seq)

    ya = _attention(attn_sinks[0], q, kk, avt, batch, seq)

    rows, cols = _mlstm_gates(gt, seq)
    hm = _mlstm(rows, cols, mqk, mvt, mo, mlstm_head_norm[0].reshape(1, SZ_MV), batch, seq)

    wpa = _to_head_order(w_proj_attn[0].reshape(ATTN_HEADS, ATTN_HEAD_DIM * d), 0).reshape(SZ_AQ, d)
    x2 = _merge(x1, ya, hm, gates, wpa.astype(BF16), w_proj_mlstm[0].astype(BF16),
                w_out[0].astype(BF16))

    out = _ffn(x2, ffn2_norm[0].reshape(1, d), ffn2_w_gate[0], ffn2_w_up[0], ffn2_w_down[0], fin,
               apply_final=True)
    return out.reshape(batch, seq, d)
```

```python
import functools

import jax
import jax.numpy as jnp
from jax import lax
from jax.experimental import pallas as pl
from jax.experimental.pallas import tpu as pltpu

F32 = jnp.float32
BF16 = jnp.bfloat16

D_MODEL = 1024
ATTN_HEAD_DIM = 64
ATTN_HEADS = 16
ATTN_KV_HEADS = 4
ATTN_GROUP = 4
WINDOW = 128
MLSTM_HEADS = 8
MLSTM_V_DIM = 128
MLSTM_QK_DIM = 64
CONV_WIDTH = 4
GATE_SOFTCAP = 15.0
D_FF = 2816
FFN_RESIDUAL_WEIGHT = 0.5
NORM_EPS = 1e-6

SZ_AQ, SZ_AK, SZ_AV = 1024, 256, 256
SZ_MQ, SZ_MK, SZ_MV, SZ_MO = 512, 512, 1024, 1024
SZ_MI, SZ_MF = 8, 8

LANES = 128
SUBLANES = 8
MXU_DIM = 256
VMEM_LIMIT_BYTES = 56 * 1024 * 1024

MLSTM_CHUNK = 128
FF_CHUNK = MXU_DIM
TM_FFN = 1024
TM_INPROJ = 1024
TM_MERGE = 1024
TM_MLSTM = 8 * MLSTM_CHUNK
ATTN_LOOKAHEAD = 4
ATTN_BLOCKS = 8
MLSTM_LOOKAHEAD = 8

LOG2E = 1.4426950408889634
K_SCALE_LOG2 = 3.0
ROW_M2, ROW_WINTER, ROW_ENEG, ROW_WK, ROW_DECAY = 0, 8, 16, 24, 32
GATE_ROWS = 40

COL_Q = 0
COL_KK = 1024
COL_MQK = 1280
COL_MO = 2304
COL_G = 3328
PACKED_WIDTH = 5376

ATTN_HEAD_ORDER = tuple((2 * (t // 4) + par) * ATTN_GROUP + t % 4
                        for t in range(ATTN_HEADS // 2) for par in range(2))


def _rms(x, g):
    return x * lax.rsqrt(jnp.mean(x * x, axis=-1, keepdims=True) + NORM_EPS) * g


def _const_spec(shape):
    zeros = (0,) * len(shape)
    return pl.BlockSpec(shape, lambda *_: zeros, pipeline_mode=pl.Buffered(1))


def _ffn_stage_weights(wg_hbm, wu_hbm, wd_hbm, wg_ref, wu_ref, wd_ref, gu_stage, d_stage, sem):
    n_chunks = D_FF // FF_CHUNK

    def copies(c, slot):
        cols = pl.ds(c * FF_CHUNK, FF_CHUNK)
        return (pltpu.make_async_copy(wg_hbm.at[:, cols], gu_stage.at[slot, 0], sem.at[slot, 0]),
                pltpu.make_async_copy(wu_hbm.at[:, cols], gu_stage.at[slot, 1], sem.at[slot, 1]),
                pltpu.make_async_copy(wd_hbm.at[cols, :], d_stage.at[slot], sem.at[slot, 2]))

    for cp in copies(0, 0):
        cp.start()
    for c in range(n_chunks):
        slot = c % 2
        if c + 1 < n_chunks:
            for cp in copies(c + 1, 1 - slot):
                cp.start()
        for cp in copies(c, slot):
            cp.wait()
        sl = slice(c * FF_CHUNK, (c + 1) * FF_CHUNK)
        wg_ref[:, sl] = gu_stage[slot, 0].astype(BF16)
        wu_ref[:, sl] = gu_stage[slot, 1].astype(BF16)
        wd_ref[sl, :] = d_stage[slot].astype(BF16)


def _ffn_kernel(x_ref, g_ref, wg_hbm, wu_hbm, wd_hbm, fin_ref, o_ref,
                wg_ref, wu_ref, wd_ref, gu_stage, d_stage, sem, acc_ref, *, apply_final):
    @pl.when(pl.program_id(0) == 0)
    def _():
        _ffn_stage_weights(wg_hbm, wu_hbm, wd_hbm, wg_ref, wu_ref, wd_ref, gu_stage, d_stage, sem)

    x = x_ref[...]
    h = _rms(x, g_ref[...]).astype(BF16)
    for c in range(D_FF // FF_CHUNK):
        sl = slice(c * FF_CHUNK, (c + 1) * FF_CHUNK)
        a = jnp.dot(h, wg_ref[:, sl], preferred_element_type=F32)
        u = jnp.dot(h, wu_ref[:, sl], preferred_element_type=F32)
        act = (a * jax.nn.sigmoid(a) * u).astype(BF16)
        d = jnp.dot(act, wd_ref[sl, :], preferred_element_type=F32)
        if c == 0:
            acc_ref[...] = d
        else:
            acc_ref[...] += d
    y = x + FFN_RESIDUAL_WEIGHT * acc_ref[...]
    if apply_final:
        y = _rms(y, fin_ref[...])
    o_ref[...] = y


def _ffn(x, g, wg, wu, wd, fin, apply_final):
    n = x.shape[0]
    tm = TM_FFN
    return pl.pallas_call(
        functools.partial(_ffn_kernel, apply_final=apply_final),
        grid=(n // tm,),
        in_specs=[
            pl.BlockSpec((tm, D_MODEL), lambda i: (i, 0)),
            _const_spec((1, D_MODEL)),
            pl.BlockSpec(memory_space=pl.ANY),
            pl.BlockSpec(memory_space=pl.ANY),
            pl.BlockSpec(memory_space=pl.ANY),
            _const_spec((1, D_MODEL)),
        ],
        out_specs=pl.BlockSpec((tm, D_MODEL), lambda i: (i, 0)),
        out_shape=jax.ShapeDtypeStruct((n, D_MODEL), F32),
        scratch_shapes=[
            pltpu.VMEM((D_MODEL, D_FF), BF16),
            pltpu.VMEM((D_MODEL, D_FF), BF16),
            pltpu.VMEM((D_FF, D_MODEL), BF16),
            pltpu.VMEM((2, 2, D_MODEL, FF_CHUNK), F32),
            pltpu.VMEM((2, FF_CHUNK, D_MODEL), F32),
            pltpu.SemaphoreType.DMA((2, 3)),
            pltpu.VMEM((tm, D_MODEL), F32),
        ],
        compiler_params=pltpu.CompilerParams(
            dimension_semantics=("arbitrary",), vmem_limit_bytes=VMEM_LIMIT_BYTES),
        name="ffn_final" if apply_final else "ffn",
    )(x, g, wg, wu, wd, fin)


def _inproj_kernel(x_ref, g_ref, w_ref, b_ref, wif_ref, bif_ref, wvt_ref, bvt_ref, conv_ref,
                   q_ref, kk_ref, avt_ref, mqk_ref, mvt_ref, mo_ref, gate_ref, gt_ref,
                   conv_scr, *, tiles_per_seq):
    i = pl.program_id(0)
    tm = x_ref.shape[0]
    h = _rms(x_ref[...], g_ref[...]).astype(BF16)

    def seg(c0, width):
        return (jnp.dot(h, w_ref[:, c0:c0 + width], preferred_element_type=F32)
                + b_ref[:, c0:c0 + width])

    def seg_t(wt_ref, bt_ref, r0, rows):
        return (lax.dot_general(wt_ref[r0:r0 + rows, :], h, (((1,), (1,)), ((), ())),
                                preferred_element_type=F32) + bt_ref[r0:r0 + rows, :])

    cw = 512

    @pl.when(i == 0)
    def _():
        conv_scr[...] = jnp.zeros_like(conv_scr)

    carry = jnp.where(i % tiles_per_seq == 0, 0.0, conv_scr[...])

    def conv_slab(c, width):
        z = seg(COL_MQK + c, width)
        conv_scr[:, c:c + width] = z[tm - SUBLANES:tm, :]
        zc = jnp.concatenate([carry[:, c:c + width], z], axis=0)
        acc = z * conv_ref[CONV_WIDTH - 1:CONV_WIDTH, c:c + width]
        for k in range(1, CONV_WIDTH):
            zk = pltpu.roll(zc, k, 0)[SUBLANES:, :]
            acc = acc + zk * conv_ref[CONV_WIDTH - 1 - k:CONV_WIDTH - k, c:c + width]
        mqk_ref[:, c:c + width] = (acc * jax.nn.sigmoid(acc)).astype(BF16)

    dw = MXU_DIM
    for k in range(1024 // dw):
        sl = slice(k * dw, (k + 1) * dw)
        conv_slab(k * dw, dw)
        q_ref[:, sl] = (seg(COL_Q + k * dw, dw) * (LOG2E * ATTN_HEAD_DIM ** -0.5)).astype(BF16)
        if k == 0:
            kk_ref[...] = seg(COL_KK, SZ_AK).astype(BF16)
        elif k == 1:
            avt_ref[...] = seg_t(wvt_ref, bvt_ref, SZ_MV, SZ_AV).astype(BF16)
        mvt_ref[sl, :] = seg_t(wvt_ref, bvt_ref, k * dw, dw).astype(BF16)
        mo_ref[:, sl] = seg(COL_MO + k * dw, dw).astype(BF16)
        gate_ref[:, 2 * k * dw:(2 * k + 2) * dw] = seg(COL_G + 2 * k * dw, 2 * dw).astype(BF16)

    gt_ref[...] = seg_t(wif_ref, bif_ref, 0, 2 * MLSTM_HEADS)


def _inproj(x, g, w, b, wif, bif, wvt, bvt, conv, seq):
    n = x.shape[0]
    tm = TM_INPROJ
    row = lambda width: pl.BlockSpec((tm, width), lambda i: (i, 0))
    col = lambda height: pl.BlockSpec((height, tm), lambda i: (0, i))
    bf = lambda width: jax.ShapeDtypeStruct((n, width), BF16)
    return pl.pallas_call(
        functools.partial(_inproj_kernel, tiles_per_seq=seq // tm),
        grid=(n // tm,),
        in_specs=[
            row(D_MODEL),
            _const_spec((1, D_MODEL)),
            _const_spec((D_MODEL, PACKED_WIDTH)),
            _const_spec((1, PACKED_WIDTH)),
            _const_spec((2 * MLSTM_HEADS, D_MODEL)),
            _const_spec((2 * MLSTM_HEADS, 1)),
            _const_spec((SZ_MV + SZ_AV, D_MODEL)),
            _const_spec((SZ_MV + SZ_AV, 1)),
            _const_spec((CONV_WIDTH, 1024)),
        ],
        out_specs=[row(1024), row(SZ_AK), col(SZ_AV), row(1024), col(SZ_MV), row(1024),
                   row(2048), col(2 * MLSTM_HEADS)],
        out_shape=[bf(1024), bf(SZ_AK), jax.ShapeDtypeStruct((SZ_AV, n), BF16), bf(1024),
                   jax.ShapeDtypeStruct((SZ_MV, n), BF16), bf(1024), bf(2048),
                   jax.ShapeDtypeStruct((2 * MLSTM_HEADS, n), F32)],
        scratch_shapes=[pltpu.VMEM((SUBLANES, 1024), F32)],
        compiler_params=pltpu.CompilerParams(
            dimension_semantics=("arbitrary",), vmem_limit_bytes=VMEM_LIMIT_BYTES),
        name="inproj",
    )(x, g, w, b, wif, bif, wvt, bvt, conv)


def _attn_kernel(sink_ref, q_ref, kc_ref, kp_ref, vtc_ref, vtp_ref, o_ref, k_scr, vt_scr):
    w = WINDOW
    contract_lanes = (((1,), (1,)), ((), ()))
    k_scr[0:w, :] = kp_ref[...]
    k_scr[w:, :] = kc_ref[...]
    vt_scr[:, 0:w] = vtp_ref[...]
    vt_scr[:, w:] = vtc_ref[...]
    kj = lax.broadcasted_iota(jnp.int32, (2 * w, w), 0)
    qi = lax.broadcasted_iota(jnp.int32, (2 * w, w), 1)
    band = (kj > qi) & (kj <= qi + w)
    band_first = band & ((kj >= w) | (pl.program_id(1) > 0))
    lo = lax.broadcasted_iota(jnp.int32, (w, LANES), 1) < ATTN_HEAD_DIM
    zero = jnp.zeros((), BF16)
    pairs = [(jb, t) for jb in range(q_ref.shape[0] // w) for t in range(ATTN_HEADS // 2)]

    def scores(jb, t):
        q2 = q_ref[jb * w:(jb + 1) * w, t * LANES:(t + 1) * LANES]
        qm2 = jnp.concatenate([jnp.where(lo, q2, zero), jnp.where(lo, zero, q2)], axis=0)
        kk = k_scr[jb * w:(jb + 2) * w, (t // 4) * LANES:(t // 4 + 1) * LANES]
        return lax.dot_general(kk, qm2, contract_lanes, preferred_element_type=F32)

    def head_out(jb, t, par, s):
        s = jnp.where(band_first if jb == 0 else band, s, -jnp.inf)
        sink = sink_ref[ATTN_HEAD_ORDER[2 * t + par]] * LOG2E
        mx = jnp.maximum(jnp.max(s, axis=0, keepdims=True), sink)
        pr = jnp.exp2(s - mx)
        denom = jnp.sum(pr, axis=0, keepdims=True) + jnp.exp2(sink - mx)
        kvh = 2 * (t // 4) + par
        vt = vt_scr[kvh * ATTN_HEAD_DIM:(kvh + 1) * ATTN_HEAD_DIM, jb * w:(jb + 2) * w]
        return jnp.dot(vt, pr.astype(BF16), preferred_element_type=F32) * (1.0 / denom)

    pending = [scores(*u) for u in pairs[:ATTN_LOOKAHEAD]]
    for idx, (jb, t) in enumerate(pairs):
        s2 = pending[idx]
        pending[idx] = None
        outs = [head_out(jb, t, par, s2[:, par * w:(par + 1) * w]) for par in range(2)]
        if idx + ATTN_LOOKAHEAD < len(pairs):
            pending.append(scores(*pairs[idx + ATTN_LOOKAHEAD]))
        pair = jnp.concatenate(outs, axis=0)
        o_ref[jb * w:(jb + 1) * w, t * LANES:(t + 1) * LANES] = pair.T.astype(BF16)


def _attention(sinks, q, kk, vt, batch, seq):
    n = q.shape[0]
    tq = ATTN_BLOCKS * WINDOW
    nt = seq // tq
    cur = lambda b, j: b * nt + j
    prev = lambda b, j: (b * nt + j) * ATTN_BLOCKS - jnp.minimum(j, 1)
    return pl.pallas_call(
        _attn_kernel,
        grid=(batch, nt),
        in_specs=[
            pl.BlockSpec(memory_space=pltpu.SMEM),
            pl.BlockSpec((tq, 1024), lambda b, j: (cur(b, j), 0)),
            pl.BlockSpec((tq, SZ_AK), lambda b, j: (cur(b, j), 0)),
            pl.BlockSpec((WINDOW, SZ_AK), lambda b, j: (prev(b, j), 0)),
            pl.BlockSpec((SZ_AV, tq), lambda b, j: (0, cur(b, j))),
            pl.BlockSpec((SZ_AV, WINDOW), lambda b, j: (0, prev(b, j))),
        ],
        out_specs=pl.BlockSpec((tq, 1024), lambda b, j: (cur(b, j), 0)),
        out_shape=jax.ShapeDtypeStruct((n, 1024), BF16),
        scratch_shapes=[pltpu.VMEM((WINDOW + tq, SZ_AK), BF16),
                        pltpu.VMEM((SZ_AV, WINDOW + tq), BF16)],
        compiler_params=pltpu.CompilerParams(
            dimension_semantics=("arbitrary", "arbitrary"), vmem_limit_bytes=VMEM_LIMIT_BYTES),
        name="attention",
    )(sinks, q, kk, kk, vt, vt)


def _log_sigmoid(x):
    return -(jnp.maximum(-x, 0.0) + jnp.log(1.0 + jnp.exp(-jnp.abs(x))))


def _mlstm_gate_kernel(g_ref, rows_ref, cols_ref, *, seq):
    L = MLSTM_CHUNK
    nh = MLSTM_HEADS
    n = g_ref.shape[1]
    g = g_ref[...]
    ig = GATE_SOFTCAP * jnp.tanh(g[0:nh] / GATE_SOFTCAP)
    fg = GATE_SOFTCAP * jnp.tanh(g[nh:2 * nh] / GATE_SOFTCAP)
    lf = _log_sigmoid(fg)
    pos = lax.broadcasted_iota(jnp.int32, (nh, n), 1) & (L - 1)
    a = lf
    sh = 1
    while sh < L:
        a = a + jnp.where(pos >= sh, pltpu.roll(a, sh, 1), 0.0)
        sh *= 2
    bvec = ig - a
    pm = bvec
    sh = 1
    while sh < L:
        pm = jnp.maximum(pm, jnp.where(pos >= sh, pltpu.roll(pm, sh, 1), -jnp.inf))
        sh *= 2
    pad = jnp.zeros((L - nh, L), F32)
    m = [jnp.zeros((nh, L), F32) for _ in range(n // seq)]
    for c in range(seq // L):
        for s in range(n // seq):
            sl = slice(s * seq + c * L, s * seq + (c + 1) * L)
            a_c, b_c = a[:, sl], bvec[:, sl]
            mrow = jnp.maximum(m[s], pm[:, sl])
            total = jnp.broadcast_to(a_c[:, L - 1:L], (nh, L))
            mlast = jnp.broadcast_to(mrow[:, L - 1:L], (nh, L))
            rows_ref[ROW_M2:ROW_M2 + nh, sl] = mrow * LOG2E
            rows_ref[ROW_WINTER:ROW_WINTER + nh, sl] = jnp.exp(m[s] - mrow)
            rows_ref[ROW_ENEG:ROW_ENEG + nh, sl] = jnp.exp(-(a_c + mrow))
            rows_ref[ROW_WK:ROW_WK + nh, sl] = jnp.exp(b_c - mlast) * MLSTM_QK_DIM ** -0.5
            rows_ref[ROW_DECAY:ROW_DECAY + nh, sl] = jnp.exp(m[s] - mlast)
            m[s] = total + mlast
            cols_ref[sl, :] = jnp.concatenate([b_c * LOG2E - K_SCALE_LOG2, pad], axis=0).T


def _mlstm_gates(gt, seq):
    n = gt.shape[1]
    return pl.pallas_call(
        functools.partial(_mlstm_gate_kernel, seq=seq),
        grid=(1,),
        in_specs=[pl.BlockSpec((2 * MLSTM_HEADS, n), lambda i: (0, 0))],
        out_specs=[pl.BlockSpec((GATE_ROWS, n), lambda i: (0, 0)),
                   pl.BlockSpec((n, LANES), lambda i: (0, 0))],
        out_shape=[jax.ShapeDtypeStruct((GATE_ROWS, n), F32),
                   jax.ShapeDtypeStruct((n, LANES), F32)],
        compiler_params=pltpu.CompilerParams(
            dimension_semantics=("arbitrary",), vmem_limit_bytes=VMEM_LIMIT_BYTES),
        name="mlstm_gates",
    )(gt)


def _mlstm_kernel(rows_ref, cols_ref, qk_ref, vt_ref, og_ref, hn_ref, o_ref, ct_scr, n_scr):
    L = MLSTM_CHUNK
    nh = MLSTM_HEADS
    contract_lanes = (((1,), (1,)), ((), ()))

    @pl.when(pl.program_id(1) == 0)
    def _():
        ct_scr[...] = jnp.zeros_like(ct_scr)
        n_scr[...] = jnp.zeros_like(n_scr)

    row_i = lax.broadcasted_iota(jnp.int32, (L, L), 0)
    col_i = lax.broadcasted_iota(jnp.int32, (L, L), 1)
    causal_t = row_i <= col_i
    lo_lane = col_i < MLSTM_QK_DIM
    zero = jnp.zeros((), BF16)

    n_chunks = qk_ref.shape[0] // L
    units = [(ci, p, par) for ci in range(n_chunks) for p in range(nh // 2) for par in range(2)]
    ct = [ct_scr[p] for p in range(nh // 2)]
    nn = [n_scr[p] for p in range(nh // 2)]
    gate_rows = {}
    upd = {}

    def chunk_gates(ci):
        if ci not in gate_rows:
            rs = slice(ci * L, (ci + 1) * L)
            wk = rows_ref[ROW_WK:ROW_WK + nh, rs]
            gate_rows[ci] = dict(
                m2=rows_ref[ROW_M2:ROW_M2 + nh, rs], w_inter=rows_ref[ROW_WINTER:ROW_WINTER + nh, rs],
                e_neg=rows_ref[ROW_ENEG:ROW_ENEG + nh, rs], wk=wk,
                decay=rows_ref[ROW_DECAY:ROW_DECAY + nh, rs], b2cols=cols_ref[rs, :],
                wk_b=jnp.concatenate([wk, wk], axis=0).astype(BF16))
        return gate_rows[ci]

    def front(ci, p, par):
        g = chunk_gates(ci)
        rs = slice(ci * L, (ci + 1) * L)
        hh = 2 * p + par
        q2 = qk_ref[rs, p * LANES:(p + 1) * LANES]
        k2 = qk_ref[rs, SZ_MQ + p * LANES:SZ_MQ + (p + 1) * LANES]
        qm = jnp.where(lo_lane, q2, zero) if par == 0 else jnp.where(lo_lane, zero, q2)
        vt = vt_ref[hh * MLSTM_V_DIM:(hh + 1) * MLSTM_V_DIM, rs]
        st = lax.dot_general(k2, qm, contract_lanes, preferred_element_type=F32)
        n2b = jnp.broadcast_to(nn[p], (2 * SUBLANES, LANES)).astype(BF16)
        qn = lax.dot_general(n2b, qm, contract_lanes, preferred_element_type=F32)[0:1, :]
        inter = lax.dot_general(ct[p].astype(BF16), qm, contract_lanes,
                                preferred_element_type=F32)
        vw = (vt.astype(F32) * g["wk"][hh:hh + 1, :]).astype(BF16)
        upd[(ci, p, par)] = jnp.dot(vw, k2, preferred_element_type=F32)
        if par == 1:
            h0, h1 = 2 * p, 2 * p + 1
            dec = jnp.where(lo_lane[0:1, :], g["decay"][h0:h0 + 1, :], g["decay"][h1:h1 + 1, :])
            n_inc = jnp.dot(g["wk_b"], k2, preferred_element_type=F32)
            ct[p] = dec * ct[p] + jnp.where(lo_lane, upd.pop((ci, p, 0)), upd.pop((ci, p, 1)))
            nn[p] = dec * nn[p] + jnp.where(lo_lane[0:1, :], n_inc[h0:h0 + 1, :],
                                            n_inc[h1:h1 + 1, :])
        return st, qn, inter

    def back(ci, p, par, st, qn, inter):
        g = chunk_gates(ci)
        rs = slice(ci * L, (ci + 1) * L)
        hh = 2 * p + par
        vsl = slice(hh * MLSTM_V_DIM, (hh + 1) * MLSTM_V_DIM)
        dt = jnp.where(causal_t,
                       jnp.exp2(g["b2cols"][:, hh:hh + 1] - g["m2"][hh:hh + 1, :]), 0.0)
        sct = st * dt
        wi = g["w_inter"][hh:hh + 1, :]
        den = jnp.sum(sct, axis=0, keepdims=True) + wi * qn
        numt = jnp.dot(vt_ref[vsl, rs], sct.astype(BF16), preferred_element_type=F32) + wi * inter
        rden = 1.0 / jnp.maximum(jnp.abs(den), g["e_neg"][hh:hh + 1, :])
        ms = jnp.mean(numt * numt, axis=0, keepdims=True)
        scale = rden * lax.rsqrt(rden * rden * ms + NORM_EPS)
        hv = (numt * scale).T
        gate = jax.nn.sigmoid(og_ref[rs, vsl].astype(F32))
        o_ref[rs, vsl] = (hv * hn_ref[:, vsl] * gate).astype(BF16)

    pending = [front(*u) for u in units[:MLSTM_LOOKAHEAD]]
    for idx, u in enumerate(units):
        back(*u, *pending[idx])
        pending[idx] = None
        if idx + MLSTM_LOOKAHEAD < len(units):
            pending.append(front(*units[idx + MLSTM_LOOKAHEAD]))
    for p in range(nh // 2):
        ct_scr[p] = ct[p]
        n_scr[p] = nn[p]


def _mlstm(rows, cols, qk, v, og, hn, batch, seq):
    n = qk.shape[0]
    tm = TM_MLSTM
    nt = seq // tm
    row = pl.BlockSpec((tm, 1024), lambda b, c: (b * nt + c, 0))
    return pl.pallas_call(
        _mlstm_kernel,
        grid=(batch, nt),
        in_specs=[
            pl.BlockSpec((GATE_ROWS, tm), lambda b, c: (0, b * nt + c)),
            pl.BlockSpec((tm, LANES), lambda b, c: (b * nt + c, 0)),
            row,
            pl.BlockSpec((SZ_MV, tm), lambda b, c: (0, b * nt + c)),
            row,
            _const_spec((1, 1024)),
        ],
        out_specs=row,
        out_shape=jax.ShapeDtypeStruct((n, 1024), BF16),
        scratch_shapes=[
            pltpu.VMEM((MLSTM_HEADS // 2, 2 * MLSTM_QK_DIM, MLSTM_V_DIM), F32),
            pltpu.VMEM((MLSTM_HEADS // 2, 1, 2 * MLSTM_QK_DIM), F32),
        ],
        compiler_params=pltpu.CompilerParams(
            dimension_semantics=("arbitrary", "arbitrary"), vmem_limit_bytes=VMEM_LIMIT_BYTES),
        name="mlstm",
    )(rows, cols, qk, v, og, hn)


def _merge_kernel(x_ref, a_ref, hm_ref, gate_ref, wpa_ref, wpm_ref, wo_ref, o_ref):
    ya = jnp.dot(a_ref[...], wpa_ref[...], preferred_element_type=F32)
    ym = jnp.dot(hm_ref[...], wpm_ref[...], preferred_element_type=F32)
    ga = jax.nn.sigmoid(gate_ref[:, 0:D_MODEL].astype(F32))
    gm = jax.nn.sigmoid(gate_ref[:, D_MODEL:2 * D_MODEL].astype(F32))
    merged = (ga * ya + gm * ym).astype(BF16)
    o_ref[...] = x_ref[...] + jnp.dot(merged, wo_ref[...], preferred_element_type=F32)


def _merge(x, a, hm, gates, wpa, wpm, wo):
    n = x.shape[0]
    tm = TM_MERGE
    row = lambda width: pl.BlockSpec((tm, width), lambda i: (i, 0))
    return pl.pallas_call(
        _merge_kernel,
        grid=(n // tm,),
        in_specs=[row(D_MODEL), row(1024), row(1024), row(2048),
                  _const_spec((1024, D_MODEL)), _const_spec((1024, D_MODEL)),
                  _const_spec((D_MODEL, D_MODEL))],
        out_specs=row(D_MODEL),
        out_shape=jax.ShapeDtypeStruct((n, D_MODEL), F32),
        compiler_params=pltpu.CompilerParams(
            dimension_semantics=("arbitrary",), vmem_limit_bytes=VMEM_LIMIT_BYTES),
        name="merge",
    )(x, a, hm, gates, wpa, wpm, wo)


def _to_head_order(t, axis):
    shape = t.shape
    t = t.reshape(shape[:axis] + (2, 2, ATTN_GROUP) + shape[axis + 1:])
    t = jnp.swapaxes(t, axis + 1, axis + 2)
    return t.reshape(shape)


def _pack_inproj(w_in, b_in):
    o = 0
    parts = {}
    for name, size in (("aq", SZ_AQ), ("ak", SZ_AK), ("av", SZ_AV), ("mq", SZ_MQ), ("mk", SZ_MK),
                       ("mv", SZ_MV), ("mo", SZ_MO), ("mi", SZ_MI), ("mf", SZ_MF),
                       ("g", 2 * D_MODEL)):
        parts[name] = (w_in[:, o:o + size], b_in[o:o + size])
        o += size

    def pair_heads(t):
        lead = t.shape[:-1]
        return _to_head_order(t.reshape(lead + (ATTN_HEADS, ATTN_HEAD_DIM)), len(lead)).reshape(
            lead + (SZ_AQ,))

    order = [tuple(pair_heads(t) for t in parts["aq"]), parts["ak"],
             parts["mq"], parts["mk"], parts["mo"], parts["g"]]
    w = jnp.concatenate([t[0] for t in order], axis=1).astype(BF16)
    b = jnp.concatenate([t[1] for t in order], axis=0).reshape(1, PACKED_WIDTH)
    wif = jnp.concatenate([parts["mi"][0], parts["mf"][0]], axis=1).T.astype(BF16)
    bif = jnp.concatenate([parts["mi"][1], parts["mf"][1]], axis=0).reshape(2 * MLSTM_HEADS, 1)
    wvt = jnp.concatenate([parts["mv"][0], parts["av"][0]], axis=1).T.astype(BF16)
    bvt = jnp.concatenate([parts["mv"][1], parts["av"][1]], axis=0).reshape(SZ_MV + SZ_AV, 1)
    return w, b, wif, bif, wvt, bvt


def kernel(x, ffn1_norm, ffn1_w_gate, ffn1_w_up, ffn1_w_down, mix_norm, w_in, b_in, attn_sinks,
           mlstm_conv, mlstm_head_norm, w_proj_attn, w_proj_mlstm, w_out, ffn2_norm, ffn2_w_gate,
           ffn2_w_up, ffn2_w_down, final_norm):
    batch, seq, d = x.shape
    assert d == D_MODEL and seq % TM_INPROJ == 0 and (batch * seq) % TM_FFN == 0
    assert ffn1_norm.shape[0] == 1, "one layer"
    n = batch * seq
    xf = x.reshape(n, d)
    fin = final_norm.reshape(1, d)

    x1 = _ffn(xf, ffn1_norm[0].reshape(1, d), ffn1_w_gate[0], ffn1_w_up[0], ffn1_w_down[0], fin,
              apply_final=False)

    w, b, wif, bif, wvt, bvt = _pack_inproj(w_in[0], b_in[0])
    q, kk, avt, mqk, mvt, mo, gates, gt = _inproj(x1, mix_norm[0].reshape(1, d), w, b, wif, bif,
                                                  wvt, bvt, mlstm_conv[0], seq)

    ya = _attention(attn_sinks[0], q, kk, avt, batch, seq)

    rows, cols = _mlstm_gates(gt, seq)
    hm = _mlstm(rows, cols, mqk, mvt, mo, mlstm_head_norm[0].reshape(1, SZ_MV), batch, seq)

    wpa = _to_head_order(w_proj_attn[0].reshape(ATTN_HEADS, ATTN_HEAD_DIM * d), 0).reshape(SZ_AQ, d)
    x2 = _merge(x1, ya, hm, gates, wpa.astype(BF16), w_proj_mlstm[0].astype(BF16),
                w_out[0].astype(BF16))

    out = _ffn(x2, ffn2_norm[0].reshape(1, d), ffn2_w_gate[0], ffn2_w_up[0], ffn2_w_down[0], fin,
               apply_final=True)
    return out.reshape(batch, seq, d)
```

```python
import functools

import jax
import jax.numpy as jnp
from jax import lax
from jax.experimental import pallas as pl
from jax.experimental.pallas import tpu as pltpu

F32 = jnp.float32
BF16 = jnp.bfloat16

D_MODEL = 1024
ATTN_HEAD_DIM = 64
ATTN_HEADS = 16
ATTN_KV_HEADS = 4
ATTN_GROUP = 4
WINDOW = 128
MLSTM_HEADS = 8
MLSTM_V_DIM = 128
MLSTM_QK_DIM = 64
CONV_WIDTH = 4
GATE_SOFTCAP = 15.0
D_FF = 2816
FFN_RESIDUAL_WEIGHT = 0.5
NORM_EPS = 1e-6

SZ_AQ, SZ_AK, SZ_AV = 1024, 256, 256
SZ_MQ, SZ_MK, SZ_MV, SZ_MO = 512, 512, 1024, 1024
SZ_MI, SZ_MF = 8, 8

LANES = 128
SUBLANES = 8
MXU_DIM = 256
VMEM_LIMIT_BYTES = 56 * 1024 * 1024

MLSTM_CHUNK = 128
FF_CHUNK = MXU_DIM
TM_FFN = 1024
TM_INPROJ = 1024
TM_MERGE = 1024
TM_MLSTM = 8 * MLSTM_CHUNK
ATTN_LOOKAHEAD = 4
ATTN_BLOCKS = 8
MLSTM_LOOKAHEAD = 8

LOG2E = 1.4426950408889634
K_SCALE_LOG2 = 3.0
ROW_M2, ROW_WINTER, ROW_ENEG, ROW_WK, ROW_DECAY = 0, 8, 16, 24, 32
GATE_ROWS = 40

COL_Q = 0
COL_KK = 1024
COL_MQK = 1280
COL_MO = 2304
COL_G = 3328
PACKED_WIDTH = 5376

ATTN_HEAD_ORDER = tuple((2 * (t // 4) + par) * ATTN_GROUP + t % 4
                        for t in range(ATTN_HEADS // 2) for par in range(2))


def _rms(x, g):
    return x * lax.rsqrt(jnp.mean(x * x, axis=-1, keepdims=True) + NORM_EPS) * g


def _const_spec(shape):
    zeros = (0,) * len(shape)
    return pl.BlockSpec(shape, lambda *_: zeros, pipeline_mode=pl.Buffered(1))


def _ffn_stage_weights(wg_hbm, wu_hbm, wd_hbm, wg_ref, wu_ref, wd_ref, gu_stage, d_stage, sem):
    n_chunks = D_FF // FF_CHUNK

    def copies(c, slot):
        cols = pl.ds(c * FF_CHUNK, FF_CHUNK)
        return (pltpu.make_async_copy(wg_hbm.at[:, cols], gu_stage.at[slot, 0], sem.at[slot, 0]),
                pltpu.make_async_copy(wu_hbm.at[:, cols], gu_stage.at[slot, 1], sem.at[slot, 1]),
                pltpu.make_async_copy(wd_hbm.at[cols, :], d_stage.at[slot], sem.at[slot, 2]))

    for cp in copies(0, 0):
        cp.start()
    for c in range(n_chunks):
        slot = c % 2
        if c + 1 < n_chunks:
            for cp in copies(c + 1, 1 - slot):
                cp.start()
        for cp in copies(c, slot):
            cp.wait()
        sl = slice(c * FF_CHUNK, (c + 1) * FF_CHUNK)
        wg_ref[:, sl] = gu_stage[slot, 0].astype(BF16)
        wu_ref[:, sl] = gu_stage[slot, 1].astype(BF16)
        wd_ref[sl, :] = d_stage[slot].astype(BF16)


def _ffn_kernel(x_ref, g_ref, wg_hbm, wu_hbm, wd_hbm, fin_ref, o_ref,
                wg_ref, wu_ref, wd_ref, gu_stage, d_stage, sem, acc_ref, *, apply_final):
    @pl.when(pl.program_id(0) == 0)
    def _():
        _ffn_stage_weights(wg_hbm, wu_hbm, wd_hbm, wg_ref, wu_ref, wd_ref, gu_stage, d_stage, sem)

    x = x_ref[...]
    h = _rms(x, g_ref[...]).astype(BF16)
    for c in range(D_FF // FF_CHUNK):
        sl = slice(c * FF_CHUNK, (c + 1) * FF_CHUNK)
        a = jnp.dot(h, wg_ref[:, sl], preferred_element_type=F32)
        u = jnp.dot(h, wu_ref[:, sl], preferred_element_type=F32)
        act = (a * jax.nn.sigmoid(a) * u).astype(BF16)
        d = jnp.dot(act, wd_ref[sl, :], preferred_element_type=F32)
        if c == 0:
            acc_ref[...] = d
        else:
            acc_ref[...] += d
    y = x + FFN_RESIDUAL_WEIGHT * acc_ref[...]
    if apply_final:
        y = _rms(y, fin_ref[...])
    o_ref[...] = y


def _ffn(x, g, wg, wu, wd, fin, apply_final):
    n = x.shape[0]
    tm = TM_FFN
    return pl.pallas_call(
        functools.partial(_ffn_kernel, apply_final=apply_final),
        grid=(n // tm,),
        in_specs=[
            pl.BlockSpec((tm, D_MODEL), lambda i: (i, 0)),
            _const_spec((1, D_MODEL)),
            pl.BlockSpec(memory_space=pl.ANY),
            pl.BlockSpec(memory_space=pl.ANY),
            pl.BlockSpec(memory_space=pl.ANY),
            _const_spec((1, D_MODEL)),
        ],
        out_specs=pl.BlockSpec((tm, D_MODEL), lambda i: (i, 0)),
        out_shape=jax.ShapeDtypeStruct((n, D_MODEL), F32),
        scratch_shapes=[
            pltpu.VMEM((D_MODEL, D_FF), BF16),
            pltpu.VMEM((D_MODEL, D_FF), BF16),
            pltpu.VMEM((D_FF, D_MODEL), BF16),
            pltpu.VMEM((2, 2, D_MODEL, FF_CHUNK), F32),
            pltpu.VMEM((2, FF_CHUNK, D_MODEL), F32),
            pltpu.SemaphoreType.DMA((2, 3)),
            pltpu.VMEM((tm, D_MODEL), F32),
        ],
        compiler_params=pltpu.CompilerParams(
            dimension_semantics=("arbitrary",), vmem_limit_bytes=VMEM_LIMIT_BYTES),
        name="ffn_final" if apply_final else "ffn",
    )(x, g, wg, wu, wd, fin)


def _inproj_kernel(x_ref, g_ref, w_ref, b_ref, wif_ref, bif_ref, wvt_ref, bvt_ref, conv_ref,
                   q_ref, kk_ref, avt_ref, mqk_ref, mvt_ref, mo_ref, gate_ref, gt_ref,
                   conv_scr, *, tiles_per_seq):
    i = pl.program_id(0)
    tm = x_ref.shape[0]
    h = _rms(x_ref[...], g_ref[...]).astype(BF16)

    def seg(c0, width):
        return (jnp.dot(h, w_ref[:, c0:c0 + width], preferred_element_type=F32)
                + b_ref[:, c0:c0 + width])

    def seg_t(wt_ref, bt_ref, r0, rows):
        return (lax.dot_general(wt_ref[r0:r0 + rows, :], h, (((1,), (1,)), ((), ())),
                                preferred_element_type=F32) + bt_ref[r0:r0 + rows, :])

    cw = 512

    @pl.when(i == 0)
    def _():
        conv_scr[...] = jnp.zeros_like(conv_scr)

    carry = jnp.where(i % tiles_per_seq == 0, 0.0, conv_scr[...])

    def conv_slab(c, width):
        z = seg(COL_MQK + c, width)
        conv_scr[:, c:c + width] = z[tm - SUBLANES:tm, :]
        zc = jnp.concatenate([carry[:, c:c + width], z], axis=0)
        acc = z * conv_ref[CONV_WIDTH - 1:CONV_WIDTH, c:c + width]
        for k in range(1, CONV_WIDTH):
            zk = pltpu.roll(zc, k, 0)[SUBLANES:, :]
            acc = acc + zk * conv_ref[CONV_WIDTH - 1 - k:CONV_WIDTH - k, c:c + width]
        mqk_ref[:, c:c + width] = (acc * jax.nn.sigmoid(acc)).astype(BF16)

    dw = MXU_DIM
    for k in range(1024 // dw):
        sl = slice(k * dw, (k + 1) * dw)
        conv_slab(k * dw, dw)
        q_ref[:, sl] = (seg(COL_Q + k * dw, dw) * (LOG2E * ATTN_HEAD_DIM ** -0.5)).astype(BF16)
        if k == 0:
            kk_ref[...] = seg(COL_KK, SZ_AK).astype(BF16)
        elif k == 1:
            avt_ref[...] = seg_t(wvt_ref, bvt_ref, SZ_MV, SZ_AV).astype(BF16)
        mvt_ref[sl, :] = seg_t(wvt_ref, bvt_ref, k * dw, dw).astype(BF16)
        mo_ref[:, sl] = seg(COL_MO + k * dw, dw).astype(BF16)
        gate_ref[:, 2 * k * dw:(2 * k + 2) * dw] = seg(COL_G + 2 * k * dw, 2 * dw).astype(BF16)

    gt_ref[...] = seg_t(wif_ref, bif_ref, 0, 2 * MLSTM_HEADS)


def _inproj(x, g, w, b, wif, bif, wvt, bvt, conv, seq):
    n = x.shape[0]
    tm = TM_INPROJ
    row = lambda width: pl.BlockSpec((tm, width), lambda i: (i, 0))
    col = lambda height: pl.BlockSpec((height, tm), lambda i: (0, i))
    bf = lambda width: jax.ShapeDtypeStruct((n, width), BF16)
    return pl.pallas_call(
        functools.partial(_inproj_kernel, tiles_per_seq=seq // tm),
        grid=(n // tm,),
        in_specs=[
            row(D_MODEL),
            _const_spec((1, D_MODEL)),
            _const_spec((D_MODEL, PACKED_WIDTH)),
            _const_spec((1, PACKED_WIDTH)),
            _const_spec((2 * MLSTM_HEADS, D_MODEL)),
            _const_spec((2 * MLSTM_HEADS, 1)),
            _const_spec((SZ_MV + SZ_AV, D_MODEL)),
            _const_spec((SZ_MV + SZ_AV, 1)),
            _const_spec((CONV_WIDTH, 1024)),
        ],
        out_specs=[row(1024), row(SZ_AK), col(SZ_AV), row(1024), col(SZ_MV), row(1024),
                   row(2048), col(2 * MLSTM_HEADS)],
        out_shape=[bf(1024), bf(SZ_AK), jax.ShapeDtypeStruct((SZ_AV, n), BF16), bf(1024),
                   jax.ShapeDtypeStruct((SZ_MV, n), BF16), bf(1024), bf(2048),
                   jax.ShapeDtypeStruct((2 * MLSTM_HEADS, n), F32)],
        scratch_shapes=[pltpu.VMEM((SUBLANES, 1024), F32)],
        compiler_params=pltpu.CompilerParams(
            dimension_semantics=("arbitrary",), vmem_limit_bytes=VMEM_LIMIT_BYTES),
        name="inproj",
    )(x, g, w, b, wif, bif, wvt, bvt, conv)


def _attn_kernel(sink_ref, q_ref, kc_ref, kp_ref, vtc_ref, vtp_ref, o_ref, k_scr, vt_scr):
    w = WINDOW
    contract_lanes = (((1,), (1,)), ((), ()))
    k_scr[0:w, :] = kp_ref[...]
    k_scr[w:, :] = kc_ref[...]
    vt_scr[:, 0:w] = vtp_ref[...]
    vt_scr[:, w:] = vtc_ref[...]
    kj = lax.broadcasted_iota(jnp.int32, (2 * w, w), 0)
    qi = lax.broadcasted_iota(jnp.int32, (2 * w, w), 1)
    band = (kj > qi) & (kj <= qi + w)
    band_first = band & ((kj >= w) | (pl.program_id(1) > 0))
    lo = lax.broadcasted_iota(jnp.int32, (w, LANES), 1) < ATTN_HEAD_DIM
    zero = jnp.zeros((), BF16)
    pairs = [(jb, t) for jb in range(q_ref.shape[0] // w) for t in range(ATTN_HEADS // 2)]

    def scores(jb, t):
        q2 = q_ref[jb * w:(jb + 1) * w, t * LANES:(t + 1) * LANES]
        qm2 = jnp.concatenate([jnp.where(lo, q2, zero), jnp.where(lo, zero, q2)], axis=0)
        kk = k_scr[jb * w:(jb + 2) * w, (t // 4) * LANES:(t // 4 + 1) * LANES]
        return lax.dot_general(kk, qm2, contract_lanes, preferred_element_type=F32)

    def head_out(jb, t, par, s):
        s = jnp.where(band_first if jb == 0 else band, s, -jnp.inf)
        sink = sink_ref[ATTN_HEAD_ORDER[2 * t + par]] * LOG2E
        mx = jnp.maximum(jnp.max(s, axis=0, keepdims=True), sink)
        pr = jnp.exp2(s - mx)
        denom = jnp.sum(pr, axis=0, keepdims=True) + jnp.exp2(sink - mx)
        kvh = 2 * (t // 4) + par
        vt = vt_scr[kvh * ATTN_HEAD_DIM:(kvh + 1) * ATTN_HEAD_DIM, jb * w:(jb + 2) * w]
        return jnp.dot(vt, pr.astype(BF16), preferred_element_type=F32) * (1.0 / denom)

    pending = [scores(*u) for u in pairs[:ATTN_LOOKAHEAD]]
    for idx, (jb, t) in enumerate(pairs):
        s2 = pending[idx]
        pending[idx] = None
        outs = [head_out(jb, t, par, s2[:, par * w:(par + 1) * w]) for par in range(2)]
        if idx + ATTN_LOOKAHEAD < len(pairs):
            pending.append(scores(*pairs[idx + ATTN_LOOKAHEAD]))
        pair = jnp.concatenate(outs, axis=0)
        o_ref[jb * w:(jb + 1) * w, t * LANES:(t + 1) * LANES] = pair.T.astype(BF16)


def _attention(sinks, q, kk, vt, batch, seq):
    n = q.shape[0]
    tq = ATTN_BLOCKS * WINDOW
    nt = seq // tq
    cur = lambda b, j: b * nt + j
    prev = lambda b, j: (b * nt + j) * ATTN_BLOCKS - jnp.minimum(j, 1)
    return pl.pallas_call(
        _attn_kernel,
        grid=(batch, nt),
        in_specs=[
            pl.BlockSpec(memory_space=pltpu.SMEM),
            pl.BlockSpec((tq, 1024), lambda b, j: (cur(b, j), 0)),
            pl.BlockSpec((tq, SZ_AK), lambda b, j: (cur(b, j), 0)),
            pl.BlockSpec((WINDOW, SZ_AK), lambda b, j: (prev(b, j), 0)),
            pl.BlockSpec((SZ_AV, tq), lambda b, j: (0, cur(b, j))),
            pl.BlockSpec((SZ_AV, WINDOW), lambda b, j: (0, prev(b, j))),
        ],
        out_specs=pl.BlockSpec((tq, 1024), lambda b, j: (cur(b, j), 0)),
        out_shape=jax.ShapeDtypeStruct((n, 1024), BF16),
        scratch_shapes=[pltpu.VMEM((WINDOW + tq, SZ_AK), BF16),
                        pltpu.VMEM((SZ_AV, WINDOW + tq), BF16)],
        compiler_params=pltpu.CompilerParams(
            dimension_semantics=("arbitrary", "arbitrary"), vmem_limit_bytes=VMEM_LIMIT_BYTES),
        name="attention",
    )(sinks, q, kk, kk, vt, vt)


def _log_sigmoid(x):
    return -(jnp.maximum(-x, 0.0) + jnp.log(1.0 + jnp.exp(-jnp.abs(x))))


def _mlstm_gate_kernel(g_ref, rows_ref, cols_ref, *, seq):
    L = MLSTM_CHUNK
    nh = MLSTM_HEADS
    n = g_ref.shape[1]
    g = g_ref[...]
    ig = GATE_SOFTCAP * jnp.tanh(g[0:nh] / GATE_SOFTCAP)
    fg = GATE_SOFTCAP * jnp.tanh(g[nh:2 * nh] / GATE_SOFTCAP)
    lf = _log_sigmoid(fg)
    pos = lax.broadcasted_iota(jnp.int32, (nh, n), 1) & (L - 1)
    a = lf
    sh = 1
    while sh < L:
        a = a + jnp.where(pos >= sh, pltpu.roll(a, sh, 1), 0.0)
        sh *= 2
    bvec = ig - a
    pm = bvec
    sh = 1
    while sh < L:
        pm = jnp.maximum(pm, jnp.where(pos >= sh, pltpu.roll(pm, sh, 1), -jnp.inf))
        sh *= 2
    pad = jnp.zeros((L - nh, L), F32)
    m = [jnp.zeros((nh, L), F32) for _ in range(n // seq)]
    for c in range(seq // L):
        for s in range(n // seq):
            sl = slice(s * seq + c * L, s * seq + (c + 1) * L)
            a_c, b_c = a[:, sl], bvec[:, sl]
            mrow = jnp.maximum(m[s], pm[:, sl])
            total = jnp.broadcast_to(a_c[:, L - 1:L], (nh, L))
            mlast = jnp.broadcast_to(mrow[:, L - 1:L], (nh, L))
            rows_ref[ROW_M2:ROW_M2 + nh, sl] = mrow * LOG2E
            rows_ref[ROW_WINTER:ROW_WINTER + nh, sl] = jnp.exp(m[s] - mrow)
            rows_ref[ROW_ENEG:ROW_ENEG + nh, sl] = jnp.exp(-(a_c + mrow))
            rows_ref[ROW_WK:ROW_WK + nh, sl] = jnp.exp(b_c - mlast) * MLSTM_QK_DIM ** -0.5
            rows_ref[ROW_DECAY:ROW_DECAY + nh, sl] = jnp.exp(m[s] - mlast)
            m[s] = total + mlast
            cols_ref[sl, :] = jnp.concatenate([b_c * LOG2E - K_SCALE_LOG2, pad], axis=0).T


def _mlstm_gates(gt, seq):
    n = gt.shape[1]
    return pl.pallas_call(
        functools.partial(_mlstm_gate_kernel, seq=seq),
        grid=(1,),
        in_specs=[pl.BlockSpec((2 * MLSTM_HEADS, n), lambda i: (0, 0))],
        out_specs=[pl.BlockSpec((GATE_ROWS, n), lambda i: (0, 0)),
                   pl.BlockSpec((n, LANES), lambda i: (0, 0))],
        out_shape=[jax.ShapeDtypeStruct((GATE_ROWS, n), F32),
                   jax.ShapeDtypeStruct((n, LANES), F32)],
        compiler_params=pltpu.CompilerParams(
            dimension_semantics=("arbitrary",), vmem_limit_bytes=VMEM_LIMIT_BYTES),
        name="mlstm_gates",
    )(gt)


def _mlstm_kernel(rows_ref, cols_ref, qk_ref, vt_ref, og_ref, hn_ref, o_ref, ct_scr, n_scr):
    L = MLSTM_CHUNK
    nh = MLSTM_HEADS
    contract_lanes = (((1,), (1,)), ((), ()))

    @pl.when(pl.program_id(1) == 0)
    def _():
        ct_scr[...] = jnp.zeros_like(ct_scr)
        n_scr[...] = jnp.zeros_like(n_scr)

    row_i = lax.broadcasted_iota(jnp.int32, (L, L), 0)
    col_i = lax.broadcasted_iota(jnp.int32, (L, L), 1)
    causal_t = row_i <= col_i
    lo_lane = col_i < MLSTM_QK_DIM
    zero = jnp.zeros((), BF16)

    n_chunks = qk_ref.shape[0] // L
    units = [(ci, p, par) for ci in range(n_chunks) for p in range(nh // 2) for par in range(2)]
    ct = [ct_scr[p] for p in range(nh // 2)]
    nn = [n_scr[p] for p in range(nh // 2)]
    gate_rows = {}
    upd = {}

    def chunk_gates(ci):
        if ci not in gate_rows:
            rs = slice(ci * L, (ci + 1) * L)
            wk = rows_ref[ROW_WK:ROW_WK + nh, rs]
            gate_rows[ci] = dict(
                m2=rows_ref[ROW_M2:ROW_M2 + nh, rs], w_inter=rows_ref[ROW_WINTER:ROW_WINTER + nh, rs],
                e_neg=rows_ref[ROW_ENEG:ROW_ENEG + nh, rs], wk=wk,
                decay=rows_ref[ROW_DECAY:ROW_DECAY + nh, rs], b2cols=cols_ref[rs, :],
                wk_b=jnp.concatenate([wk, wk], axis=0).astype(BF16))
        return gate_rows[ci]

    def front(ci, p, par):
        g = chunk_gates(ci)
        rs = slice(ci * L, (ci + 1) * L)
        hh = 2 * p + par
        q2 = qk_ref[rs, p * LANES:(p + 1) * LANES]
        k2 = qk_ref[rs, SZ_MQ + p * LANES:SZ_MQ + (p + 1) * LANES]
        qm = jnp.where(lo_lane, q2, zero) if par == 0 else jnp.where(lo_lane, zero, q2)
        vt = vt_ref[hh * MLSTM_V_DIM:(hh + 1) * MLSTM_V_DIM, rs]
        st = lax.dot_general(k2, qm, contract_lanes, preferred_element_type=F32)
        n2b = jnp.broadcast_to(nn[p], (2 * SUBLANES, LANES)).astype(BF16)
        qn = lax.dot_general(n2b, qm, contract_lanes, preferred_element_type=F32)[0:1, :]
        inter = lax.dot_general(ct[p].astype(BF16), qm, contract_lanes,
                                preferred_element_type=F32)
        vw = (vt.astype(F32) * g["wk"][hh:hh + 1, :]).astype(BF16)
        upd[(ci, p, par)] = jnp.dot(vw, k2, preferred_element_type=F32)
        if par == 1:
            h0, h1 = 2 * p, 2 * p + 1
            dec = jnp.where(lo_lane[0:1, :], g["decay"][h0:h0 + 1, :], g["decay"][h1:h1 + 1, :])
            n_inc = jnp.dot(g["wk_b"], k2, preferred_element_type=F32)
            ct[p] = dec * ct[p] + jnp.where(lo_lane, upd.pop((ci, p, 0)), upd.pop((ci, p, 1)))
            nn[p] = dec * nn[p] + jnp.where(lo_lane[0:1, :], n_inc[h0:h0 + 1, :],
                                            n_inc[h1:h1 + 1, :])
        return st, qn, inter

    def back(ci, p, par, st, qn, inter):
        g = chunk_gates(ci)
        rs = slice(ci * L, (ci + 1) * L)
        hh = 2 * p + par
        vsl = slice(hh * MLSTM_V_DIM, (hh + 1) * MLSTM_V_DIM)
        dt = jnp.where(causal_t,
                       jnp.exp2(g["b2cols"][:, hh:hh + 1] - g["m2"][hh:hh + 1, :]), 0.0)
        sct = st * dt
        wi = g["w_inter"][hh:hh + 1, :]
        den = jnp.sum(sct, axis=0, keepdims=True) + wi * qn
        numt = jnp.dot(vt_ref[vsl, rs], sct.astype(BF16), preferred_element_type=F32) + wi * inter
        rden = 1.0 / jnp.maximum(jnp.abs(den), g["e_neg"][hh:hh + 1, :])
        ms = jnp.mean(numt * numt, axis=0, keepdims=True)
        scale = rden * lax.rsqrt(rden * rden * ms + NORM_EPS)
        hv = (numt * scale).T
        gate = jax.nn.sigmoid(og_ref[rs, vsl].astype(F32))
        o_ref[rs, vsl] = (hv * hn_ref[:, vsl] * gate).astype(BF16)

    pending = [front(*u) for u in units[:MLSTM_LOOKAHEAD]]
    for idx, u in enumerate(units):
        back(*u, *pending[idx])
        pending[idx] = None
        if idx + MLSTM_LOOKAHEAD < len(units):
            pending.append(front(*units[idx + MLSTM_LOOKAHEAD]))
    for p in range(nh // 2):
        ct_scr[p] = ct[p]
        n_scr[p] = nn[p]


def _mlstm(rows, cols, qk, v, og, hn, batch, seq):
    n = qk.shape[0]
    tm = TM_MLSTM
    nt = seq // tm
    row = pl.BlockSpec((tm, 1024), lambda b, c: (b * nt + c, 0))
    return pl.pallas_call(
        _mlstm_kernel,
        grid=(batch, nt),
        in_specs=[
            pl.BlockSpec((GATE_ROWS, tm), lambda b, c: (0, b * nt + c)),
            pl.BlockSpec((tm, LANES), lambda b, c: (b * nt + c, 0)),
            row,
            pl.BlockSpec((SZ_MV, tm), lambda b, c: (0, b * nt + c)),
            row,
            _const_spec((1, 1024)),
        ],
        out_specs=row,
        out_shape=jax.ShapeDtypeStruct((n, 1024), BF16),
        scratch_shapes=[
            pltpu.VMEM((MLSTM_HEADS // 2, 2 * MLSTM_QK_DIM, MLSTM_V_DIM), F32),
            pltpu.VMEM((MLSTM_HEADS // 2, 1, 2 * MLSTM_QK_DIM), F32),
        ],
        compiler_params=pltpu.CompilerParams(
            dimension_semantics=("arbitrary", "arbitrary"), vmem_limit_bytes=VMEM_LIMIT_BYTES),
        name="mlstm",
    )(rows, cols, qk, v, og, hn)


def _merge_stage_weights(wpa_hbm, wpm_hbm, wo_hbm, wpa_ref, wpm_ref, wo_ref, stage, sem):
    hd = ATTN_HEAD_DIM
    wpa_copies = [pltpu.make_async_copy(wpa_hbm.at[pl.ds(head * hd, hd), :],
                                        stage.at[0, pl.ds(pos * hd, hd), :], sem.at[0])
                  for pos, head in enumerate(ATTN_HEAD_ORDER)]
    wpm_copy = pltpu.make_async_copy(wpm_hbm, stage.at[1], sem.at[1])
    wo_copy = pltpu.make_async_copy(wo_hbm, stage.at[0], sem.at[0])
    for cp in wpa_copies:
        cp.start()
    wpm_copy.start()
    for cp in wpa_copies:
        cp.wait()
    wpa_ref[...] = stage[0].astype(BF16)
    wo_copy.start()
    wpm_copy.wait()
    wpm_ref[...] = stage[1].astype(BF16)
    wo_copy.wait()
    wo_ref[...] = stage[0].astype(BF16)


def _merge_kernel(x_ref, a_ref, hm_ref, gate_ref, wpa_hbm, wpm_hbm, wo_hbm, o_ref,
                  wpa_ref, wpm_ref, wo_ref, stage, sem):
    @pl.when(pl.program_id(0) == 0)
    def _():
        _merge_stage_weights(wpa_hbm, wpm_hbm, wo_hbm, wpa_ref, wpm_ref, wo_ref, stage, sem)

    ya = jnp.dot(a_ref[...], wpa_ref[...], preferred_element_type=F32)
    ym = jnp.dot(hm_ref[...], wpm_ref[...], preferred_element_type=F32)
    ga = jax.nn.sigmoid(gate_ref[:, 0:D_MODEL].astype(F32))
    gm = jax.nn.sigmoid(gate_ref[:, D_MODEL:2 * D_MODEL].astype(F32))
    merged = (ga * ya + gm * ym).astype(BF16)
    o_ref[...] = x_ref[...] + jnp.dot(merged, wo_ref[...], preferred_element_type=F32)


def _merge(x, a, hm, gates, wpa, wpm, wo):
    n = x.shape[0]
    tm = TM_MERGE
    row = lambda width: pl.BlockSpec((tm, width), lambda i: (i, 0))
    return pl.pallas_call(
        _merge_kernel,
        grid=(n // tm,),
        in_specs=[row(D_MODEL), row(1024), row(1024), row(2048),
                  pl.BlockSpec(memory_space=pl.ANY), pl.BlockSpec(memory_space=pl.ANY),
                  pl.BlockSpec(memory_space=pl.ANY)],
        out_specs=row(D_MODEL),
        out_shape=jax.ShapeDtypeStruct((n, D_MODEL), F32),
        scratch_shapes=[pltpu.VMEM((D_MODEL, D_MODEL), BF16)] * 3 + [
            pltpu.VMEM((2, D_MODEL, D_MODEL), F32), pltpu.SemaphoreType.DMA((2,))],
        compiler_params=pltpu.CompilerParams(
            dimension_semantics=("arbitrary",), vmem_limit_bytes=VMEM_LIMIT_BYTES),
        name="merge",
    )(x, a, hm, gates, wpa, wpm, wo)


def _to_head_order(t, axis):
    shape = t.shape
    t = t.reshape(shape[:axis] + (2, 2, ATTN_GROUP) + shape[axis + 1:])
    t = jnp.swapaxes(t, axis + 1, axis + 2)
    return t.reshape(shape)


def _pack_inproj(w_in, b_in):
    o = 0
    parts = {}
    for name, size in (("aq", SZ_AQ), ("ak", SZ_AK), ("av", SZ_AV), ("mq", SZ_MQ), ("mk", SZ_MK),
                       ("mv", SZ_MV), ("mo", SZ_MO), ("mi", SZ_MI), ("mf", SZ_MF),
                       ("g", 2 * D_MODEL)):
        parts[name] = (w_in[:, o:o + size], b_in[o:o + size])
        o += size

    def pair_heads(t):
        lead = t.shape[:-1]
        return _to_head_order(t.reshape(lead + (ATTN_HEADS, ATTN_HEAD_DIM)), len(lead)).reshape(
            lead + (SZ_AQ,))

    order = [tuple(pair_heads(t) for t in parts["aq"]), parts["ak"],
             parts["mq"], parts["mk"], parts["mo"], parts["g"]]
    w = jnp.concatenate([t[0].astype(BF16) for t in order], axis=1)
    b = jnp.concatenate([t[1] for t in order], axis=0).reshape(1, PACKED_WIDTH)
    wif = jnp.concatenate([parts["mi"][0], parts["mf"][0]], axis=1).T.astype(BF16)
    bif = jnp.concatenate([parts["mi"][1], parts["mf"][1]], axis=0).reshape(2 * MLSTM_HEADS, 1)
    wvt = jnp.concatenate([parts["mv"][0].astype(BF16), parts["av"][0].astype(BF16)],
                          axis=1).T
    bvt = jnp.concatenate([parts["mv"][1], parts["av"][1]], axis=0).reshape(SZ_MV + SZ_AV, 1)
    return w, b, wif, bif, wvt, bvt


def kernel(x, ffn1_norm, ffn1_w_gate, ffn1_w_up, ffn1_w_down, mix_norm, w_in, b_in, attn_sinks,
           mlstm_conv, mlstm_head_norm, w_proj_attn, w_proj_mlstm, w_out, ffn2_norm, ffn2_w_gate,
           ffn2_w_up, ffn2_w_down, final_norm):
    batch, seq, d = x.shape
    assert d == D_MODEL and seq % TM_INPROJ == 0 and (batch * seq) % TM_FFN == 0
    assert ffn1_norm.shape[0] == 1, "one layer"
    n = batch * seq
    xf = x.reshape(n, d)
    fin = final_norm.reshape(1, d)

    x1 = _ffn(xf, ffn1_norm[0].reshape(1, d), ffn1_w_gate[0], ffn1_w_up[0], ffn1_w_down[0], fin,
              apply_final=False)

    w, b, wif, bif, wvt, bvt = _pack_inproj(w_in[0], b_in[0])
    q, kk, avt, mqk, mvt, mo, gates, gt = _inproj(x1, mix_norm[0].reshape(1, d), w, b, wif, bif,
                                                  wvt, bvt, mlstm_conv[0], seq)

    ya = _attention(attn_sinks[0], q, kk, avt, batch, seq)

    rows, cols = _mlstm_gates(gt, seq)
    hm = _mlstm(rows, cols, mqk, mvt, mo, mlstm_head_norm[0].reshape(1, SZ_MV), batch, seq)

    x2 = _merge(x1, ya, hm, gates, w_proj_attn[0], w_proj_mlstm[0], w_out[0])

    out = _ffn(x2, ffn2_norm[0].reshape(1, d), ffn2_w_gate[0], ffn2_w_up[0], ffn2_w_down[0], fin,
               apply_final=True)
    return out.reshape(batch, seq, d)
```

```python
import functools

import jax
import jax.numpy as jnp
from jax import lax
from jax.experimental import pallas as pl
from jax.experimental.pallas import tpu as pltpu

F32 = jnp.float32
BF16 = jnp.bfloat16

D_MODEL = 1024
ATTN_HEAD_DIM = 64
ATTN_HEADS = 16
ATTN_KV_HEADS = 4
ATTN_GROUP = 4
WINDOW = 128
MLSTM_HEADS = 8
MLSTM_V_DIM = 128
MLSTM_QK_DIM = 64
CONV_WIDTH = 4
GATE_SOFTCAP = 15.0
D_FF = 2816
FFN_RESIDUAL_WEIGHT = 0.5
NORM_EPS = 1e-6

SZ_AQ, SZ_AK, SZ_AV = 1024, 256, 256
SZ_MQ, SZ_MK, SZ_MV, SZ_MO = 512, 512, 1024, 1024
SZ_MI, SZ_MF = 8, 8

LANES = 128
SUBLANES = 8
MXU_DIM = 256
VMEM_LIMIT_BYTES = 56 * 1024 * 1024

MLSTM_CHUNK = 128
FF_CHUNK = MXU_DIM
TM_FFN = 1024
TM_INPROJ = 1024
TM_MERGE = 1024
TM_MLSTM = 8 * MLSTM_CHUNK
ATTN_LOOKAHEAD = 4
ATTN_BLOCKS = 8
MLSTM_LOOKAHEAD = 8

LOG2E = 1.4426950408889634
K_SCALE_LOG2 = 3.0
ROW_M2, ROW_WINTER, ROW_ENEG, ROW_WK, ROW_DECAY = 0, 8, 16, 24, 32
GATE_ROWS = 40

COL_Q = 0
COL_KK = 1024
COL_MQK = 1280
COL_MO = 2304
COL_G = 3328
PACKED_WIDTH = 5376

ATTN_HEAD_ORDER = tuple((2 * (t // 4) + par) * ATTN_GROUP + t % 4
                        for t in range(ATTN_HEADS // 2) for par in range(2))


def _rms(x, g):
    return x * lax.rsqrt(jnp.mean(x * x, axis=-1, keepdims=True) + NORM_EPS) * g


def _const_spec(shape):
    zeros = (0,) * len(shape)
    return pl.BlockSpec(shape, lambda *_: zeros, pipeline_mode=pl.Buffered(1))


def _ffn_stage_weights(wg_hbm, wu_hbm, wd_hbm, wg_ref, wu_ref, wd_ref, gu_stage, d_stage, sem):
    n_chunks = D_FF // FF_CHUNK

    def copies(c, slot):
        cols = pl.ds(c * FF_CHUNK, FF_CHUNK)
        return (pltpu.make_async_copy(wg_hbm.at[:, cols], gu_stage.at[slot, 0], sem.at[slot, 0]),
                pltpu.make_async_copy(wu_hbm.at[:, cols], gu_stage.at[slot, 1], sem.at[slot, 1]),
                pltpu.make_async_copy(wd_hbm.at[cols, :], d_stage.at[slot], sem.at[slot, 2]))

    for cp in copies(0, 0):
        cp.start()
    for c in range(n_chunks):
        slot = c % 2
        if c + 1 < n_chunks:
            for cp in copies(c + 1, 1 - slot):
                cp.start()
        for cp in copies(c, slot):
            cp.wait()
        sl = slice(c * FF_CHUNK, (c + 1) * FF_CHUNK)
        wg_ref[:, sl] = gu_stage[slot, 0].astype(BF16)
        wu_ref[:, sl] = gu_stage[slot, 1].astype(BF16)
        wd_ref[sl, :] = d_stage[slot].astype(BF16)


def _ffn_kernel(x_ref, g_ref, wg_hbm, wu_hbm, wd_hbm, fin_ref, o_ref,
                wg_ref, wu_ref, wd_ref, gu_stage, d_stage, sem, acc_ref, *, apply_final):
    @pl.when(pl.program_id(0) == 0)
    def _():
        _ffn_stage_weights(wg_hbm, wu_hbm, wd_hbm, wg_ref, wu_ref, wd_ref, gu_stage, d_stage, sem)

    x = x_ref[...]
    h = _rms(x, g_ref[...]).astype(BF16)
    for c in range(D_FF // FF_CHUNK):
        sl = slice(c * FF_CHUNK, (c + 1) * FF_CHUNK)
        a = jnp.dot(h, wg_ref[:, sl], preferred_element_type=F32)
        u = jnp.dot(h, wu_ref[:, sl], preferred_element_type=F32)
        act = (a * jax.nn.sigmoid(a) * u).astype(BF16)
        d = jnp.dot(act, wd_ref[sl, :], preferred_element_type=F32)
        if c == 0:
            acc_ref[...] = d
        else:
            acc_ref[...] += d
    y = x + FFN_RESIDUAL_WEIGHT * acc_ref[...]
    if apply_final:
        y = _rms(y, fin_ref[...])
    o_ref[...] = y


def _ffn(x, g, wg, wu, wd, fin, apply_final):
    n = x.shape[0]
    tm = TM_FFN
    return pl.pallas_call(
        functools.partial(_ffn_kernel, apply_final=apply_final),
        grid=(n // tm,),
        in_specs=[
            pl.BlockSpec((tm, D_MODEL), lambda i: (i, 0)),
            _const_spec((1, D_MODEL)),
            pl.BlockSpec(memory_space=pl.ANY),
            pl.BlockSpec(memory_space=pl.ANY),
            pl.BlockSpec(memory_space=pl.ANY),
            _const_spec((1, D_MODEL)),
        ],
        out_specs=pl.BlockSpec((tm, D_MODEL), lambda i: (i, 0)),
        out_shape=jax.ShapeDtypeStruct((n, D_MODEL), F32),
        scratch_shapes=[
            pltpu.VMEM((D_MODEL, D_FF), BF16),
            pltpu.VMEM((D_MODEL, D_FF), BF16),
            pltpu.VMEM((D_FF, D_MODEL), BF16),
            pltpu.VMEM((2, 2, D_MODEL, FF_CHUNK), F32),
            pltpu.VMEM((2, FF_CHUNK, D_MODEL), F32),
            pltpu.SemaphoreType.DMA((2, 3)),
            pltpu.VMEM((tm, D_MODEL), F32),
        ],
        compiler_params=pltpu.CompilerParams(
            dimension_semantics=("arbitrary",), vmem_limit_bytes=VMEM_LIMIT_BYTES),
        name="ffn_final" if apply_final else "ffn",
    )(x, g, wg, wu, wd, fin)


def _inproj_kernel(x_ref, g_ref, w_ref, b_ref, wif_ref, bif_ref, wvt_ref, bvt_ref, conv_ref,
                   q_ref, kk_ref, avt_ref, mqk_ref, mvt_ref, mo_ref, gate_ref, gt_ref,
                   conv_scr, *, tiles_per_seq):
    i = pl.program_id(0)
    tm = x_ref.shape[0]
    h = _rms(x_ref[...], g_ref[...]).astype(BF16)

    def seg(c0, width):
        return (lax.dot_general(h, w_ref[c0:c0 + width, :], (((1,), (1,)), ((), ())),
                                preferred_element_type=F32) + b_ref[:, c0:c0 + width])

    def seg_t(wt_ref, bt_ref, r0, rows):
        return (lax.dot_general(wt_ref[r0:r0 + rows, :], h, (((1,), (1,)), ((), ())),
                                preferred_element_type=F32) + bt_ref[r0:r0 + rows, :])

    cw = 512

    @pl.when(i == 0)
    def _():
        conv_scr[...] = jnp.zeros_like(conv_scr)

    carry = jnp.where(i % tiles_per_seq == 0, 0.0, conv_scr[...])

    def conv_slab(c, width):
        z = seg(COL_MQK + c, width)
        conv_scr[:, c:c + width] = z[tm - SUBLANES:tm, :]
        zc = jnp.concatenate([carry[:, c:c + width], z], axis=0)
        acc = z * conv_ref[CONV_WIDTH - 1:CONV_WIDTH, c:c + width]
        for k in range(1, CONV_WIDTH):
            zk = pltpu.roll(zc, k, 0)[SUBLANES:, :]
            acc = acc + zk * conv_ref[CONV_WIDTH - 1 - k:CONV_WIDTH - k, c:c + width]
        mqk_ref[:, c:c + width] = (acc * jax.nn.sigmoid(acc)).astype(BF16)

    dw = MXU_DIM
    for k in range(1024 // dw):
        sl = slice(k * dw, (k + 1) * dw)
        conv_slab(k * dw, dw)
        q_ref[:, sl] = (seg(COL_Q + k * dw, dw) * (LOG2E * ATTN_HEAD_DIM ** -0.5)).astype(BF16)
        if k == 0:
            kk_ref[...] = seg(COL_KK, SZ_AK).astype(BF16)
        elif k == 1:
            avt_ref[...] = seg_t(wvt_ref, bvt_ref, SZ_MV, SZ_AV).astype(BF16)
        mvt_ref[sl, :] = seg_t(wvt_ref, bvt_ref, k * dw, dw).astype(BF16)
        mo_ref[:, sl] = seg(COL_MO + k * dw, dw).astype(BF16)
        gate_ref[:, 2 * k * dw:(2 * k + 2) * dw] = seg(COL_G + 2 * k * dw, 2 * dw).astype(BF16)

    gt_ref[...] = seg_t(wif_ref, bif_ref, 0, 2 * MLSTM_HEADS)


def _inproj(x, g, w, b, wif, bif, wvt, bvt, conv, seq):
    n = x.shape[0]
    tm = TM_INPROJ
    row = lambda width: pl.BlockSpec((tm, width), lambda i: (i, 0))
    col = lambda height: pl.BlockSpec((height, tm), lambda i: (0, i))
    bf = lambda width: jax.ShapeDtypeStruct((n, width), BF16)
    return pl.pallas_call(
        functools.partial(_inproj_kernel, tiles_per_seq=seq // tm),
        grid=(n // tm,),
        in_specs=[
            row(D_MODEL),
            _const_spec((1, D_MODEL)),
            _const_spec((PACKED_WIDTH, D_MODEL)),
            _const_spec((1, PACKED_WIDTH)),
            _const_spec((2 * MLSTM_HEADS, D_MODEL)),
            _const_spec((2 * MLSTM_HEADS, 1)),
            _const_spec((SZ_MV + SZ_AV, D_MODEL)),
            _const_spec((SZ_MV + SZ_AV, 1)),
            _const_spec((CONV_WIDTH, 1024)),
        ],
        out_specs=[row(1024), row(SZ_AK), col(SZ_AV), row(1024), col(SZ_MV), row(1024),
                   row(2048), col(2 * MLSTM_HEADS)],
        out_shape=[bf(1024), bf(SZ_AK), jax.ShapeDtypeStruct((SZ_AV, n), BF16), bf(1024),
                   jax.ShapeDtypeStruct((SZ_MV, n), BF16), bf(1024), bf(2048),
                   jax.ShapeDtypeStruct((2 * MLSTM_HEADS, n), F32)],
        scratch_shapes=[pltpu.VMEM((SUBLANES, 1024), F32)],
        compiler_params=pltpu.CompilerParams(
            dimension_semantics=("arbitrary",), vmem_limit_bytes=VMEM_LIMIT_BYTES),
        name="inproj",
    )(x, g, w, b, wif, bif, wvt, bvt, conv)


def _attn_kernel(sink_ref, q_ref, kc_ref, kp_ref, vtc_ref, vtp_ref, o_ref, k_scr, vt_scr):
    w = WINDOW
    contract_lanes = (((1,), (1,)), ((), ()))
    k_scr[0:w, :] = kp_ref[...]
    k_scr[w:, :] = kc_ref[...]
    vt_scr[:, 0:w] = vtp_ref[...]
    vt_scr[:, w:] = vtc_ref[...]
    kj = lax.broadcasted_iota(jnp.int32, (2 * w, w), 0)
    qi = lax.broadcasted_iota(jnp.int32, (2 * w, w), 1)
    band = (kj > qi) & (kj <= qi + w)
    band_first = band & ((kj >= w) | (pl.program_id(1) > 0))
    lo = lax.broadcasted_iota(jnp.int32, (w, LANES), 1) < ATTN_HEAD_DIM
    zero = jnp.zeros((), BF16)
    pairs = [(jb, t) for jb in range(q_ref.shape[0] // w) for t in range(ATTN_HEADS // 2)]

    def scores(jb, t):
        q2 = q_ref[jb * w:(jb + 1) * w, t * LANES:(t + 1) * LANES]
        qm2 = jnp.concatenate([jnp.where(lo, q2, zero), jnp.where(lo, zero, q2)], axis=0)
        kk = k_scr[jb * w:(jb + 2) * w, (t // 4) * LANES:(t // 4 + 1) * LANES]
        return lax.dot_general(kk, qm2, contract_lanes, preferred_element_type=F32)

    def head_out(jb, t, par, s):
        s = jnp.where(band_first if jb == 0 else band, s, -jnp.inf)
        sink = sink_ref[ATTN_HEAD_ORDER[2 * t + par]] * LOG2E
        mx = jnp.maximum(jnp.max(s, axis=0, keepdims=True), sink)
        pr = jnp.exp2(s - mx)
        denom = jnp.sum(pr, axis=0, keepdims=True) + jnp.exp2(sink - mx)
        kvh = 2 * (t // 4) + par
        vt = vt_scr[kvh * ATTN_HEAD_DIM:(kvh + 1) * ATTN_HEAD_DIM, jb * w:(jb + 2) * w]
        return jnp.dot(vt, pr.astype(BF16), preferred_element_type=F32) * (1.0 / denom)

    pending = [scores(*u) for u in pairs[:ATTN_LOOKAHEAD]]
    for idx, (jb, t) in enumerate(pairs):
        s2 = pending[idx]
        pending[idx] = None
        outs = [head_out(jb, t, par, s2[:, par * w:(par + 1) * w]) for par in range(2)]
        if idx + ATTN_LOOKAHEAD < len(pairs):
            pending.append(scores(*pairs[idx + ATTN_LOOKAHEAD]))
        pair = jnp.concatenate(outs, axis=0)
        o_ref[jb * w:(jb + 1) * w, t * LANES:(t + 1) * LANES] = pair.T.astype(BF16)


def _attention(sinks, q, kk, vt, batch, seq):
    n = q.shape[0]
    tq = ATTN_BLOCKS * WINDOW
    nt = seq // tq
    cur = lambda b, j: b * nt + j
    prev = lambda b, j: (b * nt + j) * ATTN_BLOCKS - jnp.minimum(j, 1)
    return pl.pallas_call(
        _attn_kernel,
        grid=(batch, nt),
        in_specs=[
            pl.BlockSpec(memory_space=pltpu.SMEM),
            pl.BlockSpec((tq, 1024), lambda b, j: (cur(b, j), 0)),
            pl.BlockSpec((tq, SZ_AK), lambda b, j: (cur(b, j), 0)),
            pl.BlockSpec((WINDOW, SZ_AK), lambda b, j: (prev(b, j), 0)),
            pl.BlockSpec((SZ_AV, tq), lambda b, j: (0, cur(b, j))),
            pl.BlockSpec((SZ_AV, WINDOW), lambda b, j: (0, prev(b, j))),
        ],
        out_specs=pl.BlockSpec((tq, 1024), lambda b, j: (cur(b, j), 0)),
        out_shape=jax.ShapeDtypeStruct((n, 1024), BF16),
        scratch_shapes=[pltpu.VMEM((WINDOW + tq, SZ_AK), BF16),
                        pltpu.VMEM((SZ_AV, WINDOW + tq), BF16)],
        compiler_params=pltpu.CompilerParams(
            dimension_semantics=("arbitrary", "arbitrary"), vmem_limit_bytes=VMEM_LIMIT_BYTES),
        name="attention",
    )(sinks, q, kk, kk, vt, vt)


def _log_sigmoid(x):
    return -(jnp.maximum(-x, 0.0) + jnp.log(1.0 + jnp.exp(-jnp.abs(x))))


def _mlstm_gate_kernel(g_ref, rows_ref, cols_ref, *, seq):
    L = MLSTM_CHUNK
    nh = MLSTM_HEADS
    n = g_ref.shape[1]
    g = g_ref[...]
    ig = GATE_SOFTCAP * jnp.tanh(g[0:nh] / GATE_SOFTCAP)
    fg = GATE_SOFTCAP * jnp.tanh(g[nh:2 * nh] / GATE_SOFTCAP)
    lf = _log_sigmoid(fg)
    pos = lax.broadcasted_iota(jnp.int32, (nh, n), 1) & (L - 1)
    a = lf
    sh = 1
    while sh < L:
        a = a + jnp.where(pos >= sh, pltpu.roll(a, sh, 1), 0.0)
        sh *= 2
    bvec = ig - a
    pm = bvec
    sh = 1
    while sh < L:
        pm = jnp.maximum(pm, jnp.where(pos >= sh, pltpu.roll(pm, sh, 1), -jnp.inf))
        sh *= 2
    pad = jnp.zeros((L - nh, L), F32)
    m = [jnp.zeros((nh, L), F32) for _ in range(n // seq)]
    for c in range(seq // L):
        for s in range(n // seq):
            sl = slice(s * seq + c * L, s * seq + (c + 1) * L)
            a_c, b_c = a[:, sl], bvec[:, sl]
            mrow = jnp.maximum(m[s], pm[:, sl])
            total = jnp.broadcast_to(a_c[:, L - 1:L], (nh, L))
            mlast = jnp.broadcast_to(mrow[:, L - 1:L], (nh, L))
            rows_ref[ROW_M2:ROW_M2 + nh, sl] = mrow * LOG2E
            rows_ref[ROW_WINTER:ROW_WINTER + nh, sl] = jnp.exp(m[s] - mrow)
            rows_ref[ROW_ENEG:ROW_ENEG + nh, sl] = jnp.exp(-(a_c + mrow))
            rows_ref[ROW_WK:ROW_WK + nh, sl] = jnp.exp(b_c - mlast) * MLSTM_QK_DIM ** -0.5
            rows_ref[ROW_DECAY:ROW_DECAY + nh, sl] = jnp.exp(m[s] - mlast)
            m[s] = total + mlast
            cols_ref[sl, :] = jnp.concatenate([b_c * LOG2E - K_SCALE_LOG2, pad], axis=0).T


def _mlstm_gates(gt, seq):
    n = gt.shape[1]
    return pl.pallas_call(
        functools.partial(_mlstm_gate_kernel, seq=seq),
        grid=(1,),
        in_specs=[pl.BlockSpec((2 * MLSTM_HEADS, n), lambda i: (0, 0))],
        out_specs=[pl.BlockSpec((GATE_ROWS, n), lambda i: (0, 0)),
                   pl.BlockSpec((n, LANES), lambda i: (0, 0))],
        out_shape=[jax.ShapeDtypeStruct((GATE_ROWS, n), F32),
                   jax.ShapeDtypeStruct((n, LANES), F32)],
        compiler_params=pltpu.CompilerParams(
            dimension_semantics=("arbitrary",), vmem_limit_bytes=VMEM_LIMIT_BYTES),
        name="mlstm_gates",
    )(gt)


def _mlstm_kernel(rows_ref, cols_ref, qk_ref, vt_ref, og_ref, hn_ref, o_ref, ct_scr, n_scr):
    L = MLSTM_CHUNK
    nh = MLSTM_HEADS
    contract_lanes = (((1,), (1,)), ((), ()))

    @pl.when(pl.program_id(1) == 0)
    def _():
        ct_scr[...] = jnp.zeros_like(ct_scr)
        n_scr[...] = jnp.zeros_like(n_scr)

    row_i = lax.broadcasted_iota(jnp.int32, (L, L), 0)
    col_i = lax.broadcasted_iota(jnp.int32, (L, L), 1)
    causal_t = row_i <= col_i
    lo_lane = col_i < MLSTM_QK_DIM
    zero = jnp.zeros((), BF16)

    n_chunks = qk_ref.shape[0] // L
    units = [(ci, p, par) for ci in range(n_chunks) for p in range(nh // 2) for par in range(2)]
    ct = [ct_scr[p] for p in range(nh // 2)]
    nn = [n_scr[p] for p in range(nh // 2)]
    gate_rows = {}
    upd = {}

    def chunk_gates(ci):
        if ci not in gate_rows:
            rs = slice(ci * L, (ci + 1) * L)
            wk = rows_ref[ROW_WK:ROW_WK + nh, rs]
            gate_rows[ci] = dict(
                m2=rows_ref[ROW_M2:ROW_M2 + nh, rs], w_inter=rows_ref[ROW_WINTER:ROW_WINTER + nh, rs],
                e_neg=rows_ref[ROW_ENEG:ROW_ENEG + nh, rs], wk=wk,
                decay=rows_ref[ROW_DECAY:ROW_DECAY + nh, rs], b2cols=cols_ref[rs, :],
                wk_b=jnp.concatenate([wk, wk], axis=0).astype(BF16))
        return gate_rows[ci]

    def front(ci, p, par):
        g = chunk_gates(ci)
        rs = slice(ci * L, (ci + 1) * L)
        hh = 2 * p + par
        q2 = qk_ref[rs, p * LANES:(p + 1) * LANES]
        k2 = qk_ref[rs, SZ_MQ + p * LANES:SZ_MQ + (p + 1) * LANES]
        qm = jnp.where(lo_lane, q2, zero) if par == 0 else jnp.where(lo_lane, zero, q2)
        vt = vt_ref[hh * MLSTM_V_DIM:(hh + 1) * MLSTM_V_DIM, rs]
        st = lax.dot_general(k2, qm, contract_lanes, preferred_element_type=F32)
        n2b = jnp.broadcast_to(nn[p], (2 * SUBLANES, LANES)).astype(BF16)
        qn = lax.dot_general(n2b, qm, contract_lanes, preferred_element_type=F32)[0:1, :]
        inter = lax.dot_general(ct[p].astype(BF16), qm, contract_lanes,
                                preferred_element_type=F32)
        vw = (vt.astype(F32) * g["wk"][hh:hh + 1, :]).astype(BF16)
        upd[(ci, p, par)] = jnp.dot(vw, k2, preferred_element_type=F32)
        if par == 1:
            h0, h1 = 2 * p, 2 * p + 1
            dec = jnp.where(lo_lane[0:1, :], g["decay"][h0:h0 + 1, :], g["decay"][h1:h1 + 1, :])
            n_inc = jnp.dot(g["wk_b"], k2, preferred_element_type=F32)
            ct[p] = dec * ct[p] + jnp.where(lo_lane, upd.pop((ci, p, 0)), upd.pop((ci, p, 1)))
            nn[p] = dec * nn[p] + jnp.where(lo_lane[0:1, :], n_inc[h0:h0 + 1, :],
                                            n_inc[h1:h1 + 1, :])
        return st, qn, inter

    def back(ci, p, par, st, qn, inter):
        g = chunk_gates(ci)
        rs = slice(ci * L, (ci + 1) * L)
        hh = 2 * p + par
        vsl = slice(hh * MLSTM_V_DIM, (hh + 1) * MLSTM_V_DIM)
        dt = jnp.where(causal_t,
                       jnp.exp2(g["b2cols"][:, hh:hh + 1] - g["m2"][hh:hh + 1, :]), 0.0)
        sct = st * dt
        wi = g["w_inter"][hh:hh + 1, :]
        den = jnp.sum(sct, axis=0, keepdims=True) + wi * qn
        numt = jnp.dot(vt_ref[vsl, rs], sct.astype(BF16), preferred_element_type=F32) + wi * inter
        rden = 1.0 / jnp.maximum(jnp.abs(den), g["e_neg"][hh:hh + 1, :])
        ms = jnp.mean(numt * numt, axis=0, keepdims=True)
        scale = rden * lax.rsqrt(rden * rden * ms + NORM_EPS)
        hv = (numt * scale).T
        gate = jax.nn.sigmoid(og_ref[rs, vsl].astype(F32))
        o_ref[rs, vsl] = (hv * hn_ref[:, vsl] * gate).astype(BF16)

    pending = [front(*u) for u in units[:MLSTM_LOOKAHEAD]]
    for idx, u in enumerate(units):
        back(*u, *pending[idx])
        pending[idx] = None
        if idx + MLSTM_LOOKAHEAD < len(units):
            pending.append(front(*units[idx + MLSTM_LOOKAHEAD]))
    for p in range(nh // 2):
        ct_scr[p] = ct[p]
        n_scr[p] = nn[p]


def _mlstm(rows, cols, qk, v, og, hn, batch, seq):
    n = qk.shape[0]
    tm = TM_MLSTM
    nt = seq // tm
    row = pl.BlockSpec((tm, 1024), lambda b, c: (b * nt + c, 0))
    return pl.pallas_call(
        _mlstm_kernel,
        grid=(batch, nt),
        in_specs=[
            pl.BlockSpec((GATE_ROWS, tm), lambda b, c: (0, b * nt + c)),
            pl.BlockSpec((tm, LANES), lambda b, c: (b * nt + c, 0)),
            row,
            pl.BlockSpec((SZ_MV, tm), lambda b, c: (0, b * nt + c)),
            row,
            _const_spec((1, 1024)),
        ],
        out_specs=row,
        out_shape=jax.ShapeDtypeStruct((n, 1024), BF16),
        scratch_shapes=[
            pltpu.VMEM((MLSTM_HEADS // 2, 2 * MLSTM_QK_DIM, MLSTM_V_DIM), F32),
            pltpu.VMEM((MLSTM_HEADS // 2, 1, 2 * MLSTM_QK_DIM), F32),
        ],
        compiler_params=pltpu.CompilerParams(
            dimension_semantics=("arbitrary", "arbitrary"), vmem_limit_bytes=VMEM_LIMIT_BYTES),
        name="mlstm",
    )(rows, cols, qk, v, og, hn)


def _merge_stage_weights(wpa_hbm, wpm_hbm, wo_hbm, wpa_ref, wpm_ref, wo_ref, stage, sem):
    hd = ATTN_HEAD_DIM
    wpa_copies = [pltpu.make_async_copy(wpa_hbm.at[pl.ds(head * hd, hd), :],
                                        stage.at[0, pl.ds(pos * hd, hd), :], sem.at[0])
                  for pos, head in enumerate(ATTN_HEAD_ORDER)]
    wpm_copy = pltpu.make_async_copy(wpm_hbm, stage.at[1], sem.at[1])
    wo_copy = pltpu.make_async_copy(wo_hbm, stage.at[0], sem.at[0])
    for cp in wpa_copies:
        cp.start()
    wpm_copy.start()
    for cp in wpa_copies:
        cp.wait()
    wpa_ref[...] = stage[0].astype(BF16)
    wo_copy.start()
    wpm_copy.wait()
    wpm_ref[...] = stage[1].astype(BF16)
    wo_copy.wait()
    wo_ref[...] = stage[0].astype(BF16)


def _merge_kernel(x_ref, a_ref, hm_ref, gate_ref, wpa_hbm, wpm_hbm, wo_hbm, o_ref,
                  wpa_ref, wpm_ref, wo_ref, stage, sem):
    @pl.when(pl.program_id(0) == 0)
    def _():
        _merge_stage_weights(wpa_hbm, wpm_hbm, wo_hbm, wpa_ref, wpm_ref, wo_ref, stage, sem)

    ya = jnp.dot(a_ref[...], wpa_ref[...], preferred_element_type=F32)
    ym = jnp.dot(hm_ref[...], wpm_ref[...], preferred_element_type=F32)
    ga = jax.nn.sigmoid(gate_ref[:, 0:D_MODEL].astype(F32))
    gm = jax.nn.sigmoid(gate_ref[:, D_MODEL:2 * D_MODEL].astype(F32))
    merged = (ga * ya + gm * ym).astype(BF16)
    o_ref[...] = x_ref[...] + jnp.dot(merged, wo_ref[...], preferred_element_type=F32)


def _merge(x, a, hm, gates, wpa, wpm, wo):
    n = x.shape[0]
    tm = TM_MERGE
    row = lambda width: pl.BlockSpec((tm, width), lambda i: (i, 0))
    return pl.pallas_call(
        _merge_kernel,
        grid=(n // tm,),
        in_specs=[row(D_MODEL), row(1024), row(1024), row(2048),
                  pl.BlockSpec(memory_space=pl.ANY), pl.BlockSpec(memory_space=pl.ANY),
                  pl.BlockSpec(memory_space=pl.ANY)],
        out_specs=row(D_MODEL),
        out_shape=jax.ShapeDtypeStruct((n, D_MODEL), F32),
        scratch_shapes=[pltpu.VMEM((D_MODEL, D_MODEL), BF16)] * 3 + [
            pltpu.VMEM((2, D_MODEL, D_MODEL), F32), pltpu.SemaphoreType.DMA((2,))],
        compiler_params=pltpu.CompilerParams(
            dimension_semantics=("arbitrary",), vmem_limit_bytes=VMEM_LIMIT_BYTES),
        name="merge",
    )(x, a, hm, gates, wpa, wpm, wo)


def _to_head_order(t, axis):
    shape = t.shape
    t = t.reshape(shape[:axis] + (2, 2, ATTN_GROUP) + shape[axis + 1:])
    t = jnp.swapaxes(t, axis + 1, axis + 2)
    return t.reshape(shape)


def _pack_inproj(w_in, b_in):
    wt_all = w_in.T.astype(BF16)
    o = 0
    parts = {}
    for name, size in (("aq", SZ_AQ), ("ak", SZ_AK), ("av", SZ_AV), ("mq", SZ_MQ), ("mk", SZ_MK),
                       ("mv", SZ_MV), ("mo", SZ_MO), ("mi", SZ_MI), ("mf", SZ_MF),
                       ("g", 2 * D_MODEL)):
        parts[name] = (wt_all[o:o + size], b_in[o:o + size])
        o += size

    def pair_heads(t):
        return _to_head_order(t.reshape((ATTN_HEADS, ATTN_HEAD_DIM) + t.shape[1:]), 0).reshape(t.shape)

    order = [tuple(pair_heads(t) for t in parts["aq"]), parts["ak"],
             parts["mq"], parts["mk"], parts["mo"], parts["g"]]
    wt = jnp.concatenate([t[0] for t in order], axis=0)
    b = jnp.concatenate([t[1] for t in order], axis=0).reshape(1, PACKED_WIDTH)
    wif = jnp.concatenate([parts["mi"][0], parts["mf"][0]], axis=0)
    bif = jnp.concatenate([parts["mi"][1], parts["mf"][1]], axis=0).reshape(2 * MLSTM_HEADS, 1)
    wvt = jnp.concatenate([parts["mv"][0], parts["av"][0]], axis=0)
    bvt = jnp.concatenate([parts["mv"][1], parts["av"][1]], axis=0).reshape(SZ_MV + SZ_AV, 1)
    return wt, b, wif, bif, wvt, bvt


def kernel(x, ffn1_norm, ffn1_w_gate, ffn1_w_up, ffn1_w_down, mix_norm, w_in, b_in, attn_sinks,
           mlstm_conv, mlstm_head_norm, w_proj_attn, w_proj_mlstm, w_out, ffn2_norm, ffn2_w_gate,
           ffn2_w_up, ffn2_w_down, final_norm):
    batch, seq, d = x.shape
    assert d == D_MODEL and seq % TM_INPROJ == 0 and (batch * seq) % TM_FFN == 0
    assert ffn1_norm.shape[0] == 1, "one layer"
    n = batch * seq
    xf = x.reshape(n, d)
    fin = final_norm.reshape(1, d)

    x1 = _ffn(xf, ffn1_norm[0].reshape(1, d), ffn1_w_gate[0], ffn1_w_up[0], ffn1_w_down[0], fin,
              apply_final=False)

    w, b, wif, bif, wvt, bvt = _pack_inproj(w_in[0], b_in[0])
    q, kk, avt, mqk, mvt, mo, gates, gt = _inproj(x1, mix_norm[0].reshape(1, d), w, b, wif, bif,
                                                  wvt, bvt, mlstm_conv[0], seq)

    ya = _attention(attn_sinks[0], q, kk, avt, batch, seq)

    rows, cols = _mlstm_gates(gt, seq)
    hm = _mlstm(rows, cols, mqk, mvt, mo, mlstm_head_norm[0].reshape(1, SZ_MV), batch, seq)

    x2 = _merge(x1, ya, hm, gates, w_proj_attn[0], w_proj_mlstm[0], w_out[0])

    out = _ffn(x2, ffn2_norm[0].reshape(1, d), ffn2_w_gate[0], ffn2_w_up[0], ffn2_w_down[0], fin,
               apply_final=True)
    return out.reshape(batch, seq, d)
```

```python
import functools

import jax
import jax.numpy as jnp
from jax import lax
from jax.experimental import pallas as pl
from jax.experimental.pallas import tpu as pltpu

F32 = jnp.float32
BF16 = jnp.bfloat16

D_MODEL = 1024
ATTN_HEAD_DIM = 64
ATTN_HEADS = 16
ATTN_KV_HEADS = 4
ATTN_GROUP = 4
WINDOW = 128
MLSTM_HEADS = 8
MLSTM_V_DIM = 128
MLSTM_QK_DIM = 64
CONV_WIDTH = 4
GATE_SOFTCAP = 15.0
D_FF = 2816
FFN_RESIDUAL_WEIGHT = 0.5
NORM_EPS = 1e-6

SZ_AQ, SZ_AK, SZ_AV = 1024, 256, 256
SZ_MQ, SZ_MK, SZ_MV, SZ_MO = 512, 512, 1024, 1024
SZ_MI, SZ_MF = 8, 8

LANES = 128
SUBLANES = 8
MXU_DIM = 256
VMEM_LIMIT_BYTES = 56 * 1024 * 1024

MLSTM_CHUNK = 128
FF_CHUNK = MXU_DIM
TM_FFN = 1024
TM_INPROJ = 1024
TM_MERGE = 1024
TM_MLSTM = 8 * MLSTM_CHUNK
ATTN_LOOKAHEAD = 4
ATTN_BLOCKS = 8
MLSTM_LOOKAHEAD = 4

LOG2E = 1.4426950408889634
K_SCALE_LOG2 = 3.0
ROW_M2, ROW_WINTER, ROW_ENEG, ROW_WK, ROW_DECAY = 0, 8, 16, 24, 32
GATE_ROWS = 40

COL_Q = 0
COL_KK = 1024
COL_MQK = 1280
COL_MO = 2304
COL_G = 3328
PACKED_WIDTH = 5376

ATTN_HEAD_ORDER = tuple((2 * (t // 4) + par) * ATTN_GROUP + t % 4
                        for t in range(ATTN_HEADS // 2) for par in range(2))


def _rms(x, g):
    return x * lax.rsqrt(jnp.mean(x * x, axis=-1, keepdims=True) + NORM_EPS) * g


def _const_spec(shape):
    zeros = (0,) * len(shape)
    return pl.BlockSpec(shape, lambda *_: zeros, pipeline_mode=pl.Buffered(1))


def _ffn_stage_weights(wg_hbm, wu_hbm, wd_hbm, wg_ref, wu_ref, wd_ref, gu_stage, d_stage, sem):
    n_chunks = D_FF // FF_CHUNK

    def copies(c, slot):
        cols = pl.ds(c * FF_CHUNK, FF_CHUNK)
        return (pltpu.make_async_copy(wg_hbm.at[:, cols], gu_stage.at[slot, 0], sem.at[slot, 0]),
                pltpu.make_async_copy(wu_hbm.at[:, cols], gu_stage.at[slot, 1], sem.at[slot, 1]),
                pltpu.make_async_copy(wd_hbm.at[cols, :], d_stage.at[slot], sem.at[slot, 2]))

    for cp in copies(0, 0):
        cp.start()
    for c in range(n_chunks):
        slot = c % 2
        if c + 1 < n_chunks:
            for cp in copies(c + 1, 1 - slot):
                cp.start()
        for cp in copies(c, slot):
            cp.wait()
        sl = slice(c * FF_CHUNK, (c + 1) * FF_CHUNK)
        wg_ref[:, sl] = gu_stage[slot, 0].astype(BF16)
        wu_ref[:, sl] = gu_stage[slot, 1].astype(BF16)
        wd_ref[sl, :] = d_stage[slot].astype(BF16)


def _ffn_kernel(x_ref, g_ref, wg_hbm, wu_hbm, wd_hbm, fin_ref, o_ref,
                wg_ref, wu_ref, wd_ref, gu_stage, d_stage, sem, acc_ref, *, apply_final):
    @pl.when(pl.program_id(0) == 0)
    def _():
        _ffn_stage_weights(wg_hbm, wu_hbm, wd_hbm, wg_ref, wu_ref, wd_ref, gu_stage, d_stage, sem)

    x = x_ref[...]
    h = _rms(x, g_ref[...]).astype(BF16)
    for c in range(D_FF // FF_CHUNK):
        sl = slice(c * FF_CHUNK, (c + 1) * FF_CHUNK)
        a = jnp.dot(h, wg_ref[:, sl], preferred_element_type=F32)
        u = jnp.dot(h, wu_ref[:, sl], preferred_element_type=F32)
        act = (a * jax.nn.sigmoid(a) * u).astype(BF16)
        d = jnp.dot(act, wd_ref[sl, :], preferred_element_type=F32)
        if c == 0:
            acc_ref[...] = d
        else:
            acc_ref[...] += d
    y = x + FFN_RESIDUAL_WEIGHT * acc_ref[...]
    if apply_final:
        y = _rms(y, fin_ref[...])
    o_ref[...] = y


def _ffn(x, g, wg, wu, wd, fin, apply_final):
    n = x.shape[0]
    tm = TM_FFN
    return pl.pallas_call(
        functools.partial(_ffn_kernel, apply_final=apply_final),
        grid=(n // tm,),
        in_specs=[
            pl.BlockSpec((tm, D_MODEL), lambda i: (i, 0)),
            _const_spec((1, D_MODEL)),
            pl.BlockSpec(memory_space=pl.ANY),
            pl.BlockSpec(memory_space=pl.ANY),
            pl.BlockSpec(memory_space=pl.ANY),
            _const_spec((1, D_MODEL)),
        ],
        out_specs=pl.BlockSpec((tm, D_MODEL), lambda i: (i, 0)),
        out_shape=jax.ShapeDtypeStruct((n, D_MODEL), F32),
        scratch_shapes=[
            pltpu.VMEM((D_MODEL, D_FF), BF16),
            pltpu.VMEM((D_MODEL, D_FF), BF16),
            pltpu.VMEM((D_FF, D_MODEL), BF16),
            pltpu.VMEM((2, 2, D_MODEL, FF_CHUNK), F32),
            pltpu.VMEM((2, FF_CHUNK, D_MODEL), F32),
            pltpu.SemaphoreType.DMA((2, 3)),
            pltpu.VMEM((tm, D_MODEL), F32),
        ],
        compiler_params=pltpu.CompilerParams(
            dimension_semantics=("arbitrary",), vmem_limit_bytes=VMEM_LIMIT_BYTES),
        name="ffn_final" if apply_final else "ffn",
    )(x, g, wg, wu, wd, fin)


def _inproj_kernel(x_ref, g_ref, w_ref, b_ref, wif_ref, bif_ref, wvt_ref, bvt_ref, conv_ref,
                   q_ref, kk_ref, avt_ref, mqk_ref, mvt_ref, mo_ref, gate_ref, gt_ref,
                   conv_scr, *, tiles_per_seq):
    i = pl.program_id(0)
    tm = x_ref.shape[0]
    h = _rms(x_ref[...], g_ref[...]).astype(BF16)

    def seg(c0, width):
        return (lax.dot_general(h, w_ref[c0:c0 + width, :], (((1,), (1,)), ((), ())),
                                preferred_element_type=F32) + b_ref[:, c0:c0 + width])

    def seg_t(wt_ref, bt_ref, r0, rows):
        return (lax.dot_general(wt_ref[r0:r0 + rows, :], h, (((1,), (1,)), ((), ())),
                                preferred_element_type=F32) + bt_ref[r0:r0 + rows, :])

    @pl.when(i == 0)
    def _():
        conv_scr[...] = jnp.zeros_like(conv_scr)

    carry = jnp.where(i % tiles_per_seq == 0, 0.0, conv_scr[...])

    def conv_slab(c, width):
        z = seg(COL_MQK + c, width)
        conv_scr[:, c:c + width] = z[tm - SUBLANES:tm, :]
        zc = jnp.concatenate([carry[:, c:c + width], z], axis=0)
        acc = z * conv_ref[CONV_WIDTH - 1:CONV_WIDTH, c:c + width]
        for k in range(1, CONV_WIDTH):
            zk = pltpu.roll(zc, k, 0)[SUBLANES:, :]
            acc = acc + zk * conv_ref[CONV_WIDTH - 1 - k:CONV_WIDTH - k, c:c + width]
        mqk_ref[:, c:c + width] = (acc * jax.nn.sigmoid(acc)).astype(BF16)

    dw = MXU_DIM
    for k in range(SZ_AQ // dw):
        sl = slice(k * dw, (k + 1) * dw)
        conv_slab(k * dw, dw)
        q_ref[:, sl] = (seg(COL_Q + k * dw, dw) * (LOG2E * ATTN_HEAD_DIM ** -0.5)).astype(BF16)
        if k == 0:
            kk_ref[...] = seg(COL_KK, SZ_AK).astype(BF16)
        elif k == 1:
            avt_ref[...] = seg_t(wvt_ref, bvt_ref, SZ_MV, SZ_AV).astype(BF16)
        mvt_ref[sl, :] = seg_t(wvt_ref, bvt_ref, k * dw, dw).astype(BF16)
        mo_ref[:, sl] = seg(COL_MO + k * dw, dw).astype(BF16)
        gate_ref[:, 2 * k * dw:(2 * k + 2) * dw] = seg(COL_G + 2 * k * dw, 2 * dw).astype(BF16)

    gt_ref[...] = seg_t(wif_ref, bif_ref, 0, 2 * MLSTM_HEADS)


def _inproj(x, g, w, b, wif, bif, wvt, bvt, conv, seq):
    n = x.shape[0]
    tm = TM_INPROJ
    row = lambda width: pl.BlockSpec((tm, width), lambda i: (i, 0))
    col = lambda height: pl.BlockSpec((height, tm), lambda i: (0, i))
    bf = lambda width: jax.ShapeDtypeStruct((n, width), BF16)
    return pl.pallas_call(
        functools.partial(_inproj_kernel, tiles_per_seq=seq // tm),
        grid=(n // tm,),
        in_specs=[
            row(D_MODEL),
            _const_spec((1, D_MODEL)),
            _const_spec((PACKED_WIDTH, D_MODEL)),
            _const_spec((1, PACKED_WIDTH)),
            _const_spec((2 * MLSTM_HEADS, D_MODEL)),
            _const_spec((2 * MLSTM_HEADS, 1)),
            _const_spec((SZ_MV + SZ_AV, D_MODEL)),
            _const_spec((SZ_MV + SZ_AV, 1)),
            _const_spec((CONV_WIDTH, SZ_MQ + SZ_MK)),
        ],
        out_specs=[row(SZ_AQ), row(SZ_AK), col(SZ_AV), row(SZ_MQ + SZ_MK), col(SZ_MV), row(SZ_MO),
                   row(2 * D_MODEL), col(2 * MLSTM_HEADS)],
        out_shape=[bf(SZ_AQ), bf(SZ_AK), jax.ShapeDtypeStruct((SZ_AV, n), BF16),
                   bf(SZ_MQ + SZ_MK), jax.ShapeDtypeStruct((SZ_MV, n), BF16), bf(SZ_MO),
                   bf(2 * D_MODEL), jax.ShapeDtypeStruct((2 * MLSTM_HEADS, n), F32)],
        scratch_shapes=[pltpu.VMEM((SUBLANES, SZ_MQ + SZ_MK), F32)],
        compiler_params=pltpu.CompilerParams(
            dimension_semantics=("arbitrary",), vmem_limit_bytes=VMEM_LIMIT_BYTES),
        name="inproj",
    )(x, g, w, b, wif, bif, wvt, bvt, conv)


def _attn_kernel(sink_ref, q_ref, kc_ref, kp_ref, vtc_ref, vtp_ref, o_ref, k_scr, vt_scr):
    w = WINDOW
    contract_lanes = (((1,), (1,)), ((), ()))
    k_scr[0:w, :] = kp_ref[...]
    k_scr[w:, :] = kc_ref[...]
    vt_scr[:, 0:w] = vtp_ref[...]
    vt_scr[:, w:] = vtc_ref[...]
    kj = lax.broadcasted_iota(jnp.int32, (2 * w, w), 0)
    qi = lax.broadcasted_iota(jnp.int32, (2 * w, w), 1)
    band = (kj > qi) & (kj <= qi + w)
    band_first = band & ((kj >= w) | (pl.program_id(1) > 0))
    lo = lax.broadcasted_iota(jnp.int32, (w, LANES), 1) < ATTN_HEAD_DIM
    zero = jnp.zeros((), BF16)
    pairs = [(jb, t) for jb in range(q_ref.shape[0] // w) for t in range(ATTN_HEADS // 2)]

    def scores(jb, t):
        q2 = q_ref[jb * w:(jb + 1) * w, t * LANES:(t + 1) * LANES]
        qm2 = jnp.concatenate([jnp.where(lo, q2, zero), jnp.where(lo, zero, q2)], axis=0)
        kk = k_scr[jb * w:(jb + 2) * w, (t // 4) * LANES:(t // 4 + 1) * LANES]
        return lax.dot_general(kk, qm2, contract_lanes, preferred_element_type=F32)

    def head_out(jb, t, par, s):
        s = jnp.where(band_first if jb == 0 else band, s, -jnp.inf)
        sink = sink_ref[ATTN_HEAD_ORDER[2 * t + par]] * LOG2E
        mx = jnp.maximum(jnp.max(s, axis=0, keepdims=True), sink)
        pr = jnp.exp2(s - mx)
        denom = jnp.sum(pr, axis=0, keepdims=True) + jnp.exp2(sink - mx)
        kvh = 2 * (t // 4) + par
        vt = vt_scr[kvh * ATTN_HEAD_DIM:(kvh + 1) * ATTN_HEAD_DIM, jb * w:(jb + 2) * w]
        return jnp.dot(vt, pr.astype(BF16), preferred_element_type=F32) * (1.0 / denom)

    pending = [scores(*u) for u in pairs[:ATTN_LOOKAHEAD]]
    for idx, (jb, t) in enumerate(pairs):
        s2 = pending[idx]
        pending[idx] = None
        outs = [head_out(jb, t, par, s2[:, par * w:(par + 1) * w]) for par in range(2)]
        if idx + ATTN_LOOKAHEAD < len(pairs):
            pending.append(scores(*pairs[idx + ATTN_LOOKAHEAD]))
        pair = jnp.concatenate(outs, axis=0)
        o_ref[jb * w:(jb + 1) * w, t * LANES:(t + 1) * LANES] = pair.T.astype(BF16)


def _attention(sinks, q, kk, vt, batch, seq):
    n = q.shape[0]
    tq = ATTN_BLOCKS * WINDOW
    nt = seq // tq
    cur = lambda b, j: b * nt + j
    prev = lambda b, j: (b * nt + j) * ATTN_BLOCKS - jnp.minimum(j, 1)
    return pl.pallas_call(
        _attn_kernel,
        grid=(batch, nt),
        in_specs=[
            pl.BlockSpec(memory_space=pltpu.SMEM),
            pl.BlockSpec((tq, SZ_AQ), lambda b, j: (cur(b, j), 0)),
            pl.BlockSpec((tq, SZ_AK), lambda b, j: (cur(b, j), 0)),
            pl.BlockSpec((WINDOW, SZ_AK), lambda b, j: (prev(b, j), 0)),
            pl.BlockSpec((SZ_AV, tq), lambda b, j: (0, cur(b, j))),
            pl.BlockSpec((SZ_AV, WINDOW), lambda b, j: (0, prev(b, j))),
        ],
        out_specs=pl.BlockSpec((tq, SZ_AQ), lambda b, j: (cur(b, j), 0)),
        out_shape=jax.ShapeDtypeStruct((n, SZ_AQ), BF16),
        scratch_shapes=[pltpu.VMEM((WINDOW + tq, SZ_AK), BF16),
                        pltpu.VMEM((SZ_AV, WINDOW + tq), BF16)],
        compiler_params=pltpu.CompilerParams(
            dimension_semantics=("arbitrary", "arbitrary"), vmem_limit_bytes=VMEM_LIMIT_BYTES),
        name="attention",
    )(sinks, q, kk, kk, vt, vt)


def _log_sigmoid(x):
    return -(jnp.maximum(-x, 0.0) + jnp.log(1.0 + jnp.exp(-jnp.abs(x))))


def _mlstm_gate_kernel(g_ref, rows_ref, cols_ref, *, seq):
    L = MLSTM_CHUNK
    nh = MLSTM_HEADS
    n = g_ref.shape[1]
    g = g_ref[...]
    ig = GATE_SOFTCAP * jnp.tanh(g[0:nh] / GATE_SOFTCAP)
    fg = GATE_SOFTCAP * jnp.tanh(g[nh:2 * nh] / GATE_SOFTCAP)
    lf = _log_sigmoid(fg)
    pos = lax.broadcasted_iota(jnp.int32, (nh, n), 1) & (L - 1)
    a = lf
    sh = 1
    while sh < L:
        a = a + jnp.where(pos >= sh, pltpu.roll(a, sh, 1), 0.0)
        sh *= 2
    bvec = ig - a
    pm = bvec
    sh = 1
    while sh < L:
        pm = jnp.maximum(pm, jnp.where(pos >= sh, pltpu.roll(pm, sh, 1), -jnp.inf))
        sh *= 2
    pad = jnp.zeros((L - nh, L), F32)
    m = [jnp.zeros((nh, L), F32) for _ in range(n // seq)]
    for c in range(seq // L):
        for s in range(n // seq):
            sl = slice(s * seq + c * L, s * seq + (c + 1) * L)
            a_c, b_c = a[:, sl], bvec[:, sl]
            mrow = jnp.maximum(m[s], pm[:, sl])
            total = jnp.broadcast_to(a_c[:, L - 1:L], (nh, L))
            mlast = jnp.broadcast_to(mrow[:, L - 1:L], (nh, L))
            rows_ref[ROW_M2:ROW_M2 + nh, sl] = mrow * LOG2E
            rows_ref[ROW_WINTER:ROW_WINTER + nh, sl] = jnp.exp(m[s] - mrow)
            rows_ref[ROW_ENEG:ROW_ENEG + nh, sl] = jnp.exp(-(a_c + mrow))
            rows_ref[ROW_WK:ROW_WK + nh, sl] = jnp.exp(b_c - mlast) * MLSTM_QK_DIM ** -0.5
            rows_ref[ROW_DECAY:ROW_DECAY + nh, sl] = jnp.exp(m[s] - mlast)
            m[s] = total + mlast
            cols_ref[sl, :] = jnp.concatenate([b_c * LOG2E - K_SCALE_LOG2, pad], axis=0).T


def _mlstm_gates(gt, seq):
    n = gt.shape[1]
    return pl.pallas_call(
        functools.partial(_mlstm_gate_kernel, seq=seq),
        grid=(1,),
        in_specs=[pl.BlockSpec((2 * MLSTM_HEADS, n), lambda i: (0, 0))],
        out_specs=[pl.BlockSpec((GATE_ROWS, n), lambda i: (0, 0)),
                   pl.BlockSpec((n, LANES), lambda i: (0, 0))],
        out_shape=[jax.ShapeDtypeStruct((GATE_ROWS, n), F32),
                   jax.ShapeDtypeStruct((n, LANES), F32)],
        compiler_params=pltpu.CompilerParams(
            dimension_semantics=("arbitrary",), vmem_limit_bytes=VMEM_LIMIT_BYTES),
        name="mlstm_gates",
    )(gt)


def _mlstm_kernel(rows_ref, cols_ref, qk_ref, vt_ref, og_ref, hn_ref, o_ref, ct_scr, n_scr):
    L = MLSTM_CHUNK
    nh = MLSTM_HEADS
    contract_lanes = (((1,), (1,)), ((), ()))

    @pl.when(pl.program_id(1) == 0)
    def _():
        ct_scr[...] = jnp.zeros_like(ct_scr)
        n_scr[...] = jnp.zeros_like(n_scr)

    row_i = lax.broadcasted_iota(jnp.int32, (L, L), 0)
    col_i = lax.broadcasted_iota(jnp.int32, (L, L), 1)
    causal_t = row_i <= col_i
    lo_lane = col_i < MLSTM_QK_DIM
    zero = jnp.zeros((), BF16)

    n_chunks = qk_ref.shape[0] // L
    units = [(ci, p, par) for ci in range(n_chunks) for p in range(nh // 2) for par in range(2)]
    ct = [ct_scr[p] for p in range(nh // 2)]
    nn = [n_scr[p] for p in range(nh // 2)]
    gate_rows = {}
    upd = {}

    def chunk_gates(ci):
        if ci not in gate_rows:
            rs = slice(ci * L, (ci + 1) * L)
            wk = rows_ref[ROW_WK:ROW_WK + nh, rs]
            gate_rows[ci] = dict(
                m2=rows_ref[ROW_M2:ROW_M2 + nh, rs], w_inter=rows_ref[ROW_WINTER:ROW_WINTER + nh, rs],
                e_neg=rows_ref[ROW_ENEG:ROW_ENEG + nh, rs], wk=wk,
                decay=rows_ref[ROW_DECAY:ROW_DECAY + nh, rs], b2cols=cols_ref[rs, :],
                wk_b=jnp.concatenate([wk, wk], axis=0).astype(BF16))
        return gate_rows[ci]

    def front(ci, p, par):
        g = chunk_gates(ci)
        rs = slice(ci * L, (ci + 1) * L)
        hh = 2 * p + par
        q2 = qk_ref[rs, p * LANES:(p + 1) * LANES]
        k2 = qk_ref[rs, SZ_MQ + p * LANES:SZ_MQ + (p + 1) * LANES]
        qm = jnp.where(lo_lane, q2, zero) if par == 0 else jnp.where(lo_lane, zero, q2)
        vt = vt_ref[hh * MLSTM_V_DIM:(hh + 1) * MLSTM_V_DIM, rs]
        st = lax.dot_general(k2, qm, contract_lanes, preferred_element_type=F32)
        n2b = jnp.broadcast_to(nn[p], (2 * SUBLANES, LANES)).astype(BF16)
        qn = lax.dot_general(n2b, qm, contract_lanes, preferred_element_type=F32)[0:1, :]
        inter = lax.dot_general(ct[p].astype(BF16), qm, contract_lanes,
                                preferred_element_type=F32)
        vw = (vt.astype(F32) * g["wk"][hh:hh + 1, :]).astype(BF16)
        upd[(ci, p, par)] = jnp.dot(vw, k2, preferred_element_type=F32)
        if par == 1:
            h0, h1 = 2 * p, 2 * p + 1
            dec = jnp.where(lo_lane[0:1, :], g["decay"][h0:h0 + 1, :], g["decay"][h1:h1 + 1, :])
            n_inc = jnp.dot(g["wk_b"], k2, preferred_element_type=F32)
            ct[p] = dec * ct[p] + jnp.where(lo_lane, upd.pop((ci, p, 0)), upd.pop((ci, p, 1)))
            nn[p] = dec * nn[p] + jnp.where(lo_lane[0:1, :], n_inc[h0:h0 + 1, :],
                                            n_inc[h1:h1 + 1, :])
        return st, qn, inter

    def back(ci, p, par, st, qn, inter):
        g = chunk_gates(ci)
        rs = slice(ci * L, (ci + 1) * L)
        hh = 2 * p + par
        vsl = slice(hh * MLSTM_V_DIM, (hh + 1) * MLSTM_V_DIM)
        dt = jnp.where(causal_t,
                       jnp.exp2(g["b2cols"][:, hh:hh + 1] - g["m2"][hh:hh + 1, :]), 0.0)
        sct = st * dt
        wi = g["w_inter"][hh:hh + 1, :]
        den = jnp.sum(sct, axis=0, keepdims=True) + wi * qn
        numt = jnp.dot(vt_ref[vsl, rs], sct.astype(BF16), preferred_element_type=F32) + wi * inter
        rden = 1.0 / jnp.maximum(jnp.abs(den), g["e_neg"][hh:hh + 1, :])
        ms = jnp.mean(numt * numt, axis=0, keepdims=True)
        scale = rden * lax.rsqrt(rden * rden * ms + NORM_EPS)
        hv = (numt * scale).T
        gate = jax.nn.sigmoid(og_ref[rs, vsl].astype(F32))
        o_ref[rs, vsl] = (hv * hn_ref[:, vsl] * gate).astype(BF16)

    pending = [front(*u) for u in units[:MLSTM_LOOKAHEAD]]
    for idx, u in enumerate(units):
        back(*u, *pending[idx])
        pending[idx] = None
        if idx + MLSTM_LOOKAHEAD < len(units):
            pending.append(front(*units[idx + MLSTM_LOOKAHEAD]))
    for p in range(nh // 2):
        ct_scr[p] = ct[p]
        n_scr[p] = nn[p]


def _mlstm(rows, cols, qk, v, og, hn, batch, seq):
    n = qk.shape[0]
    tm = TM_MLSTM
    nt = seq // tm
    row = pl.BlockSpec((tm, SZ_MV), lambda b, c: (b * nt + c, 0))
    return pl.pallas_call(
        _mlstm_kernel,
        grid=(batch, nt),
        in_specs=[
            pl.BlockSpec((GATE_ROWS, tm), lambda b, c: (0, b * nt + c)),
            pl.BlockSpec((tm, LANES), lambda b, c: (b * nt + c, 0)),
            row,
            pl.BlockSpec((SZ_MV, tm), lambda b, c: (0, b * nt + c)),
            row,
            _const_spec((1, SZ_MV)),
        ],
        out_specs=row,
        out_shape=jax.ShapeDtypeStruct((n, SZ_MV), BF16),
        scratch_shapes=[
            pltpu.VMEM((MLSTM_HEADS // 2, 2 * MLSTM_QK_DIM, MLSTM_V_DIM), F32),
            pltpu.VMEM((MLSTM_HEADS // 2, 1, 2 * MLSTM_QK_DIM), F32),
        ],
        compiler_params=pltpu.CompilerParams(
            dimension_semantics=("arbitrary", "arbitrary"), vmem_limit_bytes=VMEM_LIMIT_BYTES),
        name="mlstm",
    )(rows, cols, qk, v, og, hn)


def _merge_stage_weights(wpa_hbm, wpm_hbm, wo_hbm, wpa_ref, wpm_ref, wo_ref, stage, sem):
    hd = ATTN_HEAD_DIM
    wpa_copies = [pltpu.make_async_copy(wpa_hbm.at[pl.ds(head * hd, hd), :],
                                        stage.at[0, pl.ds(pos * hd, hd), :], sem.at[0])
                  for pos, head in enumerate(ATTN_HEAD_ORDER)]
    wpm_copy = pltpu.make_async_copy(wpm_hbm, stage.at[1], sem.at[1])
    wo_copy = pltpu.make_async_copy(wo_hbm, stage.at[0], sem.at[0])
    for cp in wpa_copies:
        cp.start()
    wpm_copy.start()
    for cp in wpa_copies:
        cp.wait()
    wpa_ref[...] = stage[0].astype(BF16)
    wo_copy.start()
    wpm_copy.wait()
    wpm_ref[...] = stage[1].astype(BF16)
    wo_copy.wait()
    wo_ref[...] = stage[0].astype(BF16)


def _merge_kernel(x_ref, a_ref, hm_ref, gate_ref, wpa_hbm, wpm_hbm, wo_hbm, o_ref,
                  wpa_ref, wpm_ref, wo_ref, stage, sem):
    @pl.when(pl.program_id(0) == 0)
    def _():
        _merge_stage_weights(wpa_hbm, wpm_hbm, wo_hbm, wpa_ref, wpm_ref, wo_ref, stage, sem)

    ya = jnp.dot(a_ref[...], wpa_ref[...], preferred_element_type=F32)
    ym = jnp.dot(hm_ref[...], wpm_ref[...], preferred_element_type=F32)
    ga = jax.nn.sigmoid(gate_ref[:, 0:D_MODEL].astype(F32))
    gm = jax.nn.sigmoid(gate_ref[:, D_MODEL:2 * D_MODEL].astype(F32))
    merged = (ga * ya + gm * ym).astype(BF16)
    o_ref[...] = x_ref[...] + jnp.dot(merged, wo_ref[...], preferred_element_type=F32)


def _merge(x, a, hm, gates, wpa, wpm, wo):
    n = x.shape[0]
    tm = TM_MERGE
    row = lambda width: pl.BlockSpec((tm, width), lambda i: (i, 0))
    return pl.pallas_call(
        _merge_kernel,
        grid=(n // tm,),
        in_specs=[row(D_MODEL), row(SZ_AQ), row(SZ_MV), row(2 * D_MODEL),
                  pl.BlockSpec(memory_space=pl.ANY), pl.BlockSpec(memory_space=pl.ANY),
                  pl.BlockSpec(memory_space=pl.ANY)],
        out_specs=row(D_MODEL),
        out_shape=jax.ShapeDtypeStruct((n, D_MODEL), F32),
        scratch_shapes=[pltpu.VMEM((D_MODEL, D_MODEL), BF16)] * 3 + [
            pltpu.VMEM((2, D_MODEL, D_MODEL), F32), pltpu.SemaphoreType.DMA((2,))],
        compiler_params=pltpu.CompilerParams(
            dimension_semantics=("arbitrary",), vmem_limit_bytes=VMEM_LIMIT_BYTES),
        name="merge",
    )(x, a, hm, gates, wpa, wpm, wo)


def _to_head_order(t, axis):
    shape = t.shape
    t = t.reshape(shape[:axis] + (2, 2, ATTN_GROUP) + shape[axis + 1:])
    t = jnp.swapaxes(t, axis + 1, axis + 2)
    return t.reshape(shape)


def _pack_inproj(w_in, b_in):
    wt_all = w_in.T.astype(BF16)
    o = 0
    parts = {}
    for name, size in (("aq", SZ_AQ), ("ak", SZ_AK), ("av", SZ_AV), ("mq", SZ_MQ), ("mk", SZ_MK),
                       ("mv", SZ_MV), ("mo", SZ_MO), ("mi", SZ_MI), ("mf", SZ_MF),
                       ("g", 2 * D_MODEL)):
        parts[name] = (wt_all[o:o + size], b_in[o:o + size])
        o += size

    def pair_heads(t):
        return _to_head_order(t.reshape((ATTN_HEADS, ATTN_HEAD_DIM) + t.shape[1:]), 0).reshape(t.shape)

    order = [tuple(pair_heads(t) for t in parts["aq"]), parts["ak"],
             parts["mq"], parts["mk"], parts["mo"], parts["g"]]
    wt = jnp.concatenate([t[0] for t in order], axis=0)
    b = jnp.concatenate([t[1] for t in order], axis=0).reshape(1, PACKED_WIDTH)
    wif = jnp.concatenate([parts["mi"][0], parts["mf"][0]], axis=0)
    bif = jnp.concatenate([parts["mi"][1], parts["mf"][1]], axis=0).reshape(2 * MLSTM_HEADS, 1)
    wvt = jnp.concatenate([parts["mv"][0], parts["av"][0]], axis=0)
    bvt = jnp.concatenate([parts["mv"][1], parts["av"][1]], axis=0).reshape(SZ_MV + SZ_AV, 1)
    return wt, b, wif, bif, wvt, bvt


def kernel(x, ffn1_norm, ffn1_w_gate, ffn1_w_up, ffn1_w_down, mix_norm, w_in, b_in, attn_sinks,
           mlstm_conv, mlstm_head_norm, w_proj_attn, w_proj_mlstm, w_out, ffn2_norm, ffn2_w_gate,
           ffn2_w_up, ffn2_w_down, final_norm):
    batch, seq, d = x.shape
    assert d == D_MODEL and ffn1_norm.shape[0] == 1, "one layer of width D_MODEL"
    assert all(seq % t == 0 for t in (TM_INPROJ, TM_MLSTM, ATTN_BLOCKS * WINDOW))
    assert all((batch * seq) % t == 0 for t in (TM_FFN, TM_MERGE))
    n = batch * seq
    xf = x.reshape(n, d)
    fin = final_norm.reshape(1, d)

    x1 = _ffn(xf, ffn1_norm[0].reshape(1, d), ffn1_w_gate[0], ffn1_w_up[0], ffn1_w_down[0], fin,
              apply_final=False)

    w, b, wif, bif, wvt, bvt = _pack_inproj(w_in[0], b_in[0])
    q, kk, avt, mqk, mvt, mo, gates, gt = _inproj(x1, mix_norm[0].reshape(1, d), w, b, wif, bif,
                                                  wvt, bvt, mlstm_conv[0], seq)

    ya = _attention(attn_sinks[0], q, kk, avt, batch, seq)

    rows, cols = _mlstm_gates(gt, seq)
    hm = _mlstm(rows, cols, mqk, mvt, mo, mlstm_head_norm[0].reshape(1, SZ_MV), batch, seq)

    x2 = _merge(x1, ya, hm, gates, w_proj_attn[0], w_proj_mlstm[0], w_out[0])

    out = _ffn(x2, ffn2_norm[0].reshape(1, d), ffn2_w_gate[0], ffn2_w_up[0], ffn2_w_down[0], fin,
               apply_final=True)
    return out.reshape(batch, seq, d)
```

```python
import functools

import jax
import jax.numpy as jnp
from jax import lax
from jax.experimental import pallas as pl
from jax.experimental.pallas import tpu as pltpu

F32 = jnp.float32
BF16 = jnp.bfloat16

D_MODEL = 1024
ATTN_HEAD_DIM = 64
ATTN_HEADS = 16
ATTN_KV_HEADS = 4
ATTN_GROUP = 4
WINDOW = 128
MLSTM_HEADS = 8
MLSTM_V_DIM = 128
MLSTM_QK_DIM = 64
CONV_WIDTH = 4
GATE_SOFTCAP = 15.0
D_FF = 2816
FFN_RESIDUAL_WEIGHT = 0.5
NORM_EPS = 1e-6

SZ_AQ, SZ_AK, SZ_AV = 1024, 256, 256
SZ_MQ, SZ_MK, SZ_MV, SZ_MO = 512, 512, 1024, 1024
SZ_MI, SZ_MF = 8, 8

LANES = 128
SUBLANES = 8
MXU_DIM = 256
VMEM_LIMIT_BYTES = 56 * 1024 * 1024

MLSTM_CHUNK = 128
FF_CHUNK = MXU_DIM
TM_FFN = 1024
TM_INPROJ = 1024
TM_MERGE = 1024
TM_MLSTM = 8 * MLSTM_CHUNK
ATTN_LOOKAHEAD = 4
ATTN_BLOCKS = 8
MLSTM_LOOKAHEAD = 2

LOG2E = 1.4426950408889634
K_SCALE_LOG2 = 3.0
ROW_M2, ROW_WINTER, ROW_ENEG, ROW_WK, ROW_DECAY = 0, 8, 16, 24, 32
GATE_ROWS = 40

COL_Q = 0
COL_KK = 1024
COL_MQK = 1280
COL_MO = 2304
COL_G = 3328
PACKED_WIDTH = 5376

ATTN_HEAD_ORDER = tuple((2 * (t // 4) + par) * ATTN_GROUP + t % 4
                        for t in range(ATTN_HEADS // 2) for par in range(2))


def _rms(x, g):
    return x * lax.rsqrt(jnp.mean(x * x, axis=-1, keepdims=True) + NORM_EPS) * g


def _const_spec(shape):
    zeros = (0,) * len(shape)
    return pl.BlockSpec(shape, lambda *_: zeros, pipeline_mode=pl.Buffered(1))


def _ffn_stage_weights(wg_hbm, wu_hbm, wd_hbm, wg_ref, wu_ref, wd_ref, gu_stage, d_stage, sem):
    n_chunks = D_FF // FF_CHUNK

    def copies(c, slot):
        cols = pl.ds(c * FF_CHUNK, FF_CHUNK)
        return (pltpu.make_async_copy(wg_hbm.at[:, cols], gu_stage.at[slot, 0], sem.at[slot, 0]),
                pltpu.make_async_copy(wu_hbm.at[:, cols], gu_stage.at[slot, 1], sem.at[slot, 1]),
                pltpu.make_async_copy(wd_hbm.at[cols, :], d_stage.at[slot], sem.at[slot, 2]))

    for cp in copies(0, 0):
        cp.start()
    for c in range(n_chunks):
        slot = c % 2
        if c + 1 < n_chunks:
            for cp in copies(c + 1, 1 - slot):
                cp.start()
        for cp in copies(c, slot):
            cp.wait()
        sl = slice(c * FF_CHUNK, (c + 1) * FF_CHUNK)
        wg_ref[:, sl] = gu_stage[slot, 0].astype(BF16)
        wu_ref[:, sl] = gu_stage[slot, 1].astype(BF16)
        wd_ref[sl, :] = d_stage[slot].astype(BF16)


def _ffn_kernel(x_ref, g_ref, wg_hbm, wu_hbm, wd_hbm, fin_ref, o_ref,
                wg_ref, wu_ref, wd_ref, gu_stage, d_stage, sem, acc_ref, *, apply_final):
    @pl.when(pl.program_id(0) == 0)
    def _():
        _ffn_stage_weights(wg_hbm, wu_hbm, wd_hbm, wg_ref, wu_ref, wd_ref, gu_stage, d_stage, sem)

    x = x_ref[...]
    h = _rms(x, g_ref[...]).astype(BF16)
    for c in range(D_FF // FF_CHUNK):
        sl = slice(c * FF_CHUNK, (c + 1) * FF_CHUNK)
        a = jnp.dot(h, wg_ref[:, sl], preferred_element_type=F32)
        u = jnp.dot(h, wu_ref[:, sl], preferred_element_type=F32)
        act = (a * jax.nn.sigmoid(a) * u).astype(BF16)
        d = jnp.dot(act, wd_ref[sl, :], preferred_element_type=F32)
        if c == 0:
            acc_ref[...] = d
        else:
            acc_ref[...] += d
    y = x + FFN_RESIDUAL_WEIGHT * acc_ref[...]
    if apply_final:
        y = _rms(y, fin_ref[...])
    o_ref[...] = y


def _ffn(x, g, wg, wu, wd, fin, apply_final):
    n = x.shape[0]
    tm = TM_FFN
    return pl.pallas_call(
        functools.partial(_ffn_kernel, apply_final=apply_final),
        grid=(n // tm,),
        in_specs=[
            pl.BlockSpec((tm, D_MODEL), lambda i: (i, 0)),
            _const_spec((1, D_MODEL)),
            pl.BlockSpec(memory_space=pl.ANY),
            pl.BlockSpec(memory_space=pl.ANY),
            pl.BlockSpec(memory_space=pl.ANY),
            _const_spec((1, D_MODEL)),
        ],
        out_specs=pl.BlockSpec((tm, D_MODEL), lambda i: (i, 0)),
        out_shape=jax.ShapeDtypeStruct((n, D_MODEL), F32),
        scratch_shapes=[
            pltpu.VMEM((D_MODEL, D_FF), BF16),
            pltpu.VMEM((D_MODEL, D_FF), BF16),
            pltpu.VMEM((D_FF, D_MODEL), BF16),
            pltpu.VMEM((2, 2, D_MODEL, FF_CHUNK), F32),
            pltpu.VMEM((2, FF_CHUNK, D_MODEL), F32),
            pltpu.SemaphoreType.DMA((2, 3)),
            pltpu.VMEM((tm, D_MODEL), F32),
        ],
        compiler_params=pltpu.CompilerParams(
            dimension_semantics=("arbitrary",), vmem_limit_bytes=VMEM_LIMIT_BYTES),
        name="ffn_final" if apply_final else "ffn",
    )(x, g, wg, wu, wd, fin)


def _inproj_kernel(x_ref, g_ref, w_ref, b_ref, wif_ref, bif_ref, wvt_ref, bvt_ref, conv_ref,
                   q_ref, kk_ref, avt_ref, mqk_ref, mvt_ref, mo_ref, gate_ref, gt_ref,
                   conv_scr, *, tiles_per_seq):
    i = pl.program_id(0)
    tm = x_ref.shape[0]
    h = _rms(x_ref[...], g_ref[...]).astype(BF16)

    def seg(c0, width):
        return (lax.dot_general(h, w_ref[c0:c0 + width, :], (((1,), (1,)), ((), ())),
                                preferred_element_type=F32) + b_ref[:, c0:c0 + width])

    def seg_t(wt_ref, bt_ref, r0, rows):
        return (lax.dot_general(wt_ref[r0:r0 + rows, :], h, (((1,), (1,)), ((), ())),
                                preferred_element_type=F32) + bt_ref[r0:r0 + rows, :])

    @pl.when(i == 0)
    def _():
        conv_scr[...] = jnp.zeros_like(conv_scr)

    carry = jnp.where(i % tiles_per_seq == 0, 0.0, conv_scr[...])

    def conv_slab(c, width):
        z = seg(COL_MQK + c, width)
        conv_scr[:, c:c + width] = z[tm - SUBLANES:tm, :]
        zc = jnp.concatenate([carry[:, c:c + width], z], axis=0)
        acc = z * conv_ref[CONV_WIDTH - 1:CONV_WIDTH, c:c + width]
        for k in range(1, CONV_WIDTH):
            zk = pltpu.roll(zc, k, 0)[SUBLANES:, :]
            acc = acc + zk * conv_ref[CONV_WIDTH - 1 - k:CONV_WIDTH - k, c:c + width]
        mqk_ref[:, c:c + width] = (acc * jax.nn.sigmoid(acc)).astype(BF16)

    dw = MXU_DIM
    for k in range(SZ_AQ // dw):
        sl = slice(k * dw, (k + 1) * dw)
        conv_slab(k * dw, dw)
        q_ref[:, sl] = (seg(COL_Q + k * dw, dw) * (LOG2E * ATTN_HEAD_DIM ** -0.5)).astype(BF16)
        if k == 0:
            kk_ref[...] = seg(COL_KK, SZ_AK).astype(BF16)
        elif k == 1:
            avt_ref[...] = seg_t(wvt_ref, bvt_ref, SZ_MV, SZ_AV).astype(BF16)
        mvt_ref[sl, :] = seg_t(wvt_ref, bvt_ref, k * dw, dw).astype(BF16)
        mo_ref[:, sl] = seg(COL_MO + k * dw, dw).astype(BF16)
        gate_ref[:, 2 * k * dw:(2 * k + 2) * dw] = seg(COL_G + 2 * k * dw, 2 * dw).astype(BF16)

    gt_ref[...] = seg_t(wif_ref, bif_ref, 0, 2 * MLSTM_HEADS)


def _inproj(x, g, w, b, wif, bif, wvt, bvt, conv, seq):
    n = x.shape[0]
    tm = TM_INPROJ
    row = lambda width: pl.BlockSpec((tm, width), lambda i: (i, 0))
    col = lambda height: pl.BlockSpec((height, tm), lambda i: (0, i))
    bf = lambda width: jax.ShapeDtypeStruct((n, width), BF16)
    return pl.pallas_call(
        functools.partial(_inproj_kernel, tiles_per_seq=seq // tm),
        grid=(n // tm,),
        in_specs=[
            row(D_MODEL),
            _const_spec((1, D_MODEL)),
            _const_spec((PACKED_WIDTH, D_MODEL)),
            _const_spec((1, PACKED_WIDTH)),
            _const_spec((2 * MLSTM_HEADS, D_MODEL)),
            _const_spec((2 * MLSTM_HEADS, 1)),
            _const_spec((SZ_MV + SZ_AV, D_MODEL)),
            _const_spec((SZ_MV + SZ_AV, 1)),
            _const_spec((CONV_WIDTH, SZ_MQ + SZ_MK)),
        ],
        out_specs=[row(SZ_AQ), row(SZ_AK), col(SZ_AV), row(SZ_MQ + SZ_MK), col(SZ_MV), row(SZ_MO),
                   row(2 * D_MODEL), col(2 * MLSTM_HEADS)],
        out_shape=[bf(SZ_AQ), bf(SZ_AK), jax.ShapeDtypeStruct((SZ_AV, n), BF16),
                   bf(SZ_MQ + SZ_MK), jax.ShapeDtypeStruct((SZ_MV, n), BF16), bf(SZ_MO),
                   bf(2 * D_MODEL), jax.ShapeDtypeStruct((2 * MLSTM_HEADS, n), F32)],
        scratch_shapes=[pltpu.VMEM((SUBLANES, SZ_MQ + SZ_MK), F32)],
        compiler_params=pltpu.CompilerParams(
            dimension_semantics=("arbitrary",), vmem_limit_bytes=VMEM_LIMIT_BYTES),
        name="inproj",
    )(x, g, w, b, wif, bif, wvt, bvt, conv)


def _attn_kernel(sink_ref, q_ref, kc_ref, kp_ref, vtc_ref, vtp_ref, o_ref, k_scr, vt_scr):
    w = WINDOW
    contract_lanes = (((1,), (1,)), ((), ()))
    k_scr[0:w, :] = kp_ref[...]
    k_scr[w:, :] = kc_ref[...]
    vt_scr[:, 0:w] = vtp_ref[...]
    vt_scr[:, w:] = vtc_ref[...]
    kj = lax.broadcasted_iota(jnp.int32, (2 * w, w), 0)
    qi = lax.broadcasted_iota(jnp.int32, (2 * w, w), 1)
    band = (kj > qi) & (kj <= qi + w)
    band_first = band & ((kj >= w) | (pl.program_id(1) > 0))
    lo = lax.broadcasted_iota(jnp.int32, (w, LANES), 1) < ATTN_HEAD_DIM
    zero = jnp.zeros((), BF16)
    pairs = [(jb, t) for jb in range(q_ref.shape[0] // w) for t in range(ATTN_HEADS // 2)]

    def scores(jb, t):
        q2 = q_ref[jb * w:(jb + 1) * w, t * LANES:(t + 1) * LANES]
        qm2 = jnp.concatenate([jnp.where(lo, q2, zero), jnp.where(lo, zero, q2)], axis=0)
        kk = k_scr[jb * w:(jb + 2) * w, (t // 4) * LANES:(t // 4 + 1) * LANES]
        return lax.dot_general(kk, qm2, contract_lanes, preferred_element_type=F32)

    def head_out(jb, t, par, s):
        s = jnp.where(band_first if jb == 0 else band, s, -jnp.inf)
        sink = sink_ref[ATTN_HEAD_ORDER[2 * t + par]] * LOG2E
        mx = jnp.maximum(jnp.max(s, axis=0, keepdims=True), sink)
        pr = jnp.exp2(s - mx)
        denom = jnp.sum(pr, axis=0, keepdims=True) + jnp.exp2(sink - mx)
        kvh = 2 * (t // 4) + par
        vt = vt_scr[kvh * ATTN_HEAD_DIM:(kvh + 1) * ATTN_HEAD_DIM, jb * w:(jb + 2) * w]
        return jnp.dot(vt, pr.astype(BF16), preferred_element_type=F32) * (1.0 / denom)

    pending = [scores(*u) for u in pairs[:ATTN_LOOKAHEAD]]
    for idx, (jb, t) in enumerate(pairs):
        s2 = pending[idx]
        pending[idx] = None
        outs = [head_out(jb, t, par, s2[:, par * w:(par + 1) * w]) for par in range(2)]
        if idx + ATTN_LOOKAHEAD < len(pairs):
            pending.append(scores(*pairs[idx + ATTN_LOOKAHEAD]))
        pair = jnp.concatenate(outs, axis=0)
        o_ref[jb * w:(jb + 1) * w, t * LANES:(t + 1) * LANES] = pair.T.astype(BF16)


def _attention(sinks, q, kk, vt, batch, seq):
    n = q.shape[0]
    tq = ATTN_BLOCKS * WINDOW
    nt = seq // tq
    cur = lambda b, j: b * nt + j
    prev = lambda b, j: (b * nt + j) * ATTN_BLOCKS - jnp.minimum(j, 1)
    return pl.pallas_call(
        _attn_kernel,
        grid=(batch, nt),
        in_specs=[
            pl.BlockSpec(memory_space=pltpu.SMEM),
            pl.BlockSpec((tq, SZ_AQ), lambda b, j: (cur(b, j), 0)),
            pl.BlockSpec((tq, SZ_AK), lambda b, j: (cur(b, j), 0)),
            pl.BlockSpec((WINDOW, SZ_AK), lambda b, j: (prev(b, j), 0)),
            pl.BlockSpec((SZ_AV, tq), lambda b, j: (0, cur(b, j))),
            pl.BlockSpec((SZ_AV, WINDOW), lambda b, j: (0, prev(b, j))),
        ],
        out_specs=pl.BlockSpec((tq, SZ_AQ), lambda b, j: (cur(b, j), 0)),
        out_shape=jax.ShapeDtypeStruct((n, SZ_AQ), BF16),
        scratch_shapes=[pltpu.VMEM((WINDOW + tq, SZ_AK), BF16),
                        pltpu.VMEM((SZ_AV, WINDOW + tq), BF16)],
        compiler_params=pltpu.CompilerParams(
            dimension_semantics=("arbitrary", "arbitrary"), vmem_limit_bytes=VMEM_LIMIT_BYTES),
        name="attention",
    )(sinks, q, kk, kk, vt, vt)


def _log_sigmoid(x):
    return -(jnp.maximum(-x, 0.0) + jnp.log(1.0 + jnp.exp(-jnp.abs(x))))


def _mlstm_gate_kernel(g_ref, rows_ref, cols_ref, *, seq):
    L = MLSTM_CHUNK
    nh = MLSTM_HEADS
    n = g_ref.shape[1]
    g = g_ref[...]
    ig = GATE_SOFTCAP * jnp.tanh(g[0:nh] / GATE_SOFTCAP)
    fg = GATE_SOFTCAP * jnp.tanh(g[nh:2 * nh] / GATE_SOFTCAP)
    lf = _log_sigmoid(fg)
    pos = lax.broadcasted_iota(jnp.int32, (nh, n), 1) & (L - 1)
    a = lf
    sh = 1
    while sh < L:
        a = a + jnp.where(pos >= sh, pltpu.roll(a, sh, 1), 0.0)
        sh *= 2
    bvec = ig - a
    pm = bvec
    sh = 1
    while sh < L:
        pm = jnp.maximum(pm, jnp.where(pos >= sh, pltpu.roll(pm, sh, 1), -jnp.inf))
        sh *= 2
    pad = jnp.zeros((L - nh, L), F32)
    m = [jnp.zeros((nh, L), F32) for _ in range(n // seq)]
    for c in range(seq // L):
        for s in range(n // seq):
            sl = slice(s * seq + c * L, s * seq + (c + 1) * L)
            a_c, b_c = a[:, sl], bvec[:, sl]
            mrow = jnp.maximum(m[s], pm[:, sl])
            total = jnp.broadcast_to(a_c[:, L - 1:L], (nh, L))
            mlast = jnp.broadcast_to(mrow[:, L - 1:L], (nh, L))
            rows_ref[ROW_M2:ROW_M2 + nh, sl] = mrow * LOG2E
            rows_ref[ROW_WINTER:ROW_WINTER + nh, sl] = jnp.exp(m[s] - mrow)
            rows_ref[ROW_ENEG:ROW_ENEG + nh, sl] = jnp.exp(-(a_c + mrow))
            rows_ref[ROW_WK:ROW_WK + nh, sl] = jnp.exp(b_c - mlast) * MLSTM_QK_DIM ** -0.5
            rows_ref[ROW_DECAY:ROW_DECAY + nh, sl] = jnp.exp(m[s] - mlast)
            m[s] = total + mlast
            cols_ref[sl, :] = jnp.concatenate([b_c * LOG2E - K_SCALE_LOG2, pad], axis=0).T


def _mlstm_gates(gt, seq):
    n = gt.shape[1]
    return pl.pallas_call(
        functools.partial(_mlstm_gate_kernel, seq=seq),
        grid=(1,),
        in_specs=[pl.BlockSpec((2 * MLSTM_HEADS, n), lambda i: (0, 0))],
        out_specs=[pl.BlockSpec((GATE_ROWS, n), lambda i: (0, 0)),
                   pl.BlockSpec((n, LANES), lambda i: (0, 0))],
        out_shape=[jax.ShapeDtypeStruct((GATE_ROWS, n), F32),
                   jax.ShapeDtypeStruct((n, LANES), F32)],
        compiler_params=pltpu.CompilerParams(
            dimension_semantics=("arbitrary",), vmem_limit_bytes=VMEM_LIMIT_BYTES),
        name="mlstm_gates",
    )(gt)


def _mlstm_kernel(rows_ref, cols_ref, qk_ref, vt_ref, og_ref, hn_ref, o_ref, ct_scr, n_scr):
    L = MLSTM_CHUNK
    nh = MLSTM_HEADS
    contract_lanes = (((1,), (1,)), ((), ()))

    @pl.when(pl.program_id(1) == 0)
    def _():
        ct_scr[...] = jnp.zeros_like(ct_scr)
        n_scr[...] = jnp.zeros_like(n_scr)

    row_i = lax.broadcasted_iota(jnp.int32, (L, L), 0)
    col_i = lax.broadcasted_iota(jnp.int32, (L, L), 1)
    causal_t = row_i <= col_i
    lo_lane = col_i < MLSTM_QK_DIM
    zero = jnp.zeros((), BF16)

    n_chunks = qk_ref.shape[0] // L
    pairs = [(ci, p) for ci in range(n_chunks) for p in range(nh // 2)]
    ct = [ct_scr[p] for p in range(nh // 2)]
    nn = [n_scr[p] for p in range(nh // 2)]
    gate_rows = {}

    def chunk_gates(ci):
        if ci not in gate_rows:
            rs = slice(ci * L, (ci + 1) * L)
            wk = rows_ref[ROW_WK:ROW_WK + nh, rs]
            gate_rows[ci] = dict(
                m2=rows_ref[ROW_M2:ROW_M2 + nh, rs], w_inter=rows_ref[ROW_WINTER:ROW_WINTER + nh, rs],
                e_neg=rows_ref[ROW_ENEG:ROW_ENEG + nh, rs], wk=wk,
                decay=rows_ref[ROW_DECAY:ROW_DECAY + nh, rs], b2cols=cols_ref[rs, :],
                wk_b=jnp.concatenate([wk, wk], axis=0).astype(BF16))
        return gate_rows[ci]

    def front(ci, p):
        g = chunk_gates(ci)
        rs = slice(ci * L, (ci + 1) * L)
        h0, h1 = 2 * p, 2 * p + 1
        q2 = qk_ref[rs, p * LANES:(p + 1) * LANES]
        k2 = qk_ref[rs, SZ_MQ + p * LANES:SZ_MQ + (p + 1) * LANES]
        qm2 = jnp.concatenate([jnp.where(lo_lane, q2, zero), jnp.where(lo_lane, zero, q2)], axis=0)
        n2b = jnp.broadcast_to(nn[p], (2 * SUBLANES, LANES)).astype(BF16)
        stacked = jnp.concatenate([k2, ct[p].astype(BF16), n2b], axis=0)
        res = lax.dot_general(stacked, qm2, contract_lanes, preferred_element_type=F32)
        vw = [(vt_ref[hh * MLSTM_V_DIM:(hh + 1) * MLSTM_V_DIM, rs].astype(F32)
               * g["wk"][hh:hh + 1, :]).astype(BF16) for hh in (h0, h1)]
        upd = jnp.dot(jnp.concatenate(vw + [g["wk_b"]], axis=0), k2,
                      preferred_element_type=F32)
        n_inc = upd[2 * MLSTM_V_DIM:]
        dec = jnp.where(lo_lane[0:1, :], g["decay"][h0:h0 + 1, :], g["decay"][h1:h1 + 1, :])
        ct[p] = dec * ct[p] + jnp.where(lo_lane, upd[0:MLSTM_V_DIM], upd[MLSTM_V_DIM:2 * MLSTM_V_DIM])
        nn[p] = dec * nn[p] + jnp.where(lo_lane[0:1, :], n_inc[h0:h0 + 1, :], n_inc[h1:h1 + 1, :])
        return res

    def back(ci, p, par, res):
        st = res[0:L, par * L:(par + 1) * L]
        inter = res[L:L + MLSTM_V_DIM, par * L:(par + 1) * L]
        qn = res[L + MLSTM_V_DIM:L + MLSTM_V_DIM + 1, par * L:(par + 1) * L]
        g = chunk_gates(ci)
        rs = slice(ci * L, (ci + 1) * L)
        hh = 2 * p + par
        vsl = slice(hh * MLSTM_V_DIM, (hh + 1) * MLSTM_V_DIM)
        dt = jnp.where(causal_t,
                       jnp.exp2(g["b2cols"][:, hh:hh + 1] - g["m2"][hh:hh + 1, :]), 0.0)
        sct = st * dt
        wi = g["w_inter"][hh:hh + 1, :]
        den = jnp.sum(sct, axis=0, keepdims=True) + wi * qn
        numt = jnp.dot(vt_ref[vsl, rs], sct.astype(BF16), preferred_element_type=F32) + wi * inter
        rden = 1.0 / jnp.maximum(jnp.abs(den), g["e_neg"][hh:hh + 1, :])
        ms = jnp.mean(numt * numt, axis=0, keepdims=True)
        scale = rden * lax.rsqrt(rden * rden * ms + NORM_EPS)
        hv = (numt * scale).T
        gate = jax.nn.sigmoid(og_ref[rs, vsl].astype(F32))
        o_ref[rs, vsl] = (hv * hn_ref[:, vsl] * gate).astype(BF16)

    pending = [front(*u) for u in pairs[:MLSTM_LOOKAHEAD]]
    for idx, (ci, p) in enumerate(pairs):
        res = pending[idx]
        pending[idx] = None
        back(ci, p, 0, res)
        if idx + MLSTM_LOOKAHEAD < len(pairs):
            pending.append(front(*pairs[idx + MLSTM_LOOKAHEAD]))
        back(ci, p, 1, res)
    for p in range(nh // 2):
        ct_scr[p] = ct[p]
        n_scr[p] = nn[p]


def _mlstm(rows, cols, qk, v, og, hn, batch, seq):
    n = qk.shape[0]
    tm = TM_MLSTM
    nt = seq // tm
    row = pl.BlockSpec((tm, SZ_MV), lambda b, c: (b * nt + c, 0))
    return pl.pallas_call(
        _mlstm_kernel,
        grid=(batch, nt),
        in_specs=[
            pl.BlockSpec((GATE_ROWS, tm), lambda b, c: (0, b * nt + c)),
            pl.BlockSpec((tm, LANES), lambda b, c: (b * nt + c, 0)),
            row,
            pl.BlockSpec((SZ_MV, tm), lambda b, c: (0, b * nt + c)),
            row,
            _const_spec((1, SZ_MV)),
        ],
        out_specs=row,
        out_shape=jax.ShapeDtypeStruct((n, SZ_MV), BF16),
        scratch_shapes=[
            pltpu.VMEM((MLSTM_HEADS // 2, 2 * MLSTM_QK_DIM, MLSTM_V_DIM), F32),
            pltpu.VMEM((MLSTM_HEADS // 2, 1, 2 * MLSTM_QK_DIM), F32),
        ],
        compiler_params=pltpu.CompilerParams(
            dimension_semantics=("arbitrary", "arbitrary"), vmem_limit_bytes=VMEM_LIMIT_BYTES),
        name="mlstm",
    )(rows, cols, qk, v, og, hn)


def _merge_stage_weights(wpa_hbm, wpm_hbm, wo_hbm, wpa_ref, wpm_ref, wo_ref, stage, sem):
    hd = ATTN_HEAD_DIM
    wpa_copies = [pltpu.make_async_copy(wpa_hbm.at[pl.ds(head * hd, hd), :],
                                        stage.at[0, pl.ds(pos * hd, hd), :], sem.at[0])
                  for pos, head in enumerate(ATTN_HEAD_ORDER)]
    wpm_copy = pltpu.make_async_copy(wpm_hbm, stage.at[1], sem.at[1])
    wo_copy = pltpu.make_async_copy(wo_hbm, stage.at[0], sem.at[0])
    for cp in wpa_copies:
        cp.start()
    wpm_copy.start()
    for cp in wpa_copies:
        cp.wait()
    wpa_ref[...] = stage[0].astype(BF16)
    wo_copy.start()
    wpm_copy.wait()
    wpm_ref[...] = stage[1].astype(BF16)
    wo_copy.wait()
    wo_ref[...] = stage[0].astype(BF16)


def _merge_kernel(x_ref, a_ref, hm_ref, gate_ref, wpa_hbm, wpm_hbm, wo_hbm, o_ref,
                  wpa_ref, wpm_ref, wo_ref, stage, sem):
    @pl.when(pl.program_id(0) == 0)
    def _():
        _merge_stage_weights(wpa_hbm, wpm_hbm, wo_hbm, wpa_ref, wpm_ref, wo_ref, stage, sem)

    ya = jnp.dot(a_ref[...], wpa_ref[...], preferred_element_type=F32)
    ym = jnp.dot(hm_ref[...], wpm_ref[...], preferred_element_type=F32)
    ga = jax.nn.sigmoid(gate_ref[:, 0:D_MODEL].astype(F32))
    gm = jax.nn.sigmoid(gate_ref[:, D_MODEL:2 * D_MODEL].astype(F32))
    merged = (ga * ya + gm * ym).astype(BF16)
    o_ref[...] = x_ref[...] + jnp.dot(merged, wo_ref[...], preferred_element_type=F32)


def _merge(x, a, hm, gates, wpa, wpm, wo):
    n = x.shape[0]
    tm = TM_MERGE
    row = lambda width: pl.BlockSpec((tm, width), lambda i: (i, 0))
    return pl.pallas_call(
        _merge_kernel,
        grid=(n // tm,),
        in_specs=[row(D_MODEL), row(SZ_AQ), row(SZ_MV), row(2 * D_MODEL),
                  pl.BlockSpec(memory_space=pl.ANY), pl.BlockSpec(memory_space=pl.ANY),
                  pl.BlockSpec(memory_space=pl.ANY)],
        out_specs=row(D_MODEL),
        out_shape=jax.ShapeDtypeStruct((n, D_MODEL), F32),
        scratch_shapes=[pltpu.VMEM((D_MODEL, D_MODEL), BF16)] * 3 + [
            pltpu.VMEM((2, D_MODEL, D_MODEL), F32), pltpu.SemaphoreType.DMA((2,))],
        compiler_params=pltpu.CompilerParams(
            dimension_semantics=("arbitrary",), vmem_limit_bytes=VMEM_LIMIT_BYTES),
        name="merge",
    )(x, a, hm, gates, wpa, wpm, wo)


def _to_head_order(t, axis):
    shape = t.shape
    t = t.reshape(shape[:axis] + (2, 2, ATTN_GROUP) + shape[axis + 1:])
    t = jnp.swapaxes(t, axis + 1, axis + 2)
    return t.reshape(shape)


def _pack_inproj(w_in, b_in):
    wt_all = w_in.T.astype(BF16)
    o = 0
    parts = {}
    for name, size in (("aq", SZ_AQ), ("ak", SZ_AK), ("av", SZ_AV), ("mq", SZ_MQ), ("mk", SZ_MK),
                       ("mv", SZ_MV), ("mo", SZ_MO), ("mi", SZ_MI), ("mf", SZ_MF),
                       ("g", 2 * D_MODEL)):
        parts[name] = (wt_all[o:o + size], b_in[o:o + size])
        o += size

    def pair_heads(t):
        return _to_head_order(t.reshape((ATTN_HEADS, ATTN_HEAD_DIM) + t.shape[1:]), 0).reshape(t.shape)

    order = [tuple(pair_heads(t) for t in parts["aq"]), parts["ak"],
             parts["mq"], parts["mk"], parts["mo"], parts["g"]]
    wt = jnp.concatenate([t[0] for t in order], axis=0)
    b = jnp.concatenate([t[1] for t in order], axis=0).reshape(1, PACKED_WIDTH)
    wif = jnp.concatenate([parts["mi"][0], parts["mf"][0]], axis=0)
    bif = jnp.concatenate([parts["mi"][1], parts["mf"][1]], axis=0).reshape(2 * MLSTM_HEADS, 1)
    wvt = jnp.concatenate([parts["mv"][0], parts["av"][0]], axis=0)
    bvt = jnp.concatenate([parts["mv"][1], parts["av"][1]], axis=0).reshape(SZ_MV + SZ_AV, 1)
    return wt, b, wif, bif, wvt, bvt


def kernel(x, ffn1_norm, ffn1_w_gate, ffn1_w_up, ffn1_w_down, mix_norm, w_in, b_in, attn_sinks,
           mlstm_conv, mlstm_head_norm, w_proj_attn, w_proj_mlstm, w_out, ffn2_norm, ffn2_w_gate,
           ffn2_w_up, ffn2_w_down, final_norm):
    batch, seq, d = x.shape
    assert d == D_MODEL and ffn1_norm.shape[0] == 1, "one layer of width D_MODEL"
    assert all(seq % t == 0 for t in (TM_INPROJ, TM_MLSTM, ATTN_BLOCKS * WINDOW))
    assert all((batch * seq) % t == 0 for t in (TM_FFN, TM_MERGE))
    n = batch * seq
    xf = x.reshape(n, d)
    fin = final_norm.reshape(1, d)

    x1 = _ffn(xf, ffn1_norm[0].reshape(1, d), ffn1_w_gate[0], ffn1_w_up[0], ffn1_w_down[0], fin,
              apply_final=False)

    w, b, wif, bif, wvt, bvt = _pack_inproj(w_in[0], b_in[0])
    q, kk, avt, mqk, mvt, mo, gates, gt = _inproj(x1, mix_norm[0].reshape(1, d), w, b, wif, bif,
                                                  wvt, bvt, mlstm_conv[0], seq)

    ya = _attention(attn_sinks[0], q, kk, avt, batch, seq)

    rows, cols = _mlstm_gates(gt, seq)
    hm = _mlstm(rows, cols, mqk, mvt, mo, mlstm_head_norm[0].reshape(1, SZ_MV), batch, seq)

    x2 = _merge(x1, ya, hm, gates, w_proj_attn[0], w_proj_mlstm[0], w_out[0])

    out = _ffn(x2, ffn2_norm[0].reshape(1, d), ffn2_w_gate[0], ffn2_w_up[0], ffn2_w_down[0], fin,
               apply_final=True)
    return out.reshape(batch, seq, d)
```

```python
import functools

import jax
import jax.numpy as jnp
from jax import lax
from jax.experimental import pallas as pl
from jax.experimental.pallas import tpu as pltpu

F32 = jnp.float32
BF16 = jnp.bfloat16

D_MODEL = 1024
ATTN_HEAD_DIM = 64
ATTN_HEADS = 16
ATTN_KV_HEADS = 4
ATTN_GROUP = 4
WINDOW = 128
MLSTM_HEADS = 8
MLSTM_V_DIM = 128
MLSTM_QK_DIM = 64
CONV_WIDTH = 4
GATE_SOFTCAP = 15.0
D_FF = 2816
FFN_RESIDUAL_WEIGHT = 0.5
NORM_EPS = 1e-6

SZ_AQ, SZ_AK, SZ_AV = 1024, 256, 256
SZ_MQ, SZ_MK, SZ_MV, SZ_MO = 512, 512, 1024, 1024
SZ_MI, SZ_MF = 8, 8

LANES = 128
SUBLANES = 8
MXU_DIM = 256
VMEM_LIMIT_BYTES = 56 * 1024 * 1024

MLSTM_CHUNK = 128
FF_CHUNK = MXU_DIM
TM_FFN = 1024
TM_INPROJ = 1024
TM_MERGE = 1024
TM_MLSTM = 8 * MLSTM_CHUNK
ATTN_LOOKAHEAD = 4
ATTN_BLOCKS = 8
MLSTM_LOOKAHEAD = 2

LOG2E = 1.4426950408889634
K_SCALE_LOG2 = 3.0
ROW_M2, ROW_WINTER, ROW_ENEG, ROW_WK, ROW_DECAY = 0, 8, 16, 24, 32
GATE_ROWS = 40

COL_Q = 0
COL_KK = 1024
COL_MQK = 1280
COL_MO = 2304
COL_G = 3328
PACKED_WIDTH = 5376

ATTN_HEAD_ORDER = tuple((2 * (t // 4) + par) * ATTN_GROUP + t % 4
                        for t in range(ATTN_HEADS // 2) for par in range(2))


def _rms(x, g):
    return x * lax.rsqrt(jnp.mean(x * x, axis=-1, keepdims=True) + NORM_EPS) * g


def _const_spec(shape):
    zeros = (0,) * len(shape)
    return pl.BlockSpec(shape, lambda *_: zeros, pipeline_mode=pl.Buffered(1))


def _ffn_stage_weights(wg_hbm, wu_hbm, wd_hbm, wg_ref, wu_ref, wd_ref, gu_stage, d_stage, sem):
    n_chunks = D_FF // FF_CHUNK

    def copies(c, slot):
        cols = pl.ds(c * FF_CHUNK, FF_CHUNK)
        return (pltpu.make_async_copy(wg_hbm.at[:, cols], gu_stage.at[slot, 0], sem.at[slot, 0]),
                pltpu.make_async_copy(wu_hbm.at[:, cols], gu_stage.at[slot, 1], sem.at[slot, 1]),
                pltpu.make_async_copy(wd_hbm.at[cols, :], d_stage.at[slot], sem.at[slot, 2]))

    for cp in copies(0, 0):
        cp.start()
    for c in range(n_chunks):
        slot = c % 2
        if c + 1 < n_chunks:
            for cp in copies(c + 1, 1 - slot):
                cp.start()
        for cp in copies(c, slot):
            cp.wait()
        sl = slice(c * FF_CHUNK, (c + 1) * FF_CHUNK)
        wg_ref[:, sl] = gu_stage[slot, 0].astype(BF16)
        wu_ref[:, sl] = gu_stage[slot, 1].astype(BF16)
        wd_ref[sl, :] = d_stage[slot].astype(BF16)


def _ffn_kernel(x_ref, g_ref, wg_hbm, wu_hbm, wd_hbm, fin_ref, o_ref,
                wg_ref, wu_ref, wd_ref, gu_stage, d_stage, sem, acc_ref, *, apply_final):
    @pl.when(pl.program_id(0) == 0)
    def _():
        _ffn_stage_weights(wg_hbm, wu_hbm, wd_hbm, wg_ref, wu_ref, wd_ref, gu_stage, d_stage, sem)

    x = x_ref[...]
    h = _rms(x, g_ref[...]).astype(BF16)
    for c in range(D_FF // FF_CHUNK):
        sl = slice(c * FF_CHUNK, (c + 1) * FF_CHUNK)
        a = jnp.dot(h, wg_ref[:, sl], preferred_element_type=F32)
        u = jnp.dot(h, wu_ref[:, sl], preferred_element_type=F32)
        act = (a * jax.nn.sigmoid(a) * u).astype(BF16)
        d = jnp.dot(act, wd_ref[sl, :], preferred_element_type=F32)
        if c == 0:
            acc_ref[...] = d
        else:
            acc_ref[...] += d
    y = x + FFN_RESIDUAL_WEIGHT * acc_ref[...]
    if apply_final:
        y = _rms(y, fin_ref[...])
    o_ref[...] = y


def _ffn(x, g, wg, wu, wd, fin, apply_final):
    n = x.shape[0]
    tm = TM_FFN
    return pl.pallas_call(
        functools.partial(_ffn_kernel, apply_final=apply_final),
        grid=(n // tm,),
        in_specs=[
            pl.BlockSpec((tm, D_MODEL), lambda i: (i, 0)),
            _const_spec((1, D_MODEL)),
            pl.BlockSpec(memory_space=pl.ANY),
            pl.BlockSpec(memory_space=pl.ANY),
            pl.BlockSpec(memory_space=pl.ANY),
            _const_spec((1, D_MODEL)),
        ],
        out_specs=pl.BlockSpec((tm, D_MODEL), lambda i: (i, 0)),
        out_shape=jax.ShapeDtypeStruct((n, D_MODEL), F32),
        scratch_shapes=[
            pltpu.VMEM((D_MODEL, D_FF), BF16),
            pltpu.VMEM((D_MODEL, D_FF), BF16),
            pltpu.VMEM((D_FF, D_MODEL), BF16),
            pltpu.VMEM((2, 2, D_MODEL, FF_CHUNK), F32),
            pltpu.VMEM((2, FF_CHUNK, D_MODEL), F32),
            pltpu.SemaphoreType.DMA((2, 3)),
            pltpu.VMEM((tm, D_MODEL), F32),
        ],
        compiler_params=pltpu.CompilerParams(
            dimension_semantics=("arbitrary",), vmem_limit_bytes=VMEM_LIMIT_BYTES),
        name="ffn_final" if apply_final else "ffn",
    )(x, g, wg, wu, wd, fin)


def _inproj_kernel(x_ref, g_ref, w_ref, b_ref, wif_ref, bif_ref, wvt_ref, bvt_ref, conv_ref,
                   q_ref, kk_ref, avt_ref, mqk_ref, mvt_ref, mo_ref, gate_ref, gt_ref,
                   conv_scr, *, tiles_per_seq):
    i = pl.program_id(0)
    tm = x_ref.shape[0]
    h = _rms(x_ref[...], g_ref[...]).astype(BF16)

    def seg(c0, width):
        return (lax.dot_general(h, w_ref[c0:c0 + width, :], (((1,), (1,)), ((), ())),
                                preferred_element_type=F32) + b_ref[:, c0:c0 + width])

    def seg_t(wt_ref, bt_ref, r0, rows):
        return (lax.dot_general(wt_ref[r0:r0 + rows, :], h, (((1,), (1,)), ((), ())),
                                preferred_element_type=F32) + bt_ref[r0:r0 + rows, :])

    @pl.when(i == 0)
    def _():
        conv_scr[...] = jnp.zeros_like(conv_scr)

    carry = jnp.where(i % tiles_per_seq == 0, 0.0, conv_scr[...])

    def conv_slab(c, width):
        z = seg(COL_MQK + c, width)
        conv_scr[:, c:c + width] = z[tm - SUBLANES:tm, :]
        zc = jnp.concatenate([carry[:, c:c + width], z], axis=0)
        acc = z * conv_ref[CONV_WIDTH - 1:CONV_WIDTH, c:c + width]
        for k in range(1, CONV_WIDTH):
            zk = pltpu.roll(zc, k, 0)[SUBLANES:, :]
            acc = acc + zk * conv_ref[CONV_WIDTH - 1 - k:CONV_WIDTH - k, c:c + width]
        mqk_ref[:, c:c + width] = (acc * jax.nn.sigmoid(acc)).astype(BF16)

    dw = MXU_DIM
    for k in range(SZ_AQ // dw):
        sl = slice(k * dw, (k + 1) * dw)
        conv_slab(k * dw, dw)
        q_ref[:, sl] = (seg(COL_Q + k * dw, dw) * (LOG2E * ATTN_HEAD_DIM ** -0.5)).astype(BF16)
        if k == 0:
            kk_ref[...] = seg(COL_KK, SZ_AK).astype(BF16)
        elif k == 1:
            avt_ref[...] = seg_t(wvt_ref, bvt_ref, SZ_MV, SZ_AV).astype(BF16)
        mvt_ref[sl, :] = seg_t(wvt_ref, bvt_ref, k * dw, dw).astype(BF16)
        mo_ref[:, sl] = jax.nn.sigmoid(seg(COL_MO + k * dw, dw)).astype(BF16)
        gate_ref[:, 2 * k * dw:(2 * k + 2) * dw] = seg(COL_G + 2 * k * dw, 2 * dw).astype(BF16)

    gt_ref[...] = seg_t(wif_ref, bif_ref, 0, 2 * MLSTM_HEADS)


def _inproj(x, g, w, b, wif, bif, wvt, bvt, conv, seq):
    n = x.shape[0]
    tm = TM_INPROJ
    row = lambda width: pl.BlockSpec((tm, width), lambda i: (i, 0))
    col = lambda height: pl.BlockSpec((height, tm), lambda i: (0, i))
    bf = lambda width: jax.ShapeDtypeStruct((n, width), BF16)
    return pl.pallas_call(
        functools.partial(_inproj_kernel, tiles_per_seq=seq // tm),
        grid=(n // tm,),
        in_specs=[
            row(D_MODEL),
            _const_spec((1, D_MODEL)),
            _const_spec((PACKED_WIDTH, D_MODEL)),
            _const_spec((1, PACKED_WIDTH)),
            _const_spec((2 * MLSTM_HEADS, D_MODEL)),
            _const_spec((2 * MLSTM_HEADS, 1)),
            _const_spec((SZ_MV + SZ_AV, D_MODEL)),
            _const_spec((SZ_MV + SZ_AV, 1)),
            _const_spec((CONV_WIDTH, SZ_MQ + SZ_MK)),
        ],
        out_specs=[row(SZ_AQ), row(SZ_AK), col(SZ_AV), row(SZ_MQ + SZ_MK), col(SZ_MV), row(SZ_MO),
                   row(2 * D_MODEL), col(2 * MLSTM_HEADS)],
        out_shape=[bf(SZ_AQ), bf(SZ_AK), jax.ShapeDtypeStruct((SZ_AV, n), BF16),
                   bf(SZ_MQ + SZ_MK), jax.ShapeDtypeStruct((SZ_MV, n), BF16), bf(SZ_MO),
                   bf(2 * D_MODEL), jax.ShapeDtypeStruct((2 * MLSTM_HEADS, n), F32)],
        scratch_shapes=[pltpu.VMEM((SUBLANES, SZ_MQ + SZ_MK), F32)],
        compiler_params=pltpu.CompilerParams(
            dimension_semantics=("arbitrary",), vmem_limit_bytes=VMEM_LIMIT_BYTES),
        name="inproj",
    )(x, g, w, b, wif, bif, wvt, bvt, conv)


def _attn_kernel(sink_ref, q_ref, kc_ref, kp_ref, vtc_ref, vtp_ref, o_ref, k_scr, vt_scr):
    w = WINDOW
    contract_lanes = (((1,), (1,)), ((), ()))
    k_scr[0:w, :] = kp_ref[...]
    k_scr[w:, :] = kc_ref[...]
    vt_scr[:, 0:w] = vtp_ref[...]
    vt_scr[:, w:] = vtc_ref[...]
    kj = lax.broadcasted_iota(jnp.int32, (2 * w, w), 0)
    qi = lax.broadcasted_iota(jnp.int32, (2 * w, w), 1)
    band = (kj > qi) & (kj <= qi + w)
    band_first = band & ((kj >= w) | (pl.program_id(1) > 0))
    lo = lax.broadcasted_iota(jnp.int32, (w, LANES), 1) < ATTN_HEAD_DIM
    zero = jnp.zeros((), BF16)
    pairs = [(jb, t) for jb in range(q_ref.shape[0] // w) for t in range(ATTN_HEADS // 2)]

    def scores(jb, t):
        q2 = q_ref[jb * w:(jb + 1) * w, t * LANES:(t + 1) * LANES]
        qm2 = jnp.concatenate([jnp.where(lo, q2, zero), jnp.where(lo, zero, q2)], axis=0)
        kk = k_scr[jb * w:(jb + 2) * w, (t // 4) * LANES:(t // 4 + 1) * LANES]
        return lax.dot_general(kk, qm2, contract_lanes, preferred_element_type=F32)

    def head_out(jb, t, par, s):
        s = jnp.where(band_first if jb == 0 else band, s, -jnp.inf)
        sink = sink_ref[ATTN_HEAD_ORDER[2 * t + par]] * LOG2E
        mx = jnp.maximum(jnp.max(s, axis=0, keepdims=True), sink)
        pr = jnp.exp2(s - mx)
        denom = jnp.sum(pr, axis=0, keepdims=True) + jnp.exp2(sink - mx)
        kvh = 2 * (t // 4) + par
        vt = vt_scr[kvh * ATTN_HEAD_DIM:(kvh + 1) * ATTN_HEAD_DIM, jb * w:(jb + 2) * w]
        return jnp.dot(vt, pr.astype(BF16), preferred_element_type=F32) * (1.0 / denom)

    pending = [scores(*u) for u in pairs[:ATTN_LOOKAHEAD]]
    for idx, (jb, t) in enumerate(pairs):
        s2 = pending[idx]
        pending[idx] = None
        outs = [head_out(jb, t, par, s2[:, par * w:(par + 1) * w]) for par in range(2)]
        if idx + ATTN_LOOKAHEAD < len(pairs):
            pending.append(scores(*pairs[idx + ATTN_LOOKAHEAD]))
        pair = jnp.concatenate(outs, axis=0)
        o_ref[jb * w:(jb + 1) * w, t * LANES:(t + 1) * LANES] = pair.T.astype(BF16)


def _attention(sinks, q, kk, vt, batch, seq):
    n = q.shape[0]
    tq = ATTN_BLOCKS * WINDOW
    nt = seq // tq
    cur = lambda b, j: b * nt + j
    prev = lambda b, j: (b * nt + j) * ATTN_BLOCKS - jnp.minimum(j, 1)
    return pl.pallas_call(
        _attn_kernel,
        grid=(batch, nt),
        in_specs=[
            pl.BlockSpec(memory_space=pltpu.SMEM),
            pl.BlockSpec((tq, SZ_AQ), lambda b, j: (cur(b, j), 0)),
            pl.BlockSpec((tq, SZ_AK), lambda b, j: (cur(b, j), 0)),
            pl.BlockSpec((WINDOW, SZ_AK), lambda b, j: (prev(b, j), 0)),
            pl.BlockSpec((SZ_AV, tq), lambda b, j: (0, cur(b, j))),
            pl.BlockSpec((SZ_AV, WINDOW), lambda b, j: (0, prev(b, j))),
        ],
        out_specs=pl.BlockSpec((tq, SZ_AQ), lambda b, j: (cur(b, j), 0)),
        out_shape=jax.ShapeDtypeStruct((n, SZ_AQ), BF16),
        scratch_shapes=[pltpu.VMEM((WINDOW + tq, SZ_AK), BF16),
                        pltpu.VMEM((SZ_AV, WINDOW + tq), BF16)],
        compiler_params=pltpu.CompilerParams(
            dimension_semantics=("arbitrary", "arbitrary"), vmem_limit_bytes=VMEM_LIMIT_BYTES),
        name="attention",
    )(sinks, q, kk, kk, vt, vt)


def _log_sigmoid(x):
    return -(jnp.maximum(-x, 0.0) + jnp.log(1.0 + jnp.exp(-jnp.abs(x))))


def _mlstm_gate_kernel(g_ref, rows_ref, cols_ref, *, seq):
    L = MLSTM_CHUNK
    nh = MLSTM_HEADS
    n = g_ref.shape[1]
    g = g_ref[...]
    ig = GATE_SOFTCAP * jnp.tanh(g[0:nh] / GATE_SOFTCAP)
    fg = GATE_SOFTCAP * jnp.tanh(g[nh:2 * nh] / GATE_SOFTCAP)
    lf = _log_sigmoid(fg)
    pos = lax.broadcasted_iota(jnp.int32, (nh, n), 1) & (L - 1)
    a = lf
    sh = 1
    while sh < L:
        a = a + jnp.where(pos >= sh, pltpu.roll(a, sh, 1), 0.0)
        sh *= 2
    bvec = ig - a
    pm = bvec
    sh = 1
    while sh < L:
        pm = jnp.maximum(pm, jnp.where(pos >= sh, pltpu.roll(pm, sh, 1), -jnp.inf))
        sh *= 2
    pad = jnp.zeros((L - nh, L), F32)
    m = [jnp.zeros((nh, L), F32) for _ in range(n // seq)]
    for c in range(seq // L):
        for s in range(n // seq):
            sl = slice(s * seq + c * L, s * seq + (c + 1) * L)
            a_c, b_c = a[:, sl], bvec[:, sl]
            mrow = jnp.maximum(m[s], pm[:, sl])
            total = jnp.broadcast_to(a_c[:, L - 1:L], (nh, L))
            mlast = jnp.broadcast_to(mrow[:, L - 1:L], (nh, L))
            rows_ref[ROW_M2:ROW_M2 + nh, sl] = mrow * LOG2E
            rows_ref[ROW_WINTER:ROW_WINTER + nh, sl] = jnp.exp(m[s] - mrow)
            rows_ref[ROW_ENEG:ROW_ENEG + nh, sl] = jnp.exp(-(a_c + mrow))
            rows_ref[ROW_WK:ROW_WK + nh, sl] = jnp.exp(b_c - mlast) * MLSTM_QK_DIM ** -0.5
            rows_ref[ROW_DECAY:ROW_DECAY + nh, sl] = jnp.exp(m[s] - mlast)
            m[s] = total + mlast
            cols_ref[sl, :] = jnp.concatenate([b_c * LOG2E - K_SCALE_LOG2, pad], axis=0).T


def _mlstm_gates(gt, seq):
    n = gt.shape[1]
    return pl.pallas_call(
        functools.partial(_mlstm_gate_kernel, seq=seq),
        grid=(1,),
        in_specs=[pl.BlockSpec((2 * MLSTM_HEADS, n), lambda i: (0, 0))],
        out_specs=[pl.BlockSpec((GATE_ROWS, n), lambda i: (0, 0)),
                   pl.BlockSpec((n, LANES), lambda i: (0, 0))],
        out_shape=[jax.ShapeDtypeStruct((GATE_ROWS, n), F32),
                   jax.ShapeDtypeStruct((n, LANES), F32)],
        compiler_params=pltpu.CompilerParams(
            dimension_semantics=("arbitrary",), vmem_limit_bytes=VMEM_LIMIT_BYTES),
        name="mlstm_gates",
    )(gt)


def _mlstm_kernel(rows_ref, cols_ref, qk_ref, vt_ref, og_ref, hn_ref, o_ref, ct_scr, n_scr):
    L = MLSTM_CHUNK
    nh = MLSTM_HEADS
    contract_lanes = (((1,), (1,)), ((), ()))

    @pl.when(pl.program_id(1) == 0)
    def _():
        ct_scr[...] = jnp.zeros_like(ct_scr)
        n_scr[...] = jnp.zeros_like(n_scr)

    row_i = lax.broadcasted_iota(jnp.int32, (L, L), 0)
    col_i = lax.broadcasted_iota(jnp.int32, (L, L), 1)
    causal_t = row_i <= col_i
    lo_lane = col_i < MLSTM_QK_DIM
    zero = jnp.zeros((), BF16)

    n_chunks = qk_ref.shape[0] // L
    pairs = [(ci, p) for ci in range(n_chunks) for p in range(nh // 2)]
    ct = [ct_scr[p] for p in range(nh // 2)]
    nn = [n_scr[p] for p in range(nh // 2)]
    gate_rows = {}

    def chunk_gates(ci):
        if ci not in gate_rows:
            rs = slice(ci * L, (ci + 1) * L)
            wk = rows_ref[ROW_WK:ROW_WK + nh, rs]
            gate_rows[ci] = dict(
                m2=rows_ref[ROW_M2:ROW_M2 + nh, rs], w_inter=rows_ref[ROW_WINTER:ROW_WINTER + nh, rs],
                e_neg=rows_ref[ROW_ENEG:ROW_ENEG + nh, rs], wk=wk,
                decay=rows_ref[ROW_DECAY:ROW_DECAY + nh, rs], b2cols=cols_ref[rs, :],
                wk_b=jnp.concatenate([wk, wk], axis=0).astype(BF16))
        return gate_rows[ci]

    def front(ci, p):
        g = chunk_gates(ci)
        rs = slice(ci * L, (ci + 1) * L)
        h0, h1 = 2 * p, 2 * p + 1
        q2 = qk_ref[rs, p * LANES:(p + 1) * LANES]
        k2 = qk_ref[rs, SZ_MQ + p * LANES:SZ_MQ + (p + 1) * LANES]
        qm2 = jnp.concatenate([jnp.where(lo_lane, q2, zero), jnp.where(lo_lane, zero, q2)], axis=0)
        n2b = jnp.broadcast_to(nn[p], (2 * SUBLANES, LANES)).astype(BF16)
        stacked = jnp.concatenate([k2, ct[p].astype(BF16), n2b], axis=0)
        res = lax.dot_general(stacked, qm2, contract_lanes, preferred_element_type=F32)
        vw = [(vt_ref[hh * MLSTM_V_DIM:(hh + 1) * MLSTM_V_DIM, rs].astype(F32)
               * g["wk"][hh:hh + 1, :]).astype(BF16) for hh in (h0, h1)]
        upd = jnp.dot(jnp.concatenate(vw + [g["wk_b"]], axis=0), k2,
                      preferred_element_type=F32)
        n_inc = upd[2 * MLSTM_V_DIM:]
        dec = jnp.where(lo_lane[0:1, :], g["decay"][h0:h0 + 1, :], g["decay"][h1:h1 + 1, :])
        ct[p] = dec * ct[p] + jnp.where(lo_lane, upd[0:MLSTM_V_DIM], upd[MLSTM_V_DIM:2 * MLSTM_V_DIM])
        nn[p] = dec * nn[p] + jnp.where(lo_lane[0:1, :], n_inc[h0:h0 + 1, :], n_inc[h1:h1 + 1, :])
        return res

    def back(ci, p, par, res):
        st = res[0:L, par * L:(par + 1) * L]
        inter = res[L:L + MLSTM_V_DIM, par * L:(par + 1) * L]
        qn = res[L + MLSTM_V_DIM:L + MLSTM_V_DIM + 1, par * L:(par + 1) * L]
        g = chunk_gates(ci)
        rs = slice(ci * L, (ci + 1) * L)
        hh = 2 * p + par
        vsl = slice(hh * MLSTM_V_DIM, (hh + 1) * MLSTM_V_DIM)
        dt = jnp.where(causal_t,
                       jnp.exp2(g["b2cols"][:, hh:hh + 1] - g["m2"][hh:hh + 1, :]), 0.0)
        sct = st * dt
        wi = g["w_inter"][hh:hh + 1, :]
        den = jnp.sum(sct, axis=0, keepdims=True) + wi * qn
        numt = jnp.dot(vt_ref[vsl, rs], sct.astype(BF16), preferred_element_type=F32) + wi * inter
        rden = 1.0 / jnp.maximum(jnp.abs(den), g["e_neg"][hh:hh + 1, :])
        ms = jnp.mean(numt * numt, axis=0, keepdims=True)
        scale = rden * lax.rsqrt(rden * rden * ms + NORM_EPS)
        hv = (numt * scale).T
        o_ref[rs, vsl] = (hv * hn_ref[:, vsl] * og_ref[rs, vsl].astype(F32)).astype(BF16)

    pending = [front(*u) for u in pairs[:MLSTM_LOOKAHEAD]]
    for idx, (ci, p) in enumerate(pairs):
        res = pending[idx]
        pending[idx] = None
        back(ci, p, 0, res)
        if idx + MLSTM_LOOKAHEAD < len(pairs):
            pending.append(front(*pairs[idx + MLSTM_LOOKAHEAD]))
        back(ci, p, 1, res)
    for p in range(nh // 2):
        ct_scr[p] = ct[p]
        n_scr[p] = nn[p]


def _mlstm(rows, cols, qk, v, og, hn, batch, seq):
    n = qk.shape[0]
    tm = TM_MLSTM
    nt = seq // tm
    row = pl.BlockSpec((tm, SZ_MV), lambda b, c: (b * nt + c, 0))
    return pl.pallas_call(
        _mlstm_kernel,
        grid=(batch, nt),
        in_specs=[
            pl.BlockSpec((GATE_ROWS, tm), lambda b, c: (0, b * nt + c)),
            pl.BlockSpec((tm, LANES), lambda b, c: (b * nt + c, 0)),
            row,
            pl.BlockSpec((SZ_MV, tm), lambda b, c: (0, b * nt + c)),
            row,
            _const_spec((1, SZ_MV)),
        ],
        out_specs=row,
        out_shape=jax.ShapeDtypeStruct((n, SZ_MV), BF16),
        scratch_shapes=[
            pltpu.VMEM((MLSTM_HEADS // 2, 2 * MLSTM_QK_DIM, MLSTM_V_DIM), F32),
            pltpu.VMEM((MLSTM_HEADS // 2, 1, 2 * MLSTM_QK_DIM), F32),
        ],
        compiler_params=pltpu.CompilerParams(
            dimension_semantics=("arbitrary", "arbitrary"), vmem_limit_bytes=VMEM_LIMIT_BYTES),
        name="mlstm",
    )(rows, cols, qk, v, og, hn)


def _merge_stage_weights(wpa_hbm, wpm_hbm, wo_hbm, wpa_ref, wpm_ref, wo_ref, stage, sem):
    hd = ATTN_HEAD_DIM
    wpa_copies = [pltpu.make_async_copy(wpa_hbm.at[pl.ds(head * hd, hd), :],
                                        stage.at[0, pl.ds(pos * hd, hd), :], sem.at[0])
                  for pos, head in enumerate(ATTN_HEAD_ORDER)]
    wpm_copy = pltpu.make_async_copy(wpm_hbm, stage.at[1], sem.at[1])
    wo_copy = pltpu.make_async_copy(wo_hbm, stage.at[0], sem.at[0])
    for cp in wpa_copies:
        cp.start()
    wpm_copy.start()
    for cp in wpa_copies:
        cp.wait()
    wpa_ref[...] = stage[0].astype(BF16)
    wo_copy.start()
    wpm_copy.wait()
    wpm_ref[...] = stage[1].astype(BF16)
    wo_copy.wait()
    wo_ref[...] = stage[0].astype(BF16)


def _merge_kernel(x_ref, a_ref, hm_ref, gate_ref, wpa_hbm, wpm_hbm, wo_hbm, o_ref,
                  wpa_ref, wpm_ref, wo_ref, stage, sem):
    @pl.when(pl.program_id(0) == 0)
    def _():
        _merge_stage_weights(wpa_hbm, wpm_hbm, wo_hbm, wpa_ref, wpm_ref, wo_ref, stage, sem)

    ya = jnp.dot(a_ref[...], wpa_ref[...], preferred_element_type=F32)
    ym = jnp.dot(hm_ref[...], wpm_ref[...], preferred_element_type=F32)
    ga = jax.nn.sigmoid(gate_ref[:, 0:D_MODEL].astype(F32))
    gm = jax.nn.sigmoid(gate_ref[:, D_MODEL:2 * D_MODEL].astype(F32))
    merged = (ga * ya + gm * ym).astype(BF16)
    o_ref[...] = x_ref[...] + jnp.dot(merged, wo_ref[...], preferred_element_type=F32)


def _merge(x, a, hm, gates, wpa, wpm, wo):
    n = x.shape[0]
    tm = TM_MERGE
    row = lambda width: pl.BlockSpec((tm, width), lambda i: (i, 0))
    return pl.pallas_call(
        _merge_kernel,
        grid=(n // tm,),
        in_specs=[row(D_MODEL), row(SZ_AQ), row(SZ_MV), row(2 * D_MODEL),
                  pl.BlockSpec(memory_space=pl.ANY), pl.BlockSpec(memory_space=pl.ANY),
                  pl.BlockSpec(memory_space=pl.ANY)],
        out_specs=row(D_MODEL),
        out_shape=jax.ShapeDtypeStruct((n, D_MODEL), F32),
        scratch_shapes=[pltpu.VMEM((D_MODEL, D_MODEL), BF16)] * 3 + [
            pltpu.VMEM((2, D_MODEL, D_MODEL), F32), pltpu.SemaphoreType.DMA((2,))],
        compiler_params=pltpu.CompilerParams(
            dimension_semantics=("arbitrary",), vmem_limit_bytes=VMEM_LIMIT_BYTES),
        name="merge",
    )(x, a, hm, gates, wpa, wpm, wo)


def _to_head_order(t, axis):
    shape = t.shape
    t = t.reshape(shape[:axis] + (2, 2, ATTN_GROUP) + shape[axis + 1:])
    t = jnp.swapaxes(t, axis + 1, axis + 2)
    return t.reshape(shape)


def _pack_inproj(w_in, b_in):
    wt_all = w_in.T.astype(BF16)
    o = 0
    parts = {}
    for name, size in (("aq", SZ_AQ), ("ak", SZ_AK), ("av", SZ_AV), ("mq", SZ_MQ), ("mk", SZ_MK),
                       ("mv", SZ_MV), ("mo", SZ_MO), ("mi", SZ_MI), ("mf", SZ_MF),
                       ("g", 2 * D_MODEL)):
        parts[name] = (wt_all[o:o + size], b_in[o:o + size])
        o += size

    def pair_heads(t):
        return _to_head_order(t.reshape((ATTN_HEADS, ATTN_HEAD_DIM) + t.shape[1:]), 0).reshape(t.shape)

    order = [tuple(pair_heads(t) for t in parts["aq"]), parts["ak"],
             parts["mq"], parts["mk"], parts["mo"], parts["g"]]
    wt = jnp.concatenate([t[0] for t in order], axis=0)
    b = jnp.concatenate([t[1] for t in order], axis=0).reshape(1, PACKED_WIDTH)
    wif = jnp.concatenate([parts["mi"][0], parts["mf"][0]], axis=0)
    bif = jnp.concatenate([parts["mi"][1], parts["mf"][1]], axis=0).reshape(2 * MLSTM_HEADS, 1)
    wvt = jnp.concatenate([parts["mv"][0], parts["av"][0]], axis=0)
    bvt = jnp.concatenate([parts["mv"][1], parts["av"][1]], axis=0).reshape(SZ_MV + SZ_AV, 1)
    return wt, b, wif, bif, wvt, bvt


def kernel(x, ffn1_norm, ffn1_w_gate, ffn1_w_up, ffn1_w_down, mix_norm, w_in, b_in, attn_sinks,
           mlstm_conv, mlstm_head_norm, w_proj_attn, w_proj_mlstm, w_out, ffn2_norm, ffn2_w_gate,
           ffn2_w_up, ffn2_w_down, final_norm):
    batch, seq, d = x.shape
    assert d == D_MODEL and ffn1_norm.shape[0] == 1, "one layer of width D_MODEL"
    assert all(seq % t == 0 for t in (TM_INPROJ, TM_MLSTM, ATTN_BLOCKS * WINDOW))
    assert all((batch * seq) % t == 0 for t in (TM_FFN, TM_MERGE))
    n = batch * seq
    xf = x.reshape(n, d)
    fin = final_norm.reshape(1, d)

    x1 = _ffn(xf, ffn1_norm[0].reshape(1, d), ffn1_w_gate[0], ffn1_w_up[0], ffn1_w_down[0], fin,
              apply_final=False)

    w, b, wif, bif, wvt, bvt = _pack_inproj(w_in[0], b_in[0])
    q, kk, avt, mqk, mvt, mo, gates, gt = _inproj(x1, mix_norm[0].reshape(1, d), w, b, wif, bif,
                                                  wvt, bvt, mlstm_conv[0], seq)

    ya = _attention(attn_sinks[0], q, kk, avt, batch, seq)

    rows, cols = _mlstm_gates(gt, seq)
    hm = _mlstm(rows, cols, mqk, mvt, mo, mlstm_head_norm[0].reshape(1, SZ_MV), batch, seq)

    x2 = _merge(x1, ya, hm, gates, w_proj_attn[0], w_proj_mlstm[0], w_out[0])

    out = _ffn(x2, ffn2_norm[0].reshape(1, d), ffn2_w_gate[0], ffn2_w_up[0], ffn2_w_down[0], fin,
               apply_final=True)
    return out.reshape(batch, seq, d)
```

```python
import functools

import jax
import jax.numpy as jnp
from jax import lax
from jax.experimental import pallas as pl
from jax.experimental.pallas import tpu as pltpu

F32 = jnp.float32
BF16 = jnp.bfloat16

D_MODEL = 1024
ATTN_HEAD_DIM = 64
ATTN_HEADS = 16
ATTN_KV_HEADS = 4
ATTN_GROUP = 4
WINDOW = 128
MLSTM_HEADS = 8
MLSTM_V_DIM = 128
MLSTM_QK_DIM = 64
CONV_WIDTH = 4
GATE_SOFTCAP = 15.0
D_FF = 2816
FFN_RESIDUAL_WEIGHT = 0.5
NORM_EPS = 1e-6

SZ_AQ, SZ_AK, SZ_AV = 1024, 256, 256
SZ_MQ, SZ_MK, SZ_MV, SZ_MO = 512, 512, 1024, 1024
SZ_MI, SZ_MF = 8, 8

LANES = 128
SUBLANES = 8
MXU_DIM = 256
VMEM_LIMIT_BYTES = 56 * 1024 * 1024

MLSTM_CHUNK = 128
FF_CHUNK = MXU_DIM
TM_FFN = 1024
TM_INPROJ = 1024
TM_MERGE = 1024
TM_MLSTM = 8 * MLSTM_CHUNK
ATTN_LOOKAHEAD = 4
ATTN_BLOCKS = 8
MLSTM_LOOKAHEAD = 2

LOG2E = 1.4426950408889634
K_SCALE_LOG2 = 3.0
ROW_M2, ROW_WINTER, ROW_ENEG, ROW_WK, ROW_DECAY = 0, 8, 16, 24, 32
GATE_ROWS = 40

COL_KK = 0
COL_MQK = 256
COL_MO = 1280
COL_G = 2304
PACKED_WIDTH = 4352
ROW_MV, ROW_AV, ROW_AQ = 0, 1024, 1280
T_ROWS = 2304

ATTN_HEAD_ORDER = tuple((2 * (t // 4) + par) * ATTN_GROUP + t % 4
                        for t in range(ATTN_HEADS // 2) for par in range(2))


def _rms(x, g):
    return x * lax.rsqrt(jnp.mean(x * x, axis=-1, keepdims=True) + NORM_EPS) * g


def _const_spec(shape):
    zeros = (0,) * len(shape)
    return pl.BlockSpec(shape, lambda *_: zeros, pipeline_mode=pl.Buffered(1))


def _ffn_stage_weights(wg_hbm, wu_hbm, wd_hbm, wg_ref, wu_ref, wd_ref, gu_stage, d_stage, sem):
    n_chunks = D_FF // FF_CHUNK

    def copies(c, slot):
        cols = pl.ds(c * FF_CHUNK, FF_CHUNK)
        return (pltpu.make_async_copy(wg_hbm.at[:, cols], gu_stage.at[slot, 0], sem.at[slot, 0]),
                pltpu.make_async_copy(wu_hbm.at[:, cols], gu_stage.at[slot, 1], sem.at[slot, 1]),
                pltpu.make_async_copy(wd_hbm.at[cols, :], d_stage.at[slot], sem.at[slot, 2]))

    for cp in copies(0, 0):
        cp.start()
    for c in range(n_chunks):
        slot = c % 2
        if c + 1 < n_chunks:
            for cp in copies(c + 1, 1 - slot):
                cp.start()
        for cp in copies(c, slot):
            cp.wait()
        sl = slice(c * FF_CHUNK, (c + 1) * FF_CHUNK)
        wg_ref[:, sl] = gu_stage[slot, 0].astype(BF16)
        wu_ref[:, sl] = gu_stage[slot, 1].astype(BF16)
        wd_ref[sl, :] = d_stage[slot].astype(BF16)


def _ffn_kernel(x_ref, g_ref, wg_hbm, wu_hbm, wd_hbm, fin_ref, o_ref,
                wg_ref, wu_ref, wd_ref, gu_stage, d_stage, sem, acc_ref, *, apply_final):
    @pl.when(pl.program_id(0) == 0)
    def _():
        _ffn_stage_weights(wg_hbm, wu_hbm, wd_hbm, wg_ref, wu_ref, wd_ref, gu_stage, d_stage, sem)

    x = x_ref[...]
    h = _rms(x, g_ref[...]).astype(BF16)
    for c in range(D_FF // FF_CHUNK):
        sl = slice(c * FF_CHUNK, (c + 1) * FF_CHUNK)
        a = jnp.dot(h, wg_ref[:, sl], preferred_element_type=F32)
        u = jnp.dot(h, wu_ref[:, sl], preferred_element_type=F32)
        act = (a * jax.nn.sigmoid(a) * u).astype(BF16)
        d = jnp.dot(act, wd_ref[sl, :], preferred_element_type=F32)
        if c == 0:
            acc_ref[...] = d
        else:
            acc_ref[...] += d
    y = x + FFN_RESIDUAL_WEIGHT * acc_ref[...]
    if apply_final:
        y = _rms(y, fin_ref[...])
    o_ref[...] = y


def _ffn(x, g, wg, wu, wd, fin, apply_final):
    n = x.shape[0]
    tm = TM_FFN
    return pl.pallas_call(
        functools.partial(_ffn_kernel, apply_final=apply_final),
        grid=(n // tm,),
        in_specs=[
            pl.BlockSpec((tm, D_MODEL), lambda i: (i, 0)),
            _const_spec((1, D_MODEL)),
            pl.BlockSpec(memory_space=pl.ANY),
            pl.BlockSpec(memory_space=pl.ANY),
            pl.BlockSpec(memory_space=pl.ANY),
            _const_spec((1, D_MODEL)),
        ],
        out_specs=pl.BlockSpec((tm, D_MODEL), lambda i: (i, 0)),
        out_shape=jax.ShapeDtypeStruct((n, D_MODEL), F32),
        scratch_shapes=[
            pltpu.VMEM((D_MODEL, D_FF), BF16),
            pltpu.VMEM((D_MODEL, D_FF), BF16),
            pltpu.VMEM((D_FF, D_MODEL), BF16),
            pltpu.VMEM((2, 2, D_MODEL, FF_CHUNK), F32),
            pltpu.VMEM((2, FF_CHUNK, D_MODEL), F32),
            pltpu.SemaphoreType.DMA((2, 3)),
            pltpu.VMEM((tm, D_MODEL), F32),
        ],
        compiler_params=pltpu.CompilerParams(
            dimension_semantics=("arbitrary",), vmem_limit_bytes=VMEM_LIMIT_BYTES),
        name="ffn_final" if apply_final else "ffn",
    )(x, g, wg, wu, wd, fin)


def _inproj_kernel(x_ref, g_ref, w_ref, b_ref, wif_ref, bif_ref, wvt_ref, bvt_ref, conv_ref,
                   q_ref, kk_ref, avt_ref, mqk_ref, mvt_ref, mo_ref, gate_ref, gt_ref,
                   conv_scr, *, tiles_per_seq):
    i = pl.program_id(0)
    tm = x_ref.shape[0]
    h = _rms(x_ref[...], g_ref[...]).astype(BF16)

    def seg(c0, width):
        return (lax.dot_general(h, w_ref[c0:c0 + width, :], (((1,), (1,)), ((), ())),
                                preferred_element_type=F32) + b_ref[:, c0:c0 + width])

    def seg_t(wt_ref, bt_ref, r0, rows):
        return (lax.dot_general(wt_ref[r0:r0 + rows, :], h, (((1,), (1,)), ((), ())),
                                preferred_element_type=F32) + bt_ref[r0:r0 + rows, :])

    @pl.when(i == 0)
    def _():
        conv_scr[...] = jnp.zeros_like(conv_scr)

    carry = jnp.where(i % tiles_per_seq == 0, 0.0, conv_scr[...])

    def conv_slab(c, width):
        z = seg(COL_MQK + c, width)
        conv_scr[:, c:c + width] = z[tm - SUBLANES:tm, :]
        zc = jnp.concatenate([carry[:, c:c + width], z], axis=0)
        acc = z * conv_ref[CONV_WIDTH - 1:CONV_WIDTH, c:c + width]
        for k in range(1, CONV_WIDTH):
            zk = pltpu.roll(zc, k, 0)[SUBLANES:, :]
            acc = acc + zk * conv_ref[CONV_WIDTH - 1 - k:CONV_WIDTH - k, c:c + width]
        mqk_ref[:, c:c + width] = (acc * jax.nn.sigmoid(acc)).astype(BF16)

    dw = MXU_DIM
    for k in range(SZ_AQ // dw):
        sl = slice(k * dw, (k + 1) * dw)
        conv_slab(k * dw, dw)
        q_ref[sl, :] = (seg_t(wvt_ref, bvt_ref, ROW_AQ + k * dw, dw)
                        * (LOG2E * ATTN_HEAD_DIM ** -0.5)).astype(BF16)
        if k == 0:
            kk_ref[...] = seg(COL_KK, SZ_AK).astype(BF16)
        elif k == 1:
            avt_ref[...] = seg_t(wvt_ref, bvt_ref, ROW_AV, SZ_AV).astype(BF16)
        mvt_ref[sl, :] = seg_t(wvt_ref, bvt_ref, ROW_MV + k * dw, dw).astype(BF16)
        mo_ref[:, sl] = jax.nn.sigmoid(seg(COL_MO + k * dw, dw)).astype(BF16)
        gate_ref[:, 2 * k * dw:(2 * k + 2) * dw] = seg(COL_G + 2 * k * dw, 2 * dw).astype(BF16)

    gt_ref[...] = seg_t(wif_ref, bif_ref, 0, 2 * MLSTM_HEADS)


def _inproj(x, g, w, b, wif, bif, wvt, bvt, conv, seq):
    n = x.shape[0]
    tm = TM_INPROJ
    row = lambda width: pl.BlockSpec((tm, width), lambda i: (i, 0))
    col = lambda height: pl.BlockSpec((height, tm), lambda i: (0, i))
    bf = lambda width: jax.ShapeDtypeStruct((n, width), BF16)
    return pl.pallas_call(
        functools.partial(_inproj_kernel, tiles_per_seq=seq // tm),
        grid=(n // tm,),
        in_specs=[
            row(D_MODEL),
            _const_spec((1, D_MODEL)),
            _const_spec((PACKED_WIDTH, D_MODEL)),
            _const_spec((1, PACKED_WIDTH)),
            _const_spec((2 * MLSTM_HEADS, D_MODEL)),
            _const_spec((2 * MLSTM_HEADS, 1)),
            _const_spec((T_ROWS, D_MODEL)),
            _const_spec((T_ROWS, 1)),
            _const_spec((CONV_WIDTH, SZ_MQ + SZ_MK)),
        ],
        out_specs=[col(SZ_AQ), row(SZ_AK), col(SZ_AV), row(SZ_MQ + SZ_MK), col(SZ_MV), row(SZ_MO),
                   row(2 * D_MODEL), col(2 * MLSTM_HEADS)],
        out_shape=[jax.ShapeDtypeStruct((SZ_AQ, n), BF16), bf(SZ_AK),
                   jax.ShapeDtypeStruct((SZ_AV, n), BF16),
                   bf(SZ_MQ + SZ_MK), jax.ShapeDtypeStruct((SZ_MV, n), BF16), bf(SZ_MO),
                   bf(2 * D_MODEL), jax.ShapeDtypeStruct((2 * MLSTM_HEADS, n), F32)],
        scratch_shapes=[pltpu.VMEM((SUBLANES, SZ_MQ + SZ_MK), F32)],
        compiler_params=pltpu.CompilerParams(
            dimension_semantics=("arbitrary",), vmem_limit_bytes=VMEM_LIMIT_BYTES),
        name="inproj",
    )(x, g, w, b, wif, bif, wvt, bvt, conv)


def _attn_kernel(sink_ref, q_ref, kc_ref, kp_ref, vtc_ref, vtp_ref, o_ref, k_scr, vt_scr):
    w = WINDOW
    k_scr[0:w, :] = kp_ref[...]
    k_scr[w:, :] = kc_ref[...]
    vt_scr[:, 0:w] = vtp_ref[...]
    vt_scr[:, w:] = vtc_ref[...]
    kj = lax.broadcasted_iota(jnp.int32, (2 * w, w), 0)
    qi = lax.broadcasted_iota(jnp.int32, (2 * w, w), 1)
    band = (kj > qi) & (kj <= qi + w)
    band_first = band & ((kj >= w) | (pl.program_id(1) > 0))
    pairs = [(jb, t) for jb in range(q_ref.shape[1] // w) for t in range(ATTN_HEADS // 2)]
    zeros_qt = jnp.zeros((ATTN_HEAD_DIM, w), BF16)

    def scores(jb, t):
        ha, hb = ATTN_HEAD_ORDER[2 * t], ATTN_HEAD_ORDER[2 * t + 1]
        qa = q_ref[ha * ATTN_HEAD_DIM:(ha + 1) * ATTN_HEAD_DIM, jb * w:(jb + 1) * w]
        qb = q_ref[hb * ATTN_HEAD_DIM:(hb + 1) * ATTN_HEAD_DIM, jb * w:(jb + 1) * w]
        qt2 = jnp.concatenate([jnp.concatenate([qa, zeros_qt], axis=0),
                               jnp.concatenate([zeros_qt, qb], axis=0)], axis=1)
        kk = k_scr[jb * w:(jb + 2) * w, (t // 4) * LANES:(t // 4 + 1) * LANES]
        return jnp.dot(kk, qt2, preferred_element_type=F32)

    def head_out(jb, t, par, s):
        s = jnp.where(band_first if jb == 0 else band, s, -jnp.inf)
        sink = sink_ref[ATTN_HEAD_ORDER[2 * t + par]] * LOG2E
        mx = jnp.maximum(jnp.max(s, axis=0, keepdims=True), sink)
        pr = jnp.exp2(s - mx)
        denom = jnp.sum(pr, axis=0, keepdims=True) + jnp.exp2(sink - mx)
        kvh = 2 * (t // 4) + par
        vt = vt_scr[kvh * ATTN_HEAD_DIM:(kvh + 1) * ATTN_HEAD_DIM, jb * w:(jb + 2) * w]
        return jnp.dot(vt, pr.astype(BF16), preferred_element_type=F32) * (1.0 / denom)

    pending = [scores(*u) for u in pairs[:ATTN_LOOKAHEAD]]
    for idx, (jb, t) in enumerate(pairs):
        s2 = pending[idx]
        pending[idx] = None
        outs = [head_out(jb, t, par, s2[:, par * w:(par + 1) * w]) for par in range(2)]
        if idx + ATTN_LOOKAHEAD < len(pairs):
            pending.append(scores(*pairs[idx + ATTN_LOOKAHEAD]))
        pair = jnp.concatenate(outs, axis=0)
        o_ref[jb * w:(jb + 1) * w, t * LANES:(t + 1) * LANES] = pair.T.astype(BF16)


def _attention(sinks, q, kk, vt, batch, seq):
    n = kk.shape[0]
    tq = ATTN_BLOCKS * WINDOW
    nt = seq // tq
    cur = lambda b, j: b * nt + j
    prev = lambda b, j: (b * nt + j) * ATTN_BLOCKS - jnp.minimum(j, 1)
    return pl.pallas_call(
        _attn_kernel,
        grid=(batch, nt),
        in_specs=[
            pl.BlockSpec(memory_space=pltpu.SMEM),
            pl.BlockSpec((SZ_AQ, tq), lambda b, j: (0, cur(b, j))),
            pl.BlockSpec((tq, SZ_AK), lambda b, j: (cur(b, j), 0)),
            pl.BlockSpec((WINDOW, SZ_AK), lambda b, j: (prev(b, j), 0)),
            pl.BlockSpec((SZ_AV, tq), lambda b, j: (0, cur(b, j))),
            pl.BlockSpec((SZ_AV, WINDOW), lambda b, j: (0, prev(b, j))),
        ],
        out_specs=pl.BlockSpec((tq, SZ_AQ), lambda b, j: (cur(b, j), 0)),
        out_shape=jax.ShapeDtypeStruct((n, SZ_AQ), BF16),
        scratch_shapes=[pltpu.VMEM((WINDOW + tq, SZ_AK), BF16),
                        pltpu.VMEM((SZ_AV, WINDOW + tq), BF16)],
        compiler_params=pltpu.CompilerParams(
            dimension_semantics=("arbitrary", "arbitrary"), vmem_limit_bytes=VMEM_LIMIT_BYTES),
        name="attention",
    )(sinks, q, kk, kk, vt, vt)


def _log_sigmoid(x):
    return -(jnp.maximum(-x, 0.0) + jnp.log(1.0 + jnp.exp(-jnp.abs(x))))


def _mlstm_gate_kernel(g_ref, rows_ref, cols_ref, *, seq):
    L = MLSTM_CHUNK
    nh = MLSTM_HEADS
    n = g_ref.shape[1]
    g = g_ref[...]
    ig = GATE_SOFTCAP * jnp.tanh(g[0:nh] / GATE_SOFTCAP)
    fg = GATE_SOFTCAP * jnp.tanh(g[nh:2 * nh] / GATE_SOFTCAP)
    lf = _log_sigmoid(fg)
    pos = lax.broadcasted_iota(jnp.int32, (nh, n), 1) & (L - 1)
    a = lf
    sh = 1
    while sh < L:
        a = a + jnp.where(pos >= sh, pltpu.roll(a, sh, 1), 0.0)
        sh *= 2
    bvec = ig - a
    pm = bvec
    sh = 1
    while sh < L:
        pm = jnp.maximum(pm, jnp.where(pos >= sh, pltpu.roll(pm, sh, 1), -jnp.inf))
        sh *= 2
    pad = jnp.zeros((L - nh, L), F32)
    m = [jnp.zeros((nh, L), F32) for _ in range(n // seq)]
    for c in range(seq // L):
        for s in range(n // seq):
            sl = slice(s * seq + c * L, s * seq + (c + 1) * L)
            a_c, b_c = a[:, sl], bvec[:, sl]
            mrow = jnp.maximum(m[s], pm[:, sl])
            total = jnp.broadcast_to(a_c[:, L - 1:L], (nh, L))
            mlast = jnp.broadcast_to(mrow[:, L - 1:L], (nh, L))
            rows_ref[ROW_M2:ROW_M2 + nh, sl] = mrow * LOG2E
            rows_ref[ROW_WINTER:ROW_WINTER + nh, sl] = jnp.exp(m[s] - mrow)
            rows_ref[ROW_ENEG:ROW_ENEG + nh, sl] = jnp.exp(-(a_c + mrow))
            rows_ref[ROW_WK:ROW_WK + nh, sl] = jnp.exp(b_c - mlast) * MLSTM_QK_DIM ** -0.5
            rows_ref[ROW_DECAY:ROW_DECAY + nh, sl] = jnp.exp(m[s] - mlast)
            m[s] = total + mlast
            cols_ref[sl, :] = jnp.concatenate([b_c * LOG2E - K_SCALE_LOG2, pad], axis=0).T


def _mlstm_gates(gt, seq):
    n = gt.shape[1]
    return pl.pallas_call(
        functools.partial(_mlstm_gate_kernel, seq=seq),
        grid=(1,),
        in_specs=[pl.BlockSpec((2 * MLSTM_HEADS, n), lambda i: (0, 0))],
        out_specs=[pl.BlockSpec((GATE_ROWS, n), lambda i: (0, 0)),
                   pl.BlockSpec((n, LANES), lambda i: (0, 0))],
        out_shape=[jax.ShapeDtypeStruct((GATE_ROWS, n), F32),
                   jax.ShapeDtypeStruct((n, LANES), F32)],
        compiler_params=pltpu.CompilerParams(
            dimension_semantics=("arbitrary",), vmem_limit_bytes=VMEM_LIMIT_BYTES),
        name="mlstm_gates",
    )(gt)


def _mlstm_kernel(rows_ref, cols_ref, qk_ref, vt_ref, og_ref, hn_ref, o_ref, ct_scr, n_scr):
    L = MLSTM_CHUNK
    nh = MLSTM_HEADS
    contract_lanes = (((1,), (1,)), ((), ()))

    @pl.when(pl.program_id(1) == 0)
    def _():
        ct_scr[...] = jnp.zeros_like(ct_scr)
        n_scr[...] = jnp.zeros_like(n_scr)

    row_i = lax.broadcasted_iota(jnp.int32, (L, L), 0)
    col_i = lax.broadcasted_iota(jnp.int32, (L, L), 1)
    causal_t = row_i <= col_i
    lo_lane = col_i < MLSTM_QK_DIM
    zero = jnp.zeros((), BF16)

    n_chunks = qk_ref.shape[0] // L
    pairs = [(ci, p) for ci in range(n_chunks) for p in range(nh // 2)]
    ct = [ct_scr[p] for p in range(nh // 2)]
    nn = [n_scr[p] for p in range(nh // 2)]
    gate_rows = {}

    def chunk_gates(ci):
        if ci not in gate_rows:
            rs = slice(ci * L, (ci + 1) * L)
            wk = rows_ref[ROW_WK:ROW_WK + nh, rs]
            gate_rows[ci] = dict(
                m2=rows_ref[ROW_M2:ROW_M2 + nh, rs], w_inter=rows_ref[ROW_WINTER:ROW_WINTER + nh, rs],
                e_neg=rows_ref[ROW_ENEG:ROW_ENEG + nh, rs], wk=wk,
                decay=rows_ref[ROW_DECAY:ROW_DECAY + nh, rs], b2cols=cols_ref[rs, :],
                wk_b=jnp.concatenate([wk, wk], axis=0).astype(BF16))
        return gate_rows[ci]

    def front(ci, p):
        g = chunk_gates(ci)
        rs = slice(ci * L, (ci + 1) * L)
        h0, h1 = 2 * p, 2 * p + 1
        q2 = qk_ref[rs, p * LANES:(p + 1) * LANES]
        k2 = qk_ref[rs, SZ_MQ + p * LANES:SZ_MQ + (p + 1) * LANES]
        qm2 = jnp.concatenate([jnp.where(lo_lane, q2, zero), jnp.where(lo_lane, zero, q2)], axis=0)
        n2b = jnp.broadcast_to(nn[p], (2 * SUBLANES, LANES)).astype(BF16)
        stacked = jnp.concatenate([k2, ct[p].astype(BF16), n2b], axis=0)
        res = lax.dot_general(stacked, qm2, contract_lanes, preferred_element_type=F32)
        vw = [(vt_ref[hh * MLSTM_V_DIM:(hh + 1) * MLSTM_V_DIM, rs].astype(F32)
               * g["wk"][hh:hh + 1, :]).astype(BF16) for hh in (h0, h1)]
        upd = jnp.dot(jnp.concatenate(vw + [g["wk_b"]], axis=0), k2,
                      preferred_element_type=F32)
        n_inc = upd[2 * MLSTM_V_DIM:]
        dec = jnp.where(lo_lane[0:1, :], g["decay"][h0:h0 + 1, :], g["decay"][h1:h1 + 1, :])
        ct[p] = dec * ct[p] + jnp.where(lo_lane, upd[0:MLSTM_V_DIM], upd[MLSTM_V_DIM:2 * MLSTM_V_DIM])
        nn[p] = dec * nn[p] + jnp.where(lo_lane[0:1, :], n_inc[h0:h0 + 1, :], n_inc[h1:h1 + 1, :])
        return res

    def back(ci, p, par, res):
        st = res[0:L, par * L:(par + 1) * L]
        inter = res[L:L + MLSTM_V_DIM, par * L:(par + 1) * L]
        qn = res[L + MLSTM_V_DIM:L + MLSTM_V_DIM + 1, par * L:(par + 1) * L]
        g = chunk_gates(ci)
        rs = slice(ci * L, (ci + 1) * L)
        hh = 2 * p + par
        vsl = slice(hh * MLSTM_V_DIM, (hh + 1) * MLSTM_V_DIM)
        dt = jnp.where(causal_t,
                       jnp.exp2(g["b2cols"][:, hh:hh + 1] - g["m2"][hh:hh + 1, :]), 0.0)
        sct = st * dt
        wi = g["w_inter"][hh:hh + 1, :]
        den = jnp.sum(sct, axis=0, keepdims=True) + wi * qn
        numt = jnp.dot(vt_ref[vsl, rs], sct.astype(BF16), preferred_element_type=F32) + wi * inter
        rden = 1.0 / jnp.maximum(jnp.abs(den), g["e_neg"][hh:hh + 1, :])
        ms = jnp.mean(numt * numt, axis=0, keepdims=True)
        scale = rden * lax.rsqrt(rden * rden * ms + NORM_EPS)
        hv = (numt * scale).T
        o_ref[rs, vsl] = (hv * hn_ref[:, vsl] * og_ref[rs, vsl].astype(F32)).astype(BF16)

    pending = [front(*u) for u in pairs[:MLSTM_LOOKAHEAD]]
    for idx, (ci, p) in enumerate(pairs):
        res = pending[idx]
        pending[idx] = None
        back(ci, p, 0, res)
        if idx + MLSTM_LOOKAHEAD < len(pairs):
            pending.append(front(*pairs[idx + MLSTM_LOOKAHEAD]))
        back(ci, p, 1, res)
    for p in range(nh // 2):
        ct_scr[p] = ct[p]
        n_scr[p] = nn[p]


def _mlstm(rows, cols, qk, v, og, hn, batch, seq):
    n = qk.shape[0]
    tm = TM_MLSTM
    nt = seq // tm
    row = pl.BlockSpec((tm, SZ_MV), lambda b, c: (b * nt + c, 0))
    return pl.pallas_call(
        _mlstm_kernel,
        grid=(batch, nt),
        in_specs=[
            pl.BlockSpec((GATE_ROWS, tm), lambda b, c: (0, b * nt + c)),
            pl.BlockSpec((tm, LANES), lambda b, c: (b * nt + c, 0)),
            row,
            pl.BlockSpec((SZ_MV, tm), lambda b, c: (0, b * nt + c)),
            row,
            _const_spec((1, SZ_MV)),
        ],
        out_specs=row,
        out_shape=jax.ShapeDtypeStruct((n, SZ_MV), BF16),
        scratch_shapes=[
            pltpu.VMEM((MLSTM_HEADS // 2, 2 * MLSTM_QK_DIM, MLSTM_V_DIM), F32),
            pltpu.VMEM((MLSTM_HEADS // 2, 1, 2 * MLSTM_QK_DIM), F32),
        ],
        compiler_params=pltpu.CompilerParams(
            dimension_semantics=("arbitrary", "arbitrary"), vmem_limit_bytes=VMEM_LIMIT_BYTES),
        name="mlstm",
    )(rows, cols, qk, v, og, hn)


def _merge_stage_weights(wpa_hbm, wpm_hbm, wo_hbm, wpa_ref, wpm_ref, wo_ref, stage, sem):
    hd = ATTN_HEAD_DIM
    wpa_copies = [pltpu.make_async_copy(wpa_hbm.at[pl.ds(head * hd, hd), :],
                                        stage.at[0, pl.ds(pos * hd, hd), :], sem.at[0])
                  for pos, head in enumerate(ATTN_HEAD_ORDER)]
    wpm_copy = pltpu.make_async_copy(wpm_hbm, stage.at[1], sem.at[1])
    wo_copy = pltpu.make_async_copy(wo_hbm, stage.at[0], sem.at[0])
    for cp in wpa_copies:
        cp.start()
    wpm_copy.start()
    for cp in wpa_copies:
        cp.wait()
    wpa_ref[...] = stage[0].astype(BF16)
    wo_copy.start()
    wpm_copy.wait()
    wpm_ref[...] = stage[1].astype(BF16)
    wo_copy.wait()
    wo_ref[...] = stage[0].astype(BF16)


def _merge_kernel(x_ref, a_ref, hm_ref, gate_ref, wpa_hbm, wpm_hbm, wo_hbm, o_ref,
                  wpa_ref, wpm_ref, wo_ref, stage, sem):
    @pl.when(pl.program_id(0) == 0)
    def _():
        _merge_stage_weights(wpa_hbm, wpm_hbm, wo_hbm, wpa_ref, wpm_ref, wo_ref, stage, sem)

    ya = jnp.dot(a_ref[...], wpa_ref[...], preferred_element_type=F32)
    ym = jnp.dot(hm_ref[...], wpm_ref[...], preferred_element_type=F32)
    ga = jax.nn.sigmoid(gate_ref[:, 0:D_MODEL].astype(F32))
    gm = jax.nn.sigmoid(gate_ref[:, D_MODEL:2 * D_MODEL].astype(F32))
    merged = (ga * ya + gm * ym).astype(BF16)
    o_ref[...] = x_ref[...] + jnp.dot(merged, wo_ref[...], preferred_element_type=F32)


def _merge(x, a, hm, gates, wpa, wpm, wo):
    n = x.shape[0]
    tm = TM_MERGE
    row = lambda width: pl.BlockSpec((tm, width), lambda i: (i, 0))
    return pl.pallas_call(
        _merge_kernel,
        grid=(n // tm,),
        in_specs=[row(D_MODEL), row(SZ_AQ), row(SZ_MV), row(2 * D_MODEL),
                  pl.BlockSpec(memory_space=pl.ANY), pl.BlockSpec(memory_space=pl.ANY),
                  pl.BlockSpec(memory_space=pl.ANY)],
        out_specs=row(D_MODEL),
        out_shape=jax.ShapeDtypeStruct((n, D_MODEL), F32),
        scratch_shapes=[pltpu.VMEM((D_MODEL, D_MODEL), BF16)] * 3 + [
            pltpu.VMEM((2, D_MODEL, D_MODEL), F32), pltpu.SemaphoreType.DMA((2,))],
        compiler_params=pltpu.CompilerParams(
            dimension_semantics=("arbitrary",), vmem_limit_bytes=VMEM_LIMIT_BYTES),
        name="merge",
    )(x, a, hm, gates, wpa, wpm, wo)


def _pack_inproj(w_in, b_in):
    wt_all = w_in.T.astype(BF16)
    o = 0
    parts = {}
    for name, size in (("aq", SZ_AQ), ("ak", SZ_AK), ("av", SZ_AV), ("mq", SZ_MQ), ("mk", SZ_MK),
                       ("mv", SZ_MV), ("mo", SZ_MO), ("mi", SZ_MI), ("mf", SZ_MF),
                       ("g", 2 * D_MODEL)):
        parts[name] = (wt_all[o:o + size], b_in[o:o + size])
        o += size

    order = [parts["ak"], parts["mq"], parts["mk"], parts["mo"], parts["g"]]
    wt = jnp.concatenate([t[0] for t in order], axis=0)
    b = jnp.concatenate([t[1] for t in order], axis=0).reshape(1, PACKED_WIDTH)
    wif = jnp.concatenate([parts["mi"][0], parts["mf"][0]], axis=0)
    bif = jnp.concatenate([parts["mi"][1], parts["mf"][1]], axis=0).reshape(2 * MLSTM_HEADS, 1)
    t_order = [parts["mv"], parts["av"], parts["aq"]]
    wvt = jnp.concatenate([t[0] for t in t_order], axis=0)
    bvt = jnp.concatenate([t[1] for t in t_order], axis=0).reshape(T_ROWS, 1)
    return wt, b, wif, bif, wvt, bvt


def kernel(x, ffn1_norm, ffn1_w_gate, ffn1_w_up, ffn1_w_down, mix_norm, w_in, b_in, attn_sinks,
           mlstm_conv, mlstm_head_norm, w_proj_attn, w_proj_mlstm, w_out, ffn2_norm, ffn2_w_gate,
           ffn2_w_up, ffn2_w_down, final_norm):
    batch, seq, d = x.shape
    assert d == D_MODEL and ffn1_norm.shape[0] == 1, "one layer of width D_MODEL"
    assert all(seq % t == 0 for t in (TM_INPROJ, TM_MLSTM, ATTN_BLOCKS * WINDOW))
    assert all((batch * seq) % t == 0 for t in (TM_FFN, TM_MERGE))
    n = batch * seq
    xf = x.reshape(n, d)
    fin = final_norm.reshape(1, d)

    x1 = _ffn(xf, ffn1_norm[0].reshape(1, d), ffn1_w_gate[0], ffn1_w_up[0], ffn1_w_down[0], fin,
              apply_final=False)

    w, b, wif, bif, wvt, bvt = _pack_inproj(w_in[0], b_in[0])
    q, kk, avt, mqk, mvt, mo, gates, gt = _inproj(x1, mix_norm[0].reshape(1, d), w, b, wif, bif,
                                                  wvt, bvt, mlstm_conv[0], seq)

    ya = _attention(attn_sinks[0], q, kk, avt, batch, seq)

    rows, cols = _mlstm_gates(gt, seq)
    hm = _mlstm(rows, cols, mqk, mvt, mo, mlstm_head_norm[0].reshape(1, SZ_MV), batch, seq)

    x2 = _merge(x1, ya, hm, gates, w_proj_attn[0], w_proj_mlstm[0], w_out[0])

    out = _ffn(x2, ffn2_norm[0].reshape(1, d), ffn2_w_gate[0], ffn2_w_up[0], ffn2_w_down[0], fin,
               apply_final=True)
    return out.reshape(batch, seq, d)
```

```python
import functools

import jax
import jax.numpy as jnp
from jax import lax
from jax.experimental import pallas as pl
from jax.experimental.pallas import tpu as pltpu

F32 = jnp.float32
BF16 = jnp.bfloat16

D_MODEL = 1024
ATTN_HEAD_DIM = 64
ATTN_HEADS = 16
ATTN_KV_HEADS = 4
ATTN_GROUP = 4
WINDOW = 128
MLSTM_HEADS = 8
MLSTM_V_DIM = 128
MLSTM_QK_DIM = 64
CONV_WIDTH = 4
GATE_SOFTCAP = 15.0
D_FF = 2816
FFN_RESIDUAL_WEIGHT = 0.5
NORM_EPS = 1e-6

SZ_AQ, SZ_AK, SZ_AV = 1024, 256, 256
SZ_MQ, SZ_MK, SZ_MV, SZ_MO = 512, 512, 1024, 1024
SZ_MI, SZ_MF = 8, 8

LANES = 128
SUBLANES = 8
MXU_DIM = 256
VMEM_LIMIT_BYTES = 56 * 1024 * 1024

MLSTM_CHUNK = 128
FF_CHUNK = MXU_DIM
TM_FFN = 1024
TM_INPROJ = 1024
TM_MERGE = 1024
TM_MLSTM = 8 * MLSTM_CHUNK
ATTN_LOOKAHEAD = 4
ATTN_BLOCKS = 8
MLSTM_LOOKAHEAD = 2

LOG2E = 1.4426950408889634
K_SCALE_LOG2 = 3.0
ROW_M2, ROW_WINTER, ROW_ENEG, ROW_WK, ROW_DECAY = 0, 8, 16, 24, 32
GATE_ROWS = 40

OFF_AQ = 0
OFF_AK = OFF_AQ + SZ_AQ
OFF_AV = OFF_AK + SZ_AK
OFF_MQ = OFF_AV + SZ_AV
OFF_MK = OFF_MQ + SZ_MQ
OFF_MV = OFF_MK + SZ_MK
OFF_MO = OFF_MV + SZ_MV
OFF_MI = OFF_MO + SZ_MO
OFF_MF = OFF_MI + SZ_MI
OFF_G = OFF_MF + SZ_MF
IN_WIDTH = OFF_G + 2 * D_MODEL

ATTN_HEAD_ORDER = tuple((2 * (t // 4) + par) * ATTN_GROUP + t % 4
                        for t in range(ATTN_HEADS // 2) for par in range(2))


def _rms(x, g):
    return x * lax.rsqrt(jnp.mean(x * x, axis=-1, keepdims=True) + NORM_EPS) * g


def _const_spec(shape):
    zeros = (0,) * len(shape)
    return pl.BlockSpec(shape, lambda *_: zeros, pipeline_mode=pl.Buffered(1))


def _ffn_stage_weights(wg_hbm, wu_hbm, wd_hbm, wg_ref, wu_ref, wd_ref, gu_stage, d_stage, sem):
    n_chunks = D_FF // FF_CHUNK

    def copies(c, slot):
        cols = pl.ds(c * FF_CHUNK, FF_CHUNK)
        return (pltpu.make_async_copy(wg_hbm.at[:, cols], gu_stage.at[slot, 0], sem.at[slot, 0]),
                pltpu.make_async_copy(wu_hbm.at[:, cols], gu_stage.at[slot, 1], sem.at[slot, 1]),
                pltpu.make_async_copy(wd_hbm.at[cols, :], d_stage.at[slot], sem.at[slot, 2]))

    for cp in copies(0, 0):
        cp.start()
    for c in range(n_chunks):
        slot = c % 2
        if c + 1 < n_chunks:
            for cp in copies(c + 1, 1 - slot):
                cp.start()
        for cp in copies(c, slot):
            cp.wait()
        sl = slice(c * FF_CHUNK, (c + 1) * FF_CHUNK)
        wg_ref[:, sl] = gu_stage[slot, 0].astype(BF16)
        wu_ref[:, sl] = gu_stage[slot, 1].astype(BF16)
        wd_ref[sl, :] = d_stage[slot].astype(BF16)


def _ffn_kernel(x_ref, g_ref, wg_hbm, wu_hbm, wd_hbm, fin_ref, o_ref,
                wg_ref, wu_ref, wd_ref, gu_stage, d_stage, sem, acc_ref, *, apply_final):
    @pl.when(pl.program_id(0) == 0)
    def _():
        _ffn_stage_weights(wg_hbm, wu_hbm, wd_hbm, wg_ref, wu_ref, wd_ref, gu_stage, d_stage, sem)

    x = x_ref[...]
    h = _rms(x, g_ref[...]).astype(BF16)
    for c in range(D_FF // FF_CHUNK):
        sl = slice(c * FF_CHUNK, (c + 1) * FF_CHUNK)
        a = jnp.dot(h, wg_ref[:, sl], preferred_element_type=F32)
        u = jnp.dot(h, wu_ref[:, sl], preferred_element_type=F32)
        act = (a * jax.nn.sigmoid(a) * u).astype(BF16)
        d = jnp.dot(act, wd_ref[sl, :], preferred_element_type=F32)
        if c == 0:
            acc_ref[...] = d
        else:
            acc_ref[...] += d
    y = x + FFN_RESIDUAL_WEIGHT * acc_ref[...]
    if apply_final:
        y = _rms(y, fin_ref[...])
    o_ref[...] = y


def _ffn(x, g, wg, wu, wd, fin, apply_final):
    n = x.shape[0]
    tm = TM_FFN
    return pl.pallas_call(
        functools.partial(_ffn_kernel, apply_final=apply_final),
        grid=(n // tm,),
        in_specs=[
            pl.BlockSpec((tm, D_MODEL), lambda i: (i, 0)),
            _const_spec((1, D_MODEL)),
            pl.BlockSpec(memory_space=pl.ANY),
            pl.BlockSpec(memory_space=pl.ANY),
            pl.BlockSpec(memory_space=pl.ANY),
            _const_spec((1, D_MODEL)),
        ],
        out_specs=pl.BlockSpec((tm, D_MODEL), lambda i: (i, 0)),
        out_shape=jax.ShapeDtypeStruct((n, D_MODEL), F32),
        scratch_shapes=[
            pltpu.VMEM((D_MODEL, D_FF), BF16),
            pltpu.VMEM((D_MODEL, D_FF), BF16),
            pltpu.VMEM((D_FF, D_MODEL), BF16),
            pltpu.VMEM((2, 2, D_MODEL, FF_CHUNK), F32),
            pltpu.VMEM((2, FF_CHUNK, D_MODEL), F32),
            pltpu.SemaphoreType.DMA((2, 3)),
            pltpu.VMEM((tm, D_MODEL), F32),
        ],
        compiler_params=pltpu.CompilerParams(
            dimension_semantics=("arbitrary",), vmem_limit_bytes=VMEM_LIMIT_BYTES),
        name="ffn_final" if apply_final else "ffn",
    )(x, g, wg, wu, wd, fin)


def _inproj_kernel(x_ref, g_ref, w_ref, brow_ref, bgate_ref, bcol_ref, conv_ref,
                   q_ref, kk_ref, avt_ref, mqk_ref, mvt_ref, mo_ref, gate_ref, gt_ref,
                   conv_scr, *, tiles_per_seq):
    i = pl.program_id(0)
    tm = x_ref.shape[0]
    h = _rms(x_ref[...], g_ref[...]).astype(BF16)
    contract_lanes = (((1,), (1,)), ((), ()))

    def seg(r0, width, bias):
        return lax.dot_general(h, w_ref[r0:r0 + width, :], contract_lanes,
                               preferred_element_type=F32) + bias

    def seg_t(r0, rows):
        return (lax.dot_general(w_ref[r0:r0 + rows, :], h, contract_lanes,
                                preferred_element_type=F32) + bcol_ref[r0:r0 + rows, :])

    @pl.when(i == 0)
    def _():
        conv_scr[...] = jnp.zeros_like(conv_scr)

    carry = jnp.where(i % tiles_per_seq == 0, 0.0, conv_scr[...])

    def conv_slab(c, width):
        z = seg(OFF_MQ + c, width, brow_ref[:, OFF_MQ + c:OFF_MQ + c + width])
        conv_scr[:, c:c + width] = z[tm - SUBLANES:tm, :]
        zc = jnp.concatenate([carry[:, c:c + width], z], axis=0)
        acc = z * conv_ref[CONV_WIDTH - 1:CONV_WIDTH, c:c + width]
        for k in range(1, CONV_WIDTH):
            zk = pltpu.roll(zc, k, 0)[SUBLANES:, :]
            acc = acc + zk * conv_ref[CONV_WIDTH - 1 - k:CONV_WIDTH - k, c:c + width]
        mqk_ref[:, c:c + width] = (acc * jax.nn.sigmoid(acc)).astype(BF16)

    dw = MXU_DIM
    for k in range(SZ_AQ // dw):
        sl = slice(k * dw, (k + 1) * dw)
        conv_slab(k * dw, dw)
        q_ref[sl, :] = (seg_t(OFF_AQ + k * dw, dw) * (LOG2E * ATTN_HEAD_DIM ** -0.5)).astype(BF16)
        if k == 0:
            kk_ref[...] = seg(OFF_AK, SZ_AK, brow_ref[:, OFF_AK:OFF_AK + SZ_AK]).astype(BF16)
        elif k == 1:
            avt_ref[...] = seg_t(OFF_AV, SZ_AV).astype(BF16)
        mvt_ref[sl, :] = seg_t(OFF_MV + k * dw, dw).astype(BF16)
        mo = seg(OFF_MO + k * dw, dw, brow_ref[:, OFF_MO + k * dw:OFF_MO + (k + 1) * dw])
        mo_ref[:, sl] = jax.nn.sigmoid(mo).astype(BF16)
        gsl = slice(2 * k * dw, (2 * k + 2) * dw)
        gate_ref[:, gsl] = seg(OFF_G + 2 * k * dw, 2 * dw, bgate_ref[:, gsl]).astype(BF16)

    gt_ref[...] = seg_t(OFF_MI, SZ_MI + SZ_MF)


def _inproj(x, g, w_in, b_in, conv, seq):
    w = w_in.T.astype(BF16)
    b_row = b_in.reshape(1, IN_WIDTH)
    b_gate = b_in[OFF_G:].reshape(1, 2 * D_MODEL)
    b_col = b_in.reshape(IN_WIDTH, 1)
    n = x.shape[0]
    tm = TM_INPROJ
    row = lambda width: pl.BlockSpec((tm, width), lambda i: (i, 0))
    col = lambda height: pl.BlockSpec((height, tm), lambda i: (0, i))
    bf = lambda width: jax.ShapeDtypeStruct((n, width), BF16)
    return pl.pallas_call(
        functools.partial(_inproj_kernel, tiles_per_seq=seq // tm),
        grid=(n // tm,),
        in_specs=[
            row(D_MODEL),
            _const_spec((1, D_MODEL)),
            _const_spec((IN_WIDTH, D_MODEL)),
            _const_spec((1, IN_WIDTH)),
            _const_spec((1, 2 * D_MODEL)),
            _const_spec((IN_WIDTH, 1)),
            _const_spec((CONV_WIDTH, SZ_MQ + SZ_MK)),
        ],
        out_specs=[col(SZ_AQ), row(SZ_AK), col(SZ_AV), row(SZ_MQ + SZ_MK), col(SZ_MV), row(SZ_MO),
                   row(2 * D_MODEL), col(2 * MLSTM_HEADS)],
        out_shape=[jax.ShapeDtypeStruct((SZ_AQ, n), BF16), bf(SZ_AK),
                   jax.ShapeDtypeStruct((SZ_AV, n), BF16),
                   bf(SZ_MQ + SZ_MK), jax.ShapeDtypeStruct((SZ_MV, n), BF16), bf(SZ_MO),
                   bf(2 * D_MODEL), jax.ShapeDtypeStruct((2 * MLSTM_HEADS, n), F32)],
        scratch_shapes=[pltpu.VMEM((SUBLANES, SZ_MQ + SZ_MK), F32)],
        compiler_params=pltpu.CompilerParams(
            dimension_semantics=("arbitrary",), vmem_limit_bytes=VMEM_LIMIT_BYTES),
        name="inproj",
    )(x, g, w, b_row, b_gate, b_col, conv)


def _attn_kernel(sink_ref, q_ref, kc_ref, kp_ref, vtc_ref, vtp_ref, o_ref, k_scr, vt_scr):
    w = WINDOW
    k_scr[0:w, :] = kp_ref[...]
    k_scr[w:, :] = kc_ref[...]
    vt_scr[:, 0:w] = vtp_ref[...]
    vt_scr[:, w:] = vtc_ref[...]
    kj = lax.broadcasted_iota(jnp.int32, (2 * w, w), 0)
    qi = lax.broadcasted_iota(jnp.int32, (2 * w, w), 1)
    band = (kj > qi) & (kj <= qi + w)
    band_first = band & ((kj >= w) | (pl.program_id(1) > 0))
    pairs = [(jb, t) for jb in range(q_ref.shape[1] // w) for t in range(ATTN_HEADS // 2)]
    zeros_qt = jnp.zeros((ATTN_HEAD_DIM, w), BF16)

    def scores(jb, t):
        ha, hb = ATTN_HEAD_ORDER[2 * t], ATTN_HEAD_ORDER[2 * t + 1]
        qa = q_ref[ha * ATTN_HEAD_DIM:(ha + 1) * ATTN_HEAD_DIM, jb * w:(jb + 1) * w]
        qb = q_ref[hb * ATTN_HEAD_DIM:(hb + 1) * ATTN_HEAD_DIM, jb * w:(jb + 1) * w]
        qt2 = jnp.concatenate([jnp.concatenate([qa, zeros_qt], axis=0),
                               jnp.concatenate([zeros_qt, qb], axis=0)], axis=1)
        kk = k_scr[jb * w:(jb + 2) * w, (t // 4) * LANES:(t // 4 + 1) * LANES]
        return jnp.dot(kk, qt2, preferred_element_type=F32)

    def head_out(jb, t, par, s):
        s = jnp.where(band_first if jb == 0 else band, s, -jnp.inf)
        sink = sink_ref[ATTN_HEAD_ORDER[2 * t + par]] * LOG2E
        mx = jnp.maximum(jnp.max(s, axis=0, keepdims=True), sink)
        pr = jnp.exp2(s - mx)
        denom = jnp.sum(pr, axis=0, keepdims=True) + jnp.exp2(sink - mx)
        kvh = 2 * (t // 4) + par
        vt = vt_scr[kvh * ATTN_HEAD_DIM:(kvh + 1) * ATTN_HEAD_DIM, jb * w:(jb + 2) * w]
        return jnp.dot(vt, pr.astype(BF16), preferred_element_type=F32) * (1.0 / denom)

    pending = [scores(*u) for u in pairs[:ATTN_LOOKAHEAD]]
    for idx, (jb, t) in enumerate(pairs):
        s2 = pending[idx]
        pending[idx] = None
        outs = [head_out(jb, t, par, s2[:, par * w:(par + 1) * w]) for par in range(2)]
        if idx + ATTN_LOOKAHEAD < len(pairs):
            pending.append(scores(*pairs[idx + ATTN_LOOKAHEAD]))
        pair = jnp.concatenate(outs, axis=0)
        o_ref[jb * w:(jb + 1) * w, t * LANES:(t + 1) * LANES] = pair.T.astype(BF16)


def _attention(sinks, q, kk, vt, batch, seq):
    n = kk.shape[0]
    tq = ATTN_BLOCKS * WINDOW
    nt = seq // tq
    cur = lambda b, j: b * nt + j
    prev = lambda b, j: (b * nt + j) * ATTN_BLOCKS - jnp.minimum(j, 1)
    return pl.pallas_call(
        _attn_kernel,
        grid=(batch, nt),
        in_specs=[
            pl.BlockSpec(memory_space=pltpu.SMEM),
            pl.BlockSpec((SZ_AQ, tq), lambda b, j: (0, cur(b, j))),
            pl.BlockSpec((tq, SZ_AK), lambda b, j: (cur(b, j), 0)),
            pl.BlockSpec((WINDOW, SZ_AK), lambda b, j: (prev(b, j), 0)),
            pl.BlockSpec((SZ_AV, tq), lambda b, j: (0, cur(b, j))),
            pl.BlockSpec((SZ_AV, WINDOW), lambda b, j: (0, prev(b, j))),
        ],
        out_specs=pl.BlockSpec((tq, SZ_AQ), lambda b, j: (cur(b, j), 0)),
        out_shape=jax.ShapeDtypeStruct((n, SZ_AQ), BF16),
        scratch_shapes=[pltpu.VMEM((WINDOW + tq, SZ_AK), BF16),
                        pltpu.VMEM((SZ_AV, WINDOW + tq), BF16)],
        compiler_params=pltpu.CompilerParams(
            dimension_semantics=("arbitrary", "arbitrary"), vmem_limit_bytes=VMEM_LIMIT_BYTES),
        name="attention",
    )(sinks, q, kk, kk, vt, vt)


def _log_sigmoid(x):
    return -(jnp.maximum(-x, 0.0) + jnp.log(1.0 + jnp.exp(-jnp.abs(x))))


def _mlstm_gate_kernel(g_ref, rows_ref, cols_ref, *, seq):
    L = MLSTM_CHUNK
    nh = MLSTM_HEADS
    n = g_ref.shape[1]
    g = g_ref[...]
    ig = GATE_SOFTCAP * jnp.tanh(g[0:nh] / GATE_SOFTCAP)
    fg = GATE_SOFTCAP * jnp.tanh(g[nh:2 * nh] / GATE_SOFTCAP)
    lf = _log_sigmoid(fg)
    pos = lax.broadcasted_iota(jnp.int32, (nh, n), 1) & (L - 1)
    a = lf
    sh = 1
    while sh < L:
        a = a + jnp.where(pos >= sh, pltpu.roll(a, sh, 1), 0.0)
        sh *= 2
    bvec = ig - a
    pm = bvec
    sh = 1
    while sh < L:
        pm = jnp.maximum(pm, jnp.where(pos >= sh, pltpu.roll(pm, sh, 1), -jnp.inf))
        sh *= 2
    pad = jnp.zeros((L - nh, L), F32)
    m = [jnp.zeros((nh, L), F32) for _ in range(n // seq)]
    for c in range(seq // L):
        for s in range(n // seq):
            sl = slice(s * seq + c * L, s * seq + (c + 1) * L)
            a_c, b_c = a[:, sl], bvec[:, sl]
            mrow = jnp.maximum(m[s], pm[:, sl])
            total = jnp.broadcast_to(a_c[:, L - 1:L], (nh, L))
            mlast = jnp.broadcast_to(mrow[:, L - 1:L], (nh, L))
            rows_ref[ROW_M2:ROW_M2 + nh, sl] = mrow * LOG2E
            rows_ref[ROW_WINTER:ROW_WINTER + nh, sl] = jnp.exp(m[s] - mrow)
            rows_ref[ROW_ENEG:ROW_ENEG + nh, sl] = jnp.exp(-(a_c + mrow))
            rows_ref[ROW_WK:ROW_WK + nh, sl] = jnp.exp(b_c - mlast) * MLSTM_QK_DIM ** -0.5
            rows_ref[ROW_DECAY:ROW_DECAY + nh, sl] = jnp.exp(m[s] - mlast)
            m[s] = total + mlast
            cols_ref[sl, :] = jnp.concatenate([b_c * LOG2E - K_SCALE_LOG2, pad], axis=0).T


def _mlstm_gates(gt, seq):
    n = gt.shape[1]
    return pl.pallas_call(
        functools.partial(_mlstm_gate_kernel, seq=seq),
        grid=(1,),
        in_specs=[pl.BlockSpec((2 * MLSTM_HEADS, n), lambda i: (0, 0))],
        out_specs=[pl.BlockSpec((GATE_ROWS, n), lambda i: (0, 0)),
                   pl.BlockSpec((n, LANES), lambda i: (0, 0))],
        out_shape=[jax.ShapeDtypeStruct((GATE_ROWS, n), F32),
                   jax.ShapeDtypeStruct((n, LANES), F32)],
        compiler_params=pltpu.CompilerParams(
            dimension_semantics=("arbitrary",), vmem_limit_bytes=VMEM_LIMIT_BYTES),
        name="mlstm_gates",
    )(gt)


def _mlstm_kernel(rows_ref, cols_ref, qk_ref, vt_ref, og_ref, hn_ref, o_ref, ct_scr, n_scr):
    L = MLSTM_CHUNK
    nh = MLSTM_HEADS
    contract_lanes = (((1,), (1,)), ((), ()))

    @pl.when(pl.program_id(1) == 0)
    def _():
        ct_scr[...] = jnp.zeros_like(ct_scr)
        n_scr[...] = jnp.zeros_like(n_scr)

    row_i = lax.broadcasted_iota(jnp.int32, (L, L), 0)
    col_i = lax.broadcasted_iota(jnp.int32, (L, L), 1)
    causal_t = row_i <= col_i
    lo_lane = col_i < MLSTM_QK_DIM
    zero = jnp.zeros((), BF16)

    n_chunks = qk_ref.shape[0] // L
    pairs = [(ci, p) for ci in range(n_chunks) for p in range(nh // 2)]
    ct = [ct_scr[p] for p in range(nh // 2)]
    nn = [n_scr[p] for p in range(nh // 2)]
    gate_rows = {}

    def chunk_gates(ci):
        if ci not in gate_rows:
            rs = slice(ci * L, (ci + 1) * L)
            wk = rows_ref[ROW_WK:ROW_WK + nh, rs]
            gate_rows[ci] = dict(
                m2=rows_ref[ROW_M2:ROW_M2 + nh, rs], w_inter=rows_ref[ROW_WINTER:ROW_WINTER + nh, rs],
                e_neg=rows_ref[ROW_ENEG:ROW_ENEG + nh, rs], wk=wk,
                decay=rows_ref[ROW_DECAY:ROW_DECAY + nh, rs], b2cols=cols_ref[rs, :],
                wk_b=jnp.concatenate([wk, wk], axis=0).astype(BF16))
        return gate_rows[ci]

    def front(ci, p):
        g = chunk_gates(ci)
        rs = slice(ci * L, (ci + 1) * L)
        h0, h1 = 2 * p, 2 * p + 1
        q2 = qk_ref[rs, p * LANES:(p + 1) * LANES]
        k2 = qk_ref[rs, SZ_MQ + p * LANES:SZ_MQ + (p + 1) * LANES]
        qm2 = jnp.concatenate([jnp.where(lo_lane, q2, zero), jnp.where(lo_lane, zero, q2)], axis=0)
        n2b = jnp.broadcast_to(nn[p], (2 * SUBLANES, LANES)).astype(BF16)
        stacked = jnp.concatenate([k2, ct[p].astype(BF16), n2b], axis=0)
        res = lax.dot_general(stacked, qm2, contract_lanes, preferred_element_type=F32)
        vw = [(vt_ref[hh * MLSTM_V_DIM:(hh + 1) * MLSTM_V_DIM, rs].astype(F32)
               * g["wk"][hh:hh + 1, :]).astype(BF16) for hh in (h0, h1)]
        upd = jnp.dot(jnp.concatenate(vw + [g["wk_b"]], axis=0), k2,
                      preferred_element_type=F32)
        n_inc = upd[2 * MLSTM_V_DIM:]
        dec = jnp.where(lo_lane[0:1, :], g["decay"][h0:h0 + 1, :], g["decay"][h1:h1 + 1, :])
        ct[p] = dec * ct[p] + jnp.where(lo_lane, upd[0:MLSTM_V_DIM], upd[MLSTM_V_DIM:2 * MLSTM_V_DIM])
        nn[p] = dec * nn[p] + jnp.where(lo_lane[0:1, :], n_inc[h0:h0 + 1, :], n_inc[h1:h1 + 1, :])
        return res

    def back(ci, p, par, res):
        st = res[0:L, par * L:(par + 1) * L]
        inter = res[L:L + MLSTM_V_DIM, par * L:(par + 1) * L]
        qn = res[L + MLSTM_V_DIM:L + MLSTM_V_DIM + 1, par * L:(par + 1) * L]
        g = chunk_gates(ci)
        rs = slice(ci * L, (ci + 1) * L)
        hh = 2 * p + par
        vsl = slice(hh * MLSTM_V_DIM, (hh + 1) * MLSTM_V_DIM)
        dt = jnp.where(causal_t,
                       jnp.exp2(g["b2cols"][:, hh:hh + 1] - g["m2"][hh:hh + 1, :]), 0.0)
        sct = st * dt
        wi = g["w_inter"][hh:hh + 1, :]
        den = jnp.sum(sct, axis=0, keepdims=True) + wi * qn
        numt = jnp.dot(vt_ref[vsl, rs], sct.astype(BF16), preferred_element_type=F32) + wi * inter
        rden = 1.0 / jnp.maximum(jnp.abs(den), g["e_neg"][hh:hh + 1, :])
        ms = jnp.mean(numt * numt, axis=0, keepdims=True)
        scale = rden * lax.rsqrt(rden * rden * ms + NORM_EPS)
        hv = (numt * scale).T
        o_ref[rs, vsl] = (hv * hn_ref[:, vsl] * og_ref[rs, vsl].astype(F32)).astype(BF16)

    pending = [front(*u) for u in pairs[:MLSTM_LOOKAHEAD]]
    for idx, (ci, p) in enumerate(pairs):
        res = pending[idx]
        pending[idx] = None
        back(ci, p, 0, res)
        if idx + MLSTM_LOOKAHEAD < len(pairs):
            pending.append(front(*pairs[idx + MLSTM_LOOKAHEAD]))
        back(ci, p, 1, res)
    for p in range(nh // 2):
        ct_scr[p] = ct[p]
        n_scr[p] = nn[p]


def _mlstm(rows, cols, qk, v, og, hn, batch, seq):
    n = qk.shape[0]
    tm = TM_MLSTM
    nt = seq // tm
    row = pl.BlockSpec((tm, SZ_MV), lambda b, c: (b * nt + c, 0))
    return pl.pallas_call(
        _mlstm_kernel,
        grid=(batch, nt),
        in_specs=[
            pl.BlockSpec((GATE_ROWS, tm), lambda b, c: (0, b * nt + c)),
            pl.BlockSpec((tm, LANES), lambda b, c: (b * nt + c, 0)),
            row,
            pl.BlockSpec((SZ_MV, tm), lambda b, c: (0, b * nt + c)),
            row,
            _const_spec((1, SZ_MV)),
        ],
        out_specs=row,
        out_shape=jax.ShapeDtypeStruct((n, SZ_MV), BF16),
        scratch_shapes=[
            pltpu.VMEM((MLSTM_HEADS // 2, 2 * MLSTM_QK_DIM, MLSTM_V_DIM), F32),
            pltpu.VMEM((MLSTM_HEADS // 2, 1, 2 * MLSTM_QK_DIM), F32),
        ],
        compiler_params=pltpu.CompilerParams(
            dimension_semantics=("arbitrary", "arbitrary"), vmem_limit_bytes=VMEM_LIMIT_BYTES),
        name="mlstm",
    )(rows, cols, qk, v, og, hn)


def _merge_stage_weights(wpa_hbm, wpm_hbm, wo_hbm, wpa_ref, wpm_ref, wo_ref, stage, sem):
    hd = ATTN_HEAD_DIM
    wpa_copies = [pltpu.make_async_copy(wpa_hbm.at[pl.ds(head * hd, hd), :],
                                        stage.at[0, pl.ds(pos * hd, hd), :], sem.at[0])
                  for pos, head in enumerate(ATTN_HEAD_ORDER)]
    wpm_copy = pltpu.make_async_copy(wpm_hbm, stage.at[1], sem.at[1])
    wo_copy = pltpu.make_async_copy(wo_hbm, stage.at[0], sem.at[0])
    for cp in wpa_copies:
        cp.start()
    wpm_copy.start()
    for cp in wpa_copies:
        cp.wait()
    wpa_ref[...] = stage[0].astype(BF16)
    wo_copy.start()
    wpm_copy.wait()
    wpm_ref[...] = stage[1].astype(BF16)
    wo_copy.wait()
    wo_ref[...] = stage[0].astype(BF16)


def _merge_kernel(x_ref, a_ref, hm_ref, gate_ref, wpa_hbm, wpm_hbm, wo_hbm, o_ref,
                  wpa_ref, wpm_ref, wo_ref, stage, sem):
    @pl.when(pl.program_id(0) == 0)
    def _():
        _merge_stage_weights(wpa_hbm, wpm_hbm, wo_hbm, wpa_ref, wpm_ref, wo_ref, stage, sem)

    ya = jnp.dot(a_ref[...], wpa_ref[...], preferred_element_type=F32)
    ym = jnp.dot(hm_ref[...], wpm_ref[...], preferred_element_type=F32)
    ga = jax.nn.sigmoid(gate_ref[:, 0:D_MODEL].astype(F32))
    gm = jax.nn.sigmoid(gate_ref[:, D_MODEL:2 * D_MODEL].astype(F32))
    merged = (ga * ya + gm * ym).astype(BF16)
    o_ref[...] = x_ref[...] + jnp.dot(merged, wo_ref[...], preferred_element_type=F32)


def _merge(x, a, hm, gates, wpa, wpm, wo):
    n = x.shape[0]
    tm = TM_MERGE
    row = lambda width: pl.BlockSpec((tm, width), lambda i: (i, 0))
    return pl.pallas_call(
        _merge_kernel,
        grid=(n // tm,),
        in_specs=[row(D_MODEL), row(SZ_AQ), row(SZ_MV), row(2 * D_MODEL),
                  pl.BlockSpec(memory_space=pl.ANY), pl.BlockSpec(memory_space=pl.ANY),
                  pl.BlockSpec(memory_space=pl.ANY)],
        out_specs=row(D_MODEL),
        out_shape=jax.ShapeDtypeStruct((n, D_MODEL), F32),
        scratch_shapes=[pltpu.VMEM((D_MODEL, D_MODEL), BF16)] * 3 + [
            pltpu.VMEM((2, D_MODEL, D_MODEL), F32), pltpu.SemaphoreType.DMA((2,))],
        compiler_params=pltpu.CompilerParams(
            dimension_semantics=("arbitrary",), vmem_limit_bytes=VMEM_LIMIT_BYTES),
        name="merge",
    )(x, a, hm, gates, wpa, wpm, wo)


def kernel(x, ffn1_norm, ffn1_w_gate, ffn1_w_up, ffn1_w_down, mix_norm, w_in, b_in, attn_sinks,
           mlstm_conv, mlstm_head_norm, w_proj_attn, w_proj_mlstm, w_out, ffn2_norm, ffn2_w_gate,
           ffn2_w_up, ffn2_w_down, final_norm):
    batch, seq, d = x.shape
    assert d == D_MODEL and ffn1_norm.shape[0] == 1, "one layer of width D_MODEL"
    assert all(seq % t == 0 for t in (TM_INPROJ, TM_MLSTM, ATTN_BLOCKS * WINDOW))
    assert all((batch * seq) % t == 0 for t in (TM_FFN, TM_MERGE))
    n = batch * seq
    xf = x.reshape(n, d)
    fin = final_norm.reshape(1, d)

    x1 = _ffn(xf, ffn1_norm[0].reshape(1, d), ffn1_w_gate[0], ffn1_w_up[0], ffn1_w_down[0], fin,
              apply_final=False)

    q, kk, avt, mqk, mvt, mo, gates, gt = _inproj(x1, mix_norm[0].reshape(1, d), w_in[0], b_in[0],
                                                  mlstm_conv[0], seq)

    ya = _attention(attn_sinks[0], q, kk, avt, batch, seq)

    rows, cols = _mlstm_gates(gt, seq)
    hm = _mlstm(rows, cols, mqk, mvt, mo, mlstm_head_norm[0].reshape(1, SZ_MV), batch, seq)

    x2 = _merge(x1, ya, hm, gates, w_proj_attn[0], w_proj_mlstm[0], w_out[0])

    out = _ffn(x2, ffn2_norm[0].reshape(1, d), ffn2_w_gate[0], ffn2_w_up[0], ffn2_w_down[0], fin,
               apply_final=True)
    return out.reshape(batch, seq, d)
```

```python
import functools

import jax
import jax.numpy as jnp
from jax import lax
from jax.experimental import pallas as pl
from jax.experimental.pallas import tpu as pltpu

F32 = jnp.float32
BF16 = jnp.bfloat16

D_MODEL = 1024
ATTN_HEAD_DIM = 64
ATTN_HEADS = 16
ATTN_KV_HEADS = 4
ATTN_GROUP = 4
WINDOW = 128
MLSTM_HEADS = 8
MLSTM_V_DIM = 128
MLSTM_QK_DIM = 64
CONV_WIDTH = 4
GATE_SOFTCAP = 15.0
D_FF = 2816
FFN_RESIDUAL_WEIGHT = 0.5
NORM_EPS = 1e-6

SZ_AQ, SZ_AK, SZ_AV = 1024, 256, 256
SZ_MQ, SZ_MK, SZ_MV, SZ_MO = 512, 512, 1024, 1024
SZ_MI, SZ_MF = 8, 8

LANES = 128
SUBLANES = 8
MXU_DIM = 256
VMEM_LIMIT_BYTES = 56 * 1024 * 1024

MLSTM_CHUNK = 128
FF_CHUNK = MXU_DIM
TM_FFN = 1024
TM_INPROJ = 1024
TM_MERGE = 1024
TM_MLSTM = 8 * MLSTM_CHUNK
ATTN_LOOKAHEAD = 4
ATTN_BLOCKS = 8
MLSTM_LOOKAHEAD = 2

LOG2E = 1.4426950408889634
K_SCALE_LOG2 = 3.0
ROW_M2, ROW_WINTER, ROW_ENEG, ROW_WK, ROW_DECAY = 0, 8, 16, 24, 32
GATE_ROWS = 40

OFF_AQ = 0
OFF_AK = OFF_AQ + SZ_AQ
OFF_AV = OFF_AK + SZ_AK
OFF_MQ = OFF_AV + SZ_AV
OFF_MK = OFF_MQ + SZ_MQ
OFF_MV = OFF_MK + SZ_MK
OFF_MO = OFF_MV + SZ_MV
OFF_MI = OFF_MO + SZ_MO
OFF_MF = OFF_MI + SZ_MI
OFF_G = OFF_MF + SZ_MF
IN_WIDTH = OFF_G + 2 * D_MODEL
STAGE_ROWS = 256

ATTN_HEAD_ORDER = tuple((2 * (t // 4) + par) * ATTN_GROUP + t % 4
                        for t in range(ATTN_HEADS // 2) for par in range(2))


def _rms(x, g):
    return x * lax.rsqrt(jnp.mean(x * x, axis=-1, keepdims=True) + NORM_EPS) * g


def _const_spec(shape):
    zeros = (0,) * len(shape)
    return pl.BlockSpec(shape, lambda *_: zeros, pipeline_mode=pl.Buffered(1))


def _ffn_stage_weights(wg_hbm, wu_hbm, wd_hbm, wg_ref, wu_ref, wd_ref, gu_stage, d_stage, sem):
    n_chunks = D_FF // FF_CHUNK

    def copies(c, slot):
        cols = pl.ds(c * FF_CHUNK, FF_CHUNK)
        return (pltpu.make_async_copy(wg_hbm.at[:, cols], gu_stage.at[slot, 0], sem.at[slot, 0]),
                pltpu.make_async_copy(wu_hbm.at[:, cols], gu_stage.at[slot, 1], sem.at[slot, 1]),
                pltpu.make_async_copy(wd_hbm.at[cols, :], d_stage.at[slot], sem.at[slot, 2]))

    for cp in copies(0, 0):
        cp.start()
    for c in range(n_chunks):
        slot = c % 2
        if c + 1 < n_chunks:
            for cp in copies(c + 1, 1 - slot):
                cp.start()
        for cp in copies(c, slot):
            cp.wait()
        sl = slice(c * FF_CHUNK, (c + 1) * FF_CHUNK)
        wg_ref[:, sl] = gu_stage[slot, 0].astype(BF16)
        wu_ref[:, sl] = gu_stage[slot, 1].astype(BF16)
        wd_ref[sl, :] = d_stage[slot].astype(BF16)


def _ffn_kernel(x_ref, g_ref, wg_hbm, wu_hbm, wd_hbm, fin_ref, o_ref,
                wg_ref, wu_ref, wd_ref, gu_stage, d_stage, sem, acc_ref, *, apply_final):
    @pl.when(pl.program_id(0) == 0)
    def _():
        _ffn_stage_weights(wg_hbm, wu_hbm, wd_hbm, wg_ref, wu_ref, wd_ref, gu_stage, d_stage, sem)

    x = x_ref[...]
    h = _rms(x, g_ref[...]).astype(BF16)
    for c in range(D_FF // FF_CHUNK):
        sl = slice(c * FF_CHUNK, (c + 1) * FF_CHUNK)
        a = jnp.dot(h, wg_ref[:, sl], preferred_element_type=F32)
        u = jnp.dot(h, wu_ref[:, sl], preferred_element_type=F32)
        act = (a * jax.nn.sigmoid(a) * u).astype(BF16)
        d = jnp.dot(act, wd_ref[sl, :], preferred_element_type=F32)
        if c == 0:
            acc_ref[...] = d
        else:
            acc_ref[...] += d
    y = x + FFN_RESIDUAL_WEIGHT * acc_ref[...]
    if apply_final:
        y = _rms(y, fin_ref[...])
    o_ref[...] = y


def _ffn(x, g, wg, wu, wd, fin, apply_final):
    n = x.shape[0]
    tm = TM_FFN
    return pl.pallas_call(
        functools.partial(_ffn_kernel, apply_final=apply_final),
        grid=(n // tm,),
        in_specs=[
            pl.BlockSpec((tm, D_MODEL), lambda i: (i, 0)),
            _const_spec((1, D_MODEL)),
            pl.BlockSpec(memory_space=pl.ANY),
            pl.BlockSpec(memory_space=pl.ANY),
            pl.BlockSpec(memory_space=pl.ANY),
            _const_spec((1, D_MODEL)),
        ],
        out_specs=pl.BlockSpec((tm, D_MODEL), lambda i: (i, 0)),
        out_shape=jax.ShapeDtypeStruct((n, D_MODEL), F32),
        scratch_shapes=[
            pltpu.VMEM((D_MODEL, D_FF), BF16),
            pltpu.VMEM((D_MODEL, D_FF), BF16),
            pltpu.VMEM((D_FF, D_MODEL), BF16),
            pltpu.VMEM((2, 2, D_MODEL, FF_CHUNK), F32),
            pltpu.VMEM((2, FF_CHUNK, D_MODEL), F32),
            pltpu.SemaphoreType.DMA((2, 3)),
            pltpu.VMEM((tm, D_MODEL), F32),
        ],
        compiler_params=pltpu.CompilerParams(
            dimension_semantics=("arbitrary",), vmem_limit_bytes=VMEM_LIMIT_BYTES),
        name="ffn_final" if apply_final else "ffn",
    )(x, g, wg, wu, wd, fin)


def _inproj_stage_weights(w_hbm, w_ref, stage, sem):
    bounds = [(r0, min(STAGE_ROWS, IN_WIDTH - r0)) for r0 in range(0, IN_WIDTH, STAGE_ROWS)]

    def copy(c):
        r0, rows = bounds[c]
        return pltpu.make_async_copy(w_hbm.at[pl.ds(r0, rows), :],
                                     stage.at[c % 2, pl.ds(0, rows), :], sem.at[c % 2])

    copy(0).start()
    for c, (r0, rows) in enumerate(bounds):
        if c + 1 < len(bounds):
            copy(c + 1).start()
        copy(c).wait()
        w_ref[r0:r0 + rows, :] = stage[c % 2, 0:rows, :].astype(BF16)


def _inproj_kernel(x_ref, g_ref, w_hbm, brow_ref, bgate_ref, bcol_ref, conv_ref,
                   q_ref, kk_ref, avt_ref, mqk_ref, mvt_ref, mo_ref, gate_ref, gt_ref,
                   w_ref, stage, sem, conv_scr, *, tiles_per_seq):
    i = pl.program_id(0)
    tm = x_ref.shape[0]

    @pl.when(i == 0)
    def _():
        _inproj_stage_weights(w_hbm, w_ref, stage, sem)

    h = _rms(x_ref[...], g_ref[...]).astype(BF16)
    contract_lanes = (((1,), (1,)), ((), ()))

    def seg(r0, width, bias):
        return lax.dot_general(h, w_ref[r0:r0 + width, :], contract_lanes,
                               preferred_element_type=F32) + bias

    def seg_t(r0, rows):
        return (lax.dot_general(w_ref[r0:r0 + rows, :], h, contract_lanes,
                                preferred_element_type=F32) + bcol_ref[r0:r0 + rows, :])

    @pl.when(i == 0)
    def _():
        conv_scr[...] = jnp.zeros_like(conv_scr)

    carry = jnp.where(i % tiles_per_seq == 0, 0.0, conv_scr[...])

    def conv_slab(c, width):
        z = seg(OFF_MQ + c, width, brow_ref[:, OFF_MQ + c:OFF_MQ + c + width])
        conv_scr[:, c:c + width] = z[tm - SUBLANES:tm, :]
        zc = jnp.concatenate([carry[:, c:c + width], z], axis=0)
        acc = z * conv_ref[CONV_WIDTH - 1:CONV_WIDTH, c:c + width]
        for k in range(1, CONV_WIDTH):
            zk = pltpu.roll(zc, k, 0)[SUBLANES:, :]
            acc = acc + zk * conv_ref[CONV_WIDTH - 1 - k:CONV_WIDTH - k, c:c + width]
        mqk_ref[:, c:c + width] = (acc * jax.nn.sigmoid(acc)).astype(BF16)

    dw = MXU_DIM
    for k in range(SZ_AQ // dw):
        sl = slice(k * dw, (k + 1) * dw)
        conv_slab(k * dw, dw)
        q_ref[sl, :] = (seg_t(OFF_AQ + k * dw, dw) * (LOG2E * ATTN_HEAD_DIM ** -0.5)).astype(BF16)
        if k == 0:
            kk_ref[...] = seg(OFF_AK, SZ_AK, brow_ref[:, OFF_AK:OFF_AK + SZ_AK]).astype(BF16)
        elif k == 1:
            avt_ref[...] = seg_t(OFF_AV, SZ_AV).astype(BF16)
        mvt_ref[sl, :] = seg_t(OFF_MV + k * dw, dw).astype(BF16)
        mo = seg(OFF_MO + k * dw, dw, brow_ref[:, OFF_MO + k * dw:OFF_MO + (k + 1) * dw])
        mo_ref[:, sl] = jax.nn.sigmoid(mo).astype(BF16)
        gsl = slice(2 * k * dw, (2 * k + 2) * dw)
        gate_ref[:, gsl] = seg(OFF_G + 2 * k * dw, 2 * dw, bgate_ref[:, gsl]).astype(BF16)

    gt_ref[...] = seg_t(OFF_MI, SZ_MI + SZ_MF)


def _inproj(x, g, w_in, b_in, conv, seq):
    w = w_in.T
    b_row = b_in.reshape(1, IN_WIDTH)
    b_gate = b_in[OFF_G:].reshape(1, 2 * D_MODEL)
    b_col = b_in.reshape(IN_WIDTH, 1)
    n = x.shape[0]
    tm = TM_INPROJ
    row = lambda width: pl.BlockSpec((tm, width), lambda i: (i, 0))
    col = lambda height: pl.BlockSpec((height, tm), lambda i: (0, i))
    bf = lambda width: jax.ShapeDtypeStruct((n, width), BF16)
    return pl.pallas_call(
        functools.partial(_inproj_kernel, tiles_per_seq=seq // tm),
        grid=(n // tm,),
        in_specs=[
            row(D_MODEL),
            _const_spec((1, D_MODEL)),
            pl.BlockSpec(memory_space=pl.ANY),
            _const_spec((1, IN_WIDTH)),
            _const_spec((1, 2 * D_MODEL)),
            _const_spec((IN_WIDTH, 1)),
            _const_spec((CONV_WIDTH, SZ_MQ + SZ_MK)),
        ],
        out_specs=[col(SZ_AQ), row(SZ_AK), col(SZ_AV), row(SZ_MQ + SZ_MK), col(SZ_MV), row(SZ_MO),
                   row(2 * D_MODEL), col(2 * MLSTM_HEADS)],
        out_shape=[jax.ShapeDtypeStruct((SZ_AQ, n), BF16), bf(SZ_AK),
                   jax.ShapeDtypeStruct((SZ_AV, n), BF16),
                   bf(SZ_MQ + SZ_MK), jax.ShapeDtypeStruct((SZ_MV, n), BF16), bf(SZ_MO),
                   bf(2 * D_MODEL), jax.ShapeDtypeStruct((2 * MLSTM_HEADS, n), F32)],
        scratch_shapes=[pltpu.VMEM((IN_WIDTH, D_MODEL), BF16),
                        pltpu.VMEM((2, STAGE_ROWS, D_MODEL), F32),
                        pltpu.SemaphoreType.DMA((2,)),
                        pltpu.VMEM((SUBLANES, SZ_MQ + SZ_MK), F32)],
        compiler_params=pltpu.CompilerParams(
            dimension_semantics=("arbitrary",), vmem_limit_bytes=VMEM_LIMIT_BYTES),
        name="inproj",
    )(x, g, w, b_row, b_gate, b_col, conv)


def _attn_kernel(sink_ref, q_ref, kc_ref, kp_ref, vtc_ref, vtp_ref, o_ref, k_scr, vt_scr):
    w = WINDOW
    k_scr[0:w, :] = kp_ref[...]
    k_scr[w:, :] = kc_ref[...]
    vt_scr[:, 0:w] = vtp_ref[...]
    vt_scr[:, w:] = vtc_ref[...]
    kj = lax.broadcasted_iota(jnp.int32, (2 * w, w), 0)
    qi = lax.broadcasted_iota(jnp.int32, (2 * w, w), 1)
    band = (kj > qi) & (kj <= qi + w)
    band_first = band & ((kj >= w) | (pl.program_id(1) > 0))
    pairs = [(jb, t) for jb in range(q_ref.shape[1] // w) for t in range(ATTN_HEADS // 2)]
    zeros_qt = jnp.zeros((ATTN_HEAD_DIM, w), BF16)

    def scores(jb, t):
        ha, hb = ATTN_HEAD_ORDER[2 * t], ATTN_HEAD_ORDER[2 * t + 1]
        qa = q_ref[ha * ATTN_HEAD_DIM:(ha + 1) * ATTN_HEAD_DIM, jb * w:(jb + 1) * w]
        qb = q_ref[hb * ATTN_HEAD_DIM:(hb + 1) * ATTN_HEAD_DIM, jb * w:(jb + 1) * w]
        qt2 = jnp.concatenate([jnp.concatenate([qa, zeros_qt], axis=0),
                               jnp.concatenate([zeros_qt, qb], axis=0)], axis=1)
        kk = k_scr[jb * w:(jb + 2) * w, (t // 4) * LANES:(t // 4 + 1) * LANES]
        return jnp.dot(kk, qt2, preferred_element_type=F32)

    def head_out(jb, t, par, s):
        s = jnp.where(band_first if jb == 0 else band, s, -jnp.inf)
        sink = sink_ref[ATTN_HEAD_ORDER[2 * t + par]] * LOG2E
        mx = jnp.maximum(jnp.max(s, axis=0, keepdims=True), sink)
        pr = jnp.exp2(s - mx)
        denom = jnp.sum(pr, axis=0, keepdims=True) + jnp.exp2(sink - mx)
        kvh = 2 * (t // 4) + par
        vt = vt_scr[kvh * ATTN_HEAD_DIM:(kvh + 1) * ATTN_HEAD_DIM, jb * w:(jb + 2) * w]
        return jnp.dot(vt, pr.astype(BF16), preferred_element_type=F32) * (1.0 / denom)

    pending = [scores(*u) for u in pairs[:ATTN_LOOKAHEAD]]
    for idx, (jb, t) in enumerate(pairs):
        s2 = pending[idx]
        pending[idx] = None
        outs = [head_out(jb, t, par, s2[:, par * w:(par + 1) * w]) for par in range(2)]
        if idx + ATTN_LOOKAHEAD < len(pairs):
            pending.append(scores(*pairs[idx + ATTN_LOOKAHEAD]))
        pair = jnp.concatenate(outs, axis=0)
        o_ref[jb * w:(jb + 1) * w, t * LANES:(t + 1) * LANES] = pair.T.astype(BF16)


def _attention(sinks, q, kk, vt, batch, seq):
    n = kk.shape[0]
    tq = ATTN_BLOCKS * WINDOW
    nt = seq // tq
    cur = lambda b, j: b * nt + j
    prev = lambda b, j: (b * nt + j) * ATTN_BLOCKS - jnp.minimum(j, 1)
    return pl.pallas_call(
        _attn_kernel,
        grid=(batch, nt),
        in_specs=[
            pl.BlockSpec(memory_space=pltpu.SMEM),
            pl.BlockSpec((SZ_AQ, tq), lambda b, j: (0, cur(b, j))),
            pl.BlockSpec((tq, SZ_AK), lambda b, j: (cur(b, j), 0)),
            pl.BlockSpec((WINDOW, SZ_AK), lambda b, j: (prev(b, j), 0)),
            pl.BlockSpec((SZ_AV, tq), lambda b, j: (0, cur(b, j))),
            pl.BlockSpec((SZ_AV, WINDOW), lambda b, j: (0, prev(b, j))),
        ],
        out_specs=pl.BlockSpec((tq, SZ_AQ), lambda b, j: (cur(b, j), 0)),
        out_shape=jax.ShapeDtypeStruct((n, SZ_AQ), BF16),
        scratch_shapes=[pltpu.VMEM((WINDOW + tq, SZ_AK), BF16),
                        pltpu.VMEM((SZ_AV, WINDOW + tq), BF16)],
        compiler_params=pltpu.CompilerParams(
            dimension_semantics=("arbitrary", "arbitrary"), vmem_limit_bytes=VMEM_LIMIT_BYTES),
        name="attention",
    )(sinks, q, kk, kk, vt, vt)


def _log_sigmoid(x):
    return -(jnp.maximum(-x, 0.0) + jnp.log(1.0 + jnp.exp(-jnp.abs(x))))


def _mlstm_gate_kernel(g_ref, rows_ref, cols_ref, *, seq):
    L = MLSTM_CHUNK
    nh = MLSTM_HEADS
    n = g_ref.shape[1]
    g = g_ref[...]
    ig = GATE_SOFTCAP * jnp.tanh(g[0:nh] / GATE_SOFTCAP)
    fg = GATE_SOFTCAP * jnp.tanh(g[nh:2 * nh] / GATE_SOFTCAP)
    lf = _log_sigmoid(fg)
    pos = lax.broadcasted_iota(jnp.int32, (nh, n), 1) & (L - 1)
    a = lf
    sh = 1
    while sh < L:
        a = a + jnp.where(pos >= sh, pltpu.roll(a, sh, 1), 0.0)
        sh *= 2
    bvec = ig - a
    pm = bvec
    sh = 1
    while sh < L:
        pm = jnp.maximum(pm, jnp.where(pos >= sh, pltpu.roll(pm, sh, 1), -jnp.inf))
        sh *= 2
    pad = jnp.zeros((L - nh, L), F32)
    m = [jnp.zeros((nh, L), F32) for _ in range(n // seq)]
    for c in range(seq // L):
        for s in range(n // seq):
            sl = slice(s * seq + c * L, s * seq + (c + 1) * L)
            a_c, b_c = a[:, sl], bvec[:, sl]
            mrow = jnp.maximum(m[s], pm[:, sl])
            total = jnp.broadcast_to(a_c[:, L - 1:L], (nh, L))
            mlast = jnp.broadcast_to(mrow[:, L - 1:L], (nh, L))
            rows_ref[ROW_M2:ROW_M2 + nh, sl] = mrow * LOG2E
            rows_ref[ROW_WINTER:ROW_WINTER + nh, sl] = jnp.exp(m[s] - mrow)
            rows_ref[ROW_ENEG:ROW_ENEG + nh, sl] = jnp.exp(-(a_c + mrow))
            rows_ref[ROW_WK:ROW_WK + nh, sl] = jnp.exp(b_c - mlast) * MLSTM_QK_DIM ** -0.5
            rows_ref[ROW_DECAY:ROW_DECAY + nh, sl] = jnp.exp(m[s] - mlast)
            m[s] = total + mlast
            cols_ref[sl, :] = jnp.concatenate([b_c * LOG2E - K_SCALE_LOG2, pad], axis=0).T


def _mlstm_gates(gt, seq):
    n = gt.shape[1]
    return pl.pallas_call(
        functools.partial(_mlstm_gate_kernel, seq=seq),
        grid=(1,),
        in_specs=[pl.BlockSpec((2 * MLSTM_HEADS, n), lambda i: (0, 0))],
        out_specs=[pl.BlockSpec((GATE_ROWS, n), lambda i: (0, 0)),
                   pl.BlockSpec((n, LANES), lambda i: (0, 0))],
        out_shape=[jax.ShapeDtypeStruct((GATE_ROWS, n), F32),
                   jax.ShapeDtypeStruct((n, LANES), F32)],
        compiler_params=pltpu.CompilerParams(
            dimension_semantics=("arbitrary",), vmem_limit_bytes=VMEM_LIMIT_BYTES),
        name="mlstm_gates",
    )(gt)


def _mlstm_kernel(rows_ref, cols_ref, qk_ref, vt_ref, og_ref, hn_ref, o_ref, ct_scr, n_scr):
    L = MLSTM_CHUNK
    nh = MLSTM_HEADS
    contract_lanes = (((1,), (1,)), ((), ()))

    @pl.when(pl.program_id(1) == 0)
    def _():
        ct_scr[...] = jnp.zeros_like(ct_scr)
        n_scr[...] = jnp.zeros_like(n_scr)

    row_i = lax.broadcasted_iota(jnp.int32, (L, L), 0)
    col_i = lax.broadcasted_iota(jnp.int32, (L, L), 1)
    causal_t = row_i <= col_i
    lo_lane = col_i < MLSTM_QK_DIM
    zero = jnp.zeros((), BF16)

    n_chunks = qk_ref.shape[0] // L
    pairs = [(ci, p) for ci in range(n_chunks) for p in range(nh // 2)]
    ct = [ct_scr[p] for p in range(nh // 2)]
    nn = [n_scr[p] for p in range(nh // 2)]
    gate_rows = {}

    def chunk_gates(ci):
        if ci not in gate_rows:
            rs = slice(ci * L, (ci + 1) * L)
            wk = rows_ref[ROW_WK:ROW_WK + nh, rs]
            gate_rows[ci] = dict(
                m2=rows_ref[ROW_M2:ROW_M2 + nh, rs], w_inter=rows_ref[ROW_WINTER:ROW_WINTER + nh, rs],
                e_neg=rows_ref[ROW_ENEG:ROW_ENEG + nh, rs], wk=wk,
                decay=rows_ref[ROW_DECAY:ROW_DECAY + nh, rs], b2cols=cols_ref[rs, :],
                wk_b=jnp.concatenate([wk, wk], axis=0).astype(BF16))
        return gate_rows[ci]

    def front(ci, p):
        g = chunk_gates(ci)
        rs = slice(ci * L, (ci + 1) * L)
        h0, h1 = 2 * p, 2 * p + 1
        q2 = qk_ref[rs, p * LANES:(p + 1) * LANES]
        k2 = qk_ref[rs, SZ_MQ + p * LANES:SZ_MQ + (p + 1) * LANES]
        qm2 = jnp.concatenate([jnp.where(lo_lane, q2, zero), jnp.where(lo_lane, zero, q2)], axis=0)
        n2b = jnp.broadcast_to(nn[p], (2 * SUBLANES, LANES)).astype(BF16)
        stacked = jnp.concatenate([k2, ct[p].astype(BF16), n2b], axis=0)
        res = lax.dot_general(stacked, qm2, contract_lanes, preferred_element_type=F32)
        vw = [(vt_ref[hh * MLSTM_V_DIM:(hh + 1) * MLSTM_V_DIM, rs].astype(F32)
               * g["wk"][hh:hh + 1, :]).astype(BF16) for hh in (h0, h1)]
        upd = jnp.dot(jnp.concatenate(vw + [g["wk_b"]], axis=0), k2,
                      preferred_element_type=F32)
        n_inc = upd[2 * MLSTM_V_DIM:]
        dec = jnp.where(lo_lane[0:1, :], g["decay"][h0:h0 + 1, :], g["decay"][h1:h1 + 1, :])
        ct[p] = dec * ct[p] + jnp.where(lo_lane, upd[0:MLSTM_V_DIM], upd[MLSTM_V_DIM:2 * MLSTM_V_DIM])
        nn[p] = dec * nn[p] + jnp.where(lo_lane[0:1, :], n_inc[h0:h0 + 1, :], n_inc[h1:h1 + 1, :])
        return res

    def back(ci, p, par, res):
        st = res[0:L, par * L:(par + 1) * L]
        inter = res[L:L + MLSTM_V_DIM, par * L:(par + 1) * L]
        qn = res[L + MLSTM_V_DIM:L + MLSTM_V_DIM + 1, par * L:(par + 1) * L]
        g = chunk_gates(ci)
        rs = slice(ci * L, (ci + 1) * L)
        hh = 2 * p + par
        vsl = slice(hh * MLSTM_V_DIM, (hh + 1) * MLSTM_V_DIM)
        dt = jnp.where(causal_t,
                       jnp.exp2(g["b2cols"][:, hh:hh + 1] - g["m2"][hh:hh + 1, :]), 0.0)
        sct = st * dt
        wi = g["w_inter"][hh:hh + 1, :]
        den = jnp.sum(sct, axis=0, keepdims=True) + wi * qn
        numt = jnp.dot(vt_ref[vsl, rs], sct.astype(BF16), preferred_element_type=F32) + wi * inter
        rden = 1.0 / jnp.maximum(jnp.abs(den), g["e_neg"][hh:hh + 1, :])
        ms = jnp.mean(numt * numt, axis=0, keepdims=True)
        scale = rden * lax.rsqrt(rden * rden * ms + NORM_EPS)
        hv = (numt * scale).T
        o_ref[rs, vsl] = (hv * hn_ref[:, vsl] * og_ref[rs, vsl].astype(F32)).astype(BF16)

    pending = [front(*u) for u in pairs[:MLSTM_LOOKAHEAD]]
    for idx, (ci, p) in enumerate(pairs):
        res = pending[idx]
        pending[idx] = None
        back(ci, p, 0, res)
        if idx + MLSTM_LOOKAHEAD < len(pairs):
            pending.append(front(*pairs[idx + MLSTM_LOOKAHEAD]))
        back(ci, p, 1, res)
    for p in range(nh // 2):
        ct_scr[p] = ct[p]
        n_scr[p] = nn[p]


def _mlstm(rows, cols, qk, v, og, hn, batch, seq):
    n = qk.shape[0]
    tm = TM_MLSTM
    nt = seq // tm
    row = pl.BlockSpec((tm, SZ_MV), lambda b, c: (b * nt + c, 0))
    return pl.pallas_call(
        _mlstm_kernel,
        grid=(batch, nt),
        in_specs=[
            pl.BlockSpec((GATE_ROWS, tm), lambda b, c: (0, b * nt + c)),
            pl.BlockSpec((tm, LANES), lambda b, c: (b * nt + c, 0)),
            row,
            pl.BlockSpec((SZ_MV, tm), lambda b, c: (0, b * nt + c)),
            row,
            _const_spec((1, SZ_MV)),
        ],
        out_specs=row,
        out_shape=jax.ShapeDtypeStruct((n, SZ_MV), BF16),
        scratch_shapes=[
            pltpu.VMEM((MLSTM_HEADS // 2, 2 * MLSTM_QK_DIM, MLSTM_V_DIM), F32),
            pltpu.VMEM((MLSTM_HEADS // 2, 1, 2 * MLSTM_QK_DIM), F32),
        ],
        compiler_params=pltpu.CompilerParams(
            dimension_semantics=("arbitrary", "arbitrary"), vmem_limit_bytes=VMEM_LIMIT_BYTES),
        name="mlstm",
    )(rows, cols, qk, v, og, hn)


def _merge_stage_weights(wpa_hbm, wpm_hbm, wo_hbm, wpa_ref, wpm_ref, wo_ref, stage, sem):
    hd = ATTN_HEAD_DIM
    wpa_copies = [pltpu.make_async_copy(wpa_hbm.at[pl.ds(head * hd, hd), :],
                                        stage.at[0, pl.ds(pos * hd, hd), :], sem.at[0])
                  for pos, head in enumerate(ATTN_HEAD_ORDER)]
    wpm_copy = pltpu.make_async_copy(wpm_hbm, stage.at[1], sem.at[1])
    wo_copy = pltpu.make_async_copy(wo_hbm, stage.at[0], sem.at[0])
    for cp in wpa_copies:
        cp.start()
    wpm_copy.start()
    for cp in wpa_copies:
        cp.wait()
    wpa_ref[...] = stage[0].astype(BF16)
    wo_copy.start()
    wpm_copy.wait()
    wpm_ref[...] = stage[1].astype(BF16)
    wo_copy.wait()
    wo_ref[...] = stage[0].astype(BF16)


def _merge_kernel(x_ref, a_ref, hm_ref, gate_ref, wpa_hbm, wpm_hbm, wo_hbm, o_ref,
                  wpa_ref, wpm_ref, wo_ref, stage, sem):
    @pl.when(pl.program_id(0) == 0)
    def _():
        _merge_stage_weights(wpa_hbm, wpm_hbm, wo_hbm, wpa_ref, wpm_ref, wo_ref, stage, sem)

    ya = jnp.dot(a_ref[...], wpa_ref[...], preferred_element_type=F32)
    ym = jnp.dot(hm_ref[...], wpm_ref[...], preferred_element_type=F32)
    ga = jax.nn.sigmoid(gate_ref[:, 0:D_MODEL].astype(F32))
    gm = jax.nn.sigmoid(gate_ref[:, D_MODEL:2 * D_MODEL].astype(F32))
    merged = (ga * ya + gm * ym).astype(BF16)
    o_ref[...] = x_ref[...] + jnp.dot(merged, wo_ref[...], preferred_element_type=F32)


def _merge(x, a, hm, gates, wpa, wpm, wo):
    n = x.shape[0]
    tm = TM_MERGE
    row = lambda width: pl.BlockSpec((tm, width), lambda i: (i, 0))
    return pl.pallas_call(
        _merge_kernel,
        grid=(n // tm,),
        in_specs=[row(D_MODEL), row(SZ_AQ), row(SZ_MV), row(2 * D_MODEL),
                  pl.BlockSpec(memory_space=pl.ANY), pl.BlockSpec(memory_space=pl.ANY),
                  pl.BlockSpec(memory_space=pl.ANY)],
        out_specs=row(D_MODEL),
        out_shape=jax.ShapeDtypeStruct((n, D_MODEL), F32),
        scratch_shapes=[pltpu.VMEM((D_MODEL, D_MODEL), BF16)] * 3 + [
            pltpu.VMEM((2, D_MODEL, D_MODEL), F32), pltpu.SemaphoreType.DMA((2,))],
        compiler_params=pltpu.CompilerParams(
            dimension_semantics=("arbitrary",), vmem_limit_bytes=VMEM_LIMIT_BYTES),
        name="merge",
    )(x, a, hm, gates, wpa, wpm, wo)


def kernel(x, ffn1_norm, ffn1_w_gate, ffn1_w_up, ffn1_w_down, mix_norm, w_in, b_in, attn_sinks,
           mlstm_conv, mlstm_head_norm, w_proj_attn, w_proj_mlstm, w_out, ffn2_norm, ffn2_w_gate,
           ffn2_w_up, ffn2_w_down, final_norm):
    batch, seq, d = x.shape
    assert d == D_MODEL and ffn1_norm.shape[0] == 1, "one layer of width D_MODEL"
    assert all(seq % t == 0 for t in (TM_INPROJ, TM_MLSTM, ATTN_BLOCKS * WINDOW))
    assert all((batch * seq) % t == 0 for t in (TM_FFN, TM_MERGE))
    n = batch * seq
    xf = x.reshape(n, d)
    fin = final_norm.reshape(1, d)

    x1 = _ffn(xf, ffn1_norm[0].reshape(1, d), ffn1_w_gate[0], ffn1_w_up[0], ffn1_w_down[0], fin,
              apply_final=False)

    q, kk, avt, mqk, mvt, mo, gates, gt = _inproj(x1, mix_norm[0].reshape(1, d), w_in[0], b_in[0],
                                                  mlstm_conv[0], seq)

    ya = _attention(attn_sinks[0], q, kk, avt, batch, seq)

    rows, cols = _mlstm_gates(gt, seq)
    hm = _mlstm(rows, cols, mqk, mvt, mo, mlstm_head_norm[0].reshape(1, SZ_MV), batch, seq)

    x2 = _merge(x1, ya, hm, gates, w_proj_attn[0], w_proj_mlstm[0], w_out[0])

    out = _ffn(x2, ffn2_norm[0].reshape(1, d), ffn2_w_gate[0], ffn2_w_up[0], ffn2_w_down[0], fin,
               apply_final=True)
    return out.reshape(batch, seq, d)
```

```python
import functools

import jax
import jax.numpy as jnp
from jax import lax
from jax.experimental import pallas as pl
from jax.experimental.pallas import tpu as pltpu

F32 = jnp.float32
BF16 = jnp.bfloat16

D_MODEL = 1024
ATTN_HEAD_DIM = 64
ATTN_HEADS = 16
ATTN_KV_HEADS = 4
ATTN_GROUP = 4
WINDOW = 128
MLSTM_HEADS = 8
MLSTM_V_DIM = 128
MLSTM_QK_DIM = 64
CONV_WIDTH = 4
GATE_SOFTCAP = 15.0
D_FF = 2816
FFN_RESIDUAL_WEIGHT = 0.5
NORM_EPS = 1e-6

SZ_AQ, SZ_AK, SZ_AV = 1024, 256, 256
SZ_MQ, SZ_MK, SZ_MV, SZ_MO = 512, 512, 1024, 1024
SZ_MI, SZ_MF = 8, 8

LANES = 128
SUBLANES = 8
MXU_DIM = 256
VMEM_LIMIT_BYTES = 56 * 1024 * 1024

MLSTM_CHUNK = 128
FF_CHUNK = MXU_DIM
TM_FFN = 1024
TM_INPROJ = 1024
TM_MERGE = 1024
TM_MLSTM = 8 * MLSTM_CHUNK
ATTN_LOOKAHEAD = 4
ATTN_BLOCKS = 8
MLSTM_LOOKAHEAD = 2

LOG2E = 1.4426950408889634
K_SCALE_LOG2 = 3.0
ROW_M2, ROW_WINTER, ROW_ENEG, ROW_WK, ROW_DECAY = 0, 8, 16, 24, 32
GATE_ROWS = 40

OFF_AQ = 0
OFF_AK = OFF_AQ + SZ_AQ
OFF_AV = OFF_AK + SZ_AK
OFF_MQ = OFF_AV + SZ_AV
OFF_MK = OFF_MQ + SZ_MQ
OFF_MV = OFF_MK + SZ_MK
OFF_MO = OFF_MV + SZ_MV
OFF_MI = OFF_MO + SZ_MO
OFF_MF = OFF_MI + SZ_MI
OFF_G = OFF_MF + SZ_MF
IN_WIDTH = OFF_G + 2 * D_MODEL
FEATURE_MAJOR_GROUPS = ((OFF_AQ, SZ_AQ), (OFF_AV, SZ_AV), (OFF_MV, SZ_MV), (OFF_MI, SZ_MI + SZ_MF))
FEATURE_MAJOR_ROWS = sum(size for _, size in FEATURE_MAJOR_GROUPS)
_FEATURE_MAJOR_BIAS_ROWS = tuple(
    (off, size, sum(s for _, s in FEATURE_MAJOR_GROUPS[:k]))
    for k, (off, size) in enumerate(FEATURE_MAJOR_GROUPS))
STAGE_ROWS = 256
STAGE_SLOTS = 4

ATTN_HEAD_ORDER = tuple((2 * (t // 4) + par) * ATTN_GROUP + t % 4
                        for t in range(ATTN_HEADS // 2) for par in range(2))


def _rms(x, g):
    return x * lax.rsqrt(jnp.mean(x * x, axis=-1, keepdims=True) + NORM_EPS) * g


def _const_spec(shape):
    zeros = (0,) * len(shape)
    return pl.BlockSpec(shape, lambda *_: zeros, pipeline_mode=pl.Buffered(1))


def _ffn_stage_weights(wg_hbm, wu_hbm, wd_hbm, wg_ref, wu_ref, wd_ref, gu_stage, d_stage, sem):
    n_chunks = D_FF // FF_CHUNK

    def copies(c, slot):
        cols = pl.ds(c * FF_CHUNK, FF_CHUNK)
        return (pltpu.make_async_copy(wg_hbm.at[:, cols], gu_stage.at[slot, 0], sem.at[slot, 0]),
                pltpu.make_async_copy(wu_hbm.at[:, cols], gu_stage.at[slot, 1], sem.at[slot, 1]),
                pltpu.make_async_copy(wd_hbm.at[cols, :], d_stage.at[slot], sem.at[slot, 2]))

    for cp in copies(0, 0):
        cp.start()
    for c in range(n_chunks):
        slot = c % 2
        if c + 1 < n_chunks:
            for cp in copies(c + 1, 1 - slot):
                cp.start()
        for cp in copies(c, slot):
            cp.wait()
        sl = slice(c * FF_CHUNK, (c + 1) * FF_CHUNK)
        wg_ref[:, sl] = gu_stage[slot, 0].astype(BF16)
        wu_ref[:, sl] = gu_stage[slot, 1].astype(BF16)
        wd_ref[sl, :] = d_stage[slot].astype(BF16)


def _ffn_kernel(x_ref, g_ref, wg_hbm, wu_hbm, wd_hbm, fin_ref, o_ref,
                wg_ref, wu_ref, wd_ref, gu_stage, d_stage, sem, acc_ref, *, apply_final):
    @pl.when(pl.program_id(0) == 0)
    def _():
        _ffn_stage_weights(wg_hbm, wu_hbm, wd_hbm, wg_ref, wu_ref, wd_ref, gu_stage, d_stage, sem)

    x = x_ref[...]
    h = _rms(x, g_ref[...]).astype(BF16)
    for c in range(D_FF // FF_CHUNK):
        sl = slice(c * FF_CHUNK, (c + 1) * FF_CHUNK)
        a = jnp.dot(h, wg_ref[:, sl], preferred_element_type=F32)
        u = jnp.dot(h, wu_ref[:, sl], preferred_element_type=F32)
        act = (a * jax.nn.sigmoid(a) * u).astype(BF16)
        d = jnp.dot(act, wd_ref[sl, :], preferred_element_type=F32)
        if c == 0:
            acc_ref[...] = d
        else:
            acc_ref[...] += d
    y = x + FFN_RESIDUAL_WEIGHT * acc_ref[...]
    if apply_final:
        y = _rms(y, fin_ref[...])
    o_ref[...] = y


def _ffn(x, g, wg, wu, wd, fin, apply_final):
    n = x.shape[0]
    tm = TM_FFN
    return pl.pallas_call(
        functools.partial(_ffn_kernel, apply_final=apply_final),
        grid=(n // tm,),
        in_specs=[
            pl.BlockSpec((tm, D_MODEL), lambda i: (i, 0)),
            _const_spec((1, D_MODEL)),
            pl.BlockSpec(memory_space=pl.ANY),
            pl.BlockSpec(memory_space=pl.ANY),
            pl.BlockSpec(memory_space=pl.ANY),
            _const_spec((1, D_MODEL)),
        ],
        out_specs=pl.BlockSpec((tm, D_MODEL), lambda i: (i, 0)),
        out_shape=jax.ShapeDtypeStruct((n, D_MODEL), F32),
        scratch_shapes=[
            pltpu.VMEM((D_MODEL, D_FF), BF16),
            pltpu.VMEM((D_MODEL, D_FF), BF16),
            pltpu.VMEM((D_FF, D_MODEL), BF16),
            pltpu.VMEM((2, 2, D_MODEL, FF_CHUNK), F32),
            pltpu.VMEM((2, FF_CHUNK, D_MODEL), F32),
            pltpu.SemaphoreType.DMA((2, 3)),
            pltpu.VMEM((tm, D_MODEL), F32),
        ],
        compiler_params=pltpu.CompilerParams(
            dimension_semantics=("arbitrary",), vmem_limit_bytes=VMEM_LIMIT_BYTES),
        name="ffn_final" if apply_final else "ffn",
    )(x, g, wg, wu, wd, fin)


def _inproj_stage_weights(w_hbm, w_ref, stage, sem):
    bounds = [(r0, min(STAGE_ROWS, IN_WIDTH - r0)) for r0 in range(0, IN_WIDTH, STAGE_ROWS)]

    def copy(c):
        r0, rows = bounds[c]
        slot = c % STAGE_SLOTS
        return pltpu.make_async_copy(w_hbm.at[pl.ds(r0, rows), :],
                                     stage.at[slot, pl.ds(0, rows), :], sem.at[slot])

    for c in range(min(STAGE_SLOTS - 1, len(bounds))):
        copy(c).start()
    for c, (r0, rows) in enumerate(bounds):
        if c + STAGE_SLOTS - 1 < len(bounds):
            copy(c + STAGE_SLOTS - 1).start()
        copy(c).wait()
        w_ref[r0:r0 + rows, :] = stage[c % STAGE_SLOTS, 0:rows, :].astype(BF16)


def _inproj_kernel(x_ref, g_ref, w_hbm, brow_ref, bgate_ref, bcol_ref, conv_ref,
                   q_ref, kk_ref, avt_ref, mqk_ref, mvt_ref, mo_ref, gate_ref, gt_ref,
                   w_ref, stage, sem, conv_scr, *, tiles_per_seq):
    i = pl.program_id(0)
    tm = x_ref.shape[0]

    @pl.when(i == 0)
    def _():
        _inproj_stage_weights(w_hbm, w_ref, stage, sem)

    h = _rms(x_ref[...], g_ref[...]).astype(BF16)
    contract_lanes = (((1,), (1,)), ((), ()))

    def seg(r0, width, bias):
        return lax.dot_general(h, w_ref[r0:r0 + width, :], contract_lanes,
                               preferred_element_type=F32) + bias

    def seg_t(r0, rows):
        c0 = next(base + r0 - off for off, size, base in _FEATURE_MAJOR_BIAS_ROWS
                  if off <= r0 < off + size)
        return (lax.dot_general(w_ref[r0:r0 + rows, :], h, contract_lanes,
                                preferred_element_type=F32) + bcol_ref[c0:c0 + rows, :])

    @pl.when(i == 0)
    def _():
        conv_scr[...] = jnp.zeros_like(conv_scr)

    carry = jnp.where(i % tiles_per_seq == 0, 0.0, conv_scr[...])

    def conv_slab(c, width):
        z = seg(OFF_MQ + c, width, brow_ref[:, OFF_MQ + c:OFF_MQ + c + width])
        conv_scr[:, c:c + width] = z[tm - SUBLANES:tm, :]
        zc = jnp.concatenate([carry[:, c:c + width], z], axis=0)
        acc = z * conv_ref[CONV_WIDTH - 1:CONV_WIDTH, c:c + width]
        for k in range(1, CONV_WIDTH):
            zk = pltpu.roll(zc, k, 0)[SUBLANES:, :]
            acc = acc + zk * conv_ref[CONV_WIDTH - 1 - k:CONV_WIDTH - k, c:c + width]
        mqk_ref[:, c:c + width] = (acc * jax.nn.sigmoid(acc)).astype(BF16)

    dw = MXU_DIM
    for k in range(SZ_AQ // dw):
        sl = slice(k * dw, (k + 1) * dw)
        conv_slab(k * dw, dw)
        q_ref[sl, :] = (seg_t(OFF_AQ + k * dw, dw) * (LOG2E * ATTN_HEAD_DIM ** -0.5)).astype(BF16)
        if k == 0:
            kk_ref[...] = seg(OFF_AK, SZ_AK, brow_ref[:, OFF_AK:OFF_AK + SZ_AK]).astype(BF16)
        elif k == 1:
            avt_ref[...] = seg_t(OFF_AV, SZ_AV).astype(BF16)
        mvt_ref[sl, :] = seg_t(OFF_MV + k * dw, dw).astype(BF16)
        mo = seg(OFF_MO + k * dw, dw, brow_ref[:, OFF_MO + k * dw:OFF_MO + (k + 1) * dw])
        mo_ref[:, sl] = jax.nn.sigmoid(mo).astype(BF16)
        gsl = slice(2 * k * dw, (2 * k + 2) * dw)
        gate_ref[:, gsl] = seg(OFF_G + 2 * k * dw, 2 * dw, bgate_ref[:, gsl]).astype(BF16)

    gt_ref[...] = seg_t(OFF_MI, SZ_MI + SZ_MF)


def _inproj(x, g, w_in, b_in, conv, seq):
    w = w_in.T
    b_row = b_in.reshape(1, IN_WIDTH)
    b_gate = b_in[OFF_G:].reshape(1, 2 * D_MODEL)
    b_col = jnp.concatenate([b_in[off:off + size] for off, size in FEATURE_MAJOR_GROUPS]).reshape(
        FEATURE_MAJOR_ROWS, 1)
    n = x.shape[0]
    tm = TM_INPROJ
    row = lambda width: pl.BlockSpec((tm, width), lambda i: (i, 0))
    col = lambda height: pl.BlockSpec((height, tm), lambda i: (0, i))
    bf = lambda width: jax.ShapeDtypeStruct((n, width), BF16)
    return pl.pallas_call(
        functools.partial(_inproj_kernel, tiles_per_seq=seq // tm),
        grid=(n // tm,),
        in_specs=[
            row(D_MODEL),
            _const_spec((1, D_MODEL)),
            pl.BlockSpec(memory_space=pl.ANY),
            _const_spec((1, IN_WIDTH)),
            _const_spec((1, 2 * D_MODEL)),
            _const_spec((FEATURE_MAJOR_ROWS, 1)),
            _const_spec((CONV_WIDTH, SZ_MQ + SZ_MK)),
        ],
        out_specs=[col(SZ_AQ), row(SZ_AK), col(SZ_AV), row(SZ_MQ + SZ_MK), col(SZ_MV), row(SZ_MO),
                   row(2 * D_MODEL), col(2 * MLSTM_HEADS)],
        out_shape=[jax.ShapeDtypeStruct((SZ_AQ, n), BF16), bf(SZ_AK),
                   jax.ShapeDtypeStruct((SZ_AV, n), BF16),
                   bf(SZ_MQ + SZ_MK), jax.ShapeDtypeStruct((SZ_MV, n), BF16), bf(SZ_MO),
                   bf(2 * D_MODEL), jax.ShapeDtypeStruct((2 * MLSTM_HEADS, n), F32)],
        scratch_shapes=[pltpu.VMEM((IN_WIDTH, D_MODEL), BF16),
                        pltpu.VMEM((STAGE_SLOTS, STAGE_ROWS, D_MODEL), F32),
                        pltpu.SemaphoreType.DMA((STAGE_SLOTS,)),
                        pltpu.VMEM((SUBLANES, SZ_MQ + SZ_MK), F32)],
        compiler_params=pltpu.CompilerParams(
            dimension_semantics=("arbitrary",), vmem_limit_bytes=VMEM_LIMIT_BYTES),
        name="inproj",
    )(x, g, w, b_row, b_gate, b_col, conv)


def _attn_kernel(sink_ref, q_ref, kc_ref, kp_ref, vtc_ref, vtp_ref, o_ref, k_scr, vt_scr):
    w = WINDOW
    k_scr[0:w, :] = kp_ref[...]
    k_scr[w:, :] = kc_ref[...]
    vt_scr[:, 0:w] = vtp_ref[...]
    vt_scr[:, w:] = vtc_ref[...]
    kj = lax.broadcasted_iota(jnp.int32, (2 * w, w), 0)
    qi = lax.broadcasted_iota(jnp.int32, (2 * w, w), 1)
    band = (kj > qi) & (kj <= qi + w)
    band_first = band & ((kj >= w) | (pl.program_id(1) > 0))
    pairs = [(jb, t) for jb in range(q_ref.shape[1] // w) for t in range(ATTN_HEADS // 2)]
    zeros_qt = jnp.zeros((ATTN_HEAD_DIM, w), BF16)

    def scores(jb, t):
        ha, hb = ATTN_HEAD_ORDER[2 * t], ATTN_HEAD_ORDER[2 * t + 1]
        qa = q_ref[ha * ATTN_HEAD_DIM:(ha + 1) * ATTN_HEAD_DIM, jb * w:(jb + 1) * w]
        qb = q_ref[hb * ATTN_HEAD_DIM:(hb + 1) * ATTN_HEAD_DIM, jb * w:(jb + 1) * w]
        qt2 = jnp.concatenate([jnp.concatenate([qa, zeros_qt], axis=0),
                               jnp.concatenate([zeros_qt, qb], axis=0)], axis=1)
        kk = k_scr[jb * w:(jb + 2) * w, (t // 4) * LANES:(t // 4 + 1) * LANES]
        return jnp.dot(kk, qt2, preferred_element_type=F32)

    def head_out(jb, t, par, s):
        s = jnp.where(band_first if jb == 0 else band, s, -jnp.inf)
        sink = sink_ref[ATTN_HEAD_ORDER[2 * t + par]] * LOG2E
        mx = jnp.maximum(jnp.max(s, axis=0, keepdims=True), sink)
        pr = jnp.exp2(s - mx)
        denom = jnp.sum(pr, axis=0, keepdims=True) + jnp.exp2(sink - mx)
        kvh = 2 * (t // 4) + par
        vt = vt_scr[kvh * ATTN_HEAD_DIM:(kvh + 1) * ATTN_HEAD_DIM, jb * w:(jb + 2) * w]
        return jnp.dot(vt, pr.astype(BF16), preferred_element_type=F32) * (1.0 / denom)

    pending = [scores(*u) for u in pairs[:ATTN_LOOKAHEAD]]
    for idx, (jb, t) in enumerate(pairs):
        s2 = pending[idx]
        pending[idx] = None
        outs = [head_out(jb, t, par, s2[:, par * w:(par + 1) * w]) for par in range(2)]
        if idx + ATTN_LOOKAHEAD < len(pairs):
            pending.append(scores(*pairs[idx + ATTN_LOOKAHEAD]))
        pair = jnp.concatenate(outs, axis=0)
        o_ref[jb * w:(jb + 1) * w, t * LANES:(t + 1) * LANES] = pair.T.astype(BF16)


def _attention(sinks, q, kk, vt, batch, seq):
    n = kk.shape[0]
    tq = ATTN_BLOCKS * WINDOW
    nt = seq // tq
    cur = lambda b, j: b * nt + j
    prev = lambda b, j: (b * nt + j) * ATTN_BLOCKS - jnp.minimum(j, 1)
    return pl.pallas_call(
        _attn_kernel,
        grid=(batch, nt),
        in_specs=[
            pl.BlockSpec(memory_space=pltpu.SMEM),
            pl.BlockSpec((SZ_AQ, tq), lambda b, j: (0, cur(b, j))),
            pl.BlockSpec((tq, SZ_AK), lambda b, j: (cur(b, j), 0)),
            pl.BlockSpec((WINDOW, SZ_AK), lambda b, j: (prev(b, j), 0)),
            pl.BlockSpec((SZ_AV, tq), lambda b, j: (0, cur(b, j))),
            pl.BlockSpec((SZ_AV, WINDOW), lambda b, j: (0, prev(b, j))),
        ],
        out_specs=pl.BlockSpec((tq, SZ_AQ), lambda b, j: (cur(b, j), 0)),
        out_shape=jax.ShapeDtypeStruct((n, SZ_AQ), BF16),
        scratch_shapes=[pltpu.VMEM((WINDOW + tq, SZ_AK), BF16),
                        pltpu.VMEM((SZ_AV, WINDOW + tq), BF16)],
        compiler_params=pltpu.CompilerParams(
            dimension_semantics=("arbitrary", "arbitrary"), vmem_limit_bytes=VMEM_LIMIT_BYTES),
        name="attention",
    )(sinks, q, kk, kk, vt, vt)


def _log_sigmoid(x):
    return -(jnp.maximum(-x, 0.0) + jnp.log(1.0 + jnp.exp(-jnp.abs(x))))


def _mlstm_gate_kernel(g_ref, rows_ref, cols_ref, *, seq):
    L = MLSTM_CHUNK
    nh = MLSTM_HEADS
    n = g_ref.shape[1]
    g = g_ref[...]
    ig = GATE_SOFTCAP * jnp.tanh(g[0:nh] / GATE_SOFTCAP)
    fg = GATE_SOFTCAP * jnp.tanh(g[nh:2 * nh] / GATE_SOFTCAP)
    lf = _log_sigmoid(fg)
    pos = lax.broadcasted_iota(jnp.int32, (nh, n), 1) & (L - 1)
    a = lf
    sh = 1
    while sh < L:
        a = a + jnp.where(pos >= sh, pltpu.roll(a, sh, 1), 0.0)
        sh *= 2
    bvec = ig - a
    pm = bvec
    sh = 1
    while sh < L:
        pm = jnp.maximum(pm, jnp.where(pos >= sh, pltpu.roll(pm, sh, 1), -jnp.inf))
        sh *= 2
    pad = jnp.zeros((L - nh, L), F32)
    m = [jnp.zeros((nh, L), F32) for _ in range(n // seq)]
    for c in range(seq // L):
        for s in range(n // seq):
            sl = slice(s * seq + c * L, s * seq + (c + 1) * L)
            a_c, b_c = a[:, sl], bvec[:, sl]
            mrow = jnp.maximum(m[s], pm[:, sl])
            total = jnp.broadcast_to(a_c[:, L - 1:L], (nh, L))
            mlast = jnp.broadcast_to(mrow[:, L - 1:L], (nh, L))
            rows_ref[ROW_M2:ROW_M2 + nh, sl] = mrow * LOG2E
            rows_ref[ROW_WINTER:ROW_WINTER + nh, sl] = jnp.exp(m[s] - mrow)
            rows_ref[ROW_ENEG:ROW_ENEG + nh, sl] = jnp.exp(-(a_c + mrow))
            rows_ref[ROW_WK:ROW_WK + nh, sl] = jnp.exp(b_c - mlast) * MLSTM_QK_DIM ** -0.5
            rows_ref[ROW_DECAY:ROW_DECAY + nh, sl] = jnp.exp(m[s] - mlast)
            m[s] = total + mlast
            cols_ref[sl, :] = jnp.concatenate([b_c * LOG2E - K_SCALE_LOG2, pad], axis=0).T


def _mlstm_gates(gt, seq):
    n = gt.shape[1]
    return pl.pallas_call(
        functools.partial(_mlstm_gate_kernel, seq=seq),
        grid=(1,),
        in_specs=[pl.BlockSpec((2 * MLSTM_HEADS, n), lambda i: (0, 0))],
        out_specs=[pl.BlockSpec((GATE_ROWS, n), lambda i: (0, 0)),
                   pl.BlockSpec((n, LANES), lambda i: (0, 0))],
        out_shape=[jax.ShapeDtypeStruct((GATE_ROWS, n), F32),
                   jax.ShapeDtypeStruct((n, LANES), F32)],
        compiler_params=pltpu.CompilerParams(
            dimension_semantics=("arbitrary",), vmem_limit_bytes=VMEM_LIMIT_BYTES),
        name="mlstm_gates",
    )(gt)


def _mlstm_kernel(rows_ref, cols_ref, qk_ref, vt_ref, og_ref, hn_ref, o_ref, ct_scr, n_scr):
    L = MLSTM_CHUNK
    nh = MLSTM_HEADS
    contract_lanes = (((1,), (1,)), ((), ()))

    @pl.when(pl.program_id(1) == 0)
    def _():
        ct_scr[...] = jnp.zeros_like(ct_scr)
        n_scr[...] = jnp.zeros_like(n_scr)

    row_i = lax.broadcasted_iota(jnp.int32, (L, L), 0)
    col_i = lax.broadcasted_iota(jnp.int32, (L, L), 1)
    causal_t = row_i <= col_i
    lo_lane = col_i < MLSTM_QK_DIM
    zero = jnp.zeros((), BF16)

    n_chunks = qk_ref.shape[0] // L
    pairs = [(ci, p) for ci in range(n_chunks) for p in range(nh // 2)]
    ct = [ct_scr[p] for p in range(nh // 2)]
    nn = [n_scr[p] for p in range(nh // 2)]
    gate_rows = {}

    def chunk_gates(ci):
        if ci not in gate_rows:
            rs = slice(ci * L, (ci + 1) * L)
            wk = rows_ref[ROW_WK:ROW_WK + nh, rs]
            gate_rows[ci] = dict(
                m2=rows_ref[ROW_M2:ROW_M2 + nh, rs], w_inter=rows_ref[ROW_WINTER:ROW_WINTER + nh, rs],
                e_neg=rows_ref[ROW_ENEG:ROW_ENEG + nh, rs], wk=wk,
                decay=rows_ref[ROW_DECAY:ROW_DECAY + nh, rs], b2cols=cols_ref[rs, :],
                wk_b=jnp.concatenate([wk, wk], axis=0).astype(BF16))
        return gate_rows[ci]

    def front(ci, p):
        g = chunk_gates(ci)
        rs = slice(ci * L, (ci + 1) * L)
        h0, h1 = 2 * p, 2 * p + 1
        q2 = qk_ref[rs, p * LANES:(p + 1) * LANES]
        k2 = qk_ref[rs, SZ_MQ + p * LANES:SZ_MQ + (p + 1) * LANES]
        qm2 = jnp.concatenate([jnp.where(lo_lane, q2, zero), jnp.where(lo_lane, zero, q2)], axis=0)
        n2b = jnp.broadcast_to(nn[p], (2 * SUBLANES, LANES)).astype(BF16)
        stacked = jnp.concatenate([k2, ct[p].astype(BF16), n2b], axis=0)
        res = lax.dot_general(stacked, qm2, contract_lanes, preferred_element_type=F32)
        vw = [(vt_ref[hh * MLSTM_V_DIM:(hh + 1) * MLSTM_V_DIM, rs].astype(F32)
               * g["wk"][hh:hh + 1, :]).astype(BF16) for hh in (h0, h1)]
        upd = jnp.dot(jnp.concatenate(vw + [g["wk_b"]], axis=0), k2,
                      preferred_element_type=F32)
        n_inc = upd[2 * MLSTM_V_DIM:]
        dec = jnp.where(lo_lane[0:1, :], g["decay"][h0:h0 + 1, :], g["decay"][h1:h1 + 1, :])
        ct[p] = dec * ct[p] + jnp.where(lo_lane, upd[0:MLSTM_V_DIM], upd[MLSTM_V_DIM:2 * MLSTM_V_DIM])
        nn[p] = dec * nn[p] + jnp.where(lo_lane[0:1, :], n_inc[h0:h0 + 1, :], n_inc[h1:h1 + 1, :])
        return res

    def back(ci, p, par, res):
        st = res[0:L, par * L:(par + 1) * L]
        inter = res[L:L + MLSTM_V_DIM, par * L:(par + 1) * L]
        qn = res[L + MLSTM_V_DIM:L + MLSTM_V_DIM + 1, par * L:(par + 1) * L]
        g = chunk_gates(ci)
        rs = slice(ci * L, (ci + 1) * L)
        hh = 2 * p + par
        vsl = slice(hh * MLSTM_V_DIM, (hh + 1) * MLSTM_V_DIM)
        dt = jnp.where(causal_t,
                       jnp.exp2(g["b2cols"][:, hh:hh + 1] - g["m2"][hh:hh + 1, :]), 0.0)
        sct = st * dt
        wi = g["w_inter"][hh:hh + 1, :]
        den = jnp.sum(sct, axis=0, keepdims=True) + wi * qn
        numt = jnp.dot(vt_ref[vsl, rs], sct.astype(BF16), preferred_element_type=F32) + wi * inter
        rden = 1.0 / jnp.maximum(jnp.abs(den), g["e_neg"][hh:hh + 1, :])
        ms = jnp.mean(numt * numt, axis=0, keepdims=True)
        scale = rden * lax.rsqrt(rden * rden * ms + NORM_EPS)
        hv = (numt * scale).T
        o_ref[rs, vsl] = (hv * hn_ref[:, vsl] * og_ref[rs, vsl].astype(F32)).astype(BF16)

    pending = [front(*u) for u in pairs[:MLSTM_LOOKAHEAD]]
    for idx, (ci, p) in enumerate(pairs):
        res = pending[idx]
        pending[idx] = None
        back(ci, p, 0, res)
        if idx + MLSTM_LOOKAHEAD < len(pairs):
            pending.append(front(*pairs[idx + MLSTM_LOOKAHEAD]))
        back(ci, p, 1, res)
    for p in range(nh // 2):
        ct_scr[p] = ct[p]
        n_scr[p] = nn[p]


def _mlstm(rows, cols, qk, v, og, hn, batch, seq):
    n = qk.shape[0]
    tm = TM_MLSTM
    nt = seq // tm
    row = pl.BlockSpec((tm, SZ_MV), lambda b, c: (b * nt + c, 0))
    return pl.pallas_call(
        _mlstm_kernel,
        grid=(batch, nt),
        in_specs=[
            pl.BlockSpec((GATE_ROWS, tm), lambda b, c: (0, b * nt + c)),
            pl.BlockSpec((tm, LANES), lambda b, c: (b * nt + c, 0)),
            row,
            pl.BlockSpec((SZ_MV, tm), lambda b, c: (0, b * nt + c)),
            row,
            _const_spec((1, SZ_MV)),
        ],
        out_specs=row,
        out_shape=jax.ShapeDtypeStruct((n, SZ_MV), BF16),
        scratch_shapes=[
            pltpu.VMEM((MLSTM_HEADS // 2, 2 * MLSTM_QK_DIM, MLSTM_V_DIM), F32),
            pltpu.VMEM((MLSTM_HEADS // 2, 1, 2 * MLSTM_QK_DIM), F32),
        ],
        compiler_params=pltpu.CompilerParams(
            dimension_semantics=("arbitrary", "arbitrary"), vmem_limit_bytes=VMEM_LIMIT_BYTES),
        name="mlstm",
    )(rows, cols, qk, v, og, hn)


def _merge_stage_weights(wpa_hbm, wpm_hbm, wo_hbm, wpa_ref, wpm_ref, wo_ref, stage, sem):
    hd = ATTN_HEAD_DIM
    wpa_copies = [pltpu.make_async_copy(wpa_hbm.at[pl.ds(head * hd, hd), :],
                                        stage.at[0, pl.ds(pos * hd, hd), :], sem.at[0])
                  for pos, head in enumerate(ATTN_HEAD_ORDER)]
    wpm_copy = pltpu.make_async_copy(wpm_hbm, stage.at[1], sem.at[1])
    wo_copy = pltpu.make_async_copy(wo_hbm, stage.at[0], sem.at[0])
    for cp in wpa_copies:
        cp.start()
    wpm_copy.start()
    for cp in wpa_copies:
        cp.wait()
    wpa_ref[...] = stage[0].astype(BF16)
    wo_copy.start()
    wpm_copy.wait()
    wpm_ref[...] = stage[1].astype(BF16)
    wo_copy.wait()
    wo_ref[...] = stage[0].astype(BF16)


def _merge_kernel(x_ref, a_ref, hm_ref, gate_ref, wpa_hbm, wpm_hbm, wo_hbm, o_ref,
                  wpa_ref, wpm_ref, wo_ref, stage, sem):
    @pl.when(pl.program_id(0) == 0)
    def _():
        _merge_stage_weights(wpa_hbm, wpm_hbm, wo_hbm, wpa_ref, wpm_ref, wo_ref, stage, sem)

    ya = jnp.dot(a_ref[...], wpa_ref[...], preferred_element_type=F32)
    ym = jnp.dot(hm_ref[...], wpm_ref[...], preferred_element_type=F32)
    ga = jax.nn.sigmoid(gate_ref[:, 0:D_MODEL].astype(F32))
    gm = jax.nn.sigmoid(gate_ref[:, D_MODEL:2 * D_MODEL].astype(F32))
    merged = (ga * ya + gm * ym).astype(BF16)
    o_ref[...] = x_ref[...] + jnp.dot(merged, wo_ref[...], preferred_element_type=F32)


def _merge(x, a, hm, gates, wpa, wpm, wo):
    n = x.shape[0]
    tm = TM_MERGE
    row = lambda width: pl.BlockSpec((tm, width), lambda i: (i, 0))
    return pl.pallas_call(
        _merge_kernel,
        grid=(n // tm,),
        in_specs=[row(D_MODEL), row(SZ_AQ), row(SZ_MV), row(2 * D_MODEL),
                  pl.BlockSpec(memory_space=pl.ANY), pl.BlockSpec(memory_space=pl.ANY),
                  pl.BlockSpec(memory_space=pl.ANY)],
        out_specs=row(D_MODEL),
        out_shape=jax.ShapeDtypeStruct((n, D_MODEL), F32),
        scratch_shapes=[pltpu.VMEM((D_MODEL, D_MODEL), BF16)] * 3 + [
            pltpu.VMEM((2, D_MODEL, D_MODEL), F32), pltpu.SemaphoreType.DMA((2,))],
        compiler_params=pltpu.CompilerParams(
            dimension_semantics=("arbitrary",), vmem_limit_bytes=VMEM_LIMIT_BYTES),
        name="merge",
    )(x, a, hm, gates, wpa, wpm, wo)


def kernel(x, ffn1_norm, ffn1_w_gate, ffn1_w_up, ffn1_w_down, mix_norm, w_in, b_in, attn_sinks,
           mlstm_conv, mlstm_head_norm, w_proj_attn, w_proj_mlstm, w_out, ffn2_norm, ffn2_w_gate,
           ffn2_w_up, ffn2_w_down, final_norm):
    batch, seq, d = x.shape
    assert d == D_MODEL and ffn1_norm.shape[0] == 1, "one layer of width D_MODEL"
    assert all(seq % t == 0 for t in (TM_INPROJ, TM_MLSTM, ATTN_BLOCKS * WINDOW))
    assert all((batch * seq) % t == 0 for t in (TM_FFN, TM_MERGE))
    n = batch * seq
    xf = x.reshape(n, d)
    fin = final_norm.reshape(1, d)

    x1 = _ffn(xf, ffn1_norm[0].reshape(1, d), ffn1_w_gate[0], ffn1_w_up[0], ffn1_w_down[0], fin,
              apply_final=False)

    q, kk, avt, mqk, mvt, mo, gates, gt = _inproj(x1, mix_norm[0].reshape(1, d), w_in[0], b_in[0],
                                                  mlstm_conv[0], seq)

    ya = _attention(attn_sinks[0], q, kk, avt, batch, seq)

    rows, cols = _mlstm_gates(gt, seq)
    hm = _mlstm(rows, cols, mqk, mvt, mo, mlstm_head_norm[0].reshape(1, SZ_MV), batch, seq)

    x2 = _merge(x1, ya, hm, gates, w_proj_attn[0], w_proj_mlstm[0], w_out[0])

    out = _ffn(x2, ffn2_norm[0].reshape(1, d), ffn2_w_gate[0], ffn2_w_up[0], ffn2_w_down[0], fin,
               apply_final=True)
    return out.reshape(batch, seq, d)
```

```python
import functools

import jax
import jax.numpy as jnp
from jax import lax
from jax.experimental import pallas as pl
from jax.experimental.pallas import tpu as pltpu

F32 = jnp.float32
BF16 = jnp.bfloat16

D_MODEL = 1024
ATTN_HEAD_DIM = 64
ATTN_HEADS = 16
ATTN_KV_HEADS = 4
ATTN_GROUP = 4
WINDOW = 128
MLSTM_HEADS = 8
MLSTM_V_DIM = 128
MLSTM_QK_DIM = 64
CONV_WIDTH = 4
GATE_SOFTCAP = 15.0
D_FF = 2816
FFN_RESIDUAL_WEIGHT = 0.5
NORM_EPS = 1e-6

SZ_AQ, SZ_AK, SZ_AV = 1024, 256, 256
SZ_MQ, SZ_MK, SZ_MV, SZ_MO = 512, 512, 1024, 1024
SZ_MI, SZ_MF = 8, 8

LANES = 128
SUBLANES = 8
MXU_DIM = 256
VMEM_LIMIT_BYTES = 56 * 1024 * 1024

MLSTM_CHUNK = 128
FF_CHUNK = MXU_DIM
FFN_STAGE_SLOTS = 3
TM_FFN = 1024
TM_INPROJ = 1024
TM_MERGE = 1024
TM_MLSTM = 8 * MLSTM_CHUNK
ATTN_LOOKAHEAD = 4
ATTN_BLOCKS = 8
MLSTM_LOOKAHEAD = 2

LOG2E = 1.4426950408889634
K_SCALE_LOG2 = 3.0
ROW_M2, ROW_WINTER, ROW_ENEG, ROW_WK, ROW_DECAY = 0, 8, 16, 24, 32
GATE_ROWS = 40

OFF_AQ = 0
OFF_AK = OFF_AQ + SZ_AQ
OFF_AV = OFF_AK + SZ_AK
OFF_MQ = OFF_AV + SZ_AV
OFF_MK = OFF_MQ + SZ_MQ
OFF_MV = OFF_MK + SZ_MK
OFF_MO = OFF_MV + SZ_MV
OFF_MI = OFF_MO + SZ_MO
OFF_MF = OFF_MI + SZ_MI
OFF_G = OFF_MF + SZ_MF
IN_WIDTH = OFF_G + 2 * D_MODEL
FEATURE_MAJOR_GROUPS = ((OFF_AQ, SZ_AQ), (OFF_AV, SZ_AV), (OFF_MV, SZ_MV), (OFF_MI, SZ_MI + SZ_MF))
FEATURE_MAJOR_ROWS = sum(size for _, size in FEATURE_MAJOR_GROUPS)
_FEATURE_MAJOR_BIAS_ROWS = tuple(
    (off, size, sum(s for _, s in FEATURE_MAJOR_GROUPS[:k]))
    for k, (off, size) in enumerate(FEATURE_MAJOR_GROUPS))
STAGE_ROWS = 256
STAGE_SLOTS = 4

ATTN_HEAD_ORDER = tuple((2 * (t // 4) + par) * ATTN_GROUP + t % 4
                        for t in range(ATTN_HEADS // 2) for par in range(2))


def _rms(x, g):
    return x * lax.rsqrt(jnp.mean(x * x, axis=-1, keepdims=True) + NORM_EPS) * g


def _const_spec(shape):
    zeros = (0,) * len(shape)
    return pl.BlockSpec(shape, lambda *_: zeros, pipeline_mode=pl.Buffered(1))


def _ffn_stage_weights(wg_hbm, wu_hbm, wd_hbm, wg_ref, wu_ref, wd_ref, gu_stage, d_stage, sem):
    n_chunks = D_FF // FF_CHUNK
    ahead = FFN_STAGE_SLOTS - 1

    def copies(c):
        slot = c % FFN_STAGE_SLOTS
        cols = pl.ds(c * FF_CHUNK, FF_CHUNK)
        return (pltpu.make_async_copy(wg_hbm.at[:, cols], gu_stage.at[slot, 0], sem.at[slot, 0]),
                pltpu.make_async_copy(wu_hbm.at[:, cols], gu_stage.at[slot, 1], sem.at[slot, 1]),
                pltpu.make_async_copy(wd_hbm.at[cols, :], d_stage.at[slot], sem.at[slot, 2]))

    for c in range(min(ahead, n_chunks)):
        for cp in copies(c):
            cp.start()
    for c in range(n_chunks):
        slot = c % FFN_STAGE_SLOTS
        if c + ahead < n_chunks:
            for cp in copies(c + ahead):
                cp.start()
        for cp in copies(c):
            cp.wait()
        sl = slice(c * FF_CHUNK, (c + 1) * FF_CHUNK)
        wg_ref[:, sl] = gu_stage[slot, 0].astype(BF16)
        wu_ref[:, sl] = gu_stage[slot, 1].astype(BF16)
        wd_ref[sl, :] = d_stage[slot].astype(BF16)


def _ffn_kernel(x_ref, g_ref, wg_hbm, wu_hbm, wd_hbm, fin_ref, o_ref,
                wg_ref, wu_ref, wd_ref, gu_stage, d_stage, sem, acc_ref, *, apply_final):
    @pl.when(pl.program_id(0) == 0)
    def _():
        _ffn_stage_weights(wg_hbm, wu_hbm, wd_hbm, wg_ref, wu_ref, wd_ref, gu_stage, d_stage, sem)

    x = x_ref[...]
    h = _rms(x, g_ref[...]).astype(BF16)
    for c in range(D_FF // FF_CHUNK):
        sl = slice(c * FF_CHUNK, (c + 1) * FF_CHUNK)
        a = jnp.dot(h, wg_ref[:, sl], preferred_element_type=F32)
        u = jnp.dot(h, wu_ref[:, sl], preferred_element_type=F32)
        act = (a * jax.nn.sigmoid(a) * u).astype(BF16)
        d = jnp.dot(act, wd_ref[sl, :], preferred_element_type=F32)
        if c == 0:
            acc_ref[...] = d
        else:
            acc_ref[...] += d
    y = x + FFN_RESIDUAL_WEIGHT * acc_ref[...]
    if apply_final:
        y = _rms(y, fin_ref[...])
    o_ref[...] = y


def _ffn(x, g, wg, wu, wd, fin, apply_final):
    n = x.shape[0]
    tm = TM_FFN
    return pl.pallas_call(
        functools.partial(_ffn_kernel, apply_final=apply_final),
        grid=(n // tm,),
        in_specs=[
            pl.BlockSpec((tm, D_MODEL), lambda i: (i, 0)),
            _const_spec((1, D_MODEL)),
            pl.BlockSpec(memory_space=pl.ANY),
            pl.BlockSpec(memory_space=pl.ANY),
            pl.BlockSpec(memory_space=pl.ANY),
            _const_spec((1, D_MODEL)),
        ],
        out_specs=pl.BlockSpec((tm, D_MODEL), lambda i: (i, 0)),
        out_shape=jax.ShapeDtypeStruct((n, D_MODEL), F32),
        scratch_shapes=[
            pltpu.VMEM((D_MODEL, D_FF), BF16),
            pltpu.VMEM((D_MODEL, D_FF), BF16),
            pltpu.VMEM((D_FF, D_MODEL), BF16),
            pltpu.VMEM((FFN_STAGE_SLOTS, 2, D_MODEL, FF_CHUNK), F32),
            pltpu.VMEM((FFN_STAGE_SLOTS, FF_CHUNK, D_MODEL), F32),
            pltpu.SemaphoreType.DMA((FFN_STAGE_SLOTS, 3)),
            pltpu.VMEM((tm, D_MODEL), F32),
        ],
        compiler_params=pltpu.CompilerParams(
            dimension_semantics=("arbitrary",), vmem_limit_bytes=VMEM_LIMIT_BYTES),
        name="ffn_final" if apply_final else "ffn",
    )(x, g, wg, wu, wd, fin)


def _inproj_stage_weights(w_hbm, w_ref, stage, sem):
    bounds = [(r0, min(STAGE_ROWS, IN_WIDTH - r0)) for r0 in range(0, IN_WIDTH, STAGE_ROWS)]

    def copy(c):
        r0, rows = bounds[c]
        slot = c % STAGE_SLOTS
        return pltpu.make_async_copy(w_hbm.at[pl.ds(r0, rows), :],
                                     stage.at[slot, pl.ds(0, rows), :], sem.at[slot])

    for c in range(min(STAGE_SLOTS - 1, len(bounds))):
        copy(c).start()
    for c, (r0, rows) in enumerate(bounds):
        if c + STAGE_SLOTS - 1 < len(bounds):
            copy(c + STAGE_SLOTS - 1).start()
        copy(c).wait()
        w_ref[r0:r0 + rows, :] = stage[c % STAGE_SLOTS, 0:rows, :].astype(BF16)


def _inproj_kernel(x_ref, g_ref, w_hbm, brow_ref, bgate_ref, bcol_ref, conv_ref,
                   q_ref, kk_ref, avt_ref, mqk_ref, mvt_ref, mo_ref, gate_ref, gt_ref,
                   w_ref, stage, sem, conv_scr, *, tiles_per_seq):
    i = pl.program_id(0)
    tm = x_ref.shape[0]

    @pl.when(i == 0)
    def _():
        _inproj_stage_weights(w_hbm, w_ref, stage, sem)

    h = _rms(x_ref[...], g_ref[...]).astype(BF16)
    contract_lanes = (((1,), (1,)), ((), ()))

    def seg(r0, width, bias):
        return lax.dot_general(h, w_ref[r0:r0 + width, :], contract_lanes,
                               preferred_element_type=F32) + bias

    def seg_t(r0, rows):
        c0 = next(base + r0 - off for off, size, base in _FEATURE_MAJOR_BIAS_ROWS
                  if off <= r0 < off + size)
        return (lax.dot_general(w_ref[r0:r0 + rows, :], h, contract_lanes,
                                preferred_element_type=F32) + bcol_ref[c0:c0 + rows, :])

    @pl.when(i == 0)
    def _():
        conv_scr[...] = jnp.zeros_like(conv_scr)

    carry = jnp.where(i % tiles_per_seq == 0, 0.0, conv_scr[...])

    def conv_slab(c, width):
        z = seg(OFF_MQ + c, width, brow_ref[:, OFF_MQ + c:OFF_MQ + c + width])
        conv_scr[:, c:c + width] = z[tm - SUBLANES:tm, :]
        zc = jnp.concatenate([carry[:, c:c + width], z], axis=0)
        acc = z * conv_ref[CONV_WIDTH - 1:CONV_WIDTH, c:c + width]
        for k in range(1, CONV_WIDTH):
            zk = pltpu.roll(zc, k, 0)[SUBLANES:, :]
            acc = acc + zk * conv_ref[CONV_WIDTH - 1 - k:CONV_WIDTH - k, c:c + width]
        mqk_ref[:, c:c + width] = (acc * jax.nn.sigmoid(acc)).astype(BF16)

    dw = MXU_DIM
    for k in range(SZ_AQ // dw):
        sl = slice(k * dw, (k + 1) * dw)
        conv_slab(k * dw, dw)
        q_ref[sl, :] = (seg_t(OFF_AQ + k * dw, dw) * (LOG2E * ATTN_HEAD_DIM ** -0.5)).astype(BF16)
        if k == 0:
            kk_ref[...] = seg(OFF_AK, SZ_AK, brow_ref[:, OFF_AK:OFF_AK + SZ_AK]).astype(BF16)
        elif k == 1:
            avt_ref[...] = seg_t(OFF_AV, SZ_AV).astype(BF16)
        mvt_ref[sl, :] = seg_t(OFF_MV + k * dw, dw).astype(BF16)
        mo = seg(OFF_MO + k * dw, dw, brow_ref[:, OFF_MO + k * dw:OFF_MO + (k + 1) * dw])
        mo_ref[:, sl] = jax.nn.sigmoid(mo).astype(BF16)
        gsl = slice(2 * k * dw, (2 * k + 2) * dw)
        gate_ref[:, gsl] = seg(OFF_G + 2 * k * dw, 2 * dw, bgate_ref[:, gsl]).astype(BF16)

    gt_ref[...] = seg_t(OFF_MI, SZ_MI + SZ_MF)


def _inproj(x, g, w_in, b_in, conv, seq):
    w = w_in.T
    b_row = b_in.reshape(1, IN_WIDTH)
    b_gate = b_in[OFF_G:].reshape(1, 2 * D_MODEL)
    b_col = jnp.concatenate([b_in[off:off + size] for off, size in FEATURE_MAJOR_GROUPS]).reshape(
        FEATURE_MAJOR_ROWS, 1)
    n = x.shape[0]
    tm = TM_INPROJ
    row = lambda width: pl.BlockSpec((tm, width), lambda i: (i, 0))
    col = lambda height: pl.BlockSpec((height, tm), lambda i: (0, i))
    bf = lambda width: jax.ShapeDtypeStruct((n, width), BF16)
    return pl.pallas_call(
        functools.partial(_inproj_kernel, tiles_per_seq=seq // tm),
        grid=(n // tm,),
        in_specs=[
            row(D_MODEL),
            _const_spec((1, D_MODEL)),
            pl.BlockSpec(memory_space=pl.ANY),
            _const_spec((1, IN_WIDTH)),
            _const_spec((1, 2 * D_MODEL)),
            _const_spec((FEATURE_MAJOR_ROWS, 1)),
            _const_spec((CONV_WIDTH, SZ_MQ + SZ_MK)),
        ],
        out_specs=[col(SZ_AQ), row(SZ_AK), col(SZ_AV), row(SZ_MQ + SZ_MK), col(SZ_MV), row(SZ_MO),
                   row(2 * D_MODEL), col(2 * MLSTM_HEADS)],
        out_shape=[jax.ShapeDtypeStruct((SZ_AQ, n), BF16), bf(SZ_AK),
                   jax.ShapeDtypeStruct((SZ_AV, n), BF16),
                   bf(SZ_MQ + SZ_MK), jax.ShapeDtypeStruct((SZ_MV, n), BF16), bf(SZ_MO),
                   bf(2 * D_MODEL), jax.ShapeDtypeStruct((2 * MLSTM_HEADS, n), F32)],
        scratch_shapes=[pltpu.VMEM((IN_WIDTH, D_MODEL), BF16),
                        pltpu.VMEM((STAGE_SLOTS, STAGE_ROWS, D_MODEL), F32),
                        pltpu.SemaphoreType.DMA((STAGE_SLOTS,)),
                        pltpu.VMEM((SUBLANES, SZ_MQ + SZ_MK), F32)],
        compiler_params=pltpu.CompilerParams(
            dimension_semantics=("arbitrary",), vmem_limit_bytes=VMEM_LIMIT_BYTES),
        name="inproj",
    )(x, g, w, b_row, b_gate, b_col, conv)


def _attn_kernel(sink_ref, q_ref, kc_ref, kp_ref, vtc_ref, vtp_ref, o_ref, k_scr, vt_scr):
    w = WINDOW
    k_scr[0:w, :] = kp_ref[...]
    k_scr[w:, :] = kc_ref[...]
    vt_scr[:, 0:w] = vtp_ref[...]
    vt_scr[:, w:] = vtc_ref[...]
    kj = lax.broadcasted_iota(jnp.int32, (2 * w, w), 0)
    qi = lax.broadcasted_iota(jnp.int32, (2 * w, w), 1)
    band = (kj > qi) & (kj <= qi + w)
    band_first = band & ((kj >= w) | (pl.program_id(1) > 0))
    pairs = [(jb, t) for jb in range(q_ref.shape[1] // w) for t in range(ATTN_HEADS // 2)]
    zeros_qt = jnp.zeros((ATTN_HEAD_DIM, w), BF16)

    def scores(jb, t):
        ha, hb = ATTN_HEAD_ORDER[2 * t], ATTN_HEAD_ORDER[2 * t + 1]
        qa = q_ref[ha * ATTN_HEAD_DIM:(ha + 1) * ATTN_HEAD_DIM, jb * w:(jb + 1) * w]
        qb = q_ref[hb * ATTN_HEAD_DIM:(hb + 1) * ATTN_HEAD_DIM, jb * w:(jb + 1) * w]
        qt2 = jnp.concatenate([jnp.concatenate([qa, zeros_qt], axis=0),
                               jnp.concatenate([zeros_qt, qb], axis=0)], axis=1)
        kk = k_scr[jb * w:(jb + 2) * w, (t // 4) * LANES:(t // 4 + 1) * LANES]
        return jnp.dot(kk, qt2, preferred_element_type=F32)

    def head_out(jb, t, par, s):
        s = jnp.where(band_first if jb == 0 else band, s, -jnp.inf)
        sink = sink_ref[ATTN_HEAD_ORDER[2 * t + par]] * LOG2E
        mx = jnp.maximum(jnp.max(s, axis=0, keepdims=True), sink)
        pr = jnp.exp2(s - mx)
        denom = jnp.sum(pr, axis=0, keepdims=True) + jnp.exp2(sink - mx)
        kvh = 2 * (t // 4) + par
        vt = vt_scr[kvh * ATTN_HEAD_DIM:(kvh + 1) * ATTN_HEAD_DIM, jb * w:(jb + 2) * w]
        return jnp.dot(vt, pr.astype(BF16), preferred_element_type=F32) * (1.0 / denom)

    pending = [scores(*u) for u in pairs[:ATTN_LOOKAHEAD]]
    for idx, (jb, t) in enumerate(pairs):
        s2 = pending[idx]
        pending[idx] = None
        outs = [head_out(jb, t, par, s2[:, par * w:(par + 1) * w]) for par in range(2)]
        if idx + ATTN_LOOKAHEAD < len(pairs):
            pending.append(scores(*pairs[idx + ATTN_LOOKAHEAD]))
        pair = jnp.concatenate(outs, axis=0)
        o_ref[jb * w:(jb + 1) * w, t * LANES:(t + 1) * LANES] = pair.T.astype(BF16)


def _attention(sinks, q, kk, vt, batch, seq):
    n = kk.shape[0]
    tq = ATTN_BLOCKS * WINDOW
    nt = seq // tq
    cur = lambda b, j: b * nt + j
    prev = lambda b, j: (b * nt + j) * ATTN_BLOCKS - jnp.minimum(j, 1)
    return pl.pallas_call(
        _attn_kernel,
        grid=(batch, nt),
        in_specs=[
            pl.BlockSpec(memory_space=pltpu.SMEM),
            pl.BlockSpec((SZ_AQ, tq), lambda b, j: (0, cur(b, j))),
            pl.BlockSpec((tq, SZ_AK), lambda b, j: (cur(b, j), 0)),
            pl.BlockSpec((WINDOW, SZ_AK), lambda b, j: (prev(b, j), 0)),
            pl.BlockSpec((SZ_AV, tq), lambda b, j: (0, cur(b, j))),
            pl.BlockSpec((SZ_AV, WINDOW), lambda b, j: (0, prev(b, j))),
        ],
        out_specs=pl.BlockSpec((tq, SZ_AQ), lambda b, j: (cur(b, j), 0)),
        out_shape=jax.ShapeDtypeStruct((n, SZ_AQ), BF16),
        scratch_shapes=[pltpu.VMEM((WINDOW + tq, SZ_AK), BF16),
                        pltpu.VMEM((SZ_AV, WINDOW + tq), BF16)],
        compiler_params=pltpu.CompilerParams(
            dimension_semantics=("arbitrary", "arbitrary"), vmem_limit_bytes=VMEM_LIMIT_BYTES),
        name="attention",
    )(sinks, q, kk, kk, vt, vt)


def _log_sigmoid(x):
    return -(jnp.maximum(-x, 0.0) + jnp.log(1.0 + jnp.exp(-jnp.abs(x))))


def _mlstm_gate_kernel(g_ref, rows_ref, cols_ref, *, seq):
    L = MLSTM_CHUNK
    nh = MLSTM_HEADS
    n = g_ref.shape[1]
    g = g_ref[...]
    ig = GATE_SOFTCAP * jnp.tanh(g[0:nh] / GATE_SOFTCAP)
    fg = GATE_SOFTCAP * jnp.tanh(g[nh:2 * nh] / GATE_SOFTCAP)
    lf = _log_sigmoid(fg)
    pos = lax.broadcasted_iota(jnp.int32, (nh, n), 1) & (L - 1)
    a = lf
    sh = 1
    while sh < L:
        a = a + jnp.where(pos >= sh, pltpu.roll(a, sh, 1), 0.0)
        sh *= 2
    bvec = ig - a
    pm = bvec
    sh = 1
    while sh < L:
        pm = jnp.maximum(pm, jnp.where(pos >= sh, pltpu.roll(pm, sh, 1), -jnp.inf))
        sh *= 2
    pad = jnp.zeros((L - nh, L), F32)
    m = [jnp.zeros((nh, L), F32) for _ in range(n // seq)]
    for c in range(seq // L):
        for s in range(n // seq):
            sl = slice(s * seq + c * L, s * seq + (c + 1) * L)
            a_c, b_c = a[:, sl], bvec[:, sl]
            mrow = jnp.maximum(m[s], pm[:, sl])
            total = jnp.broadcast_to(a_c[:, L - 1:L], (nh, L))
            mlast = jnp.broadcast_to(mrow[:, L - 1:L], (nh, L))
            rows_ref[ROW_M2:ROW_M2 + nh, sl] = mrow * LOG2E
            rows_ref[ROW_WINTER:ROW_WINTER + nh, sl] = jnp.exp(m[s] - mrow)
            rows_ref[ROW_ENEG:ROW_ENEG + nh, sl] = jnp.exp(-(a_c + mrow))
            rows_ref[ROW_WK:ROW_WK + nh, sl] = jnp.exp(b_c - mlast) * MLSTM_QK_DIM ** -0.5
            rows_ref[ROW_DECAY:ROW_DECAY + nh, sl] = jnp.exp(m[s] - mlast)
            m[s] = total + mlast
            cols_ref[sl, :] = jnp.concatenate([b_c * LOG2E - K_SCALE_LOG2, pad], axis=0).T


def _mlstm_gates(gt, seq):
    n = gt.shape[1]
    return pl.pallas_call(
        functools.partial(_mlstm_gate_kernel, seq=seq),
        grid=(1,),
        in_specs=[pl.BlockSpec((2 * MLSTM_HEADS, n), lambda i: (0, 0))],
        out_specs=[pl.BlockSpec((GATE_ROWS, n), lambda i: (0, 0)),
                   pl.BlockSpec((n, LANES), lambda i: (0, 0))],
        out_shape=[jax.ShapeDtypeStruct((GATE_ROWS, n), F32),
                   jax.ShapeDtypeStruct((n, LANES), F32)],
        compiler_params=pltpu.CompilerParams(
            dimension_semantics=("arbitrary",), vmem_limit_bytes=VMEM_LIMIT_BYTES),
        name="mlstm_gates",
    )(gt)


def _mlstm_kernel(rows_ref, cols_ref, qk_ref, vt_ref, og_ref, hn_ref, o_ref, ct_scr, n_scr):
    L = MLSTM_CHUNK
    nh = MLSTM_HEADS
    contract_lanes = (((1,), (1,)), ((), ()))

    @pl.when(pl.program_id(1) == 0)
    def _():
        ct_scr[...] = jnp.zeros_like(ct_scr)
        n_scr[...] = jnp.zeros_like(n_scr)

    row_i = lax.broadcasted_iota(jnp.int32, (L, L), 0)
    col_i = lax.broadcasted_iota(jnp.int32, (L, L), 1)
    causal_t = row_i <= col_i
    lo_lane = col_i < MLSTM_QK_DIM
    zero = jnp.zeros((), BF16)

    n_chunks = qk_ref.shape[0] // L
    pairs = [(ci, p) for ci in range(n_chunks) for p in range(nh // 2)]
    ct = [ct_scr[p] for p in range(nh // 2)]
    nn = [n_scr[p] for p in range(nh // 2)]
    gate_rows = {}

    def chunk_gates(ci):
        if ci not in gate_rows:
            rs = slice(ci * L, (ci + 1) * L)
            wk = rows_ref[ROW_WK:ROW_WK + nh, rs]
            gate_rows[ci] = dict(
                m2=rows_ref[ROW_M2:ROW_M2 + nh, rs], w_inter=rows_ref[ROW_WINTER:ROW_WINTER + nh, rs],
                e_neg=rows_ref[ROW_ENEG:ROW_ENEG + nh, rs], wk=wk,
                decay=rows_ref[ROW_DECAY:ROW_DECAY + nh, rs], b2cols=cols_ref[rs, :],
                wk_b=jnp.concatenate([wk, wk], axis=0).astype(BF16))
        return gate_rows[ci]

    def front(ci, p):
        g = chunk_gates(ci)
        rs = slice(ci * L, (ci + 1) * L)
        h0, h1 = 2 * p, 2 * p + 1
        q2 = qk_ref[rs, p * LANES:(p + 1) * LANES]
        k2 = qk_ref[rs, SZ_MQ + p * LANES:SZ_MQ + (p + 1) * LANES]
        qm2 = jnp.concatenate([jnp.where(lo_lane, q2, zero), jnp.where(lo_lane, zero, q2)], axis=0)
        n2b = jnp.broadcast_to(nn[p], (2 * SUBLANES, LANES)).astype(BF16)
        stacked = jnp.concatenate([k2, ct[p].astype(BF16), n2b], axis=0)
        res = lax.dot_general(stacked, qm2, contract_lanes, preferred_element_type=F32)
        vw = [(vt_ref[hh * MLSTM_V_DIM:(hh + 1) * MLSTM_V_DIM, rs].astype(F32)
               * g["wk"][hh:hh + 1, :]).astype(BF16) for hh in (h0, h1)]
        upd = jnp.dot(jnp.concatenate(vw + [g["wk_b"]], axis=0), k2,
                      preferred_element_type=F32)
        n_inc = upd[2 * MLSTM_V_DIM:]
        dec = jnp.where(lo_lane[0:1, :], g["decay"][h0:h0 + 1, :], g["decay"][h1:h1 + 1, :])
        ct[p] = dec * ct[p] + jnp.where(lo_lane, upd[0:MLSTM_V_DIM], upd[MLSTM_V_DIM:2 * MLSTM_V_DIM])
        nn[p] = dec * nn[p] + jnp.where(lo_lane[0:1, :], n_inc[h0:h0 + 1, :], n_inc[h1:h1 + 1, :])
        return res

    def back(ci, p, par, res):
        st = res[0:L, par * L:(par + 1) * L]
        inter = res[L:L + MLSTM_V_DIM, par * L:(par + 1) * L]
        qn = res[L + MLSTM_V_DIM:L + MLSTM_V_DIM + 1, par * L:(par + 1) * L]
        g = chunk_gates(ci)
        rs = slice(ci * L, (ci + 1) * L)
        hh = 2 * p + par
        vsl = slice(hh * MLSTM_V_DIM, (hh + 1) * MLSTM_V_DIM)
        dt = jnp.where(causal_t,
                       jnp.exp2(g["b2cols"][:, hh:hh + 1] - g["m2"][hh:hh + 1, :]), 0.0)
        sct = st * dt
        wi = g["w_inter"][hh:hh + 1, :]
        den = jnp.sum(sct, axis=0, keepdims=True) + wi * qn
        numt = jnp.dot(vt_ref[vsl, rs], sct.astype(BF16), preferred_element_type=F32) + wi * inter
        rden = 1.0 / jnp.maximum(jnp.abs(den), g["e_neg"][hh:hh + 1, :])
        ms = jnp.mean(numt * numt, axis=0, keepdims=True)
        scale = rden * lax.rsqrt(rden * rden * ms + NORM_EPS)
        hv = (numt * scale).T
        o_ref[rs, vsl] = (hv * hn_ref[:, vsl] * og_ref[rs, vsl].astype(F32)).astype(BF16)

    pending = [front(*u) for u in pairs[:MLSTM_LOOKAHEAD]]
    for idx, (ci, p) in enumerate(pairs):
        res = pending[idx]
        pending[idx] = None
        back(ci, p, 0, res)
        if idx + MLSTM_LOOKAHEAD < len(pairs):
            pending.append(front(*pairs[idx + MLSTM_LOOKAHEAD]))
        back(ci, p, 1, res)
    for p in range(nh // 2):
        ct_scr[p] = ct[p]
        n_scr[p] = nn[p]


def _mlstm(rows, cols, qk, v, og, hn, batch, seq):
    n = qk.shape[0]
    tm = TM_MLSTM
    nt = seq // tm
    row = pl.BlockSpec((tm, SZ_MV), lambda b, c: (b * nt + c, 0))
    return pl.pallas_call(
        _mlstm_kernel,
        grid=(batch, nt),
        in_specs=[
            pl.BlockSpec((GATE_ROWS, tm), lambda b, c: (0, b * nt + c)),
            pl.BlockSpec((tm, LANES), lambda b, c: (b * nt + c, 0)),
            row,
            pl.BlockSpec((SZ_MV, tm), lambda b, c: (0, b * nt + c)),
            row,
            _const_spec((1, SZ_MV)),
        ],
        out_specs=row,
        out_shape=jax.ShapeDtypeStruct((n, SZ_MV), BF16),
        scratch_shapes=[
            pltpu.VMEM((MLSTM_HEADS // 2, 2 * MLSTM_QK_DIM, MLSTM_V_DIM), F32),
            pltpu.VMEM((MLSTM_HEADS // 2, 1, 2 * MLSTM_QK_DIM), F32),
        ],
        compiler_params=pltpu.CompilerParams(
            dimension_semantics=("arbitrary", "arbitrary"), vmem_limit_bytes=VMEM_LIMIT_BYTES),
        name="mlstm",
    )(rows, cols, qk, v, og, hn)


def _merge_stage_weights(wpa_hbm, wpm_hbm, wo_hbm, wpa_ref, wpm_ref, wo_ref, stage, sem):
    hd = ATTN_HEAD_DIM
    wpa_copies = [pltpu.make_async_copy(wpa_hbm.at[pl.ds(head * hd, hd), :],
                                        stage.at[0, pl.ds(pos * hd, hd), :], sem.at[0])
                  for pos, head in enumerate(ATTN_HEAD_ORDER)]
    wpm_copy = pltpu.make_async_copy(wpm_hbm, stage.at[1], sem.at[1])
    wo_copy = pltpu.make_async_copy(wo_hbm, stage.at[0], sem.at[0])
    for cp in wpa_copies:
        cp.start()
    wpm_copy.start()
    for cp in wpa_copies:
        cp.wait()
    wpa_ref[...] = stage[0].astype(BF16)
    wo_copy.start()
    wpm_copy.wait()
    wpm_ref[...] = stage[1].astype(BF16)
    wo_copy.wait()
    wo_ref[...] = stage[0].astype(BF16)


def _merge_kernel(x_ref, a_ref, hm_ref, gate_ref, wpa_hbm, wpm_hbm, wo_hbm, o_ref,
                  wpa_ref, wpm_ref, wo_ref, stage, sem):
    @pl.when(pl.program_id(0) == 0)
    def _():
        _merge_stage_weights(wpa_hbm, wpm_hbm, wo_hbm, wpa_ref, wpm_ref, wo_ref, stage, sem)

    ya = jnp.dot(a_ref[...], wpa_ref[...], preferred_element_type=F32)
    ym = jnp.dot(hm_ref[...], wpm_ref[...], preferred_element_type=F32)
    ga = jax.nn.sigmoid(gate_ref[:, 0:D_MODEL].astype(F32))
    gm = jax.nn.sigmoid(gate_ref[:, D_MODEL:2 * D_MODEL].astype(F32))
    merged = (ga * ya + gm * ym).astype(BF16)
    o_ref[...] = x_ref[...] + jnp.dot(merged, wo_ref[...], preferred_element_type=F32)


def _merge(x, a, hm, gates, wpa, wpm, wo):
    n = x.shape[0]
    tm = TM_MERGE
    row = lambda width: pl.BlockSpec((tm, width), lambda i: (i, 0))
    return pl.pallas_call(
        _merge_kernel,
        grid=(n // tm,),
        in_specs=[row(D_MODEL), row(SZ_AQ), row(SZ_MV), row(2 * D_MODEL),
                  pl.BlockSpec(memory_space=pl.ANY), pl.BlockSpec(memory_space=pl.ANY),
                  pl.BlockSpec(memory_space=pl.ANY)],
        out_specs=row(D_MODEL),
        out_shape=jax.ShapeDtypeStruct((n, D_MODEL), F32),
        scratch_shapes=[pltpu.VMEM((D_MODEL, D_MODEL), BF16)] * 3 + [
            pltpu.VMEM((2, D_MODEL, D_MODEL), F32), pltpu.SemaphoreType.DMA((2,))],
        compiler_params=pltpu.CompilerParams(
            dimension_semantics=("arbitrary",), vmem_limit_bytes=VMEM_LIMIT_BYTES),
        name="merge",
    )(x, a, hm, gates, wpa, wpm, wo)


def kernel(x, ffn1_norm, ffn1_w_gate, ffn1_w_up, ffn1_w_down, mix_norm, w_in, b_in, attn_sinks,
           mlstm_conv, mlstm_head_norm, w_proj_attn, w_proj_mlstm, w_out, ffn2_norm, ffn2_w_gate,
           ffn2_w_up, ffn2_w_down, final_norm):
    batch, seq, d = x.shape
    assert d == D_MODEL and ffn1_norm.shape[0] == 1, "one layer of width D_MODEL"
    assert all(seq % t == 0 for t in (TM_INPROJ, TM_MLSTM, ATTN_BLOCKS * WINDOW))
    assert all((batch * seq) % t == 0 for t in (TM_FFN, TM_MERGE))
    n = batch * seq
    xf = x.reshape(n, d)
    fin = final_norm.reshape(1, d)

    x1 = _ffn(xf, ffn1_norm[0].reshape(1, d), ffn1_w_gate[0], ffn1_w_up[0], ffn1_w_down[0], fin,
              apply_final=False)

    q, kk, avt, mqk, mvt, mo, gates, gt = _inproj(x1, mix_norm[0].reshape(1, d), w_in[0], b_in[0],
                                                  mlstm_conv[0], seq)

    ya = _attention(attn_sinks[0], q, kk, avt, batch, seq)

    rows, cols = _mlstm_gates(gt, seq)
    hm = _mlstm(rows, cols, mqk, mvt, mo, mlstm_head_norm[0].reshape(1, SZ_MV), batch, seq)

    x2 = _merge(x1, ya, hm, gates, w_proj_attn[0], w_proj_mlstm[0], w_out[0])

    out = _ffn(x2, ffn2_norm[0].reshape(1, d), ffn2_w_gate[0], ffn2_w_up[0], ffn2_w_down[0], fin,
               apply_final=True)
    return out.reshape(batch, seq, d)
```

```python
import functools

import jax
import jax.numpy as jnp
from jax import lax
from jax.experimental import pallas as pl
from jax.experimental.pallas import tpu as pltpu

F32 = jnp.float32
BF16 = jnp.bfloat16

D_MODEL = 1024
ATTN_HEAD_DIM = 64
ATTN_HEADS = 16
ATTN_KV_HEADS = 4
ATTN_GROUP = 4
WINDOW = 128
MLSTM_HEADS = 8
MLSTM_V_DIM = 128
MLSTM_QK_DIM = 64
CONV_WIDTH = 4
GATE_SOFTCAP = 15.0
D_FF = 2816
FFN_RESIDUAL_WEIGHT = 0.5
NORM_EPS = 1e-6

SZ_AQ, SZ_AK, SZ_AV = 1024, 256, 256
SZ_MQ, SZ_MK, SZ_MV, SZ_MO = 512, 512, 1024, 1024
SZ_MI, SZ_MF = 8, 8

LANES = 128
SUBLANES = 8
MXU_DIM = 256
VMEM_LIMIT_BYTES = 56 * 1024 * 1024

MLSTM_CHUNK = 128
FF_CHUNK = MXU_DIM
FFN_STAGE_SLOTS = 3
TM_FFN = 1024
TM_INPROJ = 1024
TM_MERGE = 1024
TM_MLSTM = 8 * MLSTM_CHUNK
ATTN_LOOKAHEAD = 4
ATTN_BLOCKS = 8
MLSTM_LOOKAHEAD = 2

LOG2E = 1.4426950408889634
K_SCALE_LOG2 = 3.0
ROW_M2, ROW_WINTER, ROW_ENEG, ROW_WK, ROW_DECAY = 0, 8, 16, 24, 32
GATE_ROWS = 40

OFF_AQ = 0
OFF_AK = OFF_AQ + SZ_AQ
OFF_AV = OFF_AK + SZ_AK
OFF_MQ = OFF_AV + SZ_AV
OFF_MK = OFF_MQ + SZ_MQ
OFF_MV = OFF_MK + SZ_MK
OFF_MO = OFF_MV + SZ_MV
OFF_MI = OFF_MO + SZ_MO
OFF_MF = OFF_MI + SZ_MI
OFF_G = OFF_MF + SZ_MF
IN_WIDTH = OFF_G + 2 * D_MODEL
FEATURE_MAJOR_GROUPS = ((OFF_AQ, SZ_AQ), (OFF_AV, SZ_AV), (OFF_MV, SZ_MV), (OFF_MI, SZ_MI + SZ_MF))
FEATURE_MAJOR_ROWS = sum(size for _, size in FEATURE_MAJOR_GROUPS)
_FEATURE_MAJOR_BIAS_ROWS = tuple(
    (off, size, sum(s for _, s in FEATURE_MAJOR_GROUPS[:k]))
    for k, (off, size) in enumerate(FEATURE_MAJOR_GROUPS))
STAGE_ROWS = 256
STAGE_SLOTS = 4

ATTN_HEAD_ORDER = tuple((2 * (t // 4) + par) * ATTN_GROUP + t % 4
                        for t in range(ATTN_HEADS // 2) for par in range(2))


def _rms(x, g):
    return x * lax.rsqrt(jnp.mean(x * x, axis=-1, keepdims=True) + NORM_EPS) * g


def _const_spec(shape):
    zeros = (0,) * len(shape)
    return pl.BlockSpec(shape, lambda *_: zeros, pipeline_mode=pl.Buffered(1))


def _ffn_stage_weights(wg_hbm, wu_hbm, wd_hbm, wg_ref, wu_ref, wd_ref, gu_stage, d_stage, sem):
    n_chunks = D_FF // FF_CHUNK
    ahead = FFN_STAGE_SLOTS - 1

    def copies(c):
        slot = c % FFN_STAGE_SLOTS
        cols = pl.ds(c * FF_CHUNK, FF_CHUNK)
        return (pltpu.make_async_copy(wg_hbm.at[:, cols], gu_stage.at[slot, 0], sem.at[slot, 0]),
                pltpu.make_async_copy(wu_hbm.at[:, cols], gu_stage.at[slot, 1], sem.at[slot, 1]),
                pltpu.make_async_copy(wd_hbm.at[cols, :], d_stage.at[slot], sem.at[slot, 2]))

    for c in range(min(ahead, n_chunks)):
        for cp in copies(c):
            cp.start()
    for c in range(n_chunks):
        slot = c % FFN_STAGE_SLOTS
        if c + ahead < n_chunks:
            for cp in copies(c + ahead):
                cp.start()
        for cp in copies(c):
            cp.wait()
        sl = slice(c * FF_CHUNK, (c + 1) * FF_CHUNK)
        wg_ref[:, sl] = gu_stage[slot, 0].astype(BF16)
        wu_ref[:, sl] = gu_stage[slot, 1].astype(BF16)
        wd_ref[sl, :] = d_stage[slot].astype(BF16)


def _ffn_kernel(x_ref, g_ref, wg_hbm, wu_hbm, wd_hbm, fin_ref, o_ref,
                wg_ref, wu_ref, wd_ref, gu_stage, d_stage, sem, *, apply_final):
    @pl.when(pl.program_id(0) == 0)
    def _():
        _ffn_stage_weights(wg_hbm, wu_hbm, wd_hbm, wg_ref, wu_ref, wd_ref, gu_stage, d_stage, sem)

    x = x_ref[...]
    h = _rms(x, g_ref[...]).astype(BF16)
    for c in range(D_FF // FF_CHUNK):
        sl = slice(c * FF_CHUNK, (c + 1) * FF_CHUNK)
        a = jnp.dot(h, wg_ref[:, sl], preferred_element_type=F32)
        u = jnp.dot(h, wu_ref[:, sl], preferred_element_type=F32)
        act = (a * jax.nn.sigmoid(a) * (u * FFN_RESIDUAL_WEIGHT)).astype(BF16)
        d = jnp.dot(act, wd_ref[sl, :], preferred_element_type=F32)
        if c == 0:
            o_ref[...] = x + d
        else:
            o_ref[...] += d
    if apply_final:
        o_ref[...] = _rms(o_ref[...], fin_ref[...])


def _ffn(x, g, wg, wu, wd, fin, apply_final):
    n = x.shape[0]
    tm = TM_FFN
    return pl.pallas_call(
        functools.partial(_ffn_kernel, apply_final=apply_final),
        grid=(n // tm,),
        in_specs=[
            pl.BlockSpec((tm, D_MODEL), lambda i: (i, 0)),
            _const_spec((1, D_MODEL)),
            pl.BlockSpec(memory_space=pl.ANY),
            pl.BlockSpec(memory_space=pl.ANY),
            pl.BlockSpec(memory_space=pl.ANY),
            _const_spec((1, D_MODEL)),
        ],
        out_specs=pl.BlockSpec((tm, D_MODEL), lambda i: (i, 0)),
        out_shape=jax.ShapeDtypeStruct((n, D_MODEL), F32),
        scratch_shapes=[
            pltpu.VMEM((D_MODEL, D_FF), BF16),
            pltpu.VMEM((D_MODEL, D_FF), BF16),
            pltpu.VMEM((D_FF, D_MODEL), BF16),
            pltpu.VMEM((FFN_STAGE_SLOTS, 2, D_MODEL, FF_CHUNK), F32),
            pltpu.VMEM((FFN_STAGE_SLOTS, FF_CHUNK, D_MODEL), F32),
            pltpu.SemaphoreType.DMA((FFN_STAGE_SLOTS, 3)),
        ],
        compiler_params=pltpu.CompilerParams(
            dimension_semantics=("arbitrary",), vmem_limit_bytes=VMEM_LIMIT_BYTES),
        name="ffn_final" if apply_final else "ffn",
    )(x, g, wg, wu, wd, fin)


def _inproj_stage_weights(w_hbm, w_ref, stage, sem):
    bounds = [(r0, min(STAGE_ROWS, IN_WIDTH - r0)) for r0 in range(0, IN_WIDTH, STAGE_ROWS)]

    def copy(c):
        r0, rows = bounds[c]
        slot = c % STAGE_SLOTS
        return pltpu.make_async_copy(w_hbm.at[pl.ds(r0, rows), :],
                                     stage.at[slot, pl.ds(0, rows), :], sem.at[slot])

    for c in range(min(STAGE_SLOTS - 1, len(bounds))):
        copy(c).start()
    for c, (r0, rows) in enumerate(bounds):
        if c + STAGE_SLOTS - 1 < len(bounds):
            copy(c + STAGE_SLOTS - 1).start()
        copy(c).wait()
        w_ref[r0:r0 + rows, :] = stage[c % STAGE_SLOTS, 0:rows, :].astype(BF16)


def _inproj_kernel(x_ref, g_ref, w_hbm, brow_ref, bgate_ref, bcol_ref, conv_ref,
                   q_ref, kk_ref, avt_ref, mqk_ref, mvt_ref, mo_ref, gate_ref, gt_ref,
                   w_ref, stage, sem, conv_scr, *, tiles_per_seq):
    i = pl.program_id(0)
    tm = x_ref.shape[0]

    @pl.when(i == 0)
    def _():
        _inproj_stage_weights(w_hbm, w_ref, stage, sem)

    h = _rms(x_ref[...], g_ref[...]).astype(BF16)
    contract_lanes = (((1,), (1,)), ((), ()))

    def seg(r0, width, bias):
        return lax.dot_general(h, w_ref[r0:r0 + width, :], contract_lanes,
                               preferred_element_type=F32) + bias

    def seg_t(r0, rows):
        c0 = next(base + r0 - off for off, size, base in _FEATURE_MAJOR_BIAS_ROWS
                  if off <= r0 < off + size)
        return (lax.dot_general(w_ref[r0:r0 + rows, :], h, contract_lanes,
                                preferred_element_type=F32) + bcol_ref[c0:c0 + rows, :])

    @pl.when(i == 0)
    def _():
        conv_scr[...] = jnp.zeros_like(conv_scr)

    carry = jnp.where(i % tiles_per_seq == 0, 0.0, conv_scr[...])

    def conv_slab(c, width):
        z = seg(OFF_MQ + c, width, brow_ref[:, OFF_MQ + c:OFF_MQ + c + width])
        conv_scr[:, c:c + width] = z[tm - SUBLANES:tm, :]
        zc = jnp.concatenate([carry[:, c:c + width], z], axis=0)
        acc = z * conv_ref[CONV_WIDTH - 1:CONV_WIDTH, c:c + width]
        for k in range(1, CONV_WIDTH):
            zk = pltpu.roll(zc, k, 0)[SUBLANES:, :]
            acc = acc + zk * conv_ref[CONV_WIDTH - 1 - k:CONV_WIDTH - k, c:c + width]
        mqk_ref[:, c:c + width] = (acc * jax.nn.sigmoid(acc)).astype(BF16)

    dw = MXU_DIM
    for k in range(SZ_AQ // dw):
        sl = slice(k * dw, (k + 1) * dw)
        conv_slab(k * dw, dw)
        q_ref[sl, :] = (seg_t(OFF_AQ + k * dw, dw) * (LOG2E * ATTN_HEAD_DIM ** -0.5)).astype(BF16)
        if k == 0:
            kk_ref[...] = seg(OFF_AK, SZ_AK, brow_ref[:, OFF_AK:OFF_AK + SZ_AK]).astype(BF16)
        elif k == 1:
            avt_ref[...] = seg_t(OFF_AV, SZ_AV).astype(BF16)
        mvt_ref[sl, :] = seg_t(OFF_MV + k * dw, dw).astype(BF16)
        mo = seg(OFF_MO + k * dw, dw, brow_ref[:, OFF_MO + k * dw:OFF_MO + (k + 1) * dw])
        mo_ref[:, sl] = jax.nn.sigmoid(mo).astype(BF16)
        gsl = slice(2 * k * dw, (2 * k + 2) * dw)
        gate_ref[:, gsl] = seg(OFF_G + 2 * k * dw, 2 * dw, bgate_ref[:, gsl]).astype(BF16)

    gt_ref[...] = seg_t(OFF_MI, SZ_MI + SZ_MF)


def _inproj(x, g, w_in, b_in, conv, seq):
    w = w_in.T
    b_row = b_in.reshape(1, IN_WIDTH)
    b_gate = b_in[OFF_G:].reshape(1, 2 * D_MODEL)
    b_col = jnp.concatenate([b_in[off:off + size] for off, size in FEATURE_MAJOR_GROUPS]).reshape(
        FEATURE_MAJOR_ROWS, 1)
    n = x.shape[0]
    tm = TM_INPROJ
    row = lambda width: pl.BlockSpec((tm, width), lambda i: (i, 0))
    col = lambda height: pl.BlockSpec((height, tm), lambda i: (0, i))
    bf = lambda width: jax.ShapeDtypeStruct((n, width), BF16)
    return pl.pallas_call(
        functools.partial(_inproj_kernel, tiles_per_seq=seq // tm),
        grid=(n // tm,),
        in_specs=[
            row(D_MODEL),
            _const_spec((1, D_MODEL)),
            pl.BlockSpec(memory_space=pl.ANY),
            _const_spec((1, IN_WIDTH)),
            _const_spec((1, 2 * D_MODEL)),
            _const_spec((FEATURE_MAJOR_ROWS, 1)),
            _const_spec((CONV_WIDTH, SZ_MQ + SZ_MK)),
        ],
        out_specs=[col(SZ_AQ), row(SZ_AK), col(SZ_AV), row(SZ_MQ + SZ_MK), col(SZ_MV), row(SZ_MO),
                   row(2 * D_MODEL), col(2 * MLSTM_HEADS)],
        out_shape=[jax.ShapeDtypeStruct((SZ_AQ, n), BF16), bf(SZ_AK),
                   jax.ShapeDtypeStruct((SZ_AV, n), BF16),
                   bf(SZ_MQ + SZ_MK), jax.ShapeDtypeStruct((SZ_MV, n), BF16), bf(SZ_MO),
                   bf(2 * D_MODEL), jax.ShapeDtypeStruct((2 * MLSTM_HEADS, n), F32)],
        scratch_shapes=[pltpu.VMEM((IN_WIDTH, D_MODEL), BF16),
                        pltpu.VMEM((STAGE_SLOTS, STAGE_ROWS, D_MODEL), F32),
                        pltpu.SemaphoreType.DMA((STAGE_SLOTS,)),
                        pltpu.VMEM((SUBLANES, SZ_MQ + SZ_MK), F32)],
        compiler_params=pltpu.CompilerParams(
            dimension_semantics=("arbitrary",), vmem_limit_bytes=VMEM_LIMIT_BYTES),
        name="inproj",
    )(x, g, w, b_row, b_gate, b_col, conv)


def _attn_kernel(sink_ref, q_ref, kc_ref, kp_ref, vtc_ref, vtp_ref, o_ref, k_scr, vt_scr):
    w = WINDOW
    k_scr[0:w, :] = kp_ref[...]
    k_scr[w:, :] = kc_ref[...]
    vt_scr[:, 0:w] = vtp_ref[...]
    vt_scr[:, w:] = vtc_ref[...]
    kj = lax.broadcasted_iota(jnp.int32, (2 * w, w), 0)
    qi = lax.broadcasted_iota(jnp.int32, (2 * w, w), 1)
    band = (kj > qi) & (kj <= qi + w)
    band_first = band & ((kj >= w) | (pl.program_id(1) > 0))
    pairs = [(jb, t) for jb in range(q_ref.shape[1] // w) for t in range(ATTN_HEADS // 2)]
    zeros_qt = jnp.zeros((ATTN_HEAD_DIM, w), BF16)

    def scores(jb, t):
        ha, hb = ATTN_HEAD_ORDER[2 * t], ATTN_HEAD_ORDER[2 * t + 1]
        qa = q_ref[ha * ATTN_HEAD_DIM:(ha + 1) * ATTN_HEAD_DIM, jb * w:(jb + 1) * w]
        qb = q_ref[hb * ATTN_HEAD_DIM:(hb + 1) * ATTN_HEAD_DIM, jb * w:(jb + 1) * w]
        qt2 = jnp.concatenate([jnp.concatenate([qa, zeros_qt], axis=0),
                               jnp.concatenate([zeros_qt, qb], axis=0)], axis=1)
        kk = k_scr[jb * w:(jb + 2) * w, (t // 4) * LANES:(t // 4 + 1) * LANES]
        return jnp.dot(kk, qt2, preferred_element_type=F32)

    def head_out(jb, t, par, s):
        s = jnp.where(band_first if jb == 0 else band, s, -jnp.inf)
        sink = sink_ref[ATTN_HEAD_ORDER[2 * t + par]] * LOG2E
        mx = jnp.maximum(jnp.max(s, axis=0, keepdims=True), sink)
        pr = jnp.exp2(s - mx)
        denom = jnp.sum(pr, axis=0, keepdims=True) + jnp.exp2(sink - mx)
        kvh = 2 * (t // 4) + par
        vt = vt_scr[kvh * ATTN_HEAD_DIM:(kvh + 1) * ATTN_HEAD_DIM, jb * w:(jb + 2) * w]
        return jnp.dot(vt, pr.astype(BF16), preferred_element_type=F32) * (1.0 / denom)

    pending = [scores(*u) for u in pairs[:ATTN_LOOKAHEAD]]
    for idx, (jb, t) in enumerate(pairs):
        s2 = pending[idx]
        pending[idx] = None
        outs = [head_out(jb, t, par, s2[:, par * w:(par + 1) * w]) for par in range(2)]
        if idx + ATTN_LOOKAHEAD < len(pairs):
            pending.append(scores(*pairs[idx + ATTN_LOOKAHEAD]))
        pair = jnp.concatenate(outs, axis=0)
        o_ref[jb * w:(jb + 1) * w, t * LANES:(t + 1) * LANES] = pair.T.astype(BF16)


def _attention(sinks, q, kk, vt, batch, seq):
    n = kk.shape[0]
    tq = ATTN_BLOCKS * WINDOW
    nt = seq // tq
    cur = lambda b, j: b * nt + j
    prev = lambda b, j: (b * nt + j) * ATTN_BLOCKS - jnp.minimum(j, 1)
    return pl.pallas_call(
        _attn_kernel,
        grid=(batch, nt),
        in_specs=[
            pl.BlockSpec(memory_space=pltpu.SMEM),
            pl.BlockSpec((SZ_AQ, tq), lambda b, j: (0, cur(b, j))),
            pl.BlockSpec((tq, SZ_AK), lambda b, j: (cur(b, j), 0)),
            pl.BlockSpec((WINDOW, SZ_AK), lambda b, j: (prev(b, j), 0)),
            pl.BlockSpec((SZ_AV, tq), lambda b, j: (0, cur(b, j))),
            pl.BlockSpec((SZ_AV, WINDOW), lambda b, j: (0, prev(b, j))),
        ],
        out_specs=pl.BlockSpec((tq, SZ_AQ), lambda b, j: (cur(b, j), 0)),
        out_shape=jax.ShapeDtypeStruct((n, SZ_AQ), BF16),
        scratch_shapes=[pltpu.VMEM((WINDOW + tq, SZ_AK), BF16),
                        pltpu.VMEM((SZ_AV, WINDOW + tq), BF16)],
        compiler_params=pltpu.CompilerParams(
            dimension_semantics=("arbitrary", "arbitrary"), vmem_limit_bytes=VMEM_LIMIT_BYTES),
        name="attention",
    )(sinks, q, kk, kk, vt, vt)


def _log_sigmoid(x):
    return -(jnp.maximum(-x, 0.0) + jnp.log(1.0 + jnp.exp(-jnp.abs(x))))


def _mlstm_gate_kernel(g_ref, rows_ref, cols_ref, *, seq):
    L = MLSTM_CHUNK
    nh = MLSTM_HEADS
    n = g_ref.shape[1]
    g = g_ref[...]
    ig = GATE_SOFTCAP * jnp.tanh(g[0:nh] / GATE_SOFTCAP)
    fg = GATE_SOFTCAP * jnp.tanh(g[nh:2 * nh] / GATE_SOFTCAP)
    lf = _log_sigmoid(fg)
    pos = lax.broadcasted_iota(jnp.int32, (nh, n), 1) & (L - 1)
    a = lf
    sh = 1
    while sh < L:
        a = a + jnp.where(pos >= sh, pltpu.roll(a, sh, 1), 0.0)
        sh *= 2
    bvec = ig - a
    pm = bvec
    sh = 1
    while sh < L:
        pm = jnp.maximum(pm, jnp.where(pos >= sh, pltpu.roll(pm, sh, 1), -jnp.inf))
        sh *= 2
    pad = jnp.zeros((L - nh, L), F32)
    m = [jnp.zeros((nh, L), F32) for _ in range(n // seq)]
    for c in range(seq // L):
        for s in range(n // seq):
            sl = slice(s * seq + c * L, s * seq + (c + 1) * L)
            a_c, b_c = a[:, sl], bvec[:, sl]
            mrow = jnp.maximum(m[s], pm[:, sl])
            total = jnp.broadcast_to(a_c[:, L - 1:L], (nh, L))
            mlast = jnp.broadcast_to(mrow[:, L - 1:L], (nh, L))
            rows_ref[ROW_M2:ROW_M2 + nh, sl] = mrow * LOG2E
            rows_ref[ROW_WINTER:ROW_WINTER + nh, sl] = jnp.exp(m[s] - mrow)
            rows_ref[ROW_ENEG:ROW_ENEG + nh, sl] = jnp.exp(-(a_c + mrow))
            rows_ref[ROW_WK:ROW_WK + nh, sl] = jnp.exp(b_c - mlast) * MLSTM_QK_DIM ** -0.5
            rows_ref[ROW_DECAY:ROW_DECAY + nh, sl] = jnp.exp(m[s] - mlast)
            m[s] = total + mlast
            cols_ref[sl, :] = jnp.concatenate([b_c * LOG2E - K_SCALE_LOG2, pad], axis=0).T


def _mlstm_gates(gt, seq):
    n = gt.shape[1]
    return pl.pallas_call(
        functools.partial(_mlstm_gate_kernel, seq=seq),
        grid=(1,),
        in_specs=[pl.BlockSpec((2 * MLSTM_HEADS, n), lambda i: (0, 0))],
        out_specs=[pl.BlockSpec((GATE_ROWS, n), lambda i: (0, 0)),
                   pl.BlockSpec((n, LANES), lambda i: (0, 0))],
        out_shape=[jax.ShapeDtypeStruct((GATE_ROWS, n), F32),
                   jax.ShapeDtypeStruct((n, LANES), F32)],
        compiler_params=pltpu.CompilerParams(
            dimension_semantics=("arbitrary",), vmem_limit_bytes=VMEM_LIMIT_BYTES),
        name="mlstm_gates",
    )(gt)


def _mlstm_kernel(rows_ref, cols_ref, qk_ref, vt_ref, og_ref, hn_ref, o_ref, ct_scr, n_scr):
    L = MLSTM_CHUNK
    nh = MLSTM_HEADS
    contract_lanes = (((1,), (1,)), ((), ()))

    @pl.when(pl.program_id(1) == 0)
    def _():
        ct_scr[...] = jnp.zeros_like(ct_scr)
        n_scr[...] = jnp.zeros_like(n_scr)

    row_i = lax.broadcasted_iota(jnp.int32, (L, L), 0)
    col_i = lax.broadcasted_iota(jnp.int32, (L, L), 1)
    causal_t = row_i <= col_i
    lo_lane = col_i < MLSTM_QK_DIM
    zero = jnp.zeros((), BF16)

    n_chunks = qk_ref.shape[0] // L
    pairs = [(ci, p) for ci in range(n_chunks) for p in range(nh // 2)]
    ct = [ct_scr[p] for p in range(nh // 2)]
    nn = [n_scr[p] for p in range(nh // 2)]
    gate_rows = {}

    def chunk_gates(ci):
        if ci not in gate_rows:
            rs = slice(ci * L, (ci + 1) * L)
            wk = rows_ref[ROW_WK:ROW_WK + nh, rs]
            gate_rows[ci] = dict(
                m2=rows_ref[ROW_M2:ROW_M2 + nh, rs], w_inter=rows_ref[ROW_WINTER:ROW_WINTER + nh, rs],
                e_neg=rows_ref[ROW_ENEG:ROW_ENEG + nh, rs], wk=wk,
                decay=rows_ref[ROW_DECAY:ROW_DECAY + nh, rs], b2cols=cols_ref[rs, :],
                wk_b=jnp.concatenate([wk, wk], axis=0).astype(BF16))
        return gate_rows[ci]

    def front(ci, p):
        g = chunk_gates(ci)
        rs = slice(ci * L, (ci + 1) * L)
        h0, h1 = 2 * p, 2 * p + 1
        q2 = qk_ref[rs, p * LANES:(p + 1) * LANES]
        k2 = qk_ref[rs, SZ_MQ + p * LANES:SZ_MQ + (p + 1) * LANES]
        qm2 = jnp.concatenate([jnp.where(lo_lane, q2, zero), jnp.where(lo_lane, zero, q2)], axis=0)
        n2b = jnp.broadcast_to(nn[p], (2 * SUBLANES, LANES)).astype(BF16)
        stacked = jnp.concatenate([k2, ct[p].astype(BF16), n2b], axis=0)
        res = lax.dot_general(stacked, qm2, contract_lanes, preferred_element_type=F32)
        vw = [(vt_ref[hh * MLSTM_V_DIM:(hh + 1) * MLSTM_V_DIM, rs].astype(F32)
               * g["wk"][hh:hh + 1, :]).astype(BF16) for hh in (h0, h1)]
        upd = jnp.dot(jnp.concatenate(vw + [g["wk_b"]], axis=0), k2,
                      preferred_element_type=F32)
        n_inc = upd[2 * MLSTM_V_DIM:]
        dec = jnp.where(lo_lane[0:1, :], g["decay"][h0:h0 + 1, :], g["decay"][h1:h1 + 1, :])
        ct[p] = dec * ct[p] + jnp.where(lo_lane, upd[0:MLSTM_V_DIM], upd[MLSTM_V_DIM:2 * MLSTM_V_DIM])
        nn[p] = dec * nn[p] + jnp.where(lo_lane[0:1, :], n_inc[h0:h0 + 1, :], n_inc[h1:h1 + 1, :])
        return res

    def back(ci, p, par, res):
        st = res[0:L, par * L:(par + 1) * L]
        inter = res[L:L + MLSTM_V_DIM, par * L:(par + 1) * L]
        qn = res[L + MLSTM_V_DIM:L + MLSTM_V_DIM + 1, par * L:(par + 1) * L]
        g = chunk_gates(ci)
        rs = slice(ci * L, (ci + 1) * L)
        hh = 2 * p + par
        vsl = slice(hh * MLSTM_V_DIM, (hh + 1) * MLSTM_V_DIM)
        dt = jnp.where(causal_t,
                       jnp.exp2(g["b2cols"][:, hh:hh + 1] - g["m2"][hh:hh + 1, :]), 0.0)
        sct = st * dt
        wi = g["w_inter"][hh:hh + 1, :]
        den = jnp.sum(sct, axis=0, keepdims=True) + wi * qn
        numt = jnp.dot(vt_ref[vsl, rs], sct.astype(BF16), preferred_element_type=F32) + wi * inter
        rden = 1.0 / jnp.maximum(jnp.abs(den), g["e_neg"][hh:hh + 1, :])
        ms = jnp.mean(numt * numt, axis=0, keepdims=True)
        scale = rden * lax.rsqrt(rden * rden * ms + NORM_EPS)
        hv = (numt * scale).T
        o_ref[rs, vsl] = (hv * hn_ref[:, vsl] * og_ref[rs, vsl].astype(F32)).astype(BF16)

    pending = [front(*u) for u in pairs[:MLSTM_LOOKAHEAD]]
    for idx, (ci, p) in enumerate(pairs):
        res = pending[idx]
        pending[idx] = None
        back(ci, p, 0, res)
        if idx + MLSTM_LOOKAHEAD < len(pairs):
            pending.append(front(*pairs[idx + MLSTM_LOOKAHEAD]))
        back(ci, p, 1, res)
    for p in range(nh // 2):
        ct_scr[p] = ct[p]
        n_scr[p] = nn[p]


def _mlstm(rows, cols, qk, v, og, hn, batch, seq):
    n = qk.shape[0]
    tm = TM_MLSTM
    nt = seq // tm
    row = pl.BlockSpec((tm, SZ_MV), lambda b, c: (b * nt + c, 0))
    return pl.pallas_call(
        _mlstm_kernel,
        grid=(batch, nt),
        in_specs=[
            pl.BlockSpec((GATE_ROWS, tm), lambda b, c: (0, b * nt + c)),
            pl.BlockSpec((tm, LANES), lambda b, c: (b * nt + c, 0)),
            row,
            pl.BlockSpec((SZ_MV, tm), lambda b, c: (0, b * nt + c)),
            row,
            _const_spec((1, SZ_MV)),
        ],
        out_specs=row,
        out_shape=jax.ShapeDtypeStruct((n, SZ_MV), BF16),
        scratch_shapes=[
            pltpu.VMEM((MLSTM_HEADS // 2, 2 * MLSTM_QK_DIM, MLSTM_V_DIM), F32),
            pltpu.VMEM((MLSTM_HEADS // 2, 1, 2 * MLSTM_QK_DIM), F32),
        ],
        compiler_params=pltpu.CompilerParams(
            dimension_semantics=("arbitrary", "arbitrary"), vmem_limit_bytes=VMEM_LIMIT_BYTES),
        name="mlstm",
    )(rows, cols, qk, v, og, hn)


def _merge_stage_weights(wpa_hbm, wpm_hbm, wo_hbm, wpa_ref, wpm_ref, wo_ref, stage, sem):
    hd = ATTN_HEAD_DIM
    wpa_copies = [pltpu.make_async_copy(wpa_hbm.at[pl.ds(head * hd, hd), :],
                                        stage.at[0, pl.ds(pos * hd, hd), :], sem.at[0])
                  for pos, head in enumerate(ATTN_HEAD_ORDER)]
    wpm_copy = pltpu.make_async_copy(wpm_hbm, stage.at[1], sem.at[1])
    wo_copy = pltpu.make_async_copy(wo_hbm, stage.at[0], sem.at[0])
    for cp in wpa_copies:
        cp.start()
    wpm_copy.start()
    for cp in wpa_copies:
        cp.wait()
    wpa_ref[...] = stage[0].astype(BF16)
    wo_copy.start()
    wpm_copy.wait()
    wpm_ref[...] = stage[1].astype(BF16)
    wo_copy.wait()
    wo_ref[...] = stage[0].astype(BF16)


def _merge_kernel(x_ref, a_ref, hm_ref, gate_ref, wpa_hbm, wpm_hbm, wo_hbm, o_ref,
                  wpa_ref, wpm_ref, wo_ref, stage, sem):
    @pl.when(pl.program_id(0) == 0)
    def _():
        _merge_stage_weights(wpa_hbm, wpm_hbm, wo_hbm, wpa_ref, wpm_ref, wo_ref, stage, sem)

    ya = jnp.dot(a_ref[...], wpa_ref[...], preferred_element_type=F32)
    ym = jnp.dot(hm_ref[...], wpm_ref[...], preferred_element_type=F32)
    ga = jax.nn.sigmoid(gate_ref[:, 0:D_MODEL].astype(F32))
    gm = jax.nn.sigmoid(gate_ref[:, D_MODEL:2 * D_MODEL].astype(F32))
    merged = (ga * ya + gm * ym).astype(BF16)
    o_ref[...] = x_ref[...] + jnp.dot(merged, wo_ref[...], preferred_element_type=F32)


def _merge(x, a, hm, gates, wpa, wpm, wo):
    n = x.shape[0]
    tm = TM_MERGE
    row = lambda width: pl.BlockSpec((tm, width), lambda i: (i, 0))
    return pl.pallas_call(
        _merge_kernel,
        grid=(n // tm,),
        in_specs=[row(D_MODEL), row(SZ_AQ), row(SZ_MV), row(2 * D_MODEL),
                  pl.BlockSpec(memory_space=pl.ANY), pl.BlockSpec(memory_space=pl.ANY),
                  pl.BlockSpec(memory_space=pl.ANY)],
        out_specs=row(D_MODEL),
        out_shape=jax.ShapeDtypeStruct((n, D_MODEL), F32),
        scratch_shapes=[pltpu.VMEM((D_MODEL, D_MODEL), BF16)] * 3 + [
            pltpu.VMEM((2, D_MODEL, D_MODEL), F32), pltpu.SemaphoreType.DMA((2,))],
        compiler_params=pltpu.CompilerParams(
            dimension_semantics=("arbitrary",), vmem_limit_bytes=VMEM_LIMIT_BYTES),
        name="merge",
    )(x, a, hm, gates, wpa, wpm, wo)


def kernel(x, ffn1_norm, ffn1_w_gate, ffn1_w_up, ffn1_w_down, mix_norm, w_in, b_in, attn_sinks,
           mlstm_conv, mlstm_head_norm, w_proj_attn, w_proj_mlstm, w_out, ffn2_norm, ffn2_w_gate,
           ffn2_w_up, ffn2_w_down, final_norm):
    batch, seq, d = x.shape
    assert d == D_MODEL and ffn1_norm.shape[0] == 1, "one layer of width D_MODEL"
    assert all(seq % t == 0 for t in (TM_INPROJ, TM_MLSTM, ATTN_BLOCKS * WINDOW))
    assert all((batch * seq) % t == 0 for t in (TM_FFN, TM_MERGE))
    n = batch * seq
    xf = x.reshape(n, d)
    fin = final_norm.reshape(1, d)

    x1 = _ffn(xf, ffn1_norm[0].reshape(1, d), ffn1_w_gate[0], ffn1_w_up[0], ffn1_w_down[0], fin,
              apply_final=False)

    q, kk, avt, mqk, mvt, mo, gates, gt = _inproj(x1, mix_norm[0].reshape(1, d), w_in[0], b_in[0],
                                                  mlstm_conv[0], seq)

    ya = _attention(attn_sinks[0], q, kk, avt, batch, seq)

    rows, cols = _mlstm_gates(gt, seq)
    hm = _mlstm(rows, cols, mqk, mvt, mo, mlstm_head_norm[0].reshape(1, SZ_MV), batch, seq)

    x2 = _merge(x1, ya, hm, gates, w_proj_attn[0], w_proj_mlstm[0], w_out[0])

    out = _ffn(x2, ffn2_norm[0].reshape(1, d), ffn2_w_gate[0], ffn2_w_up[0], ffn2_w_down[0], fin,
               apply_final=True)
    return out.reshape(batch, seq, d)
```

```python
import functools

import jax
import jax.numpy as jnp
from jax import lax
from jax.experimental import pallas as pl
from jax.experimental.pallas import tpu as pltpu

F32 = jnp.float32
BF16 = jnp.bfloat16

D_MODEL = 1024
ATTN_HEAD_DIM = 64
ATTN_HEADS = 16
ATTN_KV_HEADS = 4
ATTN_GROUP = 4
WINDOW = 128
MLSTM_HEADS = 8
MLSTM_V_DIM = 128
MLSTM_QK_DIM = 64
CONV_WIDTH = 4
GATE_SOFTCAP = 15.0
D_FF = 2816
FFN_RESIDUAL_WEIGHT = 0.5
NORM_EPS = 1e-6

SZ_AQ, SZ_AK, SZ_AV = 1024, 256, 256
SZ_MQ, SZ_MK, SZ_MV, SZ_MO = 512, 512, 1024, 1024
SZ_MI, SZ_MF = 8, 8

LANES = 128
SUBLANES = 8
MXU_DIM = 256
VMEM_LIMIT_BYTES = 56 * 1024 * 1024

MLSTM_CHUNK = 128
FF_CHUNK = MXU_DIM
FFN_STAGE_SLOTS = 3
TM_FFN = 1024
TM_INPROJ = 1024
TM_MERGE = 1024
TM_MLSTM = 8 * MLSTM_CHUNK
ATTN_LOOKAHEAD = 5
ATTN_BLOCKS = 8
MLSTM_LOOKAHEAD = 2

LOG2E = 1.4426950408889634
K_SCALE_LOG2 = 3.0
ROW_M2, ROW_WINTER, ROW_ENEG, ROW_WK, ROW_DECAY = 0, 8, 16, 24, 32
GATE_ROWS = 40

OFF_AQ = 0
OFF_AK = OFF_AQ + SZ_AQ
OFF_AV = OFF_AK + SZ_AK
OFF_MQ = OFF_AV + SZ_AV
OFF_MK = OFF_MQ + SZ_MQ
OFF_MV = OFF_MK + SZ_MK
OFF_MO = OFF_MV + SZ_MV
OFF_MI = OFF_MO + SZ_MO
OFF_MF = OFF_MI + SZ_MI
OFF_G = OFF_MF + SZ_MF
IN_WIDTH = OFF_G + 2 * D_MODEL
FEATURE_MAJOR_GROUPS = ((OFF_AQ, SZ_AQ), (OFF_AV, SZ_AV), (OFF_MV, SZ_MV), (OFF_MI, SZ_MI + SZ_MF))
FEATURE_MAJOR_ROWS = sum(size for _, size in FEATURE_MAJOR_GROUPS)
_FEATURE_MAJOR_BIAS_ROWS = tuple(
    (off, size, sum(s for _, s in FEATURE_MAJOR_GROUPS[:k]))
    for k, (off, size) in enumerate(FEATURE_MAJOR_GROUPS))
STAGE_ROWS = 256
STAGE_SLOTS = 4

ATTN_HEAD_ORDER = tuple((2 * (t // 4) + par) * ATTN_GROUP + t % 4
                        for t in range(ATTN_HEADS // 2) for par in range(2))


def _rms(x, g):
    return x * lax.rsqrt(jnp.mean(x * x, axis=-1, keepdims=True) + NORM_EPS) * g


def _const_spec(shape):
    zeros = (0,) * len(shape)
    return pl.BlockSpec(shape, lambda *_: zeros, pipeline_mode=pl.Buffered(1))


def _ffn_stage_weights(wg_hbm, wu_hbm, wd_hbm, wg_ref, wu_ref, wd_ref, gu_stage, d_stage, sem):
    n_chunks = D_FF // FF_CHUNK
    ahead = FFN_STAGE_SLOTS - 1

    def copies(c):
        slot = c % FFN_STAGE_SLOTS
        cols = pl.ds(c * FF_CHUNK, FF_CHUNK)
        return (pltpu.make_async_copy(wg_hbm.at[:, cols], gu_stage.at[slot, 0], sem.at[slot, 0]),
                pltpu.make_async_copy(wu_hbm.at[:, cols], gu_stage.at[slot, 1], sem.at[slot, 1]),
                pltpu.make_async_copy(wd_hbm.at[cols, :], d_stage.at[slot], sem.at[slot, 2]))

    for c in range(min(ahead, n_chunks)):
        for cp in copies(c):
            cp.start()
    for c in range(n_chunks):
        slot = c % FFN_STAGE_SLOTS
        if c + ahead < n_chunks:
            for cp in copies(c + ahead):
                cp.start()
        for cp in copies(c):
            cp.wait()
        sl = slice(c * FF_CHUNK, (c + 1) * FF_CHUNK)
        wg_ref[:, sl] = gu_stage[slot, 0].astype(BF16)
        wu_ref[:, sl] = gu_stage[slot, 1].astype(BF16)
        wd_ref[sl, :] = d_stage[slot].astype(BF16)


def _ffn_kernel(x_ref, g_ref, wg_hbm, wu_hbm, wd_hbm, fin_ref, o_ref,
                wg_ref, wu_ref, wd_ref, gu_stage, d_stage, sem, acc_ref, *, apply_final):
    @pl.when(pl.program_id(0) == 0)
    def _():
        _ffn_stage_weights(wg_hbm, wu_hbm, wd_hbm, wg_ref, wu_ref, wd_ref, gu_stage, d_stage, sem)

    x = x_ref[...]
    h = _rms(x, g_ref[...]).astype(BF16)
    for c in range(D_FF // FF_CHUNK):
        sl = slice(c * FF_CHUNK, (c + 1) * FF_CHUNK)
        a = jnp.dot(h, wg_ref[:, sl], preferred_element_type=F32)
        u = jnp.dot(h, wu_ref[:, sl], preferred_element_type=F32)
        act = (a * jax.nn.sigmoid(a) * u).astype(BF16)
        d = jnp.dot(act, wd_ref[sl, :], preferred_element_type=F32)
        if c == 0:
            acc_ref[...] = d
        else:
            acc_ref[...] += d
    y = x + FFN_RESIDUAL_WEIGHT * acc_ref[...]
    if apply_final:
        y = _rms(y, fin_ref[...])
    o_ref[...] = y


def _ffn(x, g, wg, wu, wd, fin, apply_final):
    n = x.shape[0]
    tm = TM_FFN
    return pl.pallas_call(
        functools.partial(_ffn_kernel, apply_final=apply_final),
        grid=(n // tm,),
        in_specs=[
            pl.BlockSpec((tm, D_MODEL), lambda i: (i, 0)),
            _const_spec((1, D_MODEL)),
            pl.BlockSpec(memory_space=pl.ANY),
            pl.BlockSpec(memory_space=pl.ANY),
            pl.BlockSpec(memory_space=pl.ANY),
            _const_spec((1, D_MODEL)),
        ],
        out_specs=pl.BlockSpec((tm, D_MODEL), lambda i: (i, 0)),
        out_shape=jax.ShapeDtypeStruct((n, D_MODEL), F32),
        scratch_shapes=[
            pltpu.VMEM((D_MODEL, D_FF), BF16),
            pltpu.VMEM((D_MODEL, D_FF), BF16),
            pltpu.VMEM((D_FF, D_MODEL), BF16),
            pltpu.VMEM((FFN_STAGE_SLOTS, 2, D_MODEL, FF_CHUNK), F32),
            pltpu.VMEM((FFN_STAGE_SLOTS, FF_CHUNK, D_MODEL), F32),
            pltpu.SemaphoreType.DMA((FFN_STAGE_SLOTS, 3)),
            pltpu.VMEM((tm, D_MODEL), F32),
        ],
        compiler_params=pltpu.CompilerParams(
            dimension_semantics=("arbitrary",), vmem_limit_bytes=VMEM_LIMIT_BYTES),
        name="ffn_final" if apply_final else "ffn",
    )(x, g, wg, wu, wd, fin)


def _inproj_stage_weights(w_hbm, w_ref, stage, sem):
    bounds = [(r0, min(STAGE_ROWS, IN_WIDTH - r0)) for r0 in range(0, IN_WIDTH, STAGE_ROWS)]

    def copy(c):
        r0, rows = bounds[c]
        slot = c % STAGE_SLOTS
        return pltpu.make_async_copy(w_hbm.at[pl.ds(r0, rows), :],
                                     stage.at[slot, pl.ds(0, rows), :], sem.at[slot])

    for c in range(min(STAGE_SLOTS - 1, len(bounds))):
        copy(c).start()
    for c, (r0, rows) in enumerate(bounds):
        if c + STAGE_SLOTS - 1 < len(bounds):
            copy(c + STAGE_SLOTS - 1).start()
        copy(c).wait()
        w_ref[r0:r0 + rows, :] = stage[c % STAGE_SLOTS, 0:rows, :].astype(BF16)


def _inproj_kernel(x_ref, g_ref, w_hbm, brow_ref, bgate_ref, bcol_ref, conv_ref,
                   q_ref, kk_ref, avt_ref, mqk_ref, mvt_ref, mo_ref, gate_ref, gt_ref,
                   w_ref, stage, sem, conv_scr, *, tiles_per_seq):
    i = pl.program_id(0)
    tm = x_ref.shape[0]

    @pl.when(i == 0)
    def _():
        _inproj_stage_weights(w_hbm, w_ref, stage, sem)

    h = _rms(x_ref[...], g_ref[...]).astype(BF16)
    contract_lanes = (((1,), (1,)), ((), ()))

    def seg(r0, width, bias):
        return lax.dot_general(h, w_ref[r0:r0 + width, :], contract_lanes,
                               preferred_element_type=F32) + bias

    def seg_t(r0, rows):
        c0 = next(base + r0 - off for off, size, base in _FEATURE_MAJOR_BIAS_ROWS
                  if off <= r0 < off + size)
        return (lax.dot_general(w_ref[r0:r0 + rows, :], h, contract_lanes,
                                preferred_element_type=F32) + bcol_ref[c0:c0 + rows, :])

    @pl.when(i == 0)
    def _():
        conv_scr[...] = jnp.zeros_like(conv_scr)

    carry = jnp.where(i % tiles_per_seq == 0, 0.0, conv_scr[...])

    def conv_slab(c, width):
        z = seg(OFF_MQ + c, width, brow_ref[:, OFF_MQ + c:OFF_MQ + c + width])
        conv_scr[:, c:c + width] = z[tm - SUBLANES:tm, :]
        zc = jnp.concatenate([carry[:, c:c + width], z], axis=0)
        acc = z * conv_ref[CONV_WIDTH - 1:CONV_WIDTH, c:c + width]
        for k in range(1, CONV_WIDTH):
            zk = pltpu.roll(zc, k, 0)[SUBLANES:, :]
            acc = acc + zk * conv_ref[CONV_WIDTH - 1 - k:CONV_WIDTH - k, c:c + width]
        mqk_ref[:, c:c + width] = (acc * jax.nn.sigmoid(acc)).astype(BF16)

    dw = MXU_DIM
    for k in range(SZ_AQ // dw):
        sl = slice(k * dw, (k + 1) * dw)
        conv_slab(k * dw, dw)
        q_ref[sl, :] = (seg_t(OFF_AQ + k * dw, dw) * (LOG2E * ATTN_HEAD_DIM ** -0.5)).astype(BF16)
        if k == 0:
            kk_ref[...] = seg(OFF_AK, SZ_AK, brow_ref[:, OFF_AK:OFF_AK + SZ_AK]).astype(BF16)
        elif k == 1:
            avt_ref[...] = seg_t(OFF_AV, SZ_AV).astype(BF16)
        mvt_ref[sl, :] = seg_t(OFF_MV + k * dw, dw).astype(BF16)
        mo = seg(OFF_MO + k * dw, dw, brow_ref[:, OFF_MO + k * dw:OFF_MO + (k + 1) * dw])
        mo_ref[:, sl] = jax.nn.sigmoid(mo).astype(BF16)
        gsl = slice(2 * k * dw, (2 * k + 2) * dw)
        gate_ref[:, gsl] = seg(OFF_G + 2 * k * dw, 2 * dw, bgate_ref[:, gsl]).astype(BF16)

    gt_ref[...] = seg_t(OFF_MI, SZ_MI + SZ_MF)


def _inproj(x, g, w_in, b_in, conv, seq):
    w = w_in.T
    b_row = b_in.reshape(1, IN_WIDTH)
    b_gate = b_in[OFF_G:].reshape(1, 2 * D_MODEL)
    b_col = jnp.concatenate([b_in[off:off + size] for off, size in FEATURE_MAJOR_GROUPS]).reshape(
        FEATURE_MAJOR_ROWS, 1)
    n = x.shape[0]
    tm = TM_INPROJ
    row = lambda width: pl.BlockSpec((tm, width), lambda i: (i, 0))
    col = lambda height: pl.BlockSpec((height, tm), lambda i: (0, i))
    bf = lambda width: jax.ShapeDtypeStruct((n, width), BF16)
    return pl.pallas_call(
        functools.partial(_inproj_kernel, tiles_per_seq=seq // tm),
        grid=(n // tm,),
        in_specs=[
            row(D_MODEL),
            _const_spec((1, D_MODEL)),
            pl.BlockSpec(memory_space=pl.ANY),
            _const_spec((1, IN_WIDTH)),
            _const_spec((1, 2 * D_MODEL)),
            _const_spec((FEATURE_MAJOR_ROWS, 1)),
            _const_spec((CONV_WIDTH, SZ_MQ + SZ_MK)),
        ],
        out_specs=[col(SZ_AQ), row(SZ_AK), col(SZ_AV), row(SZ_MQ + SZ_MK), col(SZ_MV), row(SZ_MO),
                   row(2 * D_MODEL), col(2 * MLSTM_HEADS)],
        out_shape=[jax.ShapeDtypeStruct((SZ_AQ, n), BF16), bf(SZ_AK),
                   jax.ShapeDtypeStruct((SZ_AV, n), BF16),
                   bf(SZ_MQ + SZ_MK), jax.ShapeDtypeStruct((SZ_MV, n), BF16), bf(SZ_MO),
                   bf(2 * D_MODEL), jax.ShapeDtypeStruct((2 * MLSTM_HEADS, n), F32)],
        scratch_shapes=[pltpu.VMEM((IN_WIDTH, D_MODEL), BF16),
                        pltpu.VMEM((STAGE_SLOTS, STAGE_ROWS, D_MODEL), F32),
                        pltpu.SemaphoreType.DMA((STAGE_SLOTS,)),
                        pltpu.VMEM((SUBLANES, SZ_MQ + SZ_MK), F32)],
        compiler_params=pltpu.CompilerParams(
            dimension_semantics=("arbitrary",), vmem_limit_bytes=VMEM_LIMIT_BYTES),
        name="inproj",
    )(x, g, w, b_row, b_gate, b_col, conv)


def _attn_kernel(sink_ref, q_ref, kc_ref, kp_ref, vtc_ref, vtp_ref, o_ref, k_scr, vt_scr):
    w = WINDOW
    k_scr[0:w, :] = kp_ref[...]
    k_scr[w:, :] = kc_ref[...]
    vt_scr[:, 0:w] = vtp_ref[...]
    vt_scr[:, w:] = vtc_ref[...]
    kj = lax.broadcasted_iota(jnp.int32, (2 * w, w), 0)
    qi = lax.broadcasted_iota(jnp.int32, (2 * w, w), 1)
    band = (kj > qi) & (kj <= qi + w)
    band_first = band & ((kj >= w) | (pl.program_id(1) > 0))
    pairs = [(jb, t) for jb in range(q_ref.shape[1] // w) for t in range(ATTN_HEADS // 2)]
    zeros_qt = jnp.zeros((ATTN_HEAD_DIM, w), BF16)

    def scores(jb, t):
        ha, hb = ATTN_HEAD_ORDER[2 * t], ATTN_HEAD_ORDER[2 * t + 1]
        qa = q_ref[ha * ATTN_HEAD_DIM:(ha + 1) * ATTN_HEAD_DIM, jb * w:(jb + 1) * w]
        qb = q_ref[hb * ATTN_HEAD_DIM:(hb + 1) * ATTN_HEAD_DIM, jb * w:(jb + 1) * w]
        qt2 = jnp.concatenate([jnp.concatenate([qa, zeros_qt], axis=0),
                               jnp.concatenate([zeros_qt, qb], axis=0)], axis=1)
        kk = k_scr[jb * w:(jb + 2) * w, (t // 4) * LANES:(t // 4 + 1) * LANES]
        return jnp.dot(kk, qt2, preferred_element_type=F32)

    def head_out(jb, t, par, s):
        s = jnp.where(band_first if jb == 0 else band, s, -jnp.inf)
        sink = sink_ref[ATTN_HEAD_ORDER[2 * t + par]] * LOG2E
        mx = jnp.maximum(jnp.max(s, axis=0, keepdims=True), sink)
        pr = jnp.exp2(s - mx)
        denom = jnp.sum(pr, axis=0, keepdims=True) + jnp.exp2(sink - mx)
        kvh = 2 * (t // 4) + par
        vt = vt_scr[kvh * ATTN_HEAD_DIM:(kvh + 1) * ATTN_HEAD_DIM, jb * w:(jb + 2) * w]
        return jnp.dot(vt, pr.astype(BF16), preferred_element_type=F32) * (1.0 / denom)

    pending = [scores(*u) for u in pairs[:ATTN_LOOKAHEAD]]
    for idx, (jb, t) in enumerate(pairs):
        s2 = pending[idx]
        pending[idx] = None
        outs = [head_out(jb, t, par, s2[:, par * w:(par + 1) * w]) for par in range(2)]
        if idx + ATTN_LOOKAHEAD < len(pairs):
            pending.append(scores(*pairs[idx + ATTN_LOOKAHEAD]))
        pair = jnp.concatenate(outs, axis=0)
        o_ref[jb * w:(jb + 1) * w, t * LANES:(t + 1) * LANES] = pair.T.astype(BF16)


def _attention(sinks, q, kk, vt, batch, seq):
    n = kk.shape[0]
    tq = ATTN_BLOCKS * WINDOW
    nt = seq // tq
    cur = lambda b, j: b * nt + j
    prev = lambda b, j: (b * nt + j) * ATTN_BLOCKS - jnp.minimum(j, 1)
    return pl.pallas_call(
        _attn_kernel,
        grid=(batch, nt),
        in_specs=[
            pl.BlockSpec(memory_space=pltpu.SMEM),
            pl.BlockSpec((SZ_AQ, tq), lambda b, j: (0, cur(b, j))),
            pl.BlockSpec((tq, SZ_AK), lambda b, j: (cur(b, j), 0)),
            pl.BlockSpec((WINDOW, SZ_AK), lambda b, j: (prev(b, j), 0)),
            pl.BlockSpec((SZ_AV, tq), lambda b, j: (0, cur(b, j))),
            pl.BlockSpec((SZ_AV, WINDOW), lambda b, j: (0, prev(b, j))),
        ],
        out_specs=pl.BlockSpec((tq, SZ_AQ), lambda b, j: (cur(b, j), 0)),
        out_shape=jax.ShapeDtypeStruct((n, SZ_AQ), BF16),
        scratch_shapes=[pltpu.VMEM((WINDOW + tq, SZ_AK), BF16),
                        pltpu.VMEM((SZ_AV, WINDOW + tq), BF16)],
        compiler_params=pltpu.CompilerParams(
            dimension_semantics=("arbitrary", "arbitrary"), vmem_limit_bytes=VMEM_LIMIT_BYTES),
        name="attention",
    )(sinks, q, kk, kk, vt, vt)


def _log_sigmoid(x):
    return -(jnp.maximum(-x, 0.0) + jnp.log(1.0 + jnp.exp(-jnp.abs(x))))


def _mlstm_gate_kernel(g_ref, rows_ref, cols_ref, *, seq):
    L = MLSTM_CHUNK
    nh = MLSTM_HEADS
    n = g_ref.shape[1]
    g = g_ref[...]
    ig = GATE_SOFTCAP * jnp.tanh(g[0:nh] / GATE_SOFTCAP)
    fg = GATE_SOFTCAP * jnp.tanh(g[nh:2 * nh] / GATE_SOFTCAP)
    lf = _log_sigmoid(fg)
    pos = lax.broadcasted_iota(jnp.int32, (nh, n), 1) & (L - 1)
    a = lf
    sh = 1
    while sh < L:
        a = a + jnp.where(pos >= sh, pltpu.roll(a, sh, 1), 0.0)
        sh *= 2
    bvec = ig - a
    pm = bvec
    sh = 1
    while sh < L:
        pm = jnp.maximum(pm, jnp.where(pos >= sh, pltpu.roll(pm, sh, 1), -jnp.inf))
        sh *= 2
    pad = jnp.zeros((L - nh, L), F32)
    m = [jnp.zeros((nh, L), F32) for _ in range(n // seq)]
    for c in range(seq // L):
        for s in range(n // seq):
            sl = slice(s * seq + c * L, s * seq + (c + 1) * L)
            a_c, b_c = a[:, sl], bvec[:, sl]
            mrow = jnp.maximum(m[s], pm[:, sl])
            total = jnp.broadcast_to(a_c[:, L - 1:L], (nh, L))
            mlast = jnp.broadcast_to(mrow[:, L - 1:L], (nh, L))
            rows_ref[ROW_M2:ROW_M2 + nh, sl] = mrow * LOG2E
            rows_ref[ROW_WINTER:ROW_WINTER + nh, sl] = jnp.exp(m[s] - mrow)
            rows_ref[ROW_ENEG:ROW_ENEG + nh, sl] = jnp.exp(-(a_c + mrow))
            rows_ref[ROW_WK:ROW_WK + nh, sl] = jnp.exp(b_c - mlast) * MLSTM_QK_DIM ** -0.5
            rows_ref[ROW_DECAY:ROW_DECAY + nh, sl] = jnp.exp(m[s] - mlast)
            m[s] = total + mlast
            cols_ref[sl, :] = jnp.concatenate([b_c * LOG2E - K_SCALE_LOG2, pad], axis=0).T


def _mlstm_gates(gt, seq):
    n = gt.shape[1]
    return pl.pallas_call(
        functools.partial(_mlstm_gate_kernel, seq=seq),
        grid=(1,),
        in_specs=[pl.BlockSpec((2 * MLSTM_HEADS, n), lambda i: (0, 0))],
        out_specs=[pl.BlockSpec((GATE_ROWS, n), lambda i: (0, 0)),
                   pl.BlockSpec((n, LANES), lambda i: (0, 0))],
        out_shape=[jax.ShapeDtypeStruct((GATE_ROWS, n), F32),
                   jax.ShapeDtypeStruct((n, LANES), F32)],
        compiler_params=pltpu.CompilerParams(
            dimension_semantics=("arbitrary",), vmem_limit_bytes=VMEM_LIMIT_BYTES),
        name="mlstm_gates",
    )(gt)


def _mlstm_kernel(rows_ref, cols_ref, qk_ref, vt_ref, og_ref, hn_ref, o_ref, ct_scr, n_scr):
    L = MLSTM_CHUNK
    nh = MLSTM_HEADS
    contract_lanes = (((1,), (1,)), ((), ()))

    @pl.when(pl.program_id(1) == 0)
    def _():
        ct_scr[...] = jnp.zeros_like(ct_scr)
        n_scr[...] = jnp.zeros_like(n_scr)

    row_i = lax.broadcasted_iota(jnp.int32, (L, L), 0)
    col_i = lax.broadcasted_iota(jnp.int32, (L, L), 1)
    causal_t = row_i <= col_i
    lo_lane = col_i < MLSTM_QK_DIM
    zero = jnp.zeros((), BF16)

    n_chunks = qk_ref.shape[0] // L
    pairs = [(ci, p) for ci in range(n_chunks) for p in range(nh // 2)]
    ct = [ct_scr[p] for p in range(nh // 2)]
    nn = [n_scr[p] for p in range(nh // 2)]
    gate_rows = {}

    def chunk_gates(ci):
        if ci not in gate_rows:
            rs = slice(ci * L, (ci + 1) * L)
            wk = rows_ref[ROW_WK:ROW_WK + nh, rs]
            gate_rows[ci] = dict(
                m2=rows_ref[ROW_M2:ROW_M2 + nh, rs], w_inter=rows_ref[ROW_WINTER:ROW_WINTER + nh, rs],
                e_neg=rows_ref[ROW_ENEG:ROW_ENEG + nh, rs], wk=wk,
                decay=rows_ref[ROW_DECAY:ROW_DECAY + nh, rs], b2cols=cols_ref[rs, :],
                wk_b=jnp.concatenate([wk, wk], axis=0).astype(BF16))
        return gate_rows[ci]

    def front(ci, p):
        g = chunk_gates(ci)
        rs = slice(ci * L, (ci + 1) * L)
        h0, h1 = 2 * p, 2 * p + 1
        q2 = qk_ref[rs, p * LANES:(p + 1) * LANES]
        k2 = qk_ref[rs, SZ_MQ + p * LANES:SZ_MQ + (p + 1) * LANES]
        qm2 = jnp.concatenate([jnp.where(lo_lane, q2, zero), jnp.where(lo_lane, zero, q2)], axis=0)
        n2b = jnp.broadcast_to(nn[p], (2 * SUBLANES, LANES)).astype(BF16)
        stacked = jnp.concatenate([k2, ct[p].astype(BF16), n2b], axis=0)
        res = lax.dot_general(stacked, qm2, contract_lanes, preferred_element_type=F32)
        vw = [(vt_ref[hh * MLSTM_V_DIM:(hh + 1) * MLSTM_V_DIM, rs].astype(F32)
               * g["wk"][hh:hh + 1, :]).astype(BF16) for hh in (h0, h1)]
        upd = jnp.dot(jnp.concatenate(vw + [g["wk_b"]], axis=0), k2,
                      preferred_element_type=F32)
        n_inc = upd[2 * MLSTM_V_DIM:]
        dec = jnp.where(lo_lane[0:1, :], g["decay"][h0:h0 + 1, :], g["decay"][h1:h1 + 1, :])
        ct[p] = dec * ct[p] + jnp.where(lo_lane, upd[0:MLSTM_V_DIM], upd[MLSTM_V_DIM:2 * MLSTM_V_DIM])
        nn[p] = dec * nn[p] + jnp.where(lo_lane[0:1, :], n_inc[h0:h0 + 1, :], n_inc[h1:h1 + 1, :])
        return res

    def back(ci, p, par, res):
        st = res[0:L, par * L:(par + 1) * L]
        inter = res[L:L + MLSTM_V_DIM, par * L:(par + 1) * L]
        qn = res[L + MLSTM_V_DIM:L + MLSTM_V_DIM + 1, par * L:(par + 1) * L]
        g = chunk_gates(ci)
        rs = slice(ci * L, (ci + 1) * L)
        hh = 2 * p + par
        vsl = slice(hh * MLSTM_V_DIM, (hh + 1) * MLSTM_V_DIM)
        dt = jnp.where(causal_t,
                       jnp.exp2(g["b2cols"][:, hh:hh + 1] - g["m2"][hh:hh + 1, :]), 0.0)
        sct = st * dt
        wi = g["w_inter"][hh:hh + 1, :]
        den = jnp.sum(sct, axis=0, keepdims=True) + wi * qn
        numt = jnp.dot(vt_ref[vsl, rs], sct.astype(BF16), preferred_element_type=F32) + wi * inter
        rden = 1.0 / jnp.maximum(jnp.abs(den), g["e_neg"][hh:hh + 1, :])
        ms = jnp.mean(numt * numt, axis=0, keepdims=True)
        scale = rden * lax.rsqrt(rden * rden * ms + NORM_EPS)
        hv = (numt * scale).T
        o_ref[rs, vsl] = (hv * hn_ref[:, vsl] * og_ref[rs, vsl].astype(F32)).astype(BF16)

    pending = [front(*u) for u in pairs[:MLSTM_LOOKAHEAD]]
    for idx, (ci, p) in enumerate(pairs):
        res = pending[idx]
        pending[idx] = None
        back(ci, p, 0, res)
        if idx + MLSTM_LOOKAHEAD < len(pairs):
            pending.append(front(*pairs[idx + MLSTM_LOOKAHEAD]))
        back(ci, p, 1, res)
    for p in range(nh // 2):
        ct_scr[p] = ct[p]
        n_scr[p] = nn[p]


def _mlstm(rows, cols, qk, v, og, hn, batch, seq):
    n = qk.shape[0]
    tm = TM_MLSTM
    nt = seq // tm
    row = pl.BlockSpec((tm, SZ_MV), lambda b, c: (b * nt + c, 0))
    return pl.pallas_call(
        _mlstm_kernel,
        grid=(batch, nt),
        in_specs=[
            pl.BlockSpec((GATE_ROWS, tm), lambda b, c: (0, b * nt + c)),
            pl.BlockSpec((tm, LANES), lambda b, c: (b * nt + c, 0)),
            row,
            pl.BlockSpec((SZ_MV, tm), lambda b, c: (0, b * nt + c)),
            row,
            _const_spec((1, SZ_MV)),
        ],
        out_specs=row,
        out_shape=jax.ShapeDtypeStruct((n, SZ_MV), BF16),
        scratch_shapes=[
            pltpu.VMEM((MLSTM_HEADS // 2, 2 * MLSTM_QK_DIM, MLSTM_V_DIM), F32),
            pltpu.VMEM((MLSTM_HEADS // 2, 1, 2 * MLSTM_QK_DIM), F32),
        ],
        compiler_params=pltpu.CompilerParams(
            dimension_semantics=("arbitrary", "arbitrary"), vmem_limit_bytes=VMEM_LIMIT_BYTES),
        name="mlstm",
    )(rows, cols, qk, v, og, hn)


def _merge_stage_weights(wpa_hbm, wpm_hbm, wo_hbm, wpa_ref, wpm_ref, wo_ref, stage, sem):
    hd = ATTN_HEAD_DIM
    wpa_copies = [pltpu.make_async_copy(wpa_hbm.at[pl.ds(head * hd, hd), :],
                                        stage.at[0, pl.ds(pos * hd, hd), :], sem.at[0])
                  for pos, head in enumerate(ATTN_HEAD_ORDER)]
    wpm_copy = pltpu.make_async_copy(wpm_hbm, stage.at[1], sem.at[1])
    wo_copy = pltpu.make_async_copy(wo_hbm, stage.at[0], sem.at[0])
    for cp in wpa_copies:
        cp.start()
    wpm_copy.start()
    for cp in wpa_copies:
        cp.wait()
    wpa_ref[...] = stage[0].astype(BF16)
    wo_copy.start()
    wpm_copy.wait()
    wpm_ref[...] = stage[1].astype(BF16)
    wo_copy.wait()
    wo_ref[...] = stage[0].astype(BF16)


def _merge_kernel(x_ref, a_ref, hm_ref, gate_ref, wpa_hbm, wpm_hbm, wo_hbm, o_ref,
                  wpa_ref, wpm_ref, wo_ref, stage, sem):
    @pl.when(pl.program_id(0) == 0)
    def _():
        _merge_stage_weights(wpa_hbm, wpm_hbm, wo_hbm, wpa_ref, wpm_ref, wo_ref, stage, sem)

    ya = jnp.dot(a_ref[...], wpa_ref[...], preferred_element_type=F32)
    ym = jnp.dot(hm_ref[...], wpm_ref[...], preferred_element_type=F32)
    ga = jax.nn.sigmoid(gate_ref[:, 0:D_MODEL].astype(F32))
    gm = jax.nn.sigmoid(gate_ref[:, D_MODEL:2 * D_MODEL].astype(F32))
    merged = (ga * ya + gm * ym).astype(BF16)
    o_ref[...] = x_ref[...] + jnp.dot(merged, wo_ref[...], preferred_element_type=F32)


def _merge(x, a, hm, gates, wpa, wpm, wo):
    n = x.shape[0]
    tm = TM_MERGE
    row = lambda width: pl.BlockSpec((tm, width), lambda i: (i, 0))
    return pl.pallas_call(
        _merge_kernel,
        grid=(n // tm,),
        in_specs=[row(D_MODEL), row(SZ_AQ), row(SZ_MV), row(2 * D_MODEL),
                  pl.BlockSpec(memory_space=pl.ANY), pl.BlockSpec(memory_space=pl.ANY),
                  pl.BlockSpec(memory_space=pl.ANY)],
        out_specs=row(D_MODEL),
        out_shape=jax.ShapeDtypeStruct((n, D_MODEL), F32),
        scratch_shapes=[pltpu.VMEM((D_MODEL, D_MODEL), BF16)] * 3 + [
            pltpu.VMEM((2, D_MODEL, D_MODEL), F32), pltpu.SemaphoreType.DMA((2,))],
        compiler_params=pltpu.CompilerParams(
            dimension_semantics=("arbitrary",), vmem_limit_bytes=VMEM_LIMIT_BYTES),
        name="merge",
    )(x, a, hm, gates, wpa, wpm, wo)


def kernel(x, ffn1_norm, ffn1_w_gate, ffn1_w_up, ffn1_w_down, mix_norm, w_in, b_in, attn_sinks,
           mlstm_conv, mlstm_head_norm, w_proj_attn, w_proj_mlstm, w_out, ffn2_norm, ffn2_w_gate,
           ffn2_w_up, ffn2_w_down, final_norm):
    batch, seq, d = x.shape
    assert d == D_MODEL and ffn1_norm.shape[0] == 1, "one layer of width D_MODEL"
    assert all(seq % t == 0 for t in (TM_INPROJ, TM_MLSTM, ATTN_BLOCKS * WINDOW))
    assert all((batch * seq) % t == 0 for t in (TM_FFN, TM_MERGE))
    n = batch * seq
    xf = x.reshape(n, d)
    fin = final_norm.reshape(1, d)

    x1 = _ffn(xf, ffn1_norm[0].reshape(1, d), ffn1_w_gate[0], ffn1_w_up[0], ffn1_w_down[0], fin,
              apply_final=False)

    q, kk, avt, mqk, mvt, mo, gates, gt = _inproj(x1, mix_norm[0].reshape(1, d), w_in[0], b_in[0],
                                                  mlstm_conv[0], seq)

    ya = _attention(attn_sinks[0], q, kk, avt, batch, seq)

    rows, cols = _mlstm_gates(gt, seq)
    hm = _mlstm(rows, cols, mqk, mvt, mo, mlstm_head_norm[0].reshape(1, SZ_MV), batch, seq)

    x2 = _merge(x1, ya, hm, gates, w_proj_attn[0], w_proj_mlstm[0], w_out[0])

    out = _ffn(x2, ffn2_norm[0].reshape(1, d), ffn2_w_gate[0], ffn2_w_up[0], ffn2_w_down[0], fin,
               apply_final=True)
    return out.reshape(batch, seq, d)
```

```python
import functools

import jax
import jax.numpy as jnp
from jax import lax
from jax.experimental import pallas as pl
from jax.experimental.pallas import tpu as pltpu

F32 = jnp.float32
BF16 = jnp.bfloat16

D_MODEL = 1024
ATTN_HEAD_DIM = 64
ATTN_HEADS = 16
ATTN_KV_HEADS = 4
ATTN_GROUP = 4
WINDOW = 128
MLSTM_HEADS = 8
MLSTM_V_DIM = 128
MLSTM_QK_DIM = 64
CONV_WIDTH = 4
GATE_SOFTCAP = 15.0
D_FF = 2816
FFN_RESIDUAL_WEIGHT = 0.5
NORM_EPS = 1e-6

SZ_AQ, SZ_AK, SZ_AV = 1024, 256, 256
SZ_MQ, SZ_MK, SZ_MV, SZ_MO = 512, 512, 1024, 1024
SZ_MI, SZ_MF = 8, 8

LANES = 128
SUBLANES = 8
MXU_DIM = 256
VMEM_LIMIT_BYTES = 56 * 1024 * 1024

MLSTM_CHUNK = 128
FF_CHUNK = MXU_DIM
FFN_STAGE_SLOTS = 3
TM_FFN = 1024
TM_INPROJ = 1024
TM_MERGE = 1024
TM_MLSTM = 16 * MLSTM_CHUNK
ATTN_LOOKAHEAD = 5
ATTN_BLOCKS = 16
MLSTM_LOOKAHEAD = 2

LOG2E = 1.4426950408889634
K_SCALE_LOG2 = 3.0
ROW_M2, ROW_WINTER, ROW_ENEG, ROW_WK, ROW_DECAY = 0, 8, 16, 24, 32
GATE_ROWS = 40

OFF_AQ = 0
OFF_AK = OFF_AQ + SZ_AQ
OFF_AV = OFF_AK + SZ_AK
OFF_MQ = OFF_AV + SZ_AV
OFF_MK = OFF_MQ + SZ_MQ
OFF_MV = OFF_MK + SZ_MK
OFF_MO = OFF_MV + SZ_MV
OFF_MI = OFF_MO + SZ_MO
OFF_MF = OFF_MI + SZ_MI
OFF_G = OFF_MF + SZ_MF
IN_WIDTH = OFF_G + 2 * D_MODEL
FEATURE_MAJOR_GROUPS = ((OFF_AQ, SZ_AQ), (OFF_AV, SZ_AV), (OFF_MV, SZ_MV), (OFF_MI, SZ_MI + SZ_MF))
FEATURE_MAJOR_ROWS = sum(size for _, size in FEATURE_MAJOR_GROUPS)
_FEATURE_MAJOR_BIAS_ROWS = tuple(
    (off, size, sum(s for _, s in FEATURE_MAJOR_GROUPS[:k]))
    for k, (off, size) in enumerate(FEATURE_MAJOR_GROUPS))
STAGE_ROWS = 256
STAGE_SLOTS = 4

ATTN_HEAD_ORDER = tuple((2 * (t // 4) + par) * ATTN_GROUP + t % 4
                        for t in range(ATTN_HEADS // 2) for par in range(2))


def _rms(x, g):
    return x * lax.rsqrt(jnp.mean(x * x, axis=-1, keepdims=True) + NORM_EPS) * g


def _const_spec(shape):
    zeros = (0,) * len(shape)
    return pl.BlockSpec(shape, lambda *_: zeros, pipeline_mode=pl.Buffered(1))


def _ffn_stage_weights(wg_hbm, wu_hbm, wd_hbm, wg_ref, wu_ref, wd_ref, gu_stage, d_stage, sem):
    n_chunks = D_FF // FF_CHUNK
    ahead = FFN_STAGE_SLOTS - 1

    def copies(c):
        slot = c % FFN_STAGE_SLOTS
        cols = pl.ds(c * FF_CHUNK, FF_CHUNK)
        return (pltpu.make_async_copy(wg_hbm.at[:, cols], gu_stage.at[slot, 0], sem.at[slot, 0]),
                pltpu.make_async_copy(wu_hbm.at[:, cols], gu_stage.at[slot, 1], sem.at[slot, 1]),
                pltpu.make_async_copy(wd_hbm.at[cols, :], d_stage.at[slot], sem.at[slot, 2]))

    for c in range(min(ahead, n_chunks)):
        for cp in copies(c):
            cp.start()
    for c in range(n_chunks):
        slot = c % FFN_STAGE_SLOTS
        if c + ahead < n_chunks:
            for cp in copies(c + ahead):
                cp.start()
        for cp in copies(c):
            cp.wait()
        sl = slice(c * FF_CHUNK, (c + 1) * FF_CHUNK)
        wg_ref[:, sl] = gu_stage[slot, 0].astype(BF16)
        wu_ref[:, sl] = gu_stage[slot, 1].astype(BF16)
        wd_ref[sl, :] = d_stage[slot].astype(BF16)


def _ffn_kernel(x_ref, g_ref, wg_hbm, wu_hbm, wd_hbm, fin_ref, o_ref,
                wg_ref, wu_ref, wd_ref, gu_stage, d_stage, sem, acc_ref, *, apply_final):
    @pl.when(pl.program_id(0) == 0)
    def _():
        _ffn_stage_weights(wg_hbm, wu_hbm, wd_hbm, wg_ref, wu_ref, wd_ref, gu_stage, d_stage, sem)

    x = x_ref[...]
    h = _rms(x, g_ref[...]).astype(BF16)
    for c in range(D_FF // FF_CHUNK):
        sl = slice(c * FF_CHUNK, (c + 1) * FF_CHUNK)
        a = jnp.dot(h, wg_ref[:, sl], preferred_element_type=F32)
        u = jnp.dot(h, wu_ref[:, sl], preferred_element_type=F32)
        act = (a * jax.nn.sigmoid(a) * u).astype(BF16)
        d = jnp.dot(act, wd_ref[sl, :], preferred_element_type=F32)
        if c == 0:
            acc_ref[...] = d
        else:
            acc_ref[...] += d
    y = x + FFN_RESIDUAL_WEIGHT * acc_ref[...]
    if apply_final:
        y = _rms(y, fin_ref[...])
    o_ref[...] = y


def _ffn(x, g, wg, wu, wd, fin, apply_final):
    n = x.shape[0]
    tm = TM_FFN
    return pl.pallas_call(
        functools.partial(_ffn_kernel, apply_final=apply_final),
        grid=(n // tm,),
        in_specs=[
            pl.BlockSpec((tm, D_MODEL), lambda i: (i, 0)),
            _const_spec((1, D_MODEL)),
            pl.BlockSpec(memory_space=pl.ANY),
            pl.BlockSpec(memory_space=pl.ANY),
            pl.BlockSpec(memory_space=pl.ANY),
            _const_spec((1, D_MODEL)),
        ],
        out_specs=pl.BlockSpec((tm, D_MODEL), lambda i: (i, 0)),
        out_shape=jax.ShapeDtypeStruct((n, D_MODEL), F32),
        scratch_shapes=[
            pltpu.VMEM((D_MODEL, D_FF), BF16),
            pltpu.VMEM((D_MODEL, D_FF), BF16),
            pltpu.VMEM((D_FF, D_MODEL), BF16),
            pltpu.VMEM((FFN_STAGE_SLOTS, 2, D_MODEL, FF_CHUNK), F32),
            pltpu.VMEM((FFN_STAGE_SLOTS, FF_CHUNK, D_MODEL), F32),
            pltpu.SemaphoreType.DMA((FFN_STAGE_SLOTS, 3)),
            pltpu.VMEM((tm, D_MODEL), F32),
        ],
        compiler_params=pltpu.CompilerParams(
            dimension_semantics=("arbitrary",), vmem_limit_bytes=VMEM_LIMIT_BYTES),
        name="ffn_final" if apply_final else "ffn",
    )(x, g, wg, wu, wd, fin)


def _inproj_stage_weights(w_hbm, w_ref, stage, sem):
    bounds = [(r0, min(STAGE_ROWS, IN_WIDTH - r0)) for r0 in range(0, IN_WIDTH, STAGE_ROWS)]

    def copy(c):
        r0, rows = bounds[c]
        slot = c % STAGE_SLOTS
        return pltpu.make_async_copy(w_hbm.at[pl.ds(r0, rows), :],
                                     stage.at[slot, pl.ds(0, rows), :], sem.at[slot])

    for c in range(min(STAGE_SLOTS - 1, len(bounds))):
        copy(c).start()
    for c, (r0, rows) in enumerate(bounds):
        if c + STAGE_SLOTS - 1 < len(bounds):
            copy(c + STAGE_SLOTS - 1).start()
        copy(c).wait()
        w_ref[r0:r0 + rows, :] = stage[c % STAGE_SLOTS, 0:rows, :].astype(BF16)


def _inproj_kernel(x_ref, g_ref, w_hbm, brow_ref, bgate_ref, bcol_ref, conv_ref,
                   q_ref, kk_ref, avt_ref, mqk_ref, mvt_ref, mo_ref, gate_ref, gt_ref,
                   w_ref, stage, sem, conv_scr, *, tiles_per_seq):
    i = pl.program_id(0)
    tm = x_ref.shape[0]

    @pl.when(i == 0)
    def _():
        _inproj_stage_weights(w_hbm, w_ref, stage, sem)

    h = _rms(x_ref[...], g_ref[...]).astype(BF16)
    contract_lanes = (((1,), (1,)), ((), ()))

    def seg(r0, width, bias):
        return lax.dot_general(h, w_ref[r0:r0 + width, :], contract_lanes,
                               preferred_element_type=F32) + bias

    def seg_t(r0, rows):
        c0 = next(base + r0 - off for off, size, base in _FEATURE_MAJOR_BIAS_ROWS
                  if off <= r0 < off + size)
        return (lax.dot_general(w_ref[r0:r0 + rows, :], h, contract_lanes,
                                preferred_element_type=F32) + bcol_ref[c0:c0 + rows, :])

    @pl.when(i == 0)
    def _():
        conv_scr[...] = jnp.zeros_like(conv_scr)

    carry = jnp.where(i % tiles_per_seq == 0, 0.0, conv_scr[...])

    def conv_slab(c, width):
        z = seg(OFF_MQ + c, width, brow_ref[:, OFF_MQ + c:OFF_MQ + c + width])
        conv_scr[:, c:c + width] = z[tm - SUBLANES:tm, :]
        zc = jnp.concatenate([carry[:, c:c + width], z], axis=0)
        acc = z * conv_ref[CONV_WIDTH - 1:CONV_WIDTH, c:c + width]
        for k in range(1, CONV_WIDTH):
            zk = pltpu.roll(zc, k, 0)[SUBLANES:, :]
            acc = acc + zk * conv_ref[CONV_WIDTH - 1 - k:CONV_WIDTH - k, c:c + width]
        mqk_ref[:, c:c + width] = (acc * jax.nn.sigmoid(acc)).astype(BF16)

    dw = MXU_DIM
    for k in range(SZ_AQ // dw):
        sl = slice(k * dw, (k + 1) * dw)
        conv_slab(k * dw, dw)
        q_ref[sl, :] = (seg_t(OFF_AQ + k * dw, dw) * (LOG2E * ATTN_HEAD_DIM ** -0.5)).astype(BF16)
        if k == 0:
            kk_ref[...] = seg(OFF_AK, SZ_AK, brow_ref[:, OFF_AK:OFF_AK + SZ_AK]).astype(BF16)
        elif k == 1:
            avt_ref[...] = seg_t(OFF_AV, SZ_AV).astype(BF16)
        mvt_ref[sl, :] = seg_t(OFF_MV + k * dw, dw).astype(BF16)
        mo = seg(OFF_MO + k * dw, dw, brow_ref[:, OFF_MO + k * dw:OFF_MO + (k + 1) * dw])
        mo_ref[:, sl] = jax.nn.sigmoid(mo).astype(BF16)
        gsl = slice(2 * k * dw, (2 * k + 2) * dw)
        gate_ref[:, gsl] = seg(OFF_G + 2 * k * dw, 2 * dw, bgate_ref[:, gsl]).astype(BF16)

    gt_ref[...] = seg_t(OFF_MI, SZ_MI + SZ_MF)


def _inproj(x, g, w_in, b_in, conv, seq):
    w = w_in.T
    b_row = b_in.reshape(1, IN_WIDTH)
    b_gate = b_in[OFF_G:].reshape(1, 2 * D_MODEL)
    b_col = jnp.concatenate([b_in[off:off + size] for off, size in FEATURE_MAJOR_GROUPS]).reshape(
        FEATURE_MAJOR_ROWS, 1)
    n = x.shape[0]
    tm = TM_INPROJ
    row = lambda width: pl.BlockSpec((tm, width), lambda i: (i, 0))
    col = lambda height: pl.BlockSpec((height, tm), lambda i: (0, i))
    bf = lambda width: jax.ShapeDtypeStruct((n, width), BF16)
    return pl.pallas_call(
        functools.partial(_inproj_kernel, tiles_per_seq=seq // tm),
        grid=(n // tm,),
        in_specs=[
            row(D_MODEL),
            _const_spec((1, D_MODEL)),
            pl.BlockSpec(memory_space=pl.ANY),
            _const_spec((1, IN_WIDTH)),
            _const_spec((1, 2 * D_MODEL)),
            _const_spec((FEATURE_MAJOR_ROWS, 1)),
            _const_spec((CONV_WIDTH, SZ_MQ + SZ_MK)),
        ],
        out_specs=[col(SZ_AQ), row(SZ_AK), col(SZ_AV), row(SZ_MQ + SZ_MK), col(SZ_MV), row(SZ_MO),
                   row(2 * D_MODEL), col(2 * MLSTM_HEADS)],
        out_shape=[jax.ShapeDtypeStruct((SZ_AQ, n), BF16), bf(SZ_AK),
                   jax.ShapeDtypeStruct((SZ_AV, n), BF16),
                   bf(SZ_MQ + SZ_MK), jax.ShapeDtypeStruct((SZ_MV, n), BF16), bf(SZ_MO),
                   bf(2 * D_MODEL), jax.ShapeDtypeStruct((2 * MLSTM_HEADS, n), F32)],
        scratch_shapes=[pltpu.VMEM((IN_WIDTH, D_MODEL), BF16),
                        pltpu.VMEM((STAGE_SLOTS, STAGE_ROWS, D_MODEL), F32),
                        pltpu.SemaphoreType.DMA((STAGE_SLOTS,)),
                        pltpu.VMEM((SUBLANES, SZ_MQ + SZ_MK), F32)],
        compiler_params=pltpu.CompilerParams(
            dimension_semantics=("arbitrary",), vmem_limit_bytes=VMEM_LIMIT_BYTES),
        name="inproj",
    )(x, g, w, b_row, b_gate, b_col, conv)


def _attn_kernel(sink_ref, q_ref, kc_ref, kp_ref, vtc_ref, vtp_ref, o_ref, k_scr, vt_scr):
    w = WINDOW
    k_scr[0:w, :] = kp_ref[...]
    k_scr[w:, :] = kc_ref[...]
    vt_scr[:, 0:w] = vtp_ref[...]
    vt_scr[:, w:] = vtc_ref[...]
    kj = lax.broadcasted_iota(jnp.int32, (2 * w, w), 0)
    qi = lax.broadcasted_iota(jnp.int32, (2 * w, w), 1)
    band = (kj > qi) & (kj <= qi + w)
    band_first = band & ((kj >= w) | (pl.program_id(1) > 0))
    pairs = [(jb, t) for jb in range(q_ref.shape[1] // w) for t in range(ATTN_HEADS // 2)]
    zeros_qt = jnp.zeros((ATTN_HEAD_DIM, w), BF16)

    def scores(jb, t):
        ha, hb = ATTN_HEAD_ORDER[2 * t], ATTN_HEAD_ORDER[2 * t + 1]
        qa = q_ref[ha * ATTN_HEAD_DIM:(ha + 1) * ATTN_HEAD_DIM, jb * w:(jb + 1) * w]
        qb = q_ref[hb * ATTN_HEAD_DIM:(hb + 1) * ATTN_HEAD_DIM, jb * w:(jb + 1) * w]
        qt2 = jnp.concatenate([jnp.concatenate([qa, zeros_qt], axis=0),
                               jnp.concatenate([zeros_qt, qb], axis=0)], axis=1)
        kk = k_scr[jb * w:(jb + 2) * w, (t // 4) * LANES:(t // 4 + 1) * LANES]
        return jnp.dot(kk, qt2, preferred_element_type=F32)

    def head_out(jb, t, par, s):
        s = jnp.where(band_first if jb == 0 else band, s, -jnp.inf)
        sink = sink_ref[ATTN_HEAD_ORDER[2 * t + par]] * LOG2E
        mx = jnp.maximum(jnp.max(s, axis=0, keepdims=True), sink)
        pr = jnp.exp2(s - mx)
        denom = jnp.sum(pr, axis=0, keepdims=True) + jnp.exp2(sink - mx)
        kvh = 2 * (t // 4) + par
        vt = vt_scr[kvh * ATTN_HEAD_DIM:(kvh + 1) * ATTN_HEAD_DIM, jb * w:(jb + 2) * w]
        return jnp.dot(vt, pr.astype(BF16), preferred_element_type=F32) * (1.0 / denom)

    pending = [scores(*u) for u in pairs[:ATTN_LOOKAHEAD]]
    for idx, (jb, t) in enumerate(pairs):
        s2 = pending[idx]
        pending[idx] = None
        outs = [head_out(jb, t, par, s2[:, par * w:(par + 1) * w]) for par in range(2)]
        if idx + ATTN_LOOKAHEAD < len(pairs):
            pending.append(scores(*pairs[idx + ATTN_LOOKAHEAD]))
        pair = jnp.concatenate(outs, axis=0)
        o_ref[jb * w:(jb + 1) * w, t * LANES:(t + 1) * LANES] = pair.T.astype(BF16)


def _attention(sinks, q, kk, vt, batch, seq):
    n = kk.shape[0]
    tq = ATTN_BLOCKS * WINDOW
    nt = seq // tq
    cur = lambda b, j: b * nt + j
    prev = lambda b, j: (b * nt + j) * ATTN_BLOCKS - jnp.minimum(j, 1)
    return pl.pallas_call(
        _attn_kernel,
        grid=(batch, nt),
        in_specs=[
            pl.BlockSpec(memory_space=pltpu.SMEM),
            pl.BlockSpec((SZ_AQ, tq), lambda b, j: (0, cur(b, j))),
            pl.BlockSpec((tq, SZ_AK), lambda b, j: (cur(b, j), 0)),
            pl.BlockSpec((WINDOW, SZ_AK), lambda b, j: (prev(b, j), 0)),
            pl.BlockSpec((SZ_AV, tq), lambda b, j: (0, cur(b, j))),
            pl.BlockSpec((SZ_AV, WINDOW), lambda b, j: (0, prev(b, j))),
        ],
        out_specs=pl.BlockSpec((tq, SZ_AQ), lambda b, j: (cur(b, j), 0)),
        out_shape=jax.ShapeDtypeStruct((n, SZ_AQ), BF16),
        scratch_shapes=[pltpu.VMEM((WINDOW + tq, SZ_AK), BF16),
                        pltpu.VMEM((SZ_AV, WINDOW + tq), BF16)],
        compiler_params=pltpu.CompilerParams(
            dimension_semantics=("arbitrary", "arbitrary"), vmem_limit_bytes=VMEM_LIMIT_BYTES),
        name="attention",
    )(sinks, q, kk, kk, vt, vt)


def _log_sigmoid(x):
    return -(jnp.maximum(-x, 0.0) + jnp.log(1.0 + jnp.exp(-jnp.abs(x))))


def _mlstm_gate_kernel(g_ref, rows_ref, cols_ref, *, seq):
    L = MLSTM_CHUNK
    nh = MLSTM_HEADS
    n = g_ref.shape[1]
    g = g_ref[...]
    ig = GATE_SOFTCAP * jnp.tanh(g[0:nh] / GATE_SOFTCAP)
    fg = GATE_SOFTCAP * jnp.tanh(g[nh:2 * nh] / GATE_SOFTCAP)
    lf = _log_sigmoid(fg)
    pos = lax.broadcasted_iota(jnp.int32, (nh, n), 1) & (L - 1)
    a = lf
    sh = 1
    while sh < L:
        a = a + jnp.where(pos >= sh, pltpu.roll(a, sh, 1), 0.0)
        sh *= 2
    bvec = ig - a
    pm = bvec
    sh = 1
    while sh < L:
        pm = jnp.maximum(pm, jnp.where(pos >= sh, pltpu.roll(pm, sh, 1), -jnp.inf))
        sh *= 2
    pad = jnp.zeros((L - nh, L), F32)
    m = [jnp.zeros((nh, L), F32) for _ in range(n // seq)]
    for c in range(seq // L):
        for s in range(n // seq):
            sl = slice(s * seq + c * L, s * seq + (c + 1) * L)
            a_c, b_c = a[:, sl], bvec[:, sl]
            mrow = jnp.maximum(m[s], pm[:, sl])
            total = jnp.broadcast_to(a_c[:, L - 1:L], (nh, L))
            mlast = jnp.broadcast_to(mrow[:, L - 1:L], (nh, L))
            rows_ref[ROW_M2:ROW_M2 + nh, sl] = mrow * LOG2E
            rows_ref[ROW_WINTER:ROW_WINTER + nh, sl] = jnp.exp(m[s] - mrow)
            rows_ref[ROW_ENEG:ROW_ENEG + nh, sl] = jnp.exp(-(a_c + mrow))
            rows_ref[ROW_WK:ROW_WK + nh, sl] = jnp.exp(b_c - mlast) * MLSTM_QK_DIM ** -0.5
            rows_ref[ROW_DECAY:ROW_DECAY + nh, sl] = jnp.exp(m[s] - mlast)
            m[s] = total + mlast
            cols_ref[sl, :] = jnp.concatenate([b_c * LOG2E - K_SCALE_LOG2, pad], axis=0).T


def _mlstm_gates(gt, seq):
    n = gt.shape[1]
    return pl.pallas_call(
        functools.partial(_mlstm_gate_kernel, seq=seq),
        grid=(1,),
        in_specs=[pl.BlockSpec((2 * MLSTM_HEADS, n), lambda i: (0, 0))],
        out_specs=[pl.BlockSpec((GATE_ROWS, n), lambda i: (0, 0)),
                   pl.BlockSpec((n, LANES), lambda i: (0, 0))],
        out_shape=[jax.ShapeDtypeStruct((GATE_ROWS, n), F32),
                   jax.ShapeDtypeStruct((n, LANES), F32)],
        compiler_params=pltpu.CompilerParams(
            dimension_semantics=("arbitrary",), vmem_limit_bytes=VMEM_LIMIT_BYTES),
        name="mlstm_gates",
    )(gt)


def _mlstm_kernel(rows_ref, cols_ref, qk_ref, vt_ref, og_ref, hn_ref, o_ref, ct_scr, n_scr):
    L = MLSTM_CHUNK
    nh = MLSTM_HEADS
    contract_lanes = (((1,), (1,)), ((), ()))

    @pl.when(pl.program_id(1) == 0)
    def _():
        ct_scr[...] = jnp.zeros_like(ct_scr)
        n_scr[...] = jnp.zeros_like(n_scr)

    row_i = lax.broadcasted_iota(jnp.int32, (L, L), 0)
    col_i = lax.broadcasted_iota(jnp.int32, (L, L), 1)
    causal_t = row_i <= col_i
    lo_lane = col_i < MLSTM_QK_DIM
    zero = jnp.zeros((), BF16)

    n_chunks = qk_ref.shape[0] // L
    pairs = [(ci, p) for ci in range(n_chunks) for p in range(nh // 2)]
    ct = [ct_scr[p] for p in range(nh // 2)]
    nn = [n_scr[p] for p in range(nh // 2)]
    gate_rows = {}

    def chunk_gates(ci):
        if ci not in gate_rows:
            rs = slice(ci * L, (ci + 1) * L)
            wk = rows_ref[ROW_WK:ROW_WK + nh, rs]
            gate_rows[ci] = dict(
                m2=rows_ref[ROW_M2:ROW_M2 + nh, rs], w_inter=rows_ref[ROW_WINTER:ROW_WINTER + nh, rs],
                e_neg=rows_ref[ROW_ENEG:ROW_ENEG + nh, rs], wk=wk,
                decay=rows_ref[ROW_DECAY:ROW_DECAY + nh, rs], b2cols=cols_ref[rs, :],
                wk_b=jnp.concatenate([wk, wk], axis=0).astype(BF16))
        return gate_rows[ci]

    def front(ci, p):
        g = chunk_gates(ci)
        rs = slice(ci * L, (ci + 1) * L)
        h0, h1 = 2 * p, 2 * p + 1
        q2 = qk_ref[rs, p * LANES:(p + 1) * LANES]
        k2 = qk_ref[rs, SZ_MQ + p * LANES:SZ_MQ + (p + 1) * LANES]
        qm2 = jnp.concatenate([jnp.where(lo_lane, q2, zero), jnp.where(lo_lane, zero, q2)], axis=0)
        n2b = jnp.broadcast_to(nn[p], (2 * SUBLANES, LANES)).astype(BF16)
        stacked = jnp.concatenate([k2, ct[p].astype(BF16), n2b], axis=0)
        res = lax.dot_general(stacked, qm2, contract_lanes, preferred_element_type=F32)
        vw = [(vt_ref[hh * MLSTM_V_DIM:(hh + 1) * MLSTM_V_DIM, rs].astype(F32)
               * g["wk"][hh:hh + 1, :]).astype(BF16) for hh in (h0, h1)]
        upd = jnp.dot(jnp.concatenate(vw + [g["wk_b"]], axis=0), k2,
                      preferred_element_type=F32)
        n_inc = upd[2 * MLSTM_V_DIM:]
        dec = jnp.where(lo_lane[0:1, :], g["decay"][h0:h0 + 1, :], g["decay"][h1:h1 + 1, :])
        ct[p] = dec * ct[p] + jnp.where(lo_lane, upd[0:MLSTM_V_DIM], upd[MLSTM_V_DIM:2 * MLSTM_V_DIM])
        nn[p] = dec * nn[p] + jnp.where(lo_lane[0:1, :], n_inc[h0:h0 + 1, :], n_inc[h1:h1 + 1, :])
        return res

    def back(ci, p, par, res):
        st = res[0:L, par * L:(par + 1) * L]
        inter = res[L:L + MLSTM_V_DIM, par * L:(par + 1) * L]
        qn = res[L + MLSTM_V_DIM:L + MLSTM_V_DIM + 1, par * L:(par + 1) * L]
        g = chunk_gates(ci)
        rs = slice(ci * L, (ci + 1) * L)
        hh = 2 * p + par
        vsl = slice(hh * MLSTM_V_DIM, (hh + 1) * MLSTM_V_DIM)
        dt = jnp.where(causal_t,
                       jnp.exp2(g["b2cols"][:, hh:hh + 1] - g["m2"][hh:hh + 1, :]), 0.0)
        sct = st * dt
        wi = g["w_inter"][hh:hh + 1, :]
        den = jnp.sum(sct, axis=0, keepdims=True) + wi * qn
        numt = jnp.dot(vt_ref[vsl, rs], sct.astype(BF16), preferred_element_type=F32) + wi * inter
        rden = 1.0 / jnp.maximum(jnp.abs(den), g["e_neg"][hh:hh + 1, :])
        ms = jnp.mean(numt * numt, axis=0, keepdims=True)
        scale = rden * lax.rsqrt(rden * rden * ms + NORM_EPS)
        hv = (numt * scale).T
        o_ref[rs, vsl] = (hv * hn_ref[:, vsl] * og_ref[rs, vsl].astype(F32)).astype(BF16)

    pending = [front(*u) for u in pairs[:MLSTM_LOOKAHEAD]]
    for idx, (ci, p) in enumerate(pairs):
        res = pending[idx]
        pending[idx] = None
        back(ci, p, 0, res)
        if idx + MLSTM_LOOKAHEAD < len(pairs):
            pending.append(front(*pairs[idx + MLSTM_LOOKAHEAD]))
        back(ci, p, 1, res)
    for p in range(nh // 2):
        ct_scr[p] = ct[p]
        n_scr[p] = nn[p]


def _mlstm(rows, cols, qk, v, og, hn, batch, seq):
    n = qk.shape[0]
    tm = TM_MLSTM
    nt = seq // tm
    row = pl.BlockSpec((tm, SZ_MV), lambda b, c: (b * nt + c, 0))
    return pl.pallas_call(
        _mlstm_kernel,
        grid=(batch, nt),
        in_specs=[
            pl.BlockSpec((GATE_ROWS, tm), lambda b, c: (0, b * nt + c)),
            pl.BlockSpec((tm, LANES), lambda b, c: (b * nt + c, 0)),
            row,
            pl.BlockSpec((SZ_MV, tm), lambda b, c: (0, b * nt + c)),
            row,
            _const_spec((1, SZ_MV)),
        ],
        out_specs=row,
        out_shape=jax.ShapeDtypeStruct((n, SZ_MV), BF16),
        scratch_shapes=[
            pltpu.VMEM((MLSTM_HEADS // 2, 2 * MLSTM_QK_DIM, MLSTM_V_DIM), F32),
            pltpu.VMEM((MLSTM_HEADS // 2, 1, 2 * MLSTM_QK_DIM), F32),
        ],
        compiler_params=pltpu.CompilerParams(
            dimension_semantics=("arbitrary", "arbitrary"), vmem_limit_bytes=VMEM_LIMIT_BYTES),
        name="mlstm",
    )(rows, cols, qk, v, og, hn)


def _merge_stage_weights(wpa_hbm, wpm_hbm, wo_hbm, wpa_ref, wpm_ref, wo_ref, stage, sem):
    hd = ATTN_HEAD_DIM
    wpa_copies = [pltpu.make_async_copy(wpa_hbm.at[pl.ds(head * hd, hd), :],
                                        stage.at[0, pl.ds(pos * hd, hd), :], sem.at[0])
                  for pos, head in enumerate(ATTN_HEAD_ORDER)]
    wpm_copy = pltpu.make_async_copy(wpm_hbm, stage.at[1], sem.at[1])
    wo_copy = pltpu.make_async_copy(wo_hbm, stage.at[0], sem.at[0])
    for cp in wpa_copies:
        cp.start()
    wpm_copy.start()
    for cp in wpa_copies:
        cp.wait()
    wpa_ref[...] = stage[0].astype(BF16)
    wo_copy.start()
    wpm_copy.wait()
    wpm_ref[...] = stage[1].astype(BF16)
    wo_copy.wait()
    wo_ref[...] = stage[0].astype(BF16)


def _merge_kernel(x_ref, a_ref, hm_ref, gate_ref, wpa_hbm, wpm_hbm, wo_hbm, o_ref,
                  wpa_ref, wpm_ref, wo_ref, stage, sem):
    @pl.when(pl.program_id(0) == 0)
    def _():
        _merge_stage_weights(wpa_hbm, wpm_hbm, wo_hbm, wpa_ref, wpm_ref, wo_ref, stage, sem)

    ya = jnp.dot(a_ref[...], wpa_ref[...], preferred_element_type=F32)
    ym = jnp.dot(hm_ref[...], wpm_ref[...], preferred_element_type=F32)
    ga = jax.nn.sigmoid(gate_ref[:, 0:D_MODEL].astype(F32))
    gm = jax.nn.sigmoid(gate_ref[:, D_MODEL:2 * D_MODEL].astype(F32))
    merged = (ga * ya + gm * ym).astype(BF16)
    o_ref[...] = x_ref[...] + jnp.dot(merged, wo_ref[...], preferred_element_type=F32)


def _merge(x, a, hm, gates, wpa, wpm, wo):
    n = x.shape[0]
    tm = TM_MERGE
    row = lambda width: pl.BlockSpec((tm, width), lambda i: (i, 0))
    return pl.pallas_call(
        _merge_kernel,
        grid=(n // tm,),
        in_specs=[row(D_MODEL), row(SZ_AQ), row(SZ_MV), row(2 * D_MODEL),
                  pl.BlockSpec(memory_space=pl.ANY), pl.BlockSpec(memory_space=pl.ANY),
                  pl.BlockSpec(memory_space=pl.ANY)],
        out_specs=row(D_MODEL),
        out_shape=jax.ShapeDtypeStruct((n, D_MODEL), F32),
        scratch_shapes=[pltpu.VMEM((D_MODEL, D_MODEL), BF16)] * 3 + [
            pltpu.VMEM((2, D_MODEL, D_MODEL), F32), pltpu.SemaphoreType.DMA((2,))],
        compiler_params=pltpu.CompilerParams(
            dimension_semantics=("arbitrary",), vmem_limit_bytes=VMEM_LIMIT_BYTES),
        name="merge",
    )(x, a, hm, gates, wpa, wpm, wo)


def kernel(x, ffn1_norm, ffn1_w_gate, ffn1_w_up, ffn1_w_down, mix_norm, w_in, b_in, attn_sinks,
           mlstm_conv, mlstm_head_norm, w_proj_attn, w_proj_mlstm, w_out, ffn2_norm, ffn2_w_gate,
           ffn2_w_up, ffn2_w_down, final_norm):
    batch, seq, d = x.shape
    assert d == D_MODEL and ffn1_norm.shape[0] == 1, "one layer of width D_MODEL"
    assert all(seq % t == 0 for t in (TM_INPROJ, TM_MLSTM, ATTN_BLOCKS * WINDOW))
    assert all((batch * seq) % t == 0 for t in (TM_FFN, TM_MERGE))
    n = batch * seq
    xf = x.reshape(n, d)
    fin = final_norm.reshape(1, d)

    x1 = _ffn(xf, ffn1_norm[0].reshape(1, d), ffn1_w_gate[0], ffn1_w_up[0], ffn1_w_down[0], fin,
              apply_final=False)

    q, kk, avt, mqk, mvt, mo, gates, gt = _inproj(x1, mix_norm[0].reshape(1, d), w_in[0], b_in[0],
                                                  mlstm_conv[0], seq)

    ya = _attention(attn_sinks[0], q, kk, avt, batch, seq)

    rows, cols = _mlstm_gates(gt, seq)
    hm = _mlstm(rows, cols, mqk, mvt, mo, mlstm_head_norm[0].reshape(1, SZ_MV), batch, seq)

    x2 = _merge(x1, ya, hm, gates, w_proj_attn[0], w_proj_mlstm[0], w_out[0])

    out = _ffn(x2, ffn2_norm[0].reshape(1, d), ffn2_w_gate[0], ffn2_w_up[0], ffn2_w_down[0], fin,
               apply_final=True)
    return out.reshape(batch, seq, d)
```

```python
import functools

import jax
import jax.numpy as jnp
from jax import lax
from jax.experimental import pallas as pl
from jax.experimental.pallas import tpu as pltpu

F32 = jnp.float32
BF16 = jnp.bfloat16

D_MODEL = 1024
ATTN_HEAD_DIM = 64
ATTN_HEADS = 16
ATTN_KV_HEADS = 4
ATTN_GROUP = 4
WINDOW = 128
MLSTM_HEADS = 8
MLSTM_V_DIM = 128
MLSTM_QK_DIM = 64
CONV_WIDTH = 4
GATE_SOFTCAP = 15.0
D_FF = 2816
FFN_RESIDUAL_WEIGHT = 0.5
NORM_EPS = 1e-6

SZ_AQ, SZ_AK, SZ_AV = 1024, 256, 256
SZ_MQ, SZ_MK, SZ_MV, SZ_MO = 512, 512, 1024, 1024
SZ_MI, SZ_MF = 8, 8

LANES = 128
SUBLANES = 8
MXU_DIM = 256
VMEM_LIMIT_BYTES = 56 * 1024 * 1024

MLSTM_CHUNK = 128
FF_CHUNK = MXU_DIM
FFN_STAGE_SLOTS = 3
TM_FFN = 1024
TM_INPROJ = 1024
TM_MERGE = 1024
TM_MLSTM = 8 * MLSTM_CHUNK
ATTN_LOOKAHEAD = 5
ATTN_BLOCKS = 16
MLSTM_LOOKAHEAD = 2

LOG2E = 1.4426950408889634
K_SCALE_LOG2 = 3.0
ROW_M2, ROW_WINTER, ROW_ENEG, ROW_WK, ROW_DECAY = 0, 8, 16, 24, 32
GATE_ROWS = 40

OFF_AQ = 0
OFF_AK = OFF_AQ + SZ_AQ
OFF_AV = OFF_AK + SZ_AK
OFF_MQ = OFF_AV + SZ_AV
OFF_MK = OFF_MQ + SZ_MQ
OFF_MV = OFF_MK + SZ_MK
OFF_MO = OFF_MV + SZ_MV
OFF_MI = OFF_MO + SZ_MO
OFF_MF = OFF_MI + SZ_MI
OFF_G = OFF_MF + SZ_MF
IN_WIDTH = OFF_G + 2 * D_MODEL
FEATURE_MAJOR_GROUPS = ((OFF_AQ, SZ_AQ), (OFF_AV, SZ_AV), (OFF_MV, SZ_MV), (OFF_MI, SZ_MI + SZ_MF))
FEATURE_MAJOR_ROWS = sum(size for _, size in FEATURE_MAJOR_GROUPS)
_FEATURE_MAJOR_BIAS_ROWS = tuple(
    (off, size, sum(s for _, s in FEATURE_MAJOR_GROUPS[:k]))
    for k, (off, size) in enumerate(FEATURE_MAJOR_GROUPS))
STAGE_ROWS = 256
STAGE_SLOTS = 4

ATTN_HEAD_ORDER = tuple((2 * (t // 4) + par) * ATTN_GROUP + t % 4
                        for t in range(ATTN_HEADS // 2) for par in range(2))


def _rms(x, g):
    return x * lax.rsqrt(jnp.mean(x * x, axis=-1, keepdims=True) + NORM_EPS) * g


def _const_spec(shape):
    zeros = (0,) * len(shape)
    return pl.BlockSpec(shape, lambda *_: zeros, pipeline_mode=pl.Buffered(1))


def _ffn_stage_weights(wg_hbm, wu_hbm, wd_hbm, wg_ref, wu_ref, wd_ref, gu_stage, d_stage, sem):
    n_chunks = D_FF // FF_CHUNK
    ahead = FFN_STAGE_SLOTS - 1

    def copies(c):
        slot = c % FFN_STAGE_SLOTS
        cols = pl.ds(c * FF_CHUNK, FF_CHUNK)
        return (pltpu.make_async_copy(wg_hbm.at[:, cols], gu_stage.at[slot, 0], sem.at[slot, 0]),
                pltpu.make_async_copy(wu_hbm.at[:, cols], gu_stage.at[slot, 1], sem.at[slot, 1]),
                pltpu.make_async_copy(wd_hbm.at[cols, :], d_stage.at[slot], sem.at[slot, 2]))

    for c in range(min(ahead, n_chunks)):
        for cp in copies(c):
            cp.start()
    for c in range(n_chunks):
        slot = c % FFN_STAGE_SLOTS
        if c + ahead < n_chunks:
            for cp in copies(c + ahead):
                cp.start()
        for cp in copies(c):
            cp.wait()
        sl = slice(c * FF_CHUNK, (c + 1) * FF_CHUNK)
        wg_ref[:, sl] = gu_stage[slot, 0].astype(BF16)
        wu_ref[:, sl] = gu_stage[slot, 1].astype(BF16)
        wd_ref[sl, :] = d_stage[slot].astype(BF16)


def _ffn_kernel(x_ref, g_ref, wg_hbm, wu_hbm, wd_hbm, fin_ref, o_ref,
                wg_ref, wu_ref, wd_ref, gu_stage, d_stage, sem, acc_ref, *, apply_final):
    @pl.when(pl.program_id(0) == 0)
    def _():
        _ffn_stage_weights(wg_hbm, wu_hbm, wd_hbm, wg_ref, wu_ref, wd_ref, gu_stage, d_stage, sem)

    x = x_ref[...]
    h = _rms(x, g_ref[...]).astype(BF16)
    for c in range(D_FF // FF_CHUNK):
        sl = slice(c * FF_CHUNK, (c + 1) * FF_CHUNK)
        a = jnp.dot(h, wg_ref[:, sl], preferred_element_type=F32)
        u = jnp.dot(h, wu_ref[:, sl], preferred_element_type=F32)
        act = (a * jax.nn.sigmoid(a) * u).astype(BF16)
        d = jnp.dot(act, wd_ref[sl, :], preferred_element_type=F32)
        if c == 0:
            acc_ref[...] = d
        else:
            acc_ref[...] += d
    y = x + FFN_RESIDUAL_WEIGHT * acc_ref[...]
    if apply_final:
        y = _rms(y, fin_ref[...])
    o_ref[...] = y


def _ffn(x, g, wg, wu, wd, fin, apply_final):
    n = x.shape[0]
    tm = TM_FFN
    return pl.pallas_call(
        functools.partial(_ffn_kernel, apply_final=apply_final),
        grid=(n // tm,),
        in_specs=[
            pl.BlockSpec((tm, D_MODEL), lambda i: (i, 0)),
            _const_spec((1, D_MODEL)),
            pl.BlockSpec(memory_space=pl.ANY),
            pl.BlockSpec(memory_space=pl.ANY),
            pl.BlockSpec(memory_space=pl.ANY),
            _const_spec((1, D_MODEL)),
        ],
        out_specs=pl.BlockSpec((tm, D_MODEL), lambda i: (i, 0)),
        out_shape=jax.ShapeDtypeStruct((n, D_MODEL), F32),
        scratch_shapes=[
            pltpu.VMEM((D_MODEL, D_FF), BF16),
            pltpu.VMEM((D_MODEL, D_FF), BF16),
            pltpu.VMEM((D_FF, D_MODEL), BF16),
            pltpu.VMEM((FFN_STAGE_SLOTS, 2, D_MODEL, FF_CHUNK), F32),
            pltpu.VMEM((FFN_STAGE_SLOTS, FF_CHUNK, D_MODEL), F32),
            pltpu.SemaphoreType.DMA((FFN_STAGE_SLOTS, 3)),
            pltpu.VMEM((tm, D_MODEL), F32),
        ],
        compiler_params=pltpu.CompilerParams(
            dimension_semantics=("arbitrary",), vmem_limit_bytes=VMEM_LIMIT_BYTES),
        name="ffn_final" if apply_final else "ffn",
    )(x, g, wg, wu, wd, fin)


def _inproj_stage_weights(w_hbm, w_ref, stage, sem):
    bounds = [(r0, min(STAGE_ROWS, IN_WIDTH - r0)) for r0 in range(0, IN_WIDTH, STAGE_ROWS)]

    def copy(c):
        r0, rows = bounds[c]
        slot = c % STAGE_SLOTS
        return pltpu.make_async_copy(w_hbm.at[pl.ds(r0, rows), :],
                                     stage.at[slot, pl.ds(0, rows), :], sem.at[slot])

    for c in range(min(STAGE_SLOTS - 1, len(bounds))):
        copy(c).start()
    for c, (r0, rows) in enumerate(bounds):
        if c + STAGE_SLOTS - 1 < len(bounds):
            copy(c + STAGE_SLOTS - 1).start()
        copy(c).wait()
        w_ref[r0:r0 + rows, :] = stage[c % STAGE_SLOTS, 0:rows, :].astype(BF16)


def _inproj_kernel(x_ref, g_ref, w_hbm, brow_ref, bgate_ref, bcol_ref, conv_ref,
                   q_ref, kk_ref, avt_ref, mqk_ref, mvt_ref, mo_ref, gate_ref, gt_ref,
                   w_ref, stage, sem, conv_scr, *, tiles_per_seq):
    i = pl.program_id(0)
    tm = x_ref.shape[0]

    @pl.when(i == 0)
    def _():
        _inproj_stage_weights(w_hbm, w_ref, stage, sem)

    h = _rms(x_ref[...], g_ref[...]).astype(BF16)
    contract_lanes = (((1,), (1,)), ((), ()))

    def seg(r0, width, bias):
        return lax.dot_general(h, w_ref[r0:r0 + width, :], contract_lanes,
                               preferred_element_type=F32) + bias

    def seg_t(r0, rows):
        c0 = next(base + r0 - off for off, size, base in _FEATURE_MAJOR_BIAS_ROWS
                  if off <= r0 < off + size)
        return (lax.dot_general(w_ref[r0:r0 + rows, :], h, contract_lanes,
                                preferred_element_type=F32) + bcol_ref[c0:c0 + rows, :])

    @pl.when(i == 0)
    def _():
        conv_scr[...] = jnp.zeros_like(conv_scr)

    carry = jnp.where(i % tiles_per_seq == 0, 0.0, conv_scr[...])

    def conv_slab(c, width):
        z = seg(OFF_MQ + c, width, brow_ref[:, OFF_MQ + c:OFF_MQ + c + width])
        conv_scr[:, c:c + width] = z[tm - SUBLANES:tm, :]
        zc = jnp.concatenate([carry[:, c:c + width], z], axis=0)
        acc = z * conv_ref[CONV_WIDTH - 1:CONV_WIDTH, c:c + width]
        for k in range(1, CONV_WIDTH):
            zk = pltpu.roll(zc, k, 0)[SUBLANES:, :]
            acc = acc + zk * conv_ref[CONV_WIDTH - 1 - k:CONV_WIDTH - k, c:c + width]
        mqk_ref[:, c:c + width] = (acc * jax.nn.sigmoid(acc)).astype(BF16)

    dw = MXU_DIM
    for k in range(SZ_AQ // dw):
        sl = slice(k * dw, (k + 1) * dw)
        conv_slab(k * dw, dw)
        q_ref[sl, :] = (seg_t(OFF_AQ + k * dw, dw) * (LOG2E * ATTN_HEAD_DIM ** -0.5)).astype(BF16)
        if k == 0:
            kk_ref[...] = seg(OFF_AK, SZ_AK, brow_ref[:, OFF_AK:OFF_AK + SZ_AK]).astype(BF16)
        elif k == 1:
            avt_ref[...] = seg_t(OFF_AV, SZ_AV).astype(BF16)
        mvt_ref[sl, :] = seg_t(OFF_MV + k * dw, dw).astype(BF16)
        mo = seg(OFF_MO + k * dw, dw, brow_ref[:, OFF_MO + k * dw:OFF_MO + (k + 1) * dw])
        mo_ref[:, sl] = jax.nn.sigmoid(mo).astype(BF16)
        gsl = slice(2 * k * dw, (2 * k + 2) * dw)
        gate_ref[:, gsl] = seg(OFF_G + 2 * k * dw, 2 * dw, bgate_ref[:, gsl]).astype(BF16)

    gt_ref[...] = seg_t(OFF_MI, SZ_MI + SZ_MF)


def _inproj(x, g, w_in, b_in, conv, seq):
    w = w_in.T
    b_row = b_in.reshape(1, IN_WIDTH)
    b_gate = b_in[OFF_G:].reshape(1, 2 * D_MODEL)
    b_col = jnp.concatenate([b_in[off:off + size] for off, size in FEATURE_MAJOR_GROUPS]).reshape(
        FEATURE_MAJOR_ROWS, 1)
    n = x.shape[0]
    tm = TM_INPROJ
    row = lambda width: pl.BlockSpec((tm, width), lambda i: (i, 0))
    col = lambda height: pl.BlockSpec((height, tm), lambda i: (0, i))
    bf = lambda width: jax.ShapeDtypeStruct((n, width), BF16)
    return pl.pallas_call(
        functools.partial(_inproj_kernel, tiles_per_seq=seq // tm),
        grid=(n // tm,),
        in_specs=[
            row(D_MODEL),
            _const_spec((1, D_MODEL)),
            pl.BlockSpec(memory_space=pl.ANY),
            _const_spec((1, IN_WIDTH)),
            _const_spec((1, 2 * D_MODEL)),
            _const_spec((FEATURE_MAJOR_ROWS, 1)),
            _const_spec((CONV_WIDTH, SZ_MQ + SZ_MK)),
        ],
        out_specs=[col(SZ_AQ), row(SZ_AK), col(SZ_AV), row(SZ_MQ + SZ_MK), col(SZ_MV), row(SZ_MO),
                   row(2 * D_MODEL), col(2 * MLSTM_HEADS)],
        out_shape=[jax.ShapeDtypeStruct((SZ_AQ, n), BF16), bf(SZ_AK),
                   jax.ShapeDtypeStruct((SZ_AV, n), BF16),
                   bf(SZ_MQ + SZ_MK), jax.ShapeDtypeStruct((SZ_MV, n), BF16), bf(SZ_MO),
                   bf(2 * D_MODEL), jax.ShapeDtypeStruct((2 * MLSTM_HEADS, n), F32)],
        scratch_shapes=[pltpu.VMEM((IN_WIDTH, D_MODEL), BF16),
                        pltpu.VMEM((STAGE_SLOTS, STAGE_ROWS, D_MODEL), F32),
                        pltpu.SemaphoreType.DMA((STAGE_SLOTS,)),
                        pltpu.VMEM((SUBLANES, SZ_MQ + SZ_MK), F32)],
        compiler_params=pltpu.CompilerParams(
            dimension_semantics=("arbitrary",), vmem_limit_bytes=VMEM_LIMIT_BYTES),
        name="inproj",
    )(x, g, w, b_row, b_gate, b_col, conv)


def _attn_kernel(sink_ref, q_ref, kc_ref, kp_ref, vtc_ref, vtp_ref, o_ref, k_scr, vt_scr):
    w = WINDOW
    k_scr[0:w, :] = kp_ref[...]
    k_scr[w:, :] = kc_ref[...]
    vt_scr[:, 0:w] = vtp_ref[...]
    vt_scr[:, w:] = vtc_ref[...]
    kj = lax.broadcasted_iota(jnp.int32, (2 * w, w), 0)
    qi = lax.broadcasted_iota(jnp.int32, (2 * w, w), 1)
    band = (kj > qi) & (kj <= qi + w)
    band_first = band & ((kj >= w) | (pl.program_id(1) > 0))
    pairs = [(jb, t) for jb in range(q_ref.shape[1] // w) for t in range(ATTN_HEADS // 2)]
    zeros_qt = jnp.zeros((ATTN_HEAD_DIM, w), BF16)

    def scores(jb, t):
        ha, hb = ATTN_HEAD_ORDER[2 * t], ATTN_HEAD_ORDER[2 * t + 1]
        qa = q_ref[ha * ATTN_HEAD_DIM:(ha + 1) * ATTN_HEAD_DIM, jb * w:(jb + 1) * w]
        qb = q_ref[hb * ATTN_HEAD_DIM:(hb + 1) * ATTN_HEAD_DIM, jb * w:(jb + 1) * w]
        qt2 = jnp.concatenate([jnp.concatenate([qa, zeros_qt], axis=0),
                               jnp.concatenate([zeros_qt, qb], axis=0)], axis=1)
        kk = k_scr[jb * w:(jb + 2) * w, (t // 4) * LANES:(t // 4 + 1) * LANES]
        return jnp.dot(kk, qt2, preferred_element_type=F32)

    def head_out(jb, t, par, s):
        s = jnp.where(band_first if jb == 0 else band, s, -jnp.inf)
        sink = sink_ref[ATTN_HEAD_ORDER[2 * t + par]] * LOG2E
        mx = jnp.maximum(jnp.max(s, axis=0, keepdims=True), sink)
        pr = jnp.exp2(s - mx)
        denom = jnp.sum(pr, axis=0, keepdims=True) + jnp.exp2(sink - mx)
        kvh = 2 * (t // 4) + par
        vt = vt_scr[kvh * ATTN_HEAD_DIM:(kvh + 1) * ATTN_HEAD_DIM, jb * w:(jb + 2) * w]
        return jnp.dot(vt, pr.astype(BF16), preferred_element_type=F32) * (1.0 / denom)

    pending = [scores(*u) for u in pairs[:ATTN_LOOKAHEAD]]
    for idx, (jb, t) in enumerate(pairs):
        s2 = pending[idx]
        pending[idx] = None
        outs = [head_out(jb, t, par, s2[:, par * w:(par + 1) * w]) for par in range(2)]
        if idx + ATTN_LOOKAHEAD < len(pairs):
            pending.append(scores(*pairs[idx + ATTN_LOOKAHEAD]))
        pair = jnp.concatenate(outs, axis=0)
        o_ref[jb * w:(jb + 1) * w, t * LANES:(t + 1) * LANES] = pair.T.astype(BF16)


def _attention(sinks, q, kk, vt, batch, seq):
    n = kk.shape[0]
    tq = ATTN_BLOCKS * WINDOW
    nt = seq // tq
    cur = lambda b, j: b * nt + j
    prev = lambda b, j: (b * nt + j) * ATTN_BLOCKS - jnp.minimum(j, 1)
    return pl.pallas_call(
        _attn_kernel,
        grid=(batch, nt),
        in_specs=[
            pl.BlockSpec(memory_space=pltpu.SMEM),
            pl.BlockSpec((SZ_AQ, tq), lambda b, j: (0, cur(b, j))),
            pl.BlockSpec((tq, SZ_AK), lambda b, j: (cur(b, j), 0)),
            pl.BlockSpec((WINDOW, SZ_AK), lambda b, j: (prev(b, j), 0)),
            pl.BlockSpec((SZ_AV, tq), lambda b, j: (0, cur(b, j))),
            pl.BlockSpec((SZ_AV, WINDOW), lambda b, j: (0, prev(b, j))),
        ],
        out_specs=pl.BlockSpec((tq, SZ_AQ), lambda b, j: (cur(b, j), 0)),
        out_shape=jax.ShapeDtypeStruct((n, SZ_AQ), BF16),
        scratch_shapes=[pltpu.VMEM((WINDOW + tq, SZ_AK), BF16),
                        pltpu.VMEM((SZ_AV, WINDOW + tq), BF16)],
        compiler_params=pltpu.CompilerParams(
            dimension_semantics=("arbitrary", "arbitrary"), vmem_limit_bytes=VMEM_LIMIT_BYTES),
        name="attention",
    )(sinks, q, kk, kk, vt, vt)


def _log_sigmoid(x):
    return -(jnp.maximum(-x, 0.0) + jnp.log(1.0 + jnp.exp(-jnp.abs(x))))


def _mlstm_gate_kernel(g_ref, rows_ref, cols_ref, *, seq):
    L = MLSTM_CHUNK
    nh = MLSTM_HEADS
    n = g_ref.shape[1]
    g = g_ref[...]
    ig = GATE_SOFTCAP * jnp.tanh(g[0:nh] / GATE_SOFTCAP)
    fg = GATE_SOFTCAP * jnp.tanh(g[nh:2 * nh] / GATE_SOFTCAP)
    lf = _log_sigmoid(fg)
    pos = lax.broadcasted_iota(jnp.int32, (nh, n), 1) & (L - 1)
    a = lf
    sh = 1
    while sh < L:
        a = a + jnp.where(pos >= sh, pltpu.roll(a, sh, 1), 0.0)
        sh *= 2
    bvec = ig - a
    pm = bvec
    sh = 1
    while sh < L:
        pm = jnp.maximum(pm, jnp.where(pos >= sh, pltpu.roll(pm, sh, 1), -jnp.inf))
        sh *= 2
    pad = jnp.zeros((L - nh, L), F32)
    m = [jnp.zeros((nh, L), F32) for _ in range(n // seq)]
    for c in range(seq // L):
        for s in range(n // seq):
            sl = slice(s * seq + c * L, s * seq + (c + 1) * L)
            a_c, b_c = a[:, sl], bvec[:, sl]
            mrow = jnp.maximum(m[s], pm[:, sl])
            total = jnp.broadcast_to(a_c[:, L - 1:L], (nh, L))
            mlast = jnp.broadcast_to(mrow[:, L - 1:L], (nh, L))
            rows_ref[ROW_M2:ROW_M2 + nh, sl] = mrow * LOG2E
            rows_ref[ROW_WINTER:ROW_WINTER + nh, sl] = jnp.exp(m[s] - mrow)
            rows_ref[ROW_ENEG:ROW_ENEG + nh, sl] = jnp.exp(-(a_c + mrow))
            rows_ref[ROW_WK:ROW_WK + nh, sl] = jnp.exp(b_c - mlast) * MLSTM_QK_DIM ** -0.5
            rows_ref[ROW_DECAY:ROW_DECAY + nh, sl] = jnp.exp(m[s] - mlast)
            m[s] = total + mlast
            cols_ref[sl, :] = jnp.concatenate([b_c * LOG2E - K_SCALE_LOG2, pad], axis=0).T[:, 0:nh]


def _mlstm_gates(gt, seq):
    n = gt.shape[1]
    return pl.pallas_call(
        functools.partial(_mlstm_gate_kernel, seq=seq),
        grid=(1,),
        in_specs=[pl.BlockSpec((2 * MLSTM_HEADS, n), lambda i: (0, 0))],
        out_specs=[pl.BlockSpec((GATE_ROWS, n), lambda i: (0, 0)),
                   pl.BlockSpec((n, MLSTM_HEADS), lambda i: (0, 0))],
        out_shape=[jax.ShapeDtypeStruct((GATE_ROWS, n), F32),
                   jax.ShapeDtypeStruct((n, MLSTM_HEADS), F32)],
        compiler_params=pltpu.CompilerParams(
            dimension_semantics=("arbitrary",), vmem_limit_bytes=VMEM_LIMIT_BYTES),
        name="mlstm_gates",
    )(gt)


def _mlstm_kernel(rows_ref, cols_ref, qk_ref, vt_ref, og_ref, hn_ref, o_ref, ct_scr, n_scr):
    L = MLSTM_CHUNK
    nh = MLSTM_HEADS
    contract_lanes = (((1,), (1,)), ((), ()))

    @pl.when(pl.program_id(1) == 0)
    def _():
        ct_scr[...] = jnp.zeros_like(ct_scr)
        n_scr[...] = jnp.zeros_like(n_scr)

    row_i = lax.broadcasted_iota(jnp.int32, (L, L), 0)
    col_i = lax.broadcasted_iota(jnp.int32, (L, L), 1)
    causal_t = row_i <= col_i
    lo_lane = col_i < MLSTM_QK_DIM
    zero = jnp.zeros((), BF16)

    n_chunks = qk_ref.shape[0] // L
    pairs = [(ci, p) for ci in range(n_chunks) for p in range(nh // 2)]
    ct = [ct_scr[p] for p in range(nh // 2)]
    nn = [n_scr[p] for p in range(nh // 2)]
    gate_rows = {}

    def chunk_gates(ci):
        if ci not in gate_rows:
            rs = slice(ci * L, (ci + 1) * L)
            wk = rows_ref[ROW_WK:ROW_WK + nh, rs]
            gate_rows[ci] = dict(
                m2=rows_ref[ROW_M2:ROW_M2 + nh, rs], w_inter=rows_ref[ROW_WINTER:ROW_WINTER + nh, rs],
                e_neg=rows_ref[ROW_ENEG:ROW_ENEG + nh, rs], wk=wk,
                decay=rows_ref[ROW_DECAY:ROW_DECAY + nh, rs], b2cols=cols_ref[rs, :],
                wk_b=jnp.concatenate([wk, wk], axis=0).astype(BF16))
        return gate_rows[ci]

    def front(ci, p):
        g = chunk_gates(ci)
        rs = slice(ci * L, (ci + 1) * L)
        h0, h1 = 2 * p, 2 * p + 1
        q2 = qk_ref[rs, p * LANES:(p + 1) * LANES]
        k2 = qk_ref[rs, SZ_MQ + p * LANES:SZ_MQ + (p + 1) * LANES]
        qm2 = jnp.concatenate([jnp.where(lo_lane, q2, zero), jnp.where(lo_lane, zero, q2)], axis=0)
        n2b = jnp.broadcast_to(nn[p], (2 * SUBLANES, LANES)).astype(BF16)
        stacked = jnp.concatenate([k2, ct[p].astype(BF16), n2b], axis=0)
        res = lax.dot_general(stacked, qm2, contract_lanes, preferred_element_type=F32)
        vw = [(vt_ref[hh * MLSTM_V_DIM:(hh + 1) * MLSTM_V_DIM, rs].astype(F32)
               * g["wk"][hh:hh + 1, :]).astype(BF16) for hh in (h0, h1)]
        upd = jnp.dot(jnp.concatenate(vw + [g["wk_b"]], axis=0), k2,
                      preferred_element_type=F32)
        n_inc = upd[2 * MLSTM_V_DIM:]
        dec = jnp.where(lo_lane[0:1, :], g["decay"][h0:h0 + 1, :], g["decay"][h1:h1 + 1, :])
        ct[p] = dec * ct[p] + jnp.where(lo_lane, upd[0:MLSTM_V_DIM], upd[MLSTM_V_DIM:2 * MLSTM_V_DIM])
        nn[p] = dec * nn[p] + jnp.where(lo_lane[0:1, :], n_inc[h0:h0 + 1, :], n_inc[h1:h1 + 1, :])
        return res

    def back(ci, p, par, res):
        st = res[0:L, par * L:(par + 1) * L]
        inter = res[L:L + MLSTM_V_DIM, par * L:(par + 1) * L]
        qn = res[L + MLSTM_V_DIM:L + MLSTM_V_DIM + 1, par * L:(par + 1) * L]
        g = chunk_gates(ci)
        rs = slice(ci * L, (ci + 1) * L)
        hh = 2 * p + par
        vsl = slice(hh * MLSTM_V_DIM, (hh + 1) * MLSTM_V_DIM)
        dt = jnp.where(causal_t,
                       jnp.exp2(g["b2cols"][:, hh:hh + 1] - g["m2"][hh:hh + 1, :]), 0.0)
        sct = st * dt
        wi = g["w_inter"][hh:hh + 1, :]
        den = jnp.sum(sct, axis=0, keepdims=True) + wi * qn
        numt = jnp.dot(vt_ref[vsl, rs], sct.astype(BF16), preferred_element_type=F32) + wi * inter
        rden = 1.0 / jnp.maximum(jnp.abs(den), g["e_neg"][hh:hh + 1, :])
        ms = jnp.mean(numt * numt, axis=0, keepdims=True)
        scale = rden * lax.rsqrt(rden * rden * ms + NORM_EPS)
        hv = (numt * scale).T
        o_ref[rs, vsl] = (hv * hn_ref[:, vsl] * og_ref[rs, vsl].astype(F32)).astype(BF16)

    pending = [front(*u) for u in pairs[:MLSTM_LOOKAHEAD]]
    for idx, (ci, p) in enumerate(pairs):
        res = pending[idx]
        pending[idx] = None
        back(ci, p, 0, res)
        if idx + MLSTM_LOOKAHEAD < len(pairs):
            pending.append(front(*pairs[idx + MLSTM_LOOKAHEAD]))
        back(ci, p, 1, res)
    for p in range(nh // 2):
        ct_scr[p] = ct[p]
        n_scr[p] = nn[p]


def _mlstm(rows, cols, qk, v, og, hn, batch, seq):
    n = qk.shape[0]
    tm = TM_MLSTM
    nt = seq // tm
    row = pl.BlockSpec((tm, SZ_MV), lambda b, c: (b * nt + c, 0))
    return pl.pallas_call(
        _mlstm_kernel,
        grid=(batch, nt),
        in_specs=[
            pl.BlockSpec((GATE_ROWS, tm), lambda b, c: (0, b * nt + c)),
            pl.BlockSpec((tm, MLSTM_HEADS), lambda b, c: (b * nt + c, 0)),
            row,
            pl.BlockSpec((SZ_MV, tm), lambda b, c: (0, b * nt + c)),
            row,
            _const_spec((1, SZ_MV)),
        ],
        out_specs=row,
        out_shape=jax.ShapeDtypeStruct((n, SZ_MV), BF16),
        scratch_shapes=[
            pltpu.VMEM((MLSTM_HEADS // 2, 2 * MLSTM_QK_DIM, MLSTM_V_DIM), F32),
            pltpu.VMEM((MLSTM_HEADS // 2, 1, 2 * MLSTM_QK_DIM), F32),
        ],
        compiler_params=pltpu.CompilerParams(
            dimension_semantics=("arbitrary", "arbitrary"), vmem_limit_bytes=VMEM_LIMIT_BYTES),
        name="mlstm",
    )(rows, cols, qk, v, og, hn)


def _merge_stage_weights(wpa_hbm, wpm_hbm, wo_hbm, wpa_ref, wpm_ref, wo_ref, stage, sem):
    hd = ATTN_HEAD_DIM
    wpa_copies = [pltpu.make_async_copy(wpa_hbm.at[pl.ds(head * hd, hd), :],
                                        stage.at[0, pl.ds(pos * hd, hd), :], sem.at[0])
                  for pos, head in enumerate(ATTN_HEAD_ORDER)]
    wpm_copy = pltpu.make_async_copy(wpm_hbm, stage.at[1], sem.at[1])
    wo_copy = pltpu.make_async_copy(wo_hbm, stage.at[0], sem.at[0])
    for cp in wpa_copies:
        cp.start()
    wpm_copy.start()
    for cp in wpa_copies:
        cp.wait()
    wpa_ref[...] = stage[0].astype(BF16)
    wo_copy.start()
    wpm_copy.wait()
    wpm_ref[...] = stage[1].astype(BF16)
    wo_copy.wait()
    wo_ref[...] = stage[0].astype(BF16)


def _merge_kernel(x_ref, a_ref, hm_ref, gate_ref, wpa_hbm, wpm_hbm, wo_hbm, o_ref,
                  wpa_ref, wpm_ref, wo_ref, stage, sem):
    @pl.when(pl.program_id(0) == 0)
    def _():
        _merge_stage_weights(wpa_hbm, wpm_hbm, wo_hbm, wpa_ref, wpm_ref, wo_ref, stage, sem)

    ya = jnp.dot(a_ref[...], wpa_ref[...], preferred_element_type=F32)
    ym = jnp.dot(hm_ref[...], wpm_ref[...], preferred_element_type=F32)
    ga = jax.nn.sigmoid(gate_ref[:, 0:D_MODEL].astype(F32))
    gm = jax.nn.sigmoid(gate_ref[:, D_MODEL:2 * D_MODEL].astype(F32))
    merged = (ga * ya + gm * ym).astype(BF16)
    o_ref[...] = x_ref[...] + jnp.dot(merged, wo_ref[...], preferred_element_type=F32)


def _merge(x, a, hm, gates, wpa, wpm, wo):
    n = x.shape[0]
    tm = TM_MERGE
    row = lambda width: pl.BlockSpec((tm, width), lambda i: (i, 0))
    return pl.pallas_call(
        _merge_kernel,
        grid=(n // tm,),
        in_specs=[row(D_MODEL), row(SZ_AQ), row(SZ_MV), row(2 * D_MODEL),
                  pl.BlockSpec(memory_space=pl.ANY), pl.BlockSpec(memory_space=pl.ANY),
                  pl.BlockSpec(memory_space=pl.ANY)],
        out_specs=row(D_MODEL),
        out_shape=jax.ShapeDtypeStruct((n, D_MODEL), F32),
        scratch_shapes=[pltpu.VMEM((D_MODEL, D_MODEL), BF16)] * 3 + [
            pltpu.VMEM((2, D_MODEL, D_MODEL), F32), pltpu.SemaphoreType.DMA((2,))],
        compiler_params=pltpu.CompilerParams(
            dimension_semantics=("arbitrary",), vmem_limit_bytes=VMEM_LIMIT_BYTES),
        name="merge",
    )(x, a, hm, gates, wpa, wpm, wo)


def kernel(x, ffn1_norm, ffn1_w_gate, ffn1_w_up, ffn1_w_down, mix_norm, w_in, b_in, attn_sinks,
           mlstm_conv, mlstm_head_norm, w_proj_attn, w_proj_mlstm, w_out, ffn2_norm, ffn2_w_gate,
           ffn2_w_up, ffn2_w_down, final_norm):
    batch, seq, d = x.shape
    assert d == D_MODEL and ffn1_norm.shape[0] == 1, "one layer of width D_MODEL"
    assert all(seq % t == 0 for t in (TM_INPROJ, TM_MLSTM, ATTN_BLOCKS * WINDOW))
    assert all((batch * seq) % t == 0 for t in (TM_FFN, TM_MERGE))
    n = batch * seq
    xf = x.reshape(n, d)
    fin = final_norm.reshape(1, d)

    x1 = _ffn(xf, ffn1_norm[0].reshape(1, d), ffn1_w_gate[0], ffn1_w_up[0], ffn1_w_down[0], fin,
              apply_final=False)

    q, kk, avt, mqk, mvt, mo, gates, gt = _inproj(x1, mix_norm[0].reshape(1, d), w_in[0], b_in[0],
                                                  mlstm_conv[0], seq)

    ya = _attention(attn_sinks[0], q, kk, avt, batch, seq)

    rows, cols = _mlstm_gates(gt, seq)
    hm = _mlstm(rows, cols, mqk, mvt, mo, mlstm_head_norm[0].reshape(1, SZ_MV), batch, seq)

    x2 = _merge(x1, ya, hm, gates, w_proj_attn[0], w_proj_mlstm[0], w_out[0])

    out = _ffn(x2, ffn2_norm[0].reshape(1, d), ffn2_w_gate[0], ffn2_w_up[0], ffn2_w_down[0], fin,
               apply_final=True)
    return out.reshape(batch, seq, d)
```

```python
import functools

import jax
import jax.numpy as jnp
from jax import lax
from jax.experimental import pallas as pl
from jax.experimental.pallas import tpu as pltpu

F32 = jnp.float32
BF16 = jnp.bfloat16

D_MODEL = 1024
ATTN_HEAD_DIM = 64
ATTN_HEADS = 16
ATTN_KV_HEADS = 4
ATTN_GROUP = 4
WINDOW = 128
MLSTM_HEADS = 8
MLSTM_V_DIM = 128
MLSTM_QK_DIM = 64
CONV_WIDTH = 4
GATE_SOFTCAP = 15.0
D_FF = 2816
FFN_RESIDUAL_WEIGHT = 0.5
NORM_EPS = 1e-6

SZ_AQ, SZ_AK, SZ_AV = 1024, 256, 256
SZ_MQ, SZ_MK, SZ_MV, SZ_MO = 512, 512, 1024, 1024
SZ_MI, SZ_MF = 8, 8

LANES = 128
SUBLANES = 8
MXU_DIM = 256
VMEM_LIMIT_BYTES = 56 * 1024 * 1024

MLSTM_CHUNK = 128
FF_CHUNK = MXU_DIM
FFN_STAGE_SLOTS = 3
TM_FFN = 1024
TM_INPROJ = 1024
TM_MERGE = 1024
TM_MLSTM = 8 * MLSTM_CHUNK
ATTN_LOOKAHEAD = 5
ATTN_BLOCKS = 16
MLSTM_LOOKAHEAD = 2

LOG2E = 1.4426950408889634
K_SCALE_LOG2 = 3.0
ROW_M2, ROW_WINTER, ROW_ENEG, ROW_WK, ROW_DECAY = 0, 8, 16, 24, 32
GATE_ROWS = 40

OFF_AQ = 0
OFF_AK = OFF_AQ + SZ_AQ
OFF_AV = OFF_AK + SZ_AK
OFF_MQ = OFF_AV + SZ_AV
OFF_MK = OFF_MQ + SZ_MQ
OFF_MV = OFF_MK + SZ_MK
OFF_MO = OFF_MV + SZ_MV
OFF_MI = OFF_MO + SZ_MO
OFF_MF = OFF_MI + SZ_MI
OFF_G = OFF_MF + SZ_MF
IN_WIDTH = OFF_G + 2 * D_MODEL
FEATURE_MAJOR_GROUPS = ((OFF_AQ, SZ_AQ), (OFF_AV, SZ_AV), (OFF_MV, SZ_MV), (OFF_MI, SZ_MI + SZ_MF))
FEATURE_MAJOR_ROWS = sum(size for _, size in FEATURE_MAJOR_GROUPS)
_FEATURE_MAJOR_BIAS_ROWS = tuple(
    (off, size, sum(s for _, s in FEATURE_MAJOR_GROUPS[:k]))
    for k, (off, size) in enumerate(FEATURE_MAJOR_GROUPS))
STAGE_ROWS = 256
STAGE_SLOTS = 4

ATTN_HEAD_ORDER = tuple((2 * (t // 4) + par) * ATTN_GROUP + t % 4
                        for t in range(ATTN_HEADS // 2) for par in range(2))


def _rms(x, g):
    return x * lax.rsqrt(jnp.mean(x * x, axis=-1, keepdims=True) + NORM_EPS) * g


def _const_spec(shape):
    zeros = (0,) * len(shape)
    return pl.BlockSpec(shape, lambda *_: zeros, pipeline_mode=pl.Buffered(1))


def _ffn_stage_weights(wg_hbm, wu_hbm, wd_hbm, wg_ref, wu_ref, wd_ref, gu_stage, d_stage, sem):
    n_chunks = D_FF // FF_CHUNK
    ahead = FFN_STAGE_SLOTS - 1

    def copies(c):
        slot = c % FFN_STAGE_SLOTS
        cols = pl.ds(c * FF_CHUNK, FF_CHUNK)
        return (pltpu.make_async_copy(wg_hbm.at[:, cols], gu_stage.at[slot, 0], sem.at[slot, 0]),
                pltpu.make_async_copy(wu_hbm.at[:, cols], gu_stage.at[slot, 1], sem.at[slot, 1]),
                pltpu.make_async_copy(wd_hbm.at[cols, :], d_stage.at[slot], sem.at[slot, 2]))

    for c in range(min(ahead, n_chunks)):
        for cp in copies(c):
            cp.start()
    for c in range(n_chunks):
        slot = c % FFN_STAGE_SLOTS
        if c + ahead < n_chunks:
            for cp in copies(c + ahead):
                cp.start()
        for cp in copies(c):
            cp.wait()
        sl = slice(c * FF_CHUNK, (c + 1) * FF_CHUNK)
        wg_ref[:, sl] = gu_stage[slot, 0].astype(BF16)
        wu_ref[:, sl] = gu_stage[slot, 1].astype(BF16)
        wd_ref[sl, :] = d_stage[slot].astype(BF16)


def _ffn_kernel(x_ref, g_ref, wg_hbm, wu_hbm, wd_hbm, fin_ref, o_ref,
                wg_ref, wu_ref, wd_ref, gu_stage, d_stage, sem, *, apply_final):
    @pl.when(pl.program_id(0) == 0)
    def _():
        _ffn_stage_weights(wg_hbm, wu_hbm, wd_hbm, wg_ref, wu_ref, wd_ref, gu_stage, d_stage, sem)

    x = x_ref[...]
    h = _rms(x, g_ref[...]).astype(BF16)
    for c in range(D_FF // FF_CHUNK):
        sl = slice(c * FF_CHUNK, (c + 1) * FF_CHUNK)
        a = jnp.dot(h, wg_ref[:, sl], preferred_element_type=F32)
        u = jnp.dot(h, wu_ref[:, sl], preferred_element_type=F32)
        act = (a * jax.nn.sigmoid(a) * (u * FFN_RESIDUAL_WEIGHT)).astype(BF16)
        d = jnp.dot(act, wd_ref[sl, :], preferred_element_type=F32)
        if c == 0:
            o_ref[...] = x + d
        else:
            o_ref[...] += d
    if apply_final:
        o_ref[...] = _rms(o_ref[...], fin_ref[...])


def _ffn(x, g, wg, wu, wd, fin, apply_final):
    n = x.shape[0]
    tm = TM_FFN
    return pl.pallas_call(
        functools.partial(_ffn_kernel, apply_final=apply_final),
        grid=(n // tm,),
        in_specs=[
            pl.BlockSpec((tm, D_MODEL), lambda i: (i, 0)),
            _const_spec((1, D_MODEL)),
            pl.BlockSpec(memory_space=pl.ANY),
            pl.BlockSpec(memory_space=pl.ANY),
            pl.BlockSpec(memory_space=pl.ANY),
            _const_spec((1, D_MODEL)),
        ],
        out_specs=pl.BlockSpec((tm, D_MODEL), lambda i: (i, 0)),
        out_shape=jax.ShapeDtypeStruct((n, D_MODEL), F32),
        scratch_shapes=[
            pltpu.VMEM((D_MODEL, D_FF), BF16),
            pltpu.VMEM((D_MODEL, D_FF), BF16),
            pltpu.VMEM((D_FF, D_MODEL), BF16),
            pltpu.VMEM((FFN_STAGE_SLOTS, 2, D_MODEL, FF_CHUNK), F32),
            pltpu.VMEM((FFN_STAGE_SLOTS, FF_CHUNK, D_MODEL), F32),
            pltpu.SemaphoreType.DMA((FFN_STAGE_SLOTS, 3)),
        ],
        compiler_params=pltpu.CompilerParams(
            dimension_semantics=("arbitrary",), vmem_limit_bytes=VMEM_LIMIT_BYTES),
        name="ffn_final" if apply_final else "ffn",
    )(x, g, wg, wu, wd, fin)


def _inproj_stage_weights(w_hbm, w_ref, stage, sem):
    bounds = [(r0, min(STAGE_ROWS, IN_WIDTH - r0)) for r0 in range(0, IN_WIDTH, STAGE_ROWS)]

    def copy(c):
        r0, rows = bounds[c]
        slot = c % STAGE_SLOTS
        return pltpu.make_async_copy(w_hbm.at[pl.ds(r0, rows), :],
                                     stage.at[slot, pl.ds(0, rows), :], sem.at[slot])

    for c in range(min(STAGE_SLOTS - 1, len(bounds))):
        copy(c).start()
    for c, (r0, rows) in enumerate(bounds):
        if c + STAGE_SLOTS - 1 < len(bounds):
            copy(c + STAGE_SLOTS - 1).start()
        copy(c).wait()
        w_ref[r0:r0 + rows, :] = stage[c % STAGE_SLOTS, 0:rows, :].astype(BF16)


def _inproj_kernel(x_ref, g_ref, w_hbm, brow_ref, bgate_ref, bcol_ref, conv_ref,
                   q_ref, kk_ref, avt_ref, mqk_ref, mvt_ref, mo_ref, gate_ref, gt_ref,
                   w_ref, stage, sem, conv_scr, *, tiles_per_seq):
    i = pl.program_id(0)
    tm = x_ref.shape[0]

    @pl.when(i == 0)
    def _():
        _inproj_stage_weights(w_hbm, w_ref, stage, sem)

    h = _rms(x_ref[...], g_ref[...]).astype(BF16)
    contract_lanes = (((1,), (1,)), ((), ()))

    def seg(r0, width, bias):
        return lax.dot_general(h, w_ref[r0:r0 + width, :], contract_lanes,
                               preferred_element_type=F32) + bias

    def seg_t(r0, rows):
        c0 = next(base + r0 - off for off, size, base in _FEATURE_MAJOR_BIAS_ROWS
                  if off <= r0 < off + size)
        return (lax.dot_general(w_ref[r0:r0 + rows, :], h, contract_lanes,
                                preferred_element_type=F32) + bcol_ref[c0:c0 + rows, :])

    @pl.when(i == 0)
    def _():
        conv_scr[...] = jnp.zeros_like(conv_scr)

    carry = jnp.where(i % tiles_per_seq == 0, 0.0, conv_scr[...])

    def conv_slab(c, width):
        z = seg(OFF_MQ + c, width, brow_ref[:, OFF_MQ + c:OFF_MQ + c + width])
        conv_scr[:, c:c + width] = z[tm - SUBLANES:tm, :]
        zc = jnp.concatenate([carry[:, c:c + width], z], axis=0)
        acc = z * conv_ref[CONV_WIDTH - 1:CONV_WIDTH, c:c + width]
        for k in range(1, CONV_WIDTH):
            zk = pltpu.roll(zc, k, 0)[SUBLANES:, :]
            acc = acc + zk * conv_ref[CONV_WIDTH - 1 - k:CONV_WIDTH - k, c:c + width]
        mqk_ref[:, c:c + width] = (acc * jax.nn.sigmoid(acc)).astype(BF16)

    dw = MXU_DIM
    for k in range(SZ_AQ // dw):
        sl = slice(k * dw, (k + 1) * dw)
        conv_slab(k * dw, dw)
        q_ref[sl, :] = (seg_t(OFF_AQ + k * dw, dw) * (LOG2E * ATTN_HEAD_DIM ** -0.5)).astype(BF16)
        if k == 0:
            kk_ref[...] = seg(OFF_AK, SZ_AK, brow_ref[:, OFF_AK:OFF_AK + SZ_AK]).astype(BF16)
        elif k == 1:
            avt_ref[...] = seg_t(OFF_AV, SZ_AV).astype(BF16)
        mvt_ref[sl, :] = seg_t(OFF_MV + k * dw, dw).astype(BF16)
        mo = seg(OFF_MO + k * dw, dw, brow_ref[:, OFF_MO + k * dw:OFF_MO + (k + 1) * dw])
        mo_ref[:, sl] = jax.nn.sigmoid(mo).astype(BF16)
        gsl = slice(2 * k * dw, (2 * k + 2) * dw)
        gate_ref[:, gsl] = seg(OFF_G + 2 * k * dw, 2 * dw, bgate_ref[:, gsl]).astype(BF16)

    gt_ref[...] = seg_t(OFF_MI, SZ_MI + SZ_MF)


def _inproj(x, g, w_in, b_in, conv, seq):
    w = w_in.T
    b_row = b_in.reshape(1, IN_WIDTH)
    b_gate = b_in[OFF_G:].reshape(1, 2 * D_MODEL)
    b_col = jnp.concatenate([b_in[off:off + size] for off, size in FEATURE_MAJOR_GROUPS]).reshape(
        FEATURE_MAJOR_ROWS, 1)
    n = x.shape[0]
    tm = TM_INPROJ
    row = lambda width: pl.BlockSpec((tm, width), lambda i: (i, 0))
    col = lambda height: pl.BlockSpec((height, tm), lambda i: (0, i))
    bf = lambda width: jax.ShapeDtypeStruct((n, width), BF16)
    return pl.pallas_call(
        functools.partial(_inproj_kernel, tiles_per_seq=seq // tm),
        grid=(n // tm,),
        in_specs=[
            row(D_MODEL),
            _const_spec((1, D_MODEL)),
            pl.BlockSpec(memory_space=pl.ANY),
            _const_spec((1, IN_WIDTH)),
            _const_spec((1, 2 * D_MODEL)),
            _const_spec((FEATURE_MAJOR_ROWS, 1)),
            _const_spec((CONV_WIDTH, SZ_MQ + SZ_MK)),
        ],
        out_specs=[col(SZ_AQ), row(SZ_AK), col(SZ_AV), row(SZ_MQ + SZ_MK), col(SZ_MV), row(SZ_MO),
                   row(2 * D_MODEL), col(2 * MLSTM_HEADS)],
        out_shape=[jax.ShapeDtypeStruct((SZ_AQ, n), BF16), bf(SZ_AK),
                   jax.ShapeDtypeStruct((SZ_AV, n), BF16),
                   bf(SZ_MQ + SZ_MK), jax.ShapeDtypeStruct((SZ_MV, n), BF16), bf(SZ_MO),
                   bf(2 * D_MODEL), jax.ShapeDtypeStruct((2 * MLSTM_HEADS, n), F32)],
        scratch_shapes=[pltpu.VMEM((IN_WIDTH, D_MODEL), BF16),
                        pltpu.VMEM((STAGE_SLOTS, STAGE_ROWS, D_MODEL), F32),
                        pltpu.SemaphoreType.DMA((STAGE_SLOTS,)),
                        pltpu.VMEM((SUBLANES, SZ_MQ + SZ_MK), F32)],
        compiler_params=pltpu.CompilerParams(
            dimension_semantics=("arbitrary",), vmem_limit_bytes=VMEM_LIMIT_BYTES),
        name="inproj",
    )(x, g, w, b_row, b_gate, b_col, conv)


def _attn_kernel(sink_ref, q_ref, kc_ref, kp_ref, vtc_ref, vtp_ref, o_ref, k_scr, vt_scr):
    w = WINDOW
    k_scr[0:w, :] = kp_ref[...]
    k_scr[w:, :] = kc_ref[...]
    vt_scr[:, 0:w] = vtp_ref[...]
    vt_scr[:, w:] = vtc_ref[...]
    kj = lax.broadcasted_iota(jnp.int32, (2 * w, w), 0)
    qi = lax.broadcasted_iota(jnp.int32, (2 * w, w), 1)
    band = (kj > qi) & (kj <= qi + w)
    band_first = band & ((kj >= w) | (pl.program_id(1) > 0))
    pairs = [(jb, t) for jb in range(q_ref.shape[1] // w) for t in range(ATTN_HEADS // 2)]
    zeros_qt = jnp.zeros((ATTN_HEAD_DIM, w), BF16)

    def scores(jb, t):
        ha, hb = ATTN_HEAD_ORDER[2 * t], ATTN_HEAD_ORDER[2 * t + 1]
        qa = q_ref[ha * ATTN_HEAD_DIM:(ha + 1) * ATTN_HEAD_DIM, jb * w:(jb + 1) * w]
        qb = q_ref[hb * ATTN_HEAD_DIM:(hb + 1) * ATTN_HEAD_DIM, jb * w:(jb + 1) * w]
        qt2 = jnp.concatenate([jnp.concatenate([qa, zeros_qt], axis=0),
                               jnp.concatenate([zeros_qt, qb], axis=0)], axis=1)
        kk = k_scr[jb * w:(jb + 2) * w, (t // 4) * LANES:(t // 4 + 1) * LANES]
        return jnp.dot(kk, qt2, preferred_element_type=F32)

    def head_out(jb, t, par, s):
        s = jnp.where(band_first if jb == 0 else band, s, -jnp.inf)
        sink = sink_ref[ATTN_HEAD_ORDER[2 * t + par]] * LOG2E
        mx = jnp.maximum(jnp.max(s, axis=0, keepdims=True), sink)
        pr = jnp.exp2(s - mx)
        denom = jnp.sum(pr, axis=0, keepdims=True) + jnp.exp2(sink - mx)
        kvh = 2 * (t // 4) + par
        vt = vt_scr[kvh * ATTN_HEAD_DIM:(kvh + 1) * ATTN_HEAD_DIM, jb * w:(jb + 2) * w]
        return jnp.dot(vt, pr.astype(BF16), preferred_element_type=F32) * (1.0 / denom)

    pending = [scores(*u) for u in pairs[:ATTN_LOOKAHEAD]]
    for idx, (jb, t) in enumerate(pairs):
        s2 = pending[idx]
        pending[idx] = None
        outs = [head_out(jb, t, par, s2[:, par * w:(par + 1) * w]) for par in range(2)]
        if idx + ATTN_LOOKAHEAD < len(pairs):
            pending.append(scores(*pairs[idx + ATTN_LOOKAHEAD]))
        pair = jnp.concatenate(outs, axis=0)
        o_ref[jb * w:(jb + 1) * w, t * LANES:(t + 1) * LANES] = pair.T.astype(BF16)


def _attention(sinks, q, kk, vt, batch, seq):
    n = kk.shape[0]
    tq = ATTN_BLOCKS * WINDOW
    nt = seq // tq
    cur = lambda b, j: b * nt + j
    prev = lambda b, j: (b * nt + j) * ATTN_BLOCKS - jnp.minimum(j, 1)
    return pl.pallas_call(
        _attn_kernel,
        grid=(batch, nt),
        in_specs=[
            pl.BlockSpec(memory_space=pltpu.SMEM),
            pl.BlockSpec((SZ_AQ, tq), lambda b, j: (0, cur(b, j))),
            pl.BlockSpec((tq, SZ_AK), lambda b, j: (cur(b, j), 0)),
            pl.BlockSpec((WINDOW, SZ_AK), lambda b, j: (prev(b, j), 0)),
            pl.BlockSpec((SZ_AV, tq), lambda b, j: (0, cur(b, j))),
            pl.BlockSpec((SZ_AV, WINDOW), lambda b, j: (0, prev(b, j))),
        ],
        out_specs=pl.BlockSpec((tq, SZ_AQ), lambda b, j: (cur(b, j), 0)),
        out_shape=jax.ShapeDtypeStruct((n, SZ_AQ), BF16),
        scratch_shapes=[pltpu.VMEM((WINDOW + tq, SZ_AK), BF16),
                        pltpu.VMEM((SZ_AV, WINDOW + tq), BF16)],
        compiler_params=pltpu.CompilerParams(
            dimension_semantics=("arbitrary", "arbitrary"), vmem_limit_bytes=VMEM_LIMIT_BYTES),
        name="attention",
    )(sinks, q, kk, kk, vt, vt)


def _log_sigmoid(x):
    return -(jnp.maximum(-x, 0.0) + jnp.log(1.0 + jnp.exp(-jnp.abs(x))))


def _mlstm_gate_kernel(g_ref, rows_ref, cols_ref, *, seq):
    L = MLSTM_CHUNK
    nh = MLSTM_HEADS
    n = g_ref.shape[1]
    g = g_ref[...]
    ig = GATE_SOFTCAP * jnp.tanh(g[0:nh] / GATE_SOFTCAP)
    fg = GATE_SOFTCAP * jnp.tanh(g[nh:2 * nh] / GATE_SOFTCAP)
    lf = _log_sigmoid(fg)
    pos = lax.broadcasted_iota(jnp.int32, (nh, n), 1) & (L - 1)
    a = lf
    sh = 1
    while sh < L:
        a = a + jnp.where(pos >= sh, pltpu.roll(a, sh, 1), 0.0)
        sh *= 2
    bvec = ig - a
    pm = bvec
    sh = 1
    while sh < L:
        pm = jnp.maximum(pm, jnp.where(pos >= sh, pltpu.roll(pm, sh, 1), -jnp.inf))
        sh *= 2
    pad = jnp.zeros((L - nh, L), F32)
    m = [jnp.zeros((nh, L), F32) for _ in range(n // seq)]
    for c in range(seq // L):
        for s in range(n // seq):
            sl = slice(s * seq + c * L, s * seq + (c + 1) * L)
            a_c, b_c = a[:, sl], bvec[:, sl]
            mrow = jnp.maximum(m[s], pm[:, sl])
            total = jnp.broadcast_to(a_c[:, L - 1:L], (nh, L))
            mlast = jnp.broadcast_to(mrow[:, L - 1:L], (nh, L))
            rows_ref[ROW_M2:ROW_M2 + nh, sl] = mrow * LOG2E
            rows_ref[ROW_WINTER:ROW_WINTER + nh, sl] = jnp.exp(m[s] - mrow)
            rows_ref[ROW_ENEG:ROW_ENEG + nh, sl] = jnp.exp(-(a_c + mrow))
            rows_ref[ROW_WK:ROW_WK + nh, sl] = jnp.exp(b_c - mlast) * MLSTM_QK_DIM ** -0.5
            rows_ref[ROW_DECAY:ROW_DECAY + nh, sl] = jnp.exp(m[s] - mlast)
            m[s] = total + mlast
            cols_ref[sl, :] = jnp.concatenate([b_c * LOG2E - K_SCALE_LOG2, pad], axis=0).T


def _mlstm_gates(gt, seq):
    n = gt.shape[1]
    return pl.pallas_call(
        functools.partial(_mlstm_gate_kernel, seq=seq),
        grid=(1,),
        in_specs=[pl.BlockSpec((2 * MLSTM_HEADS, n), lambda i: (0, 0))],
        out_specs=[pl.BlockSpec((GATE_ROWS, n), lambda i: (0, 0)),
                   pl.BlockSpec((n, LANES), lambda i: (0, 0))],
        out_shape=[jax.ShapeDtypeStruct((GATE_ROWS, n), F32),
                   jax.ShapeDtypeStruct((n, LANES), F32)],
        compiler_params=pltpu.CompilerParams(
            dimension_semantics=("arbitrary",), vmem_limit_bytes=VMEM_LIMIT_BYTES),
        name="mlstm_gates",
    )(gt)


def _mlstm_kernel(rows_ref, cols_ref, qk_ref, vt_ref, og_ref, hn_ref, o_ref, ct_scr, n_scr):
    L = MLSTM_CHUNK
    nh = MLSTM_HEADS
    contract_lanes = (((1,), (1,)), ((), ()))

    @pl.when(pl.program_id(1) == 0)
    def _():
        ct_scr[...] = jnp.zeros_like(ct_scr)
        n_scr[...] = jnp.zeros_like(n_scr)

    row_i = lax.broadcasted_iota(jnp.int32, (L, L), 0)
    col_i = lax.broadcasted_iota(jnp.int32, (L, L), 1)
    causal_t = row_i <= col_i
    lo_lane = col_i < MLSTM_QK_DIM
    zero = jnp.zeros((), BF16)

    n_chunks = qk_ref.shape[0] // L
    pairs = [(ci, p) for ci in range(n_chunks) for p in range(nh // 2)]
    ct = [ct_scr[p] for p in range(nh // 2)]
    nn = [n_scr[p] for p in range(nh // 2)]
    gate_rows = {}

    def chunk_gates(ci):
        if ci not in gate_rows:
            rs = slice(ci * L, (ci + 1) * L)
            wk = rows_ref[ROW_WK:ROW_WK + nh, rs]
            gate_rows[ci] = dict(
                m2=rows_ref[ROW_M2:ROW_M2 + nh, rs], w_inter=rows_ref[ROW_WINTER:ROW_WINTER + nh, rs],
                e_neg=rows_ref[ROW_ENEG:ROW_ENEG + nh, rs], wk=wk,
                decay=rows_ref[ROW_DECAY:ROW_DECAY + nh, rs], b2cols=cols_ref[rs, :],
                wk_b=jnp.concatenate([wk, wk], axis=0).astype(BF16))
        return gate_rows[ci]

    def front(ci, p):
        g = chunk_gates(ci)
        rs = slice(ci * L, (ci + 1) * L)
        h0, h1 = 2 * p, 2 * p + 1
        q2 = qk_ref[rs, p * LANES:(p + 1) * LANES]
        k2 = qk_ref[rs, SZ_MQ + p * LANES:SZ_MQ + (p + 1) * LANES]
        qm2 = jnp.concatenate([jnp.where(lo_lane, q2, zero), jnp.where(lo_lane, zero, q2)], axis=0)
        n2b = jnp.broadcast_to(nn[p], (2 * SUBLANES, LANES)).astype(BF16)
        stacked = jnp.concatenate([k2, ct[p].astype(BF16), n2b], axis=0)
        res = lax.dot_general(stacked, qm2, contract_lanes, preferred_element_type=F32)
        vw = [(vt_ref[hh * MLSTM_V_DIM:(hh + 1) * MLSTM_V_DIM, rs].astype(F32)
               * g["wk"][hh:hh + 1, :]).astype(BF16) for hh in (h0, h1)]
        upd = jnp.dot(jnp.concatenate(vw + [g["wk_b"]], axis=0), k2,
                      preferred_element_type=F32)
        n_inc = upd[2 * MLSTM_V_DIM:]
        dec = jnp.where(lo_lane[0:1, :], g["decay"][h0:h0 + 1, :], g["decay"][h1:h1 + 1, :])
        ct[p] = dec * ct[p] + jnp.where(lo_lane, upd[0:MLSTM_V_DIM], upd[MLSTM_V_DIM:2 * MLSTM_V_DIM])
        nn[p] = dec * nn[p] + jnp.where(lo_lane[0:1, :], n_inc[h0:h0 + 1, :], n_inc[h1:h1 + 1, :])
        return res

    def back(ci, p, par, res):
        st = res[0:L, par * L:(par + 1) * L]
        inter = res[L:L + MLSTM_V_DIM, par * L:(par + 1) * L]
        qn = res[L + MLSTM_V_DIM:L + MLSTM_V_DIM + 1, par * L:(par + 1) * L]
        g = chunk_gates(ci)
        rs = slice(ci * L, (ci + 1) * L)
        hh = 2 * p + par
        vsl = slice(hh * MLSTM_V_DIM, (hh + 1) * MLSTM_V_DIM)
        dt = jnp.where(causal_t,
                       jnp.exp2(g["b2cols"][:, hh:hh + 1] - g["m2"][hh:hh + 1, :]), 0.0)
        sct = st * dt
        wi = g["w_inter"][hh:hh + 1, :]
        den = jnp.sum(sct, axis=0, keepdims=True) + wi * qn
        numt = jnp.dot(vt_ref[vsl, rs], sct.astype(BF16), preferred_element_type=F32) + wi * inter
        rden = 1.0 / jnp.maximum(jnp.abs(den), g["e_neg"][hh:hh + 1, :])
        ms = jnp.mean(numt * numt, axis=0, keepdims=True)
        scale = rden * lax.rsqrt(rden * rden * ms + NORM_EPS)
        hv = (numt * scale).T
        o_ref[rs, vsl] = (hv * hn_ref[:, vsl] * og_ref[rs, vsl].astype(F32)).astype(BF16)

    pending = [front(*u) for u in pairs[:MLSTM_LOOKAHEAD]]
    for idx, (ci, p) in enumerate(pairs):
        res = pending[idx]
        pending[idx] = None
        back(ci, p, 0, res)
        if idx + MLSTM_LOOKAHEAD < len(pairs):
            pending.append(front(*pairs[idx + MLSTM_LOOKAHEAD]))
        back(ci, p, 1, res)
    for p in range(nh // 2):
        ct_scr[p] = ct[p]
        n_scr[p] = nn[p]


def _mlstm(rows, cols, qk, v, og, hn, batch, seq):
    n = qk.shape[0]
    tm = TM_MLSTM
    nt = seq // tm
    row = pl.BlockSpec((tm, SZ_MV), lambda b, c: (b * nt + c, 0))
    return pl.pallas_call(
        _mlstm_kernel,
        grid=(batch, nt),
        in_specs=[
            pl.BlockSpec((GATE_ROWS, tm), lambda b, c: (0, b * nt + c)),
            pl.BlockSpec((tm, LANES), lambda b, c: (b * nt + c, 0)),
            row,
            pl.BlockSpec((SZ_MV, tm), lambda b, c: (0, b * nt + c)),
            row,
            _const_spec((1, SZ_MV)),
        ],
        out_specs=row,
        out_shape=jax.ShapeDtypeStruct((n, SZ_MV), BF16),
        scratch_shapes=[
            pltpu.VMEM((MLSTM_HEADS // 2, 2 * MLSTM_QK_DIM, MLSTM_V_DIM), F32),
            pltpu.VMEM((MLSTM_HEADS // 2, 1, 2 * MLSTM_QK_DIM), F32),
        ],
        compiler_params=pltpu.CompilerParams(
            dimension_semantics=("arbitrary", "arbitrary"), vmem_limit_bytes=VMEM_LIMIT_BYTES),
        name="mlstm",
    )(rows, cols, qk, v, og, hn)


def _merge_stage_weights(wpa_hbm, wpm_hbm, wo_hbm, wpa_ref, wpm_ref, wo_ref, stage, sem):
    hd = ATTN_HEAD_DIM
    wpa_copies = [pltpu.make_async_copy(wpa_hbm.at[pl.ds(head * hd, hd), :],
                                        stage.at[0, pl.ds(pos * hd, hd), :], sem.at[0])
                  for pos, head in enumerate(ATTN_HEAD_ORDER)]
    wpm_copy = pltpu.make_async_copy(wpm_hbm, stage.at[1], sem.at[1])
    wo_copy = pltpu.make_async_copy(wo_hbm, stage.at[0], sem.at[0])
    for cp in wpa_copies:
        cp.start()
    wpm_copy.start()
    for cp in wpa_copies:
        cp.wait()
    wpa_ref[...] = stage[0].astype(BF16)
    wo_copy.start()
    wpm_copy.wait()
    wpm_ref[...] = stage[1].astype(BF16)
    wo_copy.wait()
    wo_ref[...] = stage[0].astype(BF16)


def _merge_kernel(x_ref, a_ref, hm_ref, gate_ref, wpa_hbm, wpm_hbm, wo_hbm, o_ref,
                  wpa_ref, wpm_ref, wo_ref, stage, sem):
    @pl.when(pl.program_id(0) == 0)
    def _():
        _merge_stage_weights(wpa_hbm, wpm_hbm, wo_hbm, wpa_ref, wpm_ref, wo_ref, stage, sem)

    ya = jnp.dot(a_ref[...], wpa_ref[...], preferred_element_type=F32)
    ym = jnp.dot(hm_ref[...], wpm_ref[...], preferred_element_type=F32)
    ga = jax.nn.sigmoid(gate_ref[:, 0:D_MODEL].astype(F32))
    gm = jax.nn.sigmoid(gate_ref[:, D_MODEL:2 * D_MODEL].astype(F32))
    merged = (ga * ya + gm * ym).astype(BF16)
    o_ref[...] = x_ref[...] + jnp.dot(merged, wo_ref[...], preferred_element_type=F32)


def _merge(x, a, hm, gates, wpa, wpm, wo):
    n = x.shape[0]
    tm = TM_MERGE
    row = lambda width: pl.BlockSpec((tm, width), lambda i: (i, 0))
    return pl.pallas_call(
        _merge_kernel,
        grid=(n // tm,),
        in_specs=[row(D_MODEL), row(SZ_AQ), row(SZ_MV), row(2 * D_MODEL),
                  pl.BlockSpec(memory_space=pl.ANY), pl.BlockSpec(memory_space=pl.ANY),
                  pl.BlockSpec(memory_space=pl.ANY)],
        out_specs=row(D_MODEL),
        out_shape=jax.ShapeDtypeStruct((n, D_MODEL), F32),
        scratch_shapes=[pltpu.VMEM((D_MODEL, D_MODEL), BF16)] * 3 + [
            pltpu.VMEM((2, D_MODEL, D_MODEL), F32), pltpu.SemaphoreType.DMA((2,))],
        compiler_params=pltpu.CompilerParams(
            dimension_semantics=("arbitrary",), vmem_limit_bytes=VMEM_LIMIT_BYTES),
        name="merge",
    )(x, a, hm, gates, wpa, wpm, wo)


def kernel(x, ffn1_norm, ffn1_w_gate, ffn1_w_up, ffn1_w_down, mix_norm, w_in, b_in, attn_sinks,
           mlstm_conv, mlstm_head_norm, w_proj_attn, w_proj_mlstm, w_out, ffn2_norm, ffn2_w_gate,
           ffn2_w_up, ffn2_w_down, final_norm):
    batch, seq, d = x.shape
    assert d == D_MODEL and ffn1_norm.shape[0] == 1, "one layer of width D_MODEL"
    assert all(seq % t == 0 for t in (TM_INPROJ, TM_MLSTM, ATTN_BLOCKS * WINDOW))
    assert all((batch * seq) % t == 0 for t in (TM_FFN, TM_MERGE))
    n = batch * seq
    xf = x.reshape(n, d)
    fin = final_norm.reshape(1, d)

    x1 = _ffn(xf, ffn1_norm[0].reshape(1, d), ffn1_w_gate[0], ffn1_w_up[0], ffn1_w_down[0], fin,
              apply_final=False)

    q, kk, avt, mqk, mvt, mo, gates, gt = _inproj(x1, mix_norm[0].reshape(1, d), w_in[0], b_in[0],
                                                  mlstm_conv[0], seq)

    ya = _attention(attn_sinks[0], q, kk, avt, batch, seq)

    rows, cols = _mlstm_gates(gt, seq)
    hm = _mlstm(rows, cols, mqk, mvt, mo, mlstm_head_norm[0].reshape(1, SZ_MV), batch, seq)

    x2 = _merge(x1, ya, hm, gates, w_proj_attn[0], w_proj_mlstm[0], w_out[0])

    out = _ffn(x2, ffn2_norm[0].reshape(1, d), ffn2_w_gate[0], ffn2_w_up[0], ffn2_w_down[0], fin,
               apply_final=True)
    return out.reshape(batch, seq, d)
```

```python
import functools

import jax
import jax.numpy as jnp
from jax import lax
from jax.experimental import pallas as pl
from jax.experimental.pallas import tpu as pltpu

F32 = jnp.float32
BF16 = jnp.bfloat16

D_MODEL = 1024
ATTN_HEAD_DIM = 64
ATTN_HEADS = 16
ATTN_KV_HEADS = 4
ATTN_GROUP = 4
WINDOW = 128
MLSTM_HEADS = 8
MLSTM_V_DIM = 128
MLSTM_QK_DIM = 64
CONV_WIDTH = 4
GATE_SOFTCAP = 15.0
D_FF = 2816
FFN_RESIDUAL_WEIGHT = 0.5
NORM_EPS = 1e-6

SZ_AQ, SZ_AK, SZ_AV = 1024, 256, 256
SZ_MQ, SZ_MK, SZ_MV, SZ_MO = 512, 512, 1024, 1024
SZ_MI, SZ_MF = 8, 8

LANES = 128
SUBLANES = 8
MXU_DIM = 256
VMEM_LIMIT_BYTES = 56 * 1024 * 1024

MLSTM_CHUNK = 128
FF_CHUNK = MXU_DIM
FFN_STAGE_SLOTS = 4
TM_FFN = 1024
TM_INPROJ = 1024
TM_MERGE = 1024
TM_MLSTM = 8 * MLSTM_CHUNK
ATTN_LOOKAHEAD = 5
ATTN_BLOCKS = 16
MLSTM_LOOKAHEAD = 2

LOG2E = 1.4426950408889634
K_SCALE_LOG2 = 3.0
ROW_M2, ROW_WINTER, ROW_ENEG, ROW_WK, ROW_DECAY = 0, 8, 16, 24, 32
GATE_ROWS = 40

OFF_AQ = 0
OFF_AK = OFF_AQ + SZ_AQ
OFF_AV = OFF_AK + SZ_AK
OFF_MQ = OFF_AV + SZ_AV
OFF_MK = OFF_MQ + SZ_MQ
OFF_MV = OFF_MK + SZ_MK
OFF_MO = OFF_MV + SZ_MV
OFF_MI = OFF_MO + SZ_MO
OFF_MF = OFF_MI + SZ_MI
OFF_G = OFF_MF + SZ_MF
IN_WIDTH = OFF_G + 2 * D_MODEL
FEATURE_MAJOR_GROUPS = ((OFF_AQ, SZ_AQ), (OFF_AV, SZ_AV), (OFF_MV, SZ_MV), (OFF_MI, SZ_MI + SZ_MF))
FEATURE_MAJOR_ROWS = sum(size for _, size in FEATURE_MAJOR_GROUPS)
_FEATURE_MAJOR_BIAS_ROWS = tuple(
    (off, size, sum(s for _, s in FEATURE_MAJOR_GROUPS[:k]))
    for k, (off, size) in enumerate(FEATURE_MAJOR_GROUPS))
STAGE_ROWS = 256
STAGE_SLOTS = 4

ATTN_HEAD_ORDER = tuple((2 * (t // 4) + par) * ATTN_GROUP + t % 4
                        for t in range(ATTN_HEADS // 2) for par in range(2))


def _rms(x, g):
    return x * lax.rsqrt(jnp.mean(x * x, axis=-1, keepdims=True) + NORM_EPS) * g


def _const_spec(shape):
    zeros = (0,) * len(shape)
    return pl.BlockSpec(shape, lambda *_: zeros, pipeline_mode=pl.Buffered(1))


def _ffn_stage_weights(wg_hbm, wu_hbm, wd_hbm, wg_ref, wu_ref, wd_ref, gu_stage, d_stage, sem):
    n_chunks = D_FF // FF_CHUNK
    ahead = FFN_STAGE_SLOTS - 1

    def copies(c):
        slot = c % FFN_STAGE_SLOTS
        cols = pl.ds(c * FF_CHUNK, FF_CHUNK)
        return (pltpu.make_async_copy(wg_hbm.at[:, cols], gu_stage.at[slot, 0], sem.at[slot, 0]),
                pltpu.make_async_copy(wu_hbm.at[:, cols], gu_stage.at[slot, 1], sem.at[slot, 1]),
                pltpu.make_async_copy(wd_hbm.at[cols, :], d_stage.at[slot], sem.at[slot, 2]))

    for c in range(min(ahead, n_chunks)):
        for cp in copies(c):
            cp.start()
    for c in range(n_chunks):
        slot = c % FFN_STAGE_SLOTS
        if c + ahead < n_chunks:
            for cp in copies(c + ahead):
                cp.start()
        for cp in copies(c):
            cp.wait()
        sl = slice(c * FF_CHUNK, (c + 1) * FF_CHUNK)
        wg_ref[:, sl] = gu_stage[slot, 0].astype(BF16)
        wu_ref[:, sl] = gu_stage[slot, 1].astype(BF16)
        wd_ref[sl, :] = d_stage[slot].astype(BF16)


def _ffn_kernel(x_ref, g_ref, wg_hbm, wu_hbm, wd_hbm, fin_ref, o_ref,
                wg_ref, wu_ref, wd_ref, gu_stage, d_stage, sem, *, apply_final):
    @pl.when(pl.program_id(0) == 0)
    def _():
        _ffn_stage_weights(wg_hbm, wu_hbm, wd_hbm, wg_ref, wu_ref, wd_ref, gu_stage, d_stage, sem)

    x = x_ref[...]
    h = _rms(x, g_ref[...]).astype(BF16)
    for c in range(D_FF // FF_CHUNK):
        sl = slice(c * FF_CHUNK, (c + 1) * FF_CHUNK)
        a = jnp.dot(h, wg_ref[:, sl], preferred_element_type=F32)
        u = jnp.dot(h, wu_ref[:, sl], preferred_element_type=F32)
        act = (a * jax.nn.sigmoid(a) * (u * FFN_RESIDUAL_WEIGHT)).astype(BF16)
        d = jnp.dot(act, wd_ref[sl, :], preferred_element_type=F32)
        if c == 0:
            o_ref[...] = x + d
        else:
            o_ref[...] += d
    if apply_final:
        o_ref[...] = _rms(o_ref[...], fin_ref[...])


def _ffn(x, g, wg, wu, wd, fin, apply_final):
    n = x.shape[0]
    tm = TM_FFN
    return pl.pallas_call(
        functools.partial(_ffn_kernel, apply_final=apply_final),
        grid=(n // tm,),
        in_specs=[
            pl.BlockSpec((tm, D_MODEL), lambda i: (i, 0)),
            _const_spec((1, D_MODEL)),
            pl.BlockSpec(memory_space=pl.ANY),
            pl.BlockSpec(memory_space=pl.ANY),
            pl.BlockSpec(memory_space=pl.ANY),
            _const_spec((1, D_MODEL)),
        ],
        out_specs=pl.BlockSpec((tm, D_MODEL), lambda i: (i, 0)),
        out_shape=jax.ShapeDtypeStruct((n, D_MODEL), F32),
        scratch_shapes=[
            pltpu.VMEM((D_MODEL, D_FF), BF16),
            pltpu.VMEM((D_MODEL, D_FF), BF16),
            pltpu.VMEM((D_FF, D_MODEL), BF16),
            pltpu.VMEM((FFN_STAGE_SLOTS, 2, D_MODEL, FF_CHUNK), F32),
            pltpu.VMEM((FFN_STAGE_SLOTS, FF_CHUNK, D_MODEL), F32),
            pltpu.SemaphoreType.DMA((FFN_STAGE_SLOTS, 3)),
        ],
        compiler_params=pltpu.CompilerParams(
            dimension_semantics=("arbitrary",), vmem_limit_bytes=VMEM_LIMIT_BYTES),
        name="ffn_final" if apply_final else "ffn",
    )(x, g, wg, wu, wd, fin)


def _inproj_stage_weights(w_hbm, w_ref, stage, sem):
    bounds = [(r0, min(STAGE_ROWS, IN_WIDTH - r0)) for r0 in range(0, IN_WIDTH, STAGE_ROWS)]

    def copy(c):
        r0, rows = bounds[c]
        slot = c % STAGE_SLOTS
        return pltpu.make_async_copy(w_hbm.at[pl.ds(r0, rows), :],
                                     stage.at[slot, pl.ds(0, rows), :], sem.at[slot])

    for c in range(min(STAGE_SLOTS - 1, len(bounds))):
        copy(c).start()
    for c, (r0, rows) in enumerate(bounds):
        if c + STAGE_SLOTS - 1 < len(bounds):
            copy(c + STAGE_SLOTS - 1).start()
        copy(c).wait()
        w_ref[r0:r0 + rows, :] = stage[c % STAGE_SLOTS, 0:rows, :].astype(BF16)


def _inproj_kernel(x_ref, g_ref, w_hbm, brow_ref, bgate_ref, bcol_ref, conv_ref,
                   q_ref, kk_ref, avt_ref, mqk_ref, mvt_ref, mo_ref, gate_ref, gt_ref,
                   w_ref, stage, sem, conv_scr, *, tiles_per_seq):
    i = pl.program_id(0)
    tm = x_ref.shape[0]

    @pl.when(i == 0)
    def _():
        _inproj_stage_weights(w_hbm, w_ref, stage, sem)

    h = _rms(x_ref[...], g_ref[...]).astype(BF16)
    contract_lanes = (((1,), (1,)), ((), ()))

    def seg(r0, width, bias):
        return lax.dot_general(h, w_ref[r0:r0 + width, :], contract_lanes,
                               preferred_element_type=F32) + bias

    def seg_t(r0, rows):
        c0 = next(base + r0 - off for off, size, base in _FEATURE_MAJOR_BIAS_ROWS
                  if off <= r0 < off + size)
        return (lax.dot_general(w_ref[r0:r0 + rows, :], h, contract_lanes,
                                preferred_element_type=F32) + bcol_ref[c0:c0 + rows, :])

    @pl.when(i == 0)
    def _():
        conv_scr[...] = jnp.zeros_like(conv_scr)

    carry = jnp.where(i % tiles_per_seq == 0, 0.0, conv_scr[...])

    def conv_slab(c, width):
        z = seg(OFF_MQ + c, width, brow_ref[:, OFF_MQ + c:OFF_MQ + c + width])
        conv_scr[:, c:c + width] = z[tm - SUBLANES:tm, :]
        zc = jnp.concatenate([carry[:, c:c + width], z], axis=0)
        acc = z * conv_ref[CONV_WIDTH - 1:CONV_WIDTH, c:c + width]
        for k in range(1, CONV_WIDTH):
            zk = pltpu.roll(zc, k, 0)[SUBLANES:, :]
            acc = acc + zk * conv_ref[CONV_WIDTH - 1 - k:CONV_WIDTH - k, c:c + width]
        mqk_ref[:, c:c + width] = (acc * jax.nn.sigmoid(acc)).astype(BF16)

    dw = MXU_DIM
    for k in range(SZ_AQ // dw):
        sl = slice(k * dw, (k + 1) * dw)
        conv_slab(k * dw, dw)
        q_ref[sl, :] = (seg_t(OFF_AQ + k * dw, dw) * (LOG2E * ATTN_HEAD_DIM ** -0.5)).astype(BF16)
        if k == 0:
            kk_ref[...] = seg(OFF_AK, SZ_AK, brow_ref[:, OFF_AK:OFF_AK + SZ_AK]).astype(BF16)
        elif k == 1:
            avt_ref[...] = seg_t(OFF_AV, SZ_AV).astype(BF16)
        mvt_ref[sl, :] = seg_t(OFF_MV + k * dw, dw).astype(BF16)
        mo = seg(OFF_MO + k * dw, dw, brow_ref[:, OFF_MO + k * dw:OFF_MO + (k + 1) * dw])
        mo_ref[:, sl] = jax.nn.sigmoid(mo).astype(BF16)
        gsl = slice(2 * k * dw, (2 * k + 2) * dw)
        gate_ref[:, gsl] = seg(OFF_G + 2 * k * dw, 2 * dw, bgate_ref[:, gsl]).astype(BF16)

    gt_ref[...] = seg_t(OFF_MI, SZ_MI + SZ_MF)


def _inproj(x, g, w_in, b_in, conv, seq):
    w = w_in.T
    b_row = b_in.reshape(1, IN_WIDTH)
    b_gate = b_in[OFF_G:].reshape(1, 2 * D_MODEL)
    b_col = jnp.concatenate([b_in[off:off + size] for off, size in FEATURE_MAJOR_GROUPS]).reshape(
        FEATURE_MAJOR_ROWS, 1)
    n = x.shape[0]
    tm = TM_INPROJ
    row = lambda width: pl.BlockSpec((tm, width), lambda i: (i, 0))
    col = lambda height: pl.BlockSpec((height, tm), lambda i: (0, i))
    bf = lambda width: jax.ShapeDtypeStruct((n, width), BF16)
    return pl.pallas_call(
        functools.partial(_inproj_kernel, tiles_per_seq=seq // tm),
        grid=(n // tm,),
        in_specs=[
            row(D_MODEL),
            _const_spec((1, D_MODEL)),
            pl.BlockSpec(memory_space=pl.ANY),
            _const_spec((1, IN_WIDTH)),
            _const_spec((1, 2 * D_MODEL)),
            _const_spec((FEATURE_MAJOR_ROWS, 1)),
            _const_spec((CONV_WIDTH, SZ_MQ + SZ_MK)),
        ],
        out_specs=[col(SZ_AQ), row(SZ_AK), col(SZ_AV), row(SZ_MQ + SZ_MK), col(SZ_MV), row(SZ_MO),
                   row(2 * D_MODEL), col(2 * MLSTM_HEADS)],
        out_shape=[jax.ShapeDtypeStruct((SZ_AQ, n), BF16), bf(SZ_AK),
                   jax.ShapeDtypeStruct((SZ_AV, n), BF16),
                   bf(SZ_MQ + SZ_MK), jax.ShapeDtypeStruct((SZ_MV, n), BF16), bf(SZ_MO),
                   bf(2 * D_MODEL), jax.ShapeDtypeStruct((2 * MLSTM_HEADS, n), F32)],
        scratch_shapes=[pltpu.VMEM((IN_WIDTH, D_MODEL), BF16),
                        pltpu.VMEM((STAGE_SLOTS, STAGE_ROWS, D_MODEL), F32),
                        pltpu.SemaphoreType.DMA((STAGE_SLOTS,)),
                        pltpu.VMEM((SUBLANES, SZ_MQ + SZ_MK), F32)],
        compiler_params=pltpu.CompilerParams(
            dimension_semantics=("arbitrary",), vmem_limit_bytes=VMEM_LIMIT_BYTES),
        name="inproj",
    )(x, g, w, b_row, b_gate, b_col, conv)


def _attn_kernel(sink_ref, q_ref, kc_ref, kp_ref, vtc_ref, vtp_ref, o_ref, k_scr, vt_scr):
    w = WINDOW
    k_scr[0:w, :] = kp_ref[...]
    k_scr[w:, :] = kc_ref[...]
    vt_scr[:, 0:w] = vtp_ref[...]
    vt_scr[:, w:] = vtc_ref[...]
    kj = lax.broadcasted_iota(jnp.int32, (2 * w, w), 0)
    qi = lax.broadcasted_iota(jnp.int32, (2 * w, w), 1)
    band = (kj > qi) & (kj <= qi + w)
    band_first = band & ((kj >= w) | (pl.program_id(1) > 0))
    pairs = [(jb, t) for jb in range(q_ref.shape[1] // w) for t in range(ATTN_HEADS // 2)]
    zeros_qt = jnp.zeros((ATTN_HEAD_DIM, w), BF16)

    def scores(jb, t):
        ha, hb = ATTN_HEAD_ORDER[2 * t], ATTN_HEAD_ORDER[2 * t + 1]
        qa = q_ref[ha * ATTN_HEAD_DIM:(ha + 1) * ATTN_HEAD_DIM, jb * w:(jb + 1) * w]
        qb = q_ref[hb * ATTN_HEAD_DIM:(hb + 1) * ATTN_HEAD_DIM, jb * w:(jb + 1) * w]
        qt2 = jnp.concatenate([jnp.concatenate([qa, zeros_qt], axis=0),
                               jnp.concatenate([zeros_qt, qb], axis=0)], axis=1)
        kk = k_scr[jb * w:(jb + 2) * w, (t // 4) * LANES:(t // 4 + 1) * LANES]
        return jnp.dot(kk, qt2, preferred_element_type=F32)

    def head_out(jb, t, par, s):
        s = jnp.where(band_first if jb == 0 else band, s, -jnp.inf)
        sink = sink_ref[ATTN_HEAD_ORDER[2 * t + par]] * LOG2E
        mx = jnp.maximum(jnp.max(s, axis=0, keepdims=True), sink)
        pr = jnp.exp2(s - mx)
        denom = jnp.sum(pr, axis=0, keepdims=True) + jnp.exp2(sink - mx)
        kvh = 2 * (t // 4) + par
        vt = vt_scr[kvh * ATTN_HEAD_DIM:(kvh + 1) * ATTN_HEAD_DIM, jb * w:(jb + 2) * w]
        return jnp.dot(vt, pr.astype(BF16), preferred_element_type=F32) * (1.0 / denom)

    pending = [scores(*u) for u in pairs[:ATTN_LOOKAHEAD]]
    for idx, (jb, t) in enumerate(pairs):
        s2 = pending[idx]
        pending[idx] = None
        outs = [head_out(jb, t, par, s2[:, par * w:(par + 1) * w]) for par in range(2)]
        if idx + ATTN_LOOKAHEAD < len(pairs):
            pending.append(scores(*pairs[idx + ATTN_LOOKAHEAD]))
        pair = jnp.concatenate(outs, axis=0)
        o_ref[jb * w:(jb + 1) * w, t * LANES:(t + 1) * LANES] = pair.T.astype(BF16)


def _attention(sinks, q, kk, vt, batch, seq):
    n = kk.shape[0]
    tq = ATTN_BLOCKS * WINDOW
    nt = seq // tq
    cur = lambda b, j: b * nt + j
    prev = lambda b, j: (b * nt + j) * ATTN_BLOCKS - jnp.minimum(j, 1)
    return pl.pallas_call(
        _attn_kernel,
        grid=(batch, nt),
        in_specs=[
            pl.BlockSpec(memory_space=pltpu.SMEM),
            pl.BlockSpec((SZ_AQ, tq), lambda b, j: (0, cur(b, j))),
            pl.BlockSpec((tq, SZ_AK), lambda b, j: (cur(b, j), 0)),
            pl.BlockSpec((WINDOW, SZ_AK), lambda b, j: (prev(b, j), 0)),
            pl.BlockSpec((SZ_AV, tq), lambda b, j: (0, cur(b, j))),
            pl.BlockSpec((SZ_AV, WINDOW), lambda b, j: (0, prev(b, j))),
        ],
        out_specs=pl.BlockSpec((tq, SZ_AQ), lambda b, j: (cur(b, j), 0)),
        out_shape=jax.ShapeDtypeStruct((n, SZ_AQ), BF16),
        scratch_shapes=[pltpu.VMEM((WINDOW + tq, SZ_AK), BF16),
                        pltpu.VMEM((SZ_AV, WINDOW + tq), BF16)],
        compiler_params=pltpu.CompilerParams(
            dimension_semantics=("arbitrary", "arbitrary"), vmem_limit_bytes=VMEM_LIMIT_BYTES),
        name="attention",
    )(sinks, q, kk, kk, vt, vt)


def _log_sigmoid(x):
    return -(jnp.maximum(-x, 0.0) + jnp.log(1.0 + jnp.exp(-jnp.abs(x))))


def _mlstm_gate_kernel(g_ref, rows_ref, cols_ref, *, seq):
    L = MLSTM_CHUNK
    nh = MLSTM_HEADS
    n = g_ref.shape[1]
    g = g_ref[...]
    ig = GATE_SOFTCAP * jnp.tanh(g[0:nh] / GATE_SOFTCAP)
    fg = GATE_SOFTCAP * jnp.tanh(g[nh:2 * nh] / GATE_SOFTCAP)
    lf = _log_sigmoid(fg)
    pos = lax.broadcasted_iota(jnp.int32, (nh, n), 1) & (L - 1)
    a = lf
    sh = 1
    while sh < L:
        a = a + jnp.where(pos >= sh, pltpu.roll(a, sh, 1), 0.0)
        sh *= 2
    bvec = ig - a
    pm = bvec
    sh = 1
    while sh < L:
        pm = jnp.maximum(pm, jnp.where(pos >= sh, pltpu.roll(pm, sh, 1), -jnp.inf))
        sh *= 2
    pad = jnp.zeros((L - nh, L), F32)
    m = [jnp.zeros((nh, L), F32) for _ in range(n // seq)]
    for c in range(seq // L):
        for s in range(n // seq):
            sl = slice(s * seq + c * L, s * seq + (c + 1) * L)
            a_c, b_c = a[:, sl], bvec[:, sl]
            mrow = jnp.maximum(m[s], pm[:, sl])
            total = jnp.broadcast_to(a_c[:, L - 1:L], (nh, L))
            mlast = jnp.broadcast_to(mrow[:, L - 1:L], (nh, L))
            rows_ref[ROW_M2:ROW_M2 + nh, sl] = mrow * LOG2E
            rows_ref[ROW_WINTER:ROW_WINTER + nh, sl] = jnp.exp(m[s] - mrow)
            rows_ref[ROW_ENEG:ROW_ENEG + nh, sl] = jnp.exp(-(a_c + mrow))
            rows_ref[ROW_WK:ROW_WK + nh, sl] = jnp.exp(b_c - mlast) * MLSTM_QK_DIM ** -0.5
            rows_ref[ROW_DECAY:ROW_DECAY + nh, sl] = jnp.exp(m[s] - mlast)
            m[s] = total + mlast
            cols_ref[sl, :] = jnp.concatenate([b_c * LOG2E - K_SCALE_LOG2, pad], axis=0).T


def _mlstm_gates(gt, seq):
    n = gt.shape[1]
    return pl.pallas_call(
        functools.partial(_mlstm_gate_kernel, seq=seq),
        grid=(1,),
        in_specs=[pl.BlockSpec((2 * MLSTM_HEADS, n), lambda i: (0, 0))],
        out_specs=[pl.BlockSpec((GATE_ROWS, n), lambda i: (0, 0)),
                   pl.BlockSpec((n, LANES), lambda i: (0, 0))],
        out_shape=[jax.ShapeDtypeStruct((GATE_ROWS, n), F32),
                   jax.ShapeDtypeStruct((n, LANES), F32)],
        compiler_params=pltpu.CompilerParams(
            dimension_semantics=("arbitrary",), vmem_limit_bytes=VMEM_LIMIT_BYTES),
        name="mlstm_gates",
    )(gt)


def _mlstm_kernel(rows_ref, cols_ref, qk_ref, vt_ref, og_ref, hn_ref, o_ref, ct_scr, n_scr):
    L = MLSTM_CHUNK
    nh = MLSTM_HEADS
    contract_lanes = (((1,), (1,)), ((), ()))

    @pl.when(pl.program_id(1) == 0)
    def _():
        ct_scr[...] = jnp.zeros_like(ct_scr)
        n_scr[...] = jnp.zeros_like(n_scr)

    row_i = lax.broadcasted_iota(jnp.int32, (L, L), 0)
    col_i = lax.broadcasted_iota(jnp.int32, (L, L), 1)
    causal_t = row_i <= col_i
    lo_lane = col_i < MLSTM_QK_DIM
    zero = jnp.zeros((), BF16)

    n_chunks = qk_ref.shape[0] // L
    pairs = [(ci, p) for ci in range(n_chunks) for p in range(nh // 2)]
    ct = [ct_scr[p] for p in range(nh // 2)]
    nn = [n_scr[p] for p in range(nh // 2)]
    gate_rows = {}

    def chunk_gates(ci):
        if ci not in gate_rows:
            rs = slice(ci * L, (ci + 1) * L)
            wk = rows_ref[ROW_WK:ROW_WK + nh, rs]
            gate_rows[ci] = dict(
                m2=rows_ref[ROW_M2:ROW_M2 + nh, rs], w_inter=rows_ref[ROW_WINTER:ROW_WINTER + nh, rs],
                e_neg=rows_ref[ROW_ENEG:ROW_ENEG + nh, rs], wk=wk,
                decay=rows_ref[ROW_DECAY:ROW_DECAY + nh, rs], b2cols=cols_ref[rs, :],
                wk_b=jnp.concatenate([wk, wk], axis=0).astype(BF16))
        return gate_rows[ci]

    def front(ci, p):
        g = chunk_gates(ci)
        rs = slice(ci * L, (ci + 1) * L)
        h0, h1 = 2 * p, 2 * p + 1
        q2 = qk_ref[rs, p * LANES:(p + 1) * LANES]
        k2 = qk_ref[rs, SZ_MQ + p * LANES:SZ_MQ + (p + 1) * LANES]
        qm2 = jnp.concatenate([jnp.where(lo_lane, q2, zero), jnp.where(lo_lane, zero, q2)], axis=0)
        n2b = jnp.broadcast_to(nn[p], (2 * SUBLANES, LANES)).astype(BF16)
        stacked = jnp.concatenate([k2, ct[p].astype(BF16), n2b], axis=0)
        res = lax.dot_general(stacked, qm2, contract_lanes, preferred_element_type=F32)
        vw = [(vt_ref[hh * MLSTM_V_DIM:(hh + 1) * MLSTM_V_DIM, rs].astype(F32)
               * g["wk"][hh:hh + 1, :]).astype(BF16) for hh in (h0, h1)]
        upd = jnp.dot(jnp.concatenate(vw + [g["wk_b"]], axis=0), k2,
                      preferred_element_type=F32)
        n_inc = upd[2 * MLSTM_V_DIM:]
        dec = jnp.where(lo_lane[0:1, :], g["decay"][h0:h0 + 1, :], g["decay"][h1:h1 + 1, :])
        ct[p] = dec * ct[p] + jnp.where(lo_lane, upd[0:MLSTM_V_DIM], upd[MLSTM_V_DIM:2 * MLSTM_V_DIM])
        nn[p] = dec * nn[p] + jnp.where(lo_lane[0:1, :], n_inc[h0:h0 + 1, :], n_inc[h1:h1 + 1, :])
        return res

    def back(ci, p, par, res):
        st = res[0:L, par * L:(par + 1) * L]
        inter = res[L:L + MLSTM_V_DIM, par * L:(par + 1) * L]
        qn = res[L + MLSTM_V_DIM:L + MLSTM_V_DIM + 1, par * L:(par + 1) * L]
        g = chunk_gates(ci)
        rs = slice(ci * L, (ci + 1) * L)
        hh = 2 * p + par
        vsl = slice(hh * MLSTM_V_DIM, (hh + 1) * MLSTM_V_DIM)
        dt = jnp.where(causal_t,
                       jnp.exp2(g["b2cols"][:, hh:hh + 1] - g["m2"][hh:hh + 1, :]), 0.0)
        sct = st * dt
        wi = g["w_inter"][hh:hh + 1, :]
        den = jnp.sum(sct, axis=0, keepdims=True) + wi * qn
        numt = jnp.dot(vt_ref[vsl, rs], sct.astype(BF16), preferred_element_type=F32) + wi * inter
        rden = 1.0 / jnp.maximum(jnp.abs(den), g["e_neg"][hh:hh + 1, :])
        ms = jnp.mean(numt * numt, axis=0, keepdims=True)
        scale = rden * lax.rsqrt(rden * rden * ms + NORM_EPS)
        hv = (numt * scale).T
        o_ref[rs, vsl] = (hv * hn_ref[:, vsl] * og_ref[rs, vsl].astype(F32)).astype(BF16)

    pending = [front(*u) for u in pairs[:MLSTM_LOOKAHEAD]]
    for idx, (ci, p) in enumerate(pairs):
        res = pending[idx]
        pending[idx] = None
        back(ci, p, 0, res)
        if idx + MLSTM_LOOKAHEAD < len(pairs):
            pending.append(front(*pairs[idx + MLSTM_LOOKAHEAD]))
        back(ci, p, 1, res)
    for p in range(nh // 2):
        ct_scr[p] = ct[p]
        n_scr[p] = nn[p]


def _mlstm(rows, cols, qk, v, og, hn, batch, seq):
    n = qk.shape[0]
    tm = TM_MLSTM
    nt = seq // tm
    row = pl.BlockSpec((tm, SZ_MV), lambda b, c: (b * nt + c, 0))
    return pl.pallas_call(
        _mlstm_kernel,
        grid=(batch, nt),
        in_specs=[
            pl.BlockSpec((GATE_ROWS, tm), lambda b, c: (0, b * nt + c)),
            pl.BlockSpec((tm, LANES), lambda b, c: (b * nt + c, 0)),
            row,
            pl.BlockSpec((SZ_MV, tm), lambda b, c: (0, b * nt + c)),
            row,
            _const_spec((1, SZ_MV)),
        ],
        out_specs=row,
        out_shape=jax.ShapeDtypeStruct((n, SZ_MV), BF16),
        scratch_shapes=[
            pltpu.VMEM((MLSTM_HEADS // 2, 2 * MLSTM_QK_DIM, MLSTM_V_DIM), F32),
            pltpu.VMEM((MLSTM_HEADS // 2, 1, 2 * MLSTM_QK_DIM), F32),
        ],
        compiler_params=pltpu.CompilerParams(
            dimension_semantics=("arbitrary", "arbitrary"), vmem_limit_bytes=VMEM_LIMIT_BYTES),
        name="mlstm",
    )(rows, cols, qk, v, og, hn)


def _merge_stage_weights(wpa_hbm, wpm_hbm, wo_hbm, wpa_ref, wpm_ref, wo_ref, stage, sem):
    hd = ATTN_HEAD_DIM
    wpa_copies = [pltpu.make_async_copy(wpa_hbm.at[pl.ds(head * hd, hd), :],
                                        stage.at[0, pl.ds(pos * hd, hd), :], sem.at[0])
                  for pos, head in enumerate(ATTN_HEAD_ORDER)]
    wpm_copy = pltpu.make_async_copy(wpm_hbm, stage.at[1], sem.at[1])
    wo_copy = pltpu.make_async_copy(wo_hbm, stage.at[0], sem.at[0])
    for cp in wpa_copies:
        cp.start()
    wpm_copy.start()
    for cp in wpa_copies:
        cp.wait()
    wpa_ref[...] = stage[0].astype(BF16)
    wo_copy.start()
    wpm_copy.wait()
    wpm_ref[...] = stage[1].astype(BF16)
    wo_copy.wait()
    wo_ref[...] = stage[0].astype(BF16)


def _merge_kernel(x_ref, a_ref, hm_ref, gate_ref, wpa_hbm, wpm_hbm, wo_hbm, o_ref,
                  wpa_ref, wpm_ref, wo_ref, stage, sem):
    @pl.when(pl.program_id(0) == 0)
    def _():
        _merge_stage_weights(wpa_hbm, wpm_hbm, wo_hbm, wpa_ref, wpm_ref, wo_ref, stage, sem)

    ya = jnp.dot(a_ref[...], wpa_ref[...], preferred_element_type=F32)
    ym = jnp.dot(hm_ref[...], wpm_ref[...], preferred_element_type=F32)
    ga = jax.nn.sigmoid(gate_ref[:, 0:D_MODEL].astype(F32))
    gm = jax.nn.sigmoid(gate_ref[:, D_MODEL:2 * D_MODEL].astype(F32))
    merged = (ga * ya + gm * ym).astype(BF16)
    o_ref[...] = x_ref[...] + jnp.dot(merged, wo_ref[...], preferred_element_type=F32)


def _merge(x, a, hm, gates, wpa, wpm, wo):
    n = x.shape[0]
    tm = TM_MERGE
    row = lambda width: pl.BlockSpec((tm, width), lambda i: (i, 0))
    return pl.pallas_call(
        _merge_kernel,
        grid=(n // tm,),
        in_specs=[row(D_MODEL), row(SZ_AQ), row(SZ_MV), row(2 * D_MODEL),
                  pl.BlockSpec(memory_space=pl.ANY), pl.BlockSpec(memory_space=pl.ANY),
                  pl.BlockSpec(memory_space=pl.ANY)],
        out_specs=row(D_MODEL),
        out_shape=jax.ShapeDtypeStruct((n, D_MODEL), F32),
        scratch_shapes=[pltpu.VMEM((D_MODEL, D_MODEL), BF16)] * 3 + [
            pltpu.VMEM((2, D_MODEL, D_MODEL), F32), pltpu.SemaphoreType.DMA((2,))],
        compiler_params=pltpu.CompilerParams(
            dimension_semantics=("arbitrary",), vmem_limit_bytes=VMEM_LIMIT_BYTES),
        name="merge",
    )(x, a, hm, gates, wpa, wpm, wo)


def kernel(x, ffn1_norm, ffn1_w_gate, ffn1_w_up, ffn1_w_down, mix_norm, w_in, b_in, attn_sinks,
           mlstm_conv, mlstm_head_norm, w_proj_attn, w_proj_mlstm, w_out, ffn2_norm, ffn2_w_gate,
           ffn2_w_up, ffn2_w_down, final_norm):
    batch, seq, d = x.shape
    assert d == D_MODEL and ffn1_norm.shape[0] == 1, "one layer of width D_MODEL"
    assert all(seq % t == 0 for t in (TM_INPROJ, TM_MLSTM, ATTN_BLOCKS * WINDOW))
    assert all((batch * seq) % t == 0 for t in (TM_FFN, TM_MERGE))
    n = batch * seq
    xf = x.reshape(n, d)
    fin = final_norm.reshape(1, d)

    x1 = _ffn(xf, ffn1_norm[0].reshape(1, d), ffn1_w_gate[0], ffn1_w_up[0], ffn1_w_down[0], fin,
              apply_final=False)

    q, kk, avt, mqk, mvt, mo, gates, gt = _inproj(x1, mix_norm[0].reshape(1, d), w_in[0], b_in[0],
                                                  mlstm_conv[0], seq)

    ya = _attention(attn_sinks[0], q, kk, avt, batch, seq)

    rows, cols = _mlstm_gates(gt, seq)
    hm = _mlstm(rows, cols, mqk, mvt, mo, mlstm_head_norm[0].reshape(1, SZ_MV), batch, seq)

    x2 = _merge(x1, ya, hm, gates, w_proj_attn[0], w_proj_mlstm[0], w_out[0])

    out = _ffn(x2, ffn2_norm[0].reshape(1, d), ffn2_w_gate[0], ffn2_w_up[0], ffn2_w_down[0], fin,
               apply_final=True)
    return out.reshape(batch, seq, d)
```

```python
import functools

import jax
import jax.numpy as jnp
from jax import lax
from jax.experimental import pallas as pl
from jax.experimental.pallas import tpu as pltpu

F32 = jnp.float32
BF16 = jnp.bfloat16

D_MODEL = 1024
ATTN_HEAD_DIM = 64
ATTN_HEADS = 16
ATTN_KV_HEADS = 4
ATTN_GROUP = 4
WINDOW = 128
MLSTM_HEADS = 8
MLSTM_V_DIM = 128
MLSTM_QK_DIM = 64
CONV_WIDTH = 4
GATE_SOFTCAP = 15.0
D_FF = 2816
FFN_RESIDUAL_WEIGHT = 0.5
NORM_EPS = 1e-6

SZ_AQ, SZ_AK, SZ_AV = 1024, 256, 256
SZ_MQ, SZ_MK, SZ_MV, SZ_MO = 512, 512, 1024, 1024
SZ_MI, SZ_MF = 8, 8

LANES = 128
SUBLANES = 8
MXU_DIM = 256
VMEM_LIMIT_BYTES = 56 * 1024 * 1024

MLSTM_CHUNK = 128
FF_CHUNK = MXU_DIM
FFN_STAGE_SLOTS = 4
TM_FFN = 1024
TM_INPROJ = 1024
TM_MERGE = 1024
TM_MLSTM = 8 * MLSTM_CHUNK
ATTN_LOOKAHEAD = 5
ATTN_BLOCKS = 16
MLSTM_LOOKAHEAD = 2

LOG2E = 1.4426950408889634
K_SCALE_LOG2 = 3.0
ROW_M2, ROW_WINTER, ROW_ENEG, ROW_WK, ROW_DECAY = 0, 8, 16, 24, 32
GATE_ROWS = 40

OFF_AQ = 0
OFF_AK = OFF_AQ + SZ_AQ
OFF_AV = OFF_AK + SZ_AK
OFF_MQ = OFF_AV + SZ_AV
OFF_MK = OFF_MQ + SZ_MQ
OFF_MV = OFF_MK + SZ_MK
OFF_MO = OFF_MV + SZ_MV
OFF_MI = OFF_MO + SZ_MO
OFF_MF = OFF_MI + SZ_MI
OFF_G = OFF_MF + SZ_MF
IN_WIDTH = OFF_G + 2 * D_MODEL
FEATURE_MAJOR_GROUPS = ((OFF_AQ, SZ_AQ), (OFF_AV, SZ_AV), (OFF_MV, SZ_MV), (OFF_MI, SZ_MI + SZ_MF))
FEATURE_MAJOR_ROWS = sum(size for _, size in FEATURE_MAJOR_GROUPS)
_FEATURE_MAJOR_BIAS_ROWS = tuple(
    (off, size, sum(s for _, s in FEATURE_MAJOR_GROUPS[:k]))
    for k, (off, size) in enumerate(FEATURE_MAJOR_GROUPS))
STAGE_ROWS = 256
STAGE_SLOTS = 4

ATTN_HEAD_ORDER = tuple((2 * (t // 4) + par) * ATTN_GROUP + t % 4
                        for t in range(ATTN_HEADS // 2) for par in range(2))


def _rms(x, g):
    return x * lax.rsqrt(jnp.mean(x * x, axis=-1, keepdims=True) + NORM_EPS) * g


def _const_spec(shape):
    zeros = (0,) * len(shape)
    return pl.BlockSpec(shape, lambda *_: zeros, pipeline_mode=pl.Buffered(1))


def _ffn_stage_weights(wg_hbm, wu_hbm, wd_hbm, wg_ref, wu_ref, wd_ref, gu_stage, d_stage, sem):
    n_chunks = D_FF // FF_CHUNK
    ahead = FFN_STAGE_SLOTS - 1

    def copies(c):
        slot = c % FFN_STAGE_SLOTS
        cols = pl.ds(c * FF_CHUNK, FF_CHUNK)
        return (pltpu.make_async_copy(wg_hbm.at[:, cols], gu_stage.at[slot, 0], sem.at[slot, 0]),
                pltpu.make_async_copy(wu_hbm.at[:, cols], gu_stage.at[slot, 1], sem.at[slot, 1]),
                pltpu.make_async_copy(wd_hbm.at[cols, :], d_stage.at[slot], sem.at[slot, 2]))

    for c in range(min(ahead, n_chunks)):
        for cp in copies(c):
            cp.start()
    for c in range(n_chunks):
        slot = c % FFN_STAGE_SLOTS
        if c + ahead < n_chunks:
            for cp in copies(c + ahead):
                cp.start()
        for cp in copies(c):
            cp.wait()
        sl = slice(c * FF_CHUNK, (c + 1) * FF_CHUNK)
        wg_ref[:, sl] = gu_stage[slot, 0].astype(BF16)
        wu_ref[:, sl] = gu_stage[slot, 1].astype(BF16)
        wd_ref[sl, :] = d_stage[slot].astype(BF16)


def _ffn_kernel(x_ref, g_ref, wg_hbm, wu_hbm, wd_hbm, fin_ref, o_ref,
                wg_ref, wu_ref, wd_ref, gu_stage, d_stage, sem, *, apply_final):
    @pl.when(pl.program_id(0) == 0)
    def _():
        _ffn_stage_weights(wg_hbm, wu_hbm, wd_hbm, wg_ref, wu_ref, wd_ref, gu_stage, d_stage, sem)

    x = x_ref[...]
    h = _rms(x, g_ref[...]).astype(BF16)
    for c in range(D_FF // FF_CHUNK):
        sl = slice(c * FF_CHUNK, (c + 1) * FF_CHUNK)
        a = jnp.dot(h, wg_ref[:, sl], preferred_element_type=F32)
        u = jnp.dot(h, wu_ref[:, sl], preferred_element_type=F32)
        act = (a * jax.nn.sigmoid(a) * (u * FFN_RESIDUAL_WEIGHT)).astype(BF16)
        d = jnp.dot(act, wd_ref[sl, :], preferred_element_type=F32)
        if c == 0:
            o_ref[...] = x + d
        else:
            o_ref[...] += d
    if apply_final:
        o_ref[...] = _rms(o_ref[...], fin_ref[...])


def _ffn(x, g, wg, wu, wd, fin, apply_final):
    n = x.shape[0]
    tm = TM_FFN
    return pl.pallas_call(
        functools.partial(_ffn_kernel, apply_final=apply_final),
        grid=(n // tm,),
        in_specs=[
            pl.BlockSpec((tm, D_MODEL), lambda i: (i, 0)),
            _const_spec((1, D_MODEL)),
            pl.BlockSpec(memory_space=pl.ANY),
            pl.BlockSpec(memory_space=pl.ANY),
            pl.BlockSpec(memory_space=pl.ANY),
            _const_spec((1, D_MODEL)),
        ],
        out_specs=pl.BlockSpec((tm, D_MODEL), lambda i: (i, 0)),
        out_shape=jax.ShapeDtypeStruct((n, D_MODEL), F32),
        scratch_shapes=[
            pltpu.VMEM((D_MODEL, D_FF), BF16),
            pltpu.VMEM((D_MODEL, D_FF), BF16),
            pltpu.VMEM((D_FF, D_MODEL), BF16),
            pltpu.VMEM((FFN_STAGE_SLOTS, 2, D_MODEL, FF_CHUNK), F32),
            pltpu.VMEM((FFN_STAGE_SLOTS, FF_CHUNK, D_MODEL), F32),
            pltpu.SemaphoreType.DMA((FFN_STAGE_SLOTS, 3)),
        ],
        compiler_params=pltpu.CompilerParams(
            dimension_semantics=("arbitrary",), vmem_limit_bytes=VMEM_LIMIT_BYTES),
        name="ffn_final" if apply_final else "ffn",
    )(x, g, wg, wu, wd, fin)


def _inproj_stage_weights(w_hbm, w_ref, stage, sem):
    bounds = [(r0, min(STAGE_ROWS, IN_WIDTH - r0)) for r0 in range(0, IN_WIDTH, STAGE_ROWS)]

    def copy(c):
        r0, rows = bounds[c]
        slot = c % STAGE_SLOTS
        return pltpu.make_async_copy(w_hbm.at[pl.ds(r0, rows), :],
                                     stage.at[slot, pl.ds(0, rows), :], sem.at[slot])

    for c in range(min(STAGE_SLOTS - 1, len(bounds))):
        copy(c).start()
    for c, (r0, rows) in enumerate(bounds):
        if c + STAGE_SLOTS - 1 < len(bounds):
            copy(c + STAGE_SLOTS - 1).start()
        copy(c).wait()
        w_ref[r0:r0 + rows, :] = stage[c % STAGE_SLOTS, 0:rows, :].astype(BF16)


def _inproj_kernel(x_ref, g_ref, w_hbm, brow_ref, bgate_ref, bcol_ref, conv_ref,
                   q_ref, kk_ref, avt_ref, mqk_ref, mvt_ref, mo_ref, gate_ref, gt_ref,
                   w_ref, stage, sem, conv_scr, *, tiles_per_seq):
    i = pl.program_id(0)
    tm = x_ref.shape[0]

    @pl.when(i == 0)
    def _():
        _inproj_stage_weights(w_hbm, w_ref, stage, sem)

    h = _rms(x_ref[...], g_ref[...]).astype(BF16)
    contract_lanes = (((1,), (1,)), ((), ()))

    def seg(r0, width, bias):
        return lax.dot_general(h, w_ref[r0:r0 + width, :], contract_lanes,
                               preferred_element_type=F32) + bias

    def seg_t(r0, rows):
        c0 = next(base + r0 - off for off, size, base in _FEATURE_MAJOR_BIAS_ROWS
                  if off <= r0 < off + size)
        return (lax.dot_general(w_ref[r0:r0 + rows, :], h, contract_lanes,
                                preferred_element_type=F32) + bcol_ref[c0:c0 + rows, :])

    @pl.when(i == 0)
    def _():
        conv_scr[...] = jnp.zeros_like(conv_scr)

    carry = jnp.where(i % tiles_per_seq == 0, 0.0, conv_scr[...])

    def conv_slab(c, width):
        z = seg(OFF_MQ + c, width, brow_ref[:, OFF_MQ + c:OFF_MQ + c + width])
        conv_scr[:, c:c + width] = z[tm - SUBLANES:tm, :]
        zc = jnp.concatenate([carry[:, c:c + width], z], axis=0)
        acc = z * conv_ref[CONV_WIDTH - 1:CONV_WIDTH, c:c + width]
        for k in range(1, CONV_WIDTH):
            zk = pltpu.roll(zc, k, 0)[SUBLANES:, :]
            acc = acc + zk * conv_ref[CONV_WIDTH - 1 - k:CONV_WIDTH - k, c:c + width]
        mqk_ref[:, c:c + width] = (acc * jax.nn.sigmoid(acc)).astype(BF16)

    dw = MXU_DIM
    for k in range(SZ_AQ // dw):
        sl = slice(k * dw, (k + 1) * dw)
        conv_slab(k * dw, dw)
        q_ref[sl, :] = (seg_t(OFF_AQ + k * dw, dw) * (LOG2E * ATTN_HEAD_DIM ** -0.5)).astype(BF16)
        if k == 0:
            kk_ref[...] = seg(OFF_AK, SZ_AK, brow_ref[:, OFF_AK:OFF_AK + SZ_AK]).astype(BF16)
        elif k == 1:
            avt_ref[...] = seg_t(OFF_AV, SZ_AV).astype(BF16)
        mvt_ref[sl, :] = seg_t(OFF_MV + k * dw, dw).astype(BF16)
        mo = seg(OFF_MO + k * dw, dw, brow_ref[:, OFF_MO + k * dw:OFF_MO + (k + 1) * dw])
        mo_ref[:, sl] = jax.nn.sigmoid(mo).astype(BF16)
        gsl = slice(2 * k * dw, (2 * k + 2) * dw)
        gate_ref[:, gsl] = seg(OFF_G + 2 * k * dw, 2 * dw, bgate_ref[:, gsl]).astype(BF16)

    gt_ref[...] = seg_t(OFF_MI, SZ_MI + SZ_MF)


def _inproj(x, g, w_in, b_in, conv, seq):
    w = w_in.T
    b_row = b_in.reshape(1, IN_WIDTH)
    b_gate = b_in[OFF_G:].reshape(1, 2 * D_MODEL)
    b_col = jnp.concatenate([b_in[off:off + size] for off, size in FEATURE_MAJOR_GROUPS]).reshape(
        FEATURE_MAJOR_ROWS, 1)
    n = x.shape[0]
    tm = TM_INPROJ
    row = lambda width: pl.BlockSpec((tm, width), lambda i: (i, 0))
    col = lambda height: pl.BlockSpec((height, tm), lambda i: (0, i))
    bf = lambda width: jax.ShapeDtypeStruct((n, width), BF16)
    return pl.pallas_call(
        functools.partial(_inproj_kernel, tiles_per_seq=seq // tm),
        grid=(n // tm,),
        in_specs=[
            row(D_MODEL),
            _const_spec((1, D_MODEL)),
            pl.BlockSpec(memory_space=pl.ANY),
            _const_spec((1, IN_WIDTH)),
            _const_spec((1, 2 * D_MODEL)),
            _const_spec((FEATURE_MAJOR_ROWS, 1)),
            _const_spec((CONV_WIDTH, SZ_MQ + SZ_MK)),
        ],
        out_specs=[col(SZ_AQ), row(SZ_AK), col(SZ_AV), row(SZ_MQ + SZ_MK), col(SZ_MV), row(SZ_MO),
                   row(2 * D_MODEL), col(2 * MLSTM_HEADS)],
        out_shape=[jax.ShapeDtypeStruct((SZ_AQ, n), BF16), bf(SZ_AK),
                   jax.ShapeDtypeStruct((SZ_AV, n), BF16),
                   bf(SZ_MQ + SZ_MK), jax.ShapeDtypeStruct((SZ_MV, n), BF16), bf(SZ_MO),
                   bf(2 * D_MODEL), jax.ShapeDtypeStruct((2 * MLSTM_HEADS, n), F32)],
        scratch_shapes=[pltpu.VMEM((IN_WIDTH, D_MODEL), BF16),
                        pltpu.VMEM((STAGE_SLOTS, STAGE_ROWS, D_MODEL), F32),
                        pltpu.SemaphoreType.DMA((STAGE_SLOTS,)),
                        pltpu.VMEM((SUBLANES, SZ_MQ + SZ_MK), F32)],
        compiler_params=pltpu.CompilerParams(
            dimension_semantics=("arbitrary",), vmem_limit_bytes=VMEM_LIMIT_BYTES),
        name="inproj",
    )(x, g, w, b_row, b_gate, b_col, conv)


def _attn_kernel(sink_ref, q_ref, kc_ref, kp_ref, vtc_ref, vtp_ref, o_ref, k_scr, vt_scr):
    w = WINDOW
    k_scr[0:w, :] = kp_ref[...]
    k_scr[w:, :] = kc_ref[...]
    vt_scr[:, 0:w] = vtp_ref[...]
    vt_scr[:, w:] = vtc_ref[...]
    kj = lax.broadcasted_iota(jnp.int32, (2 * w, w), 0)
    qi = lax.broadcasted_iota(jnp.int32, (2 * w, w), 1)
    band = (kj > qi) & (kj <= qi + w)
    band_first = band & ((kj >= w) | (pl.program_id(1) > 0))
    pairs = [(jb, t) for jb in range(q_ref.shape[1] // w) for t in range(ATTN_HEADS // 2)]
    zeros_qt = jnp.zeros((ATTN_HEAD_DIM, w), BF16)

    def scores(jb, t):
        ha, hb = ATTN_HEAD_ORDER[2 * t], ATTN_HEAD_ORDER[2 * t + 1]
        qa = q_ref[ha * ATTN_HEAD_DIM:(ha + 1) * ATTN_HEAD_DIM, jb * w:(jb + 1) * w]
        qb = q_ref[hb * ATTN_HEAD_DIM:(hb + 1) * ATTN_HEAD_DIM, jb * w:(jb + 1) * w]
        qt2 = jnp.concatenate([jnp.concatenate([qa, zeros_qt], axis=0),
                               jnp.concatenate([zeros_qt, qb], axis=0)], axis=1)
        kk = k_scr[jb * w:(jb + 2) * w, (t // 4) * LANES:(t // 4 + 1) * LANES]
        return jnp.dot(kk, qt2, preferred_element_type=F32)

    def head_out(jb, t, par, s):
        s = jnp.where(band_first if jb == 0 else band, s, -jnp.inf)
        sink = sink_ref[ATTN_HEAD_ORDER[2 * t + par]] * LOG2E
        mx = jnp.maximum(jnp.max(s, axis=0, keepdims=True), sink)
        pr = jnp.exp2(s - mx)
        denom = jnp.sum(pr, axis=0, keepdims=True) + jnp.exp2(sink - mx)
        kvh = 2 * (t // 4) + par
        vt = vt_scr[kvh * ATTN_HEAD_DIM:(kvh + 1) * ATTN_HEAD_DIM, jb * w:(jb + 2) * w]
        return jnp.dot(vt, pr.astype(BF16), preferred_element_type=F32) * (1.0 / denom)

    pending = [scores(*u) for u in pairs[:ATTN_LOOKAHEAD]]
    for idx, (jb, t) in enumerate(pairs):
        s2 = pending[idx]
        pending[idx] = None
        outs = [head_out(jb, t, par, s2[:, par * w:(par + 1) * w]) for par in range(2)]
        if idx + ATTN_LOOKAHEAD < len(pairs):
            pending.append(scores(*pairs[idx + ATTN_LOOKAHEAD]))
        pair = jnp.concatenate(outs, axis=0)
        o_ref[jb * w:(jb + 1) * w, t * LANES:(t + 1) * LANES] = pair.T.astype(BF16)


def _attention(sinks, q, kk, vt, batch, seq):
    n = kk.shape[0]
    tq = ATTN_BLOCKS * WINDOW
    nt = seq // tq
    cur = lambda b, j: b * nt + j
    prev = lambda b, j: (b * nt + j) * ATTN_BLOCKS - jnp.minimum(j, 1)
    return pl.pallas_call(
        _attn_kernel,
        grid=(batch, nt),
        in_specs=[
            pl.BlockSpec(memory_space=pltpu.SMEM),
            pl.BlockSpec((SZ_AQ, tq), lambda b, j: (0, cur(b, j))),
            pl.BlockSpec((tq, SZ_AK), lambda b, j: (cur(b, j), 0)),
            pl.BlockSpec((WINDOW, SZ_AK), lambda b, j: (prev(b, j), 0)),
            pl.BlockSpec((SZ_AV, tq), lambda b, j: (0, cur(b, j))),
            pl.BlockSpec((SZ_AV, WINDOW), lambda b, j: (0, prev(b, j))),
        ],
        out_specs=pl.BlockSpec((tq, SZ_AQ), lambda b, j: (cur(b, j), 0)),
        out_shape=jax.ShapeDtypeStruct((n, SZ_AQ), BF16),
        scratch_shapes=[pltpu.VMEM((WINDOW + tq, SZ_AK), BF16),
                        pltpu.VMEM((SZ_AV, WINDOW + tq), BF16)],
        compiler_params=pltpu.CompilerParams(
            dimension_semantics=("arbitrary", "arbitrary"), vmem_limit_bytes=VMEM_LIMIT_BYTES),
        name="attention",
    )(sinks, q, kk, kk, vt, vt)


def _log_sigmoid(x):
    return -(jnp.maximum(-x, 0.0) + jnp.log(1.0 + jnp.exp(-jnp.abs(x))))


def _mlstm_gate_kernel(g_ref, rows_ref, cols_ref, *, seq):
    L = MLSTM_CHUNK
    nh = MLSTM_HEADS
    n = g_ref.shape[1]
    g = g_ref[...]
    ig = GATE_SOFTCAP * jnp.tanh(g[0:nh] / GATE_SOFTCAP)
    fg = GATE_SOFTCAP * jnp.tanh(g[nh:2 * nh] / GATE_SOFTCAP)
    lf = _log_sigmoid(fg)
    pos = lax.broadcasted_iota(jnp.int32, (nh, n), 1) & (L - 1)
    a = lf
    sh = 1
    while sh < L:
        a = a + jnp.where(pos >= sh, pltpu.roll(a, sh, 1), 0.0)
        sh *= 2
    bvec = ig - a
    pm = bvec
    sh = 1
    while sh < L:
        pm = jnp.maximum(pm, jnp.where(pos >= sh, pltpu.roll(pm, sh, 1), -jnp.inf))
        sh *= 2
    pad = jnp.zeros((L - nh, L), F32)
    m = [jnp.zeros((nh, L), F32) for _ in range(n // seq)]
    for c in range(seq // L):
        for s in range(n // seq):
            sl = slice(s * seq + c * L, s * seq + (c + 1) * L)
            a_c, b_c = a[:, sl], bvec[:, sl]
            mrow = jnp.maximum(m[s], pm[:, sl])
            total = jnp.broadcast_to(a_c[:, L - 1:L], (nh, L))
            mlast = jnp.broadcast_to(mrow[:, L - 1:L], (nh, L))
            rows_ref[ROW_M2:ROW_M2 + nh, sl] = mrow * LOG2E
            rows_ref[ROW_WINTER:ROW_WINTER + nh, sl] = jnp.exp(m[s] - mrow)
            rows_ref[ROW_ENEG:ROW_ENEG + nh, sl] = jnp.exp(-(a_c + mrow))
            rows_ref[ROW_WK:ROW_WK + nh, sl] = jnp.exp(b_c - mlast) * MLSTM_QK_DIM ** -0.5
            rows_ref[ROW_DECAY:ROW_DECAY + nh, sl] = jnp.exp(m[s] - mlast)
            m[s] = total + mlast
            cols_ref[sl, :] = jnp.concatenate([b_c * LOG2E - K_SCALE_LOG2, pad], axis=0).T


def _mlstm_gates(gt, seq):
    n = gt.shape[1]
    return pl.pallas_call(
        functools.partial(_mlstm_gate_kernel, seq=seq),
        grid=(1,),
        in_specs=[pl.BlockSpec((2 * MLSTM_HEADS, n), lambda i: (0, 0))],
        out_specs=[pl.BlockSpec((GATE_ROWS, n), lambda i: (0, 0)),
                   pl.BlockSpec((n, LANES), lambda i: (0, 0))],
        out_shape=[jax.ShapeDtypeStruct((GATE_ROWS, n), F32),
                   jax.ShapeDtypeStruct((n, LANES), F32)],
        compiler_params=pltpu.CompilerParams(
            dimension_semantics=("arbitrary",), vmem_limit_bytes=VMEM_LIMIT_BYTES),
        name="mlstm_gates",
    )(gt)


def _mlstm_kernel(rows_ref, cols_ref, qk_ref, vt_ref, og_ref, o_ref, ct_scr, n_scr):
    L = MLSTM_CHUNK
    nh = MLSTM_HEADS
    contract_lanes = (((1,), (1,)), ((), ()))

    @pl.when(pl.program_id(1) == 0)
    def _():
        ct_scr[...] = jnp.zeros_like(ct_scr)
        n_scr[...] = jnp.zeros_like(n_scr)

    row_i = lax.broadcasted_iota(jnp.int32, (L, L), 0)
    col_i = lax.broadcasted_iota(jnp.int32, (L, L), 1)
    causal_t = row_i <= col_i
    lo_lane = col_i < MLSTM_QK_DIM
    zero = jnp.zeros((), BF16)

    n_chunks = qk_ref.shape[0] // L
    pairs = [(ci, p) for ci in range(n_chunks) for p in range(nh // 2)]
    ct = [ct_scr[p] for p in range(nh // 2)]
    nn = [n_scr[p] for p in range(nh // 2)]
    gate_rows = {}

    def chunk_gates(ci):
        if ci not in gate_rows:
            rs = slice(ci * L, (ci + 1) * L)
            wk = rows_ref[ROW_WK:ROW_WK + nh, rs]
            gate_rows[ci] = dict(
                m2=rows_ref[ROW_M2:ROW_M2 + nh, rs], w_inter=rows_ref[ROW_WINTER:ROW_WINTER + nh, rs],
                e_neg=rows_ref[ROW_ENEG:ROW_ENEG + nh, rs], wk=wk,
                decay=rows_ref[ROW_DECAY:ROW_DECAY + nh, rs], b2cols=cols_ref[rs, :],
                wk_b=jnp.concatenate([wk, wk], axis=0).astype(BF16))
        return gate_rows[ci]

    def front(ci, p):
        g = chunk_gates(ci)
        rs = slice(ci * L, (ci + 1) * L)
        h0, h1 = 2 * p, 2 * p + 1
        q2 = qk_ref[rs, p * LANES:(p + 1) * LANES]
        k2 = qk_ref[rs, SZ_MQ + p * LANES:SZ_MQ + (p + 1) * LANES]
        qm2 = jnp.concatenate([jnp.where(lo_lane, q2, zero), jnp.where(lo_lane, zero, q2)], axis=0)
        n2b = jnp.broadcast_to(nn[p], (2 * SUBLANES, LANES)).astype(BF16)
        stacked = jnp.concatenate([k2, ct[p].astype(BF16), n2b], axis=0)
        res = lax.dot_general(stacked, qm2, contract_lanes, preferred_element_type=F32)
        vw = [(vt_ref[hh * MLSTM_V_DIM:(hh + 1) * MLSTM_V_DIM, rs].astype(F32)
               * g["wk"][hh:hh + 1, :]).astype(BF16) for hh in (h0, h1)]
        upd = jnp.dot(jnp.concatenate(vw + [g["wk_b"]], axis=0), k2,
                      preferred_element_type=F32)
        n_inc = upd[2 * MLSTM_V_DIM:]
        dec = jnp.where(lo_lane[0:1, :], g["decay"][h0:h0 + 1, :], g["decay"][h1:h1 + 1, :])
        ct[p] = dec * ct[p] + jnp.where(lo_lane, upd[0:MLSTM_V_DIM], upd[MLSTM_V_DIM:2 * MLSTM_V_DIM])
        nn[p] = dec * nn[p] + jnp.where(lo_lane[0:1, :], n_inc[h0:h0 + 1, :], n_inc[h1:h1 + 1, :])
        return res

    def back(ci, p, par, res):
        st = res[0:L, par * L:(par + 1) * L]
        inter = res[L:L + MLSTM_V_DIM, par * L:(par + 1) * L]
        qn = res[L + MLSTM_V_DIM:L + MLSTM_V_DIM + 1, par * L:(par + 1) * L]
        g = chunk_gates(ci)
        rs = slice(ci * L, (ci + 1) * L)
        hh = 2 * p + par
        vsl = slice(hh * MLSTM_V_DIM, (hh + 1) * MLSTM_V_DIM)
        dt = jnp.where(causal_t,
                       jnp.exp2(g["b2cols"][:, hh:hh + 1] - g["m2"][hh:hh + 1, :]), 0.0)
        sct = st * dt
        wi = g["w_inter"][hh:hh + 1, :]
        den = jnp.sum(sct, axis=0, keepdims=True) + wi * qn
        numt = jnp.dot(vt_ref[vsl, rs], sct.astype(BF16), preferred_element_type=F32) + wi * inter
        rden = 1.0 / jnp.maximum(jnp.abs(den), g["e_neg"][hh:hh + 1, :])
        ms = jnp.mean(numt * numt, axis=0, keepdims=True)
        scale = rden * lax.rsqrt(rden * rden * ms + NORM_EPS)
        hv = (numt * scale).T
        o_ref[rs, vsl] = (hv * og_ref[rs, vsl].astype(F32)).astype(BF16)

    pending = [front(*u) for u in pairs[:MLSTM_LOOKAHEAD]]
    for idx, (ci, p) in enumerate(pairs):
        res = pending[idx]
        pending[idx] = None
        back(ci, p, 0, res)
        if idx + MLSTM_LOOKAHEAD < len(pairs):
            pending.append(front(*pairs[idx + MLSTM_LOOKAHEAD]))
        back(ci, p, 1, res)
    for p in range(nh // 2):
        ct_scr[p] = ct[p]
        n_scr[p] = nn[p]


def _mlstm(rows, cols, qk, v, og, batch, seq):
    n = qk.shape[0]
    tm = TM_MLSTM
    nt = seq // tm
    row = pl.BlockSpec((tm, SZ_MV), lambda b, c: (b * nt + c, 0))
    return pl.pallas_call(
        _mlstm_kernel,
        grid=(batch, nt),
        in_specs=[
            pl.BlockSpec((GATE_ROWS, tm), lambda b, c: (0, b * nt + c)),
            pl.BlockSpec((tm, LANES), lambda b, c: (b * nt + c, 0)),
            row,
            pl.BlockSpec((SZ_MV, tm), lambda b, c: (0, b * nt + c)),
            row,
        ],
        out_specs=row,
        out_shape=jax.ShapeDtypeStruct((n, SZ_MV), BF16),
        scratch_shapes=[
            pltpu.VMEM((MLSTM_HEADS // 2, 2 * MLSTM_QK_DIM, MLSTM_V_DIM), F32),
            pltpu.VMEM((MLSTM_HEADS // 2, 1, 2 * MLSTM_QK_DIM), F32),
        ],
        compiler_params=pltpu.CompilerParams(
            dimension_semantics=("arbitrary", "arbitrary"), vmem_limit_bytes=VMEM_LIMIT_BYTES),
        name="mlstm",
    )(rows, cols, qk, v, og)


def _merge_stage_weights(wpa_hbm, wpm_hbm, wo_hbm, hn_ref, wpa_ref, wpm_ref, wo_ref, stage, sem):
    hd = ATTN_HEAD_DIM
    wpa_copies = [pltpu.make_async_copy(wpa_hbm.at[pl.ds(head * hd, hd), :],
                                        stage.at[0, pl.ds(pos * hd, hd), :], sem.at[0])
                  for pos, head in enumerate(ATTN_HEAD_ORDER)]
    wpm_copy = pltpu.make_async_copy(wpm_hbm, stage.at[1], sem.at[1])
    wo_copy = pltpu.make_async_copy(wo_hbm, stage.at[0], sem.at[0])
    for cp in wpa_copies:
        cp.start()
    wpm_copy.start()
    for cp in wpa_copies:
        cp.wait()
    wpa_ref[...] = stage[0].astype(BF16)
    wo_copy.start()
    wpm_copy.wait()
    wpm_ref[...] = (stage[1] * hn_ref[...]).astype(BF16)
    wo_copy.wait()
    wo_ref[...] = stage[0].astype(BF16)


def _merge_kernel(x_ref, a_ref, hm_ref, gate_ref, hn_ref, wpa_hbm, wpm_hbm, wo_hbm, o_ref,
                  wpa_ref, wpm_ref, wo_ref, stage, sem):
    @pl.when(pl.program_id(0) == 0)
    def _():
        _merge_stage_weights(wpa_hbm, wpm_hbm, wo_hbm, hn_ref, wpa_ref, wpm_ref, wo_ref, stage, sem)

    ya = jnp.dot(a_ref[...], wpa_ref[...], preferred_element_type=F32)
    ym = jnp.dot(hm_ref[...], wpm_ref[...], preferred_element_type=F32)
    ga = jax.nn.sigmoid(gate_ref[:, 0:D_MODEL].astype(F32))
    gm = jax.nn.sigmoid(gate_ref[:, D_MODEL:2 * D_MODEL].astype(F32))
    merged = (ga * ya + gm * ym).astype(BF16)
    o_ref[...] = x_ref[...] + jnp.dot(merged, wo_ref[...], preferred_element_type=F32)


def _merge(x, a, hm, gates, hn_col, wpa, wpm, wo):
    n = x.shape[0]
    tm = TM_MERGE
    row = lambda width: pl.BlockSpec((tm, width), lambda i: (i, 0))
    return pl.pallas_call(
        _merge_kernel,
        grid=(n // tm,),
        in_specs=[row(D_MODEL), row(SZ_AQ), row(SZ_MV), row(2 * D_MODEL), _const_spec((SZ_MV, 1)),
                  pl.BlockSpec(memory_space=pl.ANY), pl.BlockSpec(memory_space=pl.ANY),
                  pl.BlockSpec(memory_space=pl.ANY)],
        out_specs=row(D_MODEL),
        out_shape=jax.ShapeDtypeStruct((n, D_MODEL), F32),
        scratch_shapes=[pltpu.VMEM((D_MODEL, D_MODEL), BF16)] * 3 + [
            pltpu.VMEM((2, D_MODEL, D_MODEL), F32), pltpu.SemaphoreType.DMA((2,))],
        compiler_params=pltpu.CompilerParams(
            dimension_semantics=("arbitrary",), vmem_limit_bytes=VMEM_LIMIT_BYTES),
        name="merge",
    )(x, a, hm, gates, hn_col, wpa, wpm, wo)


def kernel(x, ffn1_norm, ffn1_w_gate, ffn1_w_up, ffn1_w_down, mix_norm, w_in, b_in, attn_sinks,
           mlstm_conv, mlstm_head_norm, w_proj_attn, w_proj_mlstm, w_out, ffn2_norm, ffn2_w_gate,
           ffn2_w_up, ffn2_w_down, final_norm):
    batch, seq, d = x.shape
    assert d == D_MODEL and ffn1_norm.shape[0] == 1, "one layer of width D_MODEL"
    assert all(seq % t == 0 for t in (TM_INPROJ, TM_MLSTM, ATTN_BLOCKS * WINDOW))
    assert all((batch * seq) % t == 0 for t in (TM_FFN, TM_MERGE))
    n = batch * seq
    xf = x.reshape(n, d)
    fin = final_norm.reshape(1, d)

    x1 = _ffn(xf, ffn1_norm[0].reshape(1, d), ffn1_w_gate[0], ffn1_w_up[0], ffn1_w_down[0], fin,
              apply_final=False)

    q, kk, avt, mqk, mvt, mo, gates, gt = _inproj(x1, mix_norm[0].reshape(1, d), w_in[0], b_in[0],
                                                  mlstm_conv[0], seq)

    ya = _attention(attn_sinks[0], q, kk, avt, batch, seq)

    rows, cols = _mlstm_gates(gt, seq)
    hm = _mlstm(rows, cols, mqk, mvt, mo, batch, seq)

    x2 = _merge(x1, ya, hm, gates, mlstm_head_norm[0].reshape(SZ_MV, 1), w_proj_attn[0],
                w_proj_mlstm[0], w_out[0])

    out = _ffn(x2, ffn2_norm[0].reshape(1, d), ffn2_w_gate[0], ffn2_w_up[0], ffn2_w_down[0], fin,
               apply_final=True)
    return out.reshape(batch, seq, d)
```

```python
import functools

import jax
import jax.numpy as jnp
from jax import lax
from jax.experimental import pallas as pl
from jax.experimental.pallas import tpu as pltpu

F32 = jnp.float32
BF16 = jnp.bfloat16

D_MODEL = 1024
ATTN_HEAD_DIM = 64
ATTN_HEADS = 16
ATTN_KV_HEADS = 4
ATTN_GROUP = 4
WINDOW = 128
MLSTM_HEADS = 8
MLSTM_V_DIM = 128
MLSTM_QK_DIM = 64
CONV_WIDTH = 4
GATE_SOFTCAP = 15.0
D_FF = 2816
FFN_RESIDUAL_WEIGHT = 0.5
NORM_EPS = 1e-6

SZ_AQ, SZ_AK, SZ_AV = 1024, 256, 256
SZ_MQ, SZ_MK, SZ_MV, SZ_MO = 512, 512, 1024, 1024
SZ_MI, SZ_MF = 8, 8

LANES = 128
SUBLANES = 8
MXU_DIM = 256
VMEM_LIMIT_BYTES = 56 * 1024 * 1024

MLSTM_CHUNK = 128
FF_CHUNK = MXU_DIM
FFN_STAGE_SLOTS = 4
TM_FFN = 1024
TM_INPROJ = 1024
TM_MERGE = 1024
TM_MLSTM = 8 * MLSTM_CHUNK
ATTN_LOOKAHEAD = 5
ATTN_BLOCKS = 16
MLSTM_LOOKAHEAD = 2

LOG2E = 1.4426950408889634
K_SCALE_LOG2 = 3.0
ROW_M2, ROW_WINTER, ROW_ENEG, ROW_WK, ROW_DECAY = 0, 8, 16, 24, 32
GATE_ROWS = 40

OFF_AQ = 0
OFF_AK = OFF_AQ + SZ_AQ
OFF_AV = OFF_AK + SZ_AK
OFF_MQ = OFF_AV + SZ_AV
OFF_MK = OFF_MQ + SZ_MQ
OFF_MV = OFF_MK + SZ_MK
OFF_MO = OFF_MV + SZ_MV
OFF_MI = OFF_MO + SZ_MO
OFF_MF = OFF_MI + SZ_MI
OFF_G = OFF_MF + SZ_MF
IN_WIDTH = OFF_G + 2 * D_MODEL
FEATURE_MAJOR_GROUPS = ((OFF_AQ, SZ_AQ), (OFF_AV, SZ_AV), (OFF_MV, SZ_MV), (OFF_MI, SZ_MI + SZ_MF))
FEATURE_MAJOR_ROWS = sum(size for _, size in FEATURE_MAJOR_GROUPS)
_FEATURE_MAJOR_BIAS_ROWS = tuple(
    (off, size, sum(s for _, s in FEATURE_MAJOR_GROUPS[:k]))
    for k, (off, size) in enumerate(FEATURE_MAJOR_GROUPS))
STAGE_ROWS = 256
STAGE_SLOTS = 4

ATTN_HEAD_ORDER = tuple((2 * (t // 4) + par) * ATTN_GROUP + t % 4
                        for t in range(ATTN_HEADS // 2) for par in range(2))


def _rms(x, g):
    return x * lax.rsqrt(jnp.mean(x * x, axis=-1, keepdims=True) + NORM_EPS) * g


def _const_spec(shape):
    zeros = (0,) * len(shape)
    return pl.BlockSpec(shape, lambda *_: zeros, pipeline_mode=pl.Buffered(1))


def _ffn_stage_weights(wg_hbm, wu_hbm, wd_hbm, wg_ref, wu_ref, wd_ref, gu_stage, d_stage, sem):
    n_chunks = D_FF // FF_CHUNK
    ahead = FFN_STAGE_SLOTS - 1

    def copies(c):
        slot = c % FFN_STAGE_SLOTS
        cols = pl.ds(c * FF_CHUNK, FF_CHUNK)
        return (pltpu.make_async_copy(wg_hbm.at[:, cols], gu_stage.at[slot, 0], sem.at[slot, 0]),
                pltpu.make_async_copy(wu_hbm.at[:, cols], gu_stage.at[slot, 1], sem.at[slot, 1]),
                pltpu.make_async_copy(wd_hbm.at[cols, :], d_stage.at[slot], sem.at[slot, 2]))

    for c in range(min(ahead, n_chunks)):
        for cp in copies(c):
            cp.start()
    for c in range(n_chunks):
        slot = c % FFN_STAGE_SLOTS
        if c + ahead < n_chunks:
            for cp in copies(c + ahead):
                cp.start()
        for cp in copies(c):
            cp.wait()
        sl = slice(c * FF_CHUNK, (c + 1) * FF_CHUNK)
        wg_ref[:, sl] = gu_stage[slot, 0].astype(BF16)
        wu_ref[:, sl] = gu_stage[slot, 1].astype(BF16)
        wd_ref[sl, :] = d_stage[slot].astype(BF16)


def _ffn_kernel(x_ref, g_ref, wg_hbm, wu_hbm, wd_hbm, fin_ref, o_ref,
                wg_ref, wu_ref, wd_ref, gu_stage, d_stage, sem, *, apply_final):
    @pl.when(pl.program_id(0) == 0)
    def _():
        _ffn_stage_weights(wg_hbm, wu_hbm, wd_hbm, wg_ref, wu_ref, wd_ref, gu_stage, d_stage, sem)

    x = x_ref[...]
    h = _rms(x, g_ref[...]).astype(BF16)
    for c in range(D_FF // FF_CHUNK):
        sl = slice(c * FF_CHUNK, (c + 1) * FF_CHUNK)
        a = jnp.dot(h, wg_ref[:, sl], preferred_element_type=F32)
        u = jnp.dot(h, wu_ref[:, sl], preferred_element_type=F32)
        act = (a * jax.nn.sigmoid(a) * (u * FFN_RESIDUAL_WEIGHT)).astype(BF16)
        d = jnp.dot(act, wd_ref[sl, :], preferred_element_type=F32)
        if c == 0:
            o_ref[...] = x + d
        else:
            o_ref[...] += d
    if apply_final:
        o_ref[...] = _rms(o_ref[...], fin_ref[...])


def _ffn(x, g, wg, wu, wd, fin, apply_final):
    n = x.shape[0]
    tm = TM_FFN
    return pl.pallas_call(
        functools.partial(_ffn_kernel, apply_final=apply_final),
        grid=(n // tm,),
        in_specs=[
            pl.BlockSpec((tm, D_MODEL), lambda i: (i, 0)),
            _const_spec((1, D_MODEL)),
            pl.BlockSpec(memory_space=pl.ANY),
            pl.BlockSpec(memory_space=pl.ANY),
            pl.BlockSpec(memory_space=pl.ANY),
            _const_spec((1, D_MODEL)),
        ],
        out_specs=pl.BlockSpec((tm, D_MODEL), lambda i: (i, 0)),
        out_shape=jax.ShapeDtypeStruct((n, D_MODEL), F32),
        scratch_shapes=[
            pltpu.VMEM((D_MODEL, D_FF), BF16),
            pltpu.VMEM((D_MODEL, D_FF), BF16),
            pltpu.VMEM((D_FF, D_MODEL), BF16),
            pltpu.VMEM((FFN_STAGE_SLOTS, 2, D_MODEL, FF_CHUNK), F32),
            pltpu.VMEM((FFN_STAGE_SLOTS, FF_CHUNK, D_MODEL), F32),
            pltpu.SemaphoreType.DMA((FFN_STAGE_SLOTS, 3)),
        ],
        compiler_params=pltpu.CompilerParams(
            dimension_semantics=("arbitrary",), vmem_limit_bytes=VMEM_LIMIT_BYTES),
        name="ffn_final" if apply_final else "ffn",
    )(x, g, wg, wu, wd, fin)


def _inproj_stage_weights(w_hbm, w_ref, stage, sem):
    bounds = [(r0, min(STAGE_ROWS, IN_WIDTH - r0)) for r0 in range(0, IN_WIDTH, STAGE_ROWS)]

    def copy(c):
        r0, rows = bounds[c]
        slot = c % STAGE_SLOTS
        return pltpu.make_async_copy(w_hbm.at[pl.ds(r0, rows), :],
                                     stage.at[slot, pl.ds(0, rows), :], sem.at[slot])

    for c in range(min(STAGE_SLOTS - 1, len(bounds))):
        copy(c).start()
    for c, (r0, rows) in enumerate(bounds):
        if c + STAGE_SLOTS - 1 < len(bounds):
            copy(c + STAGE_SLOTS - 1).start()
        copy(c).wait()
        w_ref[r0:r0 + rows, :] = stage[c % STAGE_SLOTS, 0:rows, :].astype(BF16)


def _inproj_kernel(x_ref, g_ref, w_hbm, brow_ref, bgate_ref, bcol_ref, conv_ref,
                   q_ref, kk_ref, avt_ref, mqk_ref, mvt_ref, mo_ref, gate_ref, gt_ref,
                   w_ref, stage, sem, conv_scr, *, tiles_per_seq):
    i = pl.program_id(0)
    tm = x_ref.shape[0]

    @pl.when(i == 0)
    def _():
        _inproj_stage_weights(w_hbm, w_ref, stage, sem)

    h = _rms(x_ref[...], g_ref[...]).astype(BF16)
    contract_lanes = (((1,), (1,)), ((), ()))

    def seg(r0, width, bias):
        return lax.dot_general(h, w_ref[r0:r0 + width, :], contract_lanes,
                               preferred_element_type=F32) + bias

    def seg_t(r0, rows):
        c0 = next(base + r0 - off for off, size, base in _FEATURE_MAJOR_BIAS_ROWS
                  if off <= r0 < off + size)
        return (lax.dot_general(w_ref[r0:r0 + rows, :], h, contract_lanes,
                                preferred_element_type=F32) + bcol_ref[c0:c0 + rows, :])

    @pl.when(i == 0)
    def _():
        conv_scr[...] = jnp.zeros_like(conv_scr)

    carry = jnp.where(i % tiles_per_seq == 0, 0.0, conv_scr[...])

    def conv_slab(c, width):
        z = seg(OFF_MQ + c, width, brow_ref[:, OFF_MQ + c:OFF_MQ + c + width])
        conv_scr[:, c:c + width] = z[tm - SUBLANES:tm, :]
        zc = jnp.concatenate([carry[:, c:c + width], z], axis=0)
        acc = z * conv_ref[CONV_WIDTH - 1:CONV_WIDTH, c:c + width]
        for k in range(1, CONV_WIDTH):
            zk = pltpu.roll(zc, k, 0)[SUBLANES:, :]
            acc = acc + zk * conv_ref[CONV_WIDTH - 1 - k:CONV_WIDTH - k, c:c + width]
        mqk_ref[:, c:c + width] = (acc * jax.nn.sigmoid(acc)).astype(BF16)

    dw = MXU_DIM
    for k in range(SZ_AQ // dw):
        sl = slice(k * dw, (k + 1) * dw)
        conv_slab(k * dw, dw)
        q_ref[sl, :] = (seg_t(OFF_AQ + k * dw, dw) * (LOG2E * ATTN_HEAD_DIM ** -0.5)).astype(BF16)
        if k == 0:
            kk_ref[...] = seg(OFF_AK, SZ_AK, brow_ref[:, OFF_AK:OFF_AK + SZ_AK]).astype(BF16)
        elif k == 1:
            avt_ref[...] = seg_t(OFF_AV, SZ_AV).astype(BF16)
        mvt_ref[sl, :] = seg_t(OFF_MV + k * dw, dw).astype(BF16)
        mo = seg(OFF_MO + k * dw, dw, brow_ref[:, OFF_MO + k * dw:OFF_MO + (k + 1) * dw])
        mo_ref[:, sl] = jax.nn.sigmoid(mo).astype(BF16)
        gsl = slice(2 * k * dw, (2 * k + 2) * dw)
        gate_ref[:, gsl] = seg(OFF_G + 2 * k * dw, 2 * dw, bgate_ref[:, gsl]).astype(BF16)

    gt_ref[...] = seg_t(OFF_MI, SZ_MI + SZ_MF)


def _inproj(x, g, w_in, b_in, conv, seq):
    w = w_in.T
    b_row = b_in.reshape(1, IN_WIDTH)
    b_gate = b_in[OFF_G:].reshape(1, 2 * D_MODEL)
    b_col = jnp.concatenate([b_in[off:off + size] for off, size in FEATURE_MAJOR_GROUPS]).reshape(
        FEATURE_MAJOR_ROWS, 1)
    n = x.shape[0]
    tm = TM_INPROJ
    row = lambda width: pl.BlockSpec((tm, width), lambda i: (i, 0))
    col = lambda height: pl.BlockSpec((height, tm), lambda i: (0, i))
    bf = lambda width: jax.ShapeDtypeStruct((n, width), BF16)
    return pl.pallas_call(
        functools.partial(_inproj_kernel, tiles_per_seq=seq // tm),
        grid=(n // tm,),
        in_specs=[
            row(D_MODEL),
            _const_spec((1, D_MODEL)),
            pl.BlockSpec(memory_space=pl.ANY),
            _const_spec((1, IN_WIDTH)),
            _const_spec((1, 2 * D_MODEL)),
            _const_spec((FEATURE_MAJOR_ROWS, 1)),
            _const_spec((CONV_WIDTH, SZ_MQ + SZ_MK)),
        ],
        out_specs=[col(SZ_AQ), row(SZ_AK), col(SZ_AV), row(SZ_MQ + SZ_MK), col(SZ_MV), row(SZ_MO),
                   row(2 * D_MODEL), col(2 * MLSTM_HEADS)],
        out_shape=[jax.ShapeDtypeStruct((SZ_AQ, n), BF16), bf(SZ_AK),
                   jax.ShapeDtypeStruct((SZ_AV, n), BF16),
                   bf(SZ_MQ + SZ_MK), jax.ShapeDtypeStruct((SZ_MV, n), BF16), bf(SZ_MO),
                   bf(2 * D_MODEL), jax.ShapeDtypeStruct((2 * MLSTM_HEADS, n), F32)],
        scratch_shapes=[pltpu.VMEM((IN_WIDTH, D_MODEL), BF16),
                        pltpu.VMEM((STAGE_SLOTS, STAGE_ROWS, D_MODEL), F32),
                        pltpu.SemaphoreType.DMA((STAGE_SLOTS,)),
                        pltpu.VMEM((SUBLANES, SZ_MQ + SZ_MK), F32)],
        compiler_params=pltpu.CompilerParams(
            dimension_semantics=("arbitrary",), vmem_limit_bytes=VMEM_LIMIT_BYTES),
        name="inproj",
    )(x, g, w, b_row, b_gate, b_col, conv)


def _attn_kernel(sink_ref, q_ref, kc_ref, kp_ref, vtc_ref, vtp_ref, o_ref, k_scr, vt_scr):
    w = WINDOW
    k_scr[0:w, :] = kp_ref[...]
    k_scr[w:, :] = kc_ref[...]
    vt_scr[:, 0:w] = vtp_ref[...]
    vt_scr[:, w:] = vtc_ref[...]
    kj = lax.broadcasted_iota(jnp.int32, (2 * w, w), 0)
    qi = lax.broadcasted_iota(jnp.int32, (2 * w, w), 1)
    band = (kj > qi) & (kj <= qi + w)
    band_first = band & ((kj >= w) | (pl.program_id(1) > 0))
    pairs = [(jb, t) for jb in range(q_ref.shape[1] // w) for t in range(ATTN_HEADS // 2)]
    zeros_qt = jnp.zeros((ATTN_HEAD_DIM, w), BF16)

    def scores(jb, t):
        ha, hb = ATTN_HEAD_ORDER[2 * t], ATTN_HEAD_ORDER[2 * t + 1]
        qa = q_ref[ha * ATTN_HEAD_DIM:(ha + 1) * ATTN_HEAD_DIM, jb * w:(jb + 1) * w]
        qb = q_ref[hb * ATTN_HEAD_DIM:(hb + 1) * ATTN_HEAD_DIM, jb * w:(jb + 1) * w]
        qt2 = jnp.concatenate([jnp.concatenate([qa, zeros_qt], axis=0),
                               jnp.concatenate([zeros_qt, qb], axis=0)], axis=1)
        kk = k_scr[jb * w:(jb + 2) * w, (t // 4) * LANES:(t // 4 + 1) * LANES]
        return jnp.dot(kk, qt2, preferred_element_type=F32)

    def head_out(jb, t, par, s):
        s = jnp.where(band_first if jb == 0 else band, s, -jnp.inf)
        sink = sink_ref[ATTN_HEAD_ORDER[2 * t + par]] * LOG2E
        mx = jnp.maximum(jnp.max(s, axis=0, keepdims=True), sink)
        pr = jnp.exp2(s - mx)
        denom = jnp.sum(pr, axis=0, keepdims=True) + jnp.exp2(sink - mx)
        kvh = 2 * (t // 4) + par
        vt = vt_scr[kvh * ATTN_HEAD_DIM:(kvh + 1) * ATTN_HEAD_DIM, jb * w:(jb + 2) * w]
        return jnp.dot(vt, pr.astype(BF16), preferred_element_type=F32) * (1.0 / denom)

    pending = [scores(*u) for u in pairs[:ATTN_LOOKAHEAD]]
    for idx, (jb, t) in enumerate(pairs):
        s2 = pending[idx]
        pending[idx] = None
        outs = [head_out(jb, t, par, s2[:, par * w:(par + 1) * w]) for par in range(2)]
        if idx + ATTN_LOOKAHEAD < len(pairs):
            pending.append(scores(*pairs[idx + ATTN_LOOKAHEAD]))
        pair = jnp.concatenate(outs, axis=0)
        o_ref[jb * w:(jb + 1) * w, t * LANES:(t + 1) * LANES] = pair.T.astype(BF16)


def _attention(sinks, q, kk, vt, batch, seq):
    n = kk.shape[0]
    tq = ATTN_BLOCKS * WINDOW
    nt = seq // tq
    cur = lambda b, j: b * nt + j
    prev = lambda b, j: (b * nt + j) * ATTN_BLOCKS - jnp.minimum(j, 1)
    return pl.pallas_call(
        _attn_kernel,
        grid=(batch, nt),
        in_specs=[
            pl.BlockSpec(memory_space=pltpu.SMEM),
            pl.BlockSpec((SZ_AQ, tq), lambda b, j: (0, cur(b, j))),
            pl.BlockSpec((tq, SZ_AK), lambda b, j: (cur(b, j), 0)),
            pl.BlockSpec((WINDOW, SZ_AK), lambda b, j: (prev(b, j), 0)),
            pl.BlockSpec((SZ_AV, tq), lambda b, j: (0, cur(b, j))),
            pl.BlockSpec((SZ_AV, WINDOW), lambda b, j: (0, prev(b, j))),
        ],
        out_specs=pl.BlockSpec((tq, SZ_AQ), lambda b, j: (cur(b, j), 0)),
        out_shape=jax.ShapeDtypeStruct((n, SZ_AQ), BF16),
        scratch_shapes=[pltpu.VMEM((WINDOW + tq, SZ_AK), BF16),
                        pltpu.VMEM((SZ_AV, WINDOW + tq), BF16)],
        compiler_params=pltpu.CompilerParams(
            dimension_semantics=("arbitrary", "arbitrary"), vmem_limit_bytes=VMEM_LIMIT_BYTES),
        name="attention",
    )(sinks, q, kk, kk, vt, vt)


def _log_sigmoid(x):
    return -(jnp.maximum(-x, 0.0) + jnp.log(1.0 + jnp.exp(-jnp.abs(x))))


def _mlstm_gate_kernel(g_ref, rows_ref, cols_ref, *, seq):
    L = MLSTM_CHUNK
    nh = MLSTM_HEADS
    n = g_ref.shape[1]
    g = g_ref[...]
    ig = GATE_SOFTCAP * jnp.tanh(g[0:nh] / GATE_SOFTCAP)
    fg = GATE_SOFTCAP * jnp.tanh(g[nh:2 * nh] / GATE_SOFTCAP)
    lf = _log_sigmoid(fg)
    pos = lax.broadcasted_iota(jnp.int32, (nh, n), 1) & (L - 1)
    a = lf
    sh = 1
    while sh < L:
        a = a + jnp.where(pos >= sh, pltpu.roll(a, sh, 1), 0.0)
        sh *= 2
    bvec = ig - a
    pm = bvec
    sh = 1
    while sh < L:
        pm = jnp.maximum(pm, jnp.where(pos >= sh, pltpu.roll(pm, sh, 1), -jnp.inf))
        sh *= 2
    pad = jnp.zeros((L - nh, L), F32)
    m = [jnp.zeros((nh, L), F32) for _ in range(n // seq)]
    for c in range(seq // L):
        for s in range(n // seq):
            sl = slice(s * seq + c * L, s * seq + (c + 1) * L)
            a_c, b_c = a[:, sl], bvec[:, sl]
            mrow = jnp.maximum(m[s], pm[:, sl])
            total = jnp.broadcast_to(a_c[:, L - 1:L], (nh, L))
            mlast = jnp.broadcast_to(mrow[:, L - 1:L], (nh, L))
            rows_ref[ROW_M2:ROW_M2 + nh, sl] = mrow * LOG2E
            rows_ref[ROW_WINTER:ROW_WINTER + nh, sl] = jnp.exp(m[s] - mrow)
            rows_ref[ROW_ENEG:ROW_ENEG + nh, sl] = jnp.exp(-(a_c + mrow))
            rows_ref[ROW_WK:ROW_WK + nh, sl] = jnp.exp(b_c - mlast) * MLSTM_QK_DIM ** -0.5
            rows_ref[ROW_DECAY:ROW_DECAY + nh, sl] = jnp.exp(m[s] - mlast)
            m[s] = total + mlast
            cols_ref[sl, :] = jnp.concatenate([b_c * LOG2E - K_SCALE_LOG2, pad], axis=0).T


def _mlstm_gates(gt, seq):
    n = gt.shape[1]
    return pl.pallas_call(
        functools.partial(_mlstm_gate_kernel, seq=seq),
        grid=(1,),
        in_specs=[pl.BlockSpec((2 * MLSTM_HEADS, n), lambda i: (0, 0))],
        out_specs=[pl.BlockSpec((GATE_ROWS, n), lambda i: (0, 0)),
                   pl.BlockSpec((n, LANES), lambda i: (0, 0))],
        out_shape=[jax.ShapeDtypeStruct((GATE_ROWS, n), F32),
                   jax.ShapeDtypeStruct((n, LANES), F32)],
        compiler_params=pltpu.CompilerParams(
            dimension_semantics=("arbitrary",), vmem_limit_bytes=VMEM_LIMIT_BYTES),
        name="mlstm_gates",
    )(gt)


def _mlstm_kernel(rows_ref, cols_ref, qk_ref, vt_ref, og_ref, o_ref, ct_scr, n_scr):
    L = MLSTM_CHUNK
    nh = MLSTM_HEADS
    contract_lanes = (((1,), (1,)), ((), ()))

    @pl.when(pl.program_id(1) == 0)
    def _():
        ct_scr[...] = jnp.zeros_like(ct_scr)
        n_scr[...] = jnp.zeros_like(n_scr)

    row_i = lax.broadcasted_iota(jnp.int32, (L, L), 0)
    col_i = lax.broadcasted_iota(jnp.int32, (L, L), 1)
    causal_t = row_i <= col_i
    lo_lane = col_i < MLSTM_QK_DIM
    zero = jnp.zeros((), BF16)

    n_chunks = qk_ref.shape[0] // L
    pairs = [(ci, p) for ci in range(n_chunks) for p in range(nh // 2)]
    ct = [ct_scr[p] for p in range(nh // 2)]
    nn = [n_scr[p] for p in range(nh // 2)]
    gate_rows = {}

    def chunk_gates(ci):
        if ci not in gate_rows:
            rs = slice(ci * L, (ci + 1) * L)
            wk = rows_ref[ROW_WK:ROW_WK + nh, rs]
            gate_rows[ci] = dict(
                m2=rows_ref[ROW_M2:ROW_M2 + nh, rs], w_inter=rows_ref[ROW_WINTER:ROW_WINTER + nh, rs],
                e_neg=rows_ref[ROW_ENEG:ROW_ENEG + nh, rs], wk=wk,
                decay=rows_ref[ROW_DECAY:ROW_DECAY + nh, rs], b2cols=cols_ref[rs, :],
                wk_b=jnp.concatenate([wk, wk], axis=0).astype(BF16))
        return gate_rows[ci]

    def front(ci, p):
        g = chunk_gates(ci)
        rs = slice(ci * L, (ci + 1) * L)
        h0, h1 = 2 * p, 2 * p + 1
        q2 = qk_ref[rs, p * LANES:(p + 1) * LANES]
        k2 = qk_ref[rs, SZ_MQ + p * LANES:SZ_MQ + (p + 1) * LANES]
        qm2 = jnp.concatenate([jnp.where(lo_lane, q2, zero), jnp.where(lo_lane, zero, q2)], axis=0)
        n2b = jnp.broadcast_to(nn[p], (2 * SUBLANES, LANES)).astype(BF16)
        stacked = jnp.concatenate([k2, ct[p].astype(BF16), n2b], axis=0)
        res = lax.dot_general(stacked, qm2, contract_lanes, preferred_element_type=F32)
        vw = [(vt_ref[hh * MLSTM_V_DIM:(hh + 1) * MLSTM_V_DIM, rs].astype(F32)
               * g["wk"][hh:hh + 1, :]).astype(BF16) for hh in (h0, h1)]
        upd = jnp.dot(jnp.concatenate(vw + [g["wk_b"]], axis=0), k2,
                      preferred_element_type=F32)
        n_inc = upd[2 * MLSTM_V_DIM:]
        dec = jnp.where(lo_lane[0:1, :], g["decay"][h0:h0 + 1, :], g["decay"][h1:h1 + 1, :])
        ct[p] = dec * ct[p] + jnp.where(lo_lane, upd[0:MLSTM_V_DIM], upd[MLSTM_V_DIM:2 * MLSTM_V_DIM])
        nn[p] = dec * nn[p] + jnp.where(lo_lane[0:1, :], n_inc[h0:h0 + 1, :], n_inc[h1:h1 + 1, :])
        return res

    def back(ci, p, par, res):
        st = res[0:L, par * L:(par + 1) * L]
        inter = res[L:L + MLSTM_V_DIM, par * L:(par + 1) * L]
        qn = res[L + MLSTM_V_DIM:L + MLSTM_V_DIM + 1, par * L:(par + 1) * L]
        g = chunk_gates(ci)
        rs = slice(ci * L, (ci + 1) * L)
        hh = 2 * p + par
        vsl = slice(hh * MLSTM_V_DIM, (hh + 1) * MLSTM_V_DIM)
        dt = jnp.where(causal_t,
                       jnp.exp2(g["b2cols"][:, hh:hh + 1] - g["m2"][hh:hh + 1, :]), 0.0)
        sct = st * dt
        wi = g["w_inter"][hh:hh + 1, :]
        den = jnp.sum(sct, axis=0, keepdims=True) + wi * qn
        numt = jnp.dot(vt_ref[vsl, rs], sct.astype(BF16), preferred_element_type=F32) + wi * inter
        rden = 1.0 / jnp.maximum(jnp.abs(den), g["e_neg"][hh:hh + 1, :])
        ms = jnp.mean(numt * numt, axis=0, keepdims=True)
        scale = rden * lax.rsqrt(rden * rden * ms + NORM_EPS)
        hv = (numt * scale).T
        o_ref[rs, vsl] = (hv * og_ref[rs, vsl].astype(F32)).astype(BF16)

    pending = [front(*u) for u in pairs[:MLSTM_LOOKAHEAD]]
    for idx, (ci, p) in enumerate(pairs):
        res = pending[idx]
        pending[idx] = None
        back(ci, p, 0, res)
        if idx + MLSTM_LOOKAHEAD < len(pairs):
            pending.append(front(*pairs[idx + MLSTM_LOOKAHEAD]))
        back(ci, p, 1, res)
    for p in range(nh // 2):
        ct_scr[p] = ct[p]
        n_scr[p] = nn[p]


def _mlstm(rows, cols, qk, v, og, batch, seq):
    n = qk.shape[0]
    tm = TM_MLSTM
    nt = seq // tm
    row = pl.BlockSpec((tm, SZ_MV), lambda b, c: (b * nt + c, 0))
    return pl.pallas_call(
        _mlstm_kernel,
        grid=(batch, nt),
        in_specs=[
            pl.BlockSpec((GATE_ROWS, tm), lambda b, c: (0, b * nt + c)),
            pl.BlockSpec((tm, LANES), lambda b, c: (b * nt + c, 0)),
            row,
            pl.BlockSpec((SZ_MV, tm), lambda b, c: (0, b * nt + c)),
            row,
        ],
        out_specs=row,
        out_shape=jax.ShapeDtypeStruct((n, SZ_MV), BF16),
        scratch_shapes=[
            pltpu.VMEM((MLSTM_HEADS // 2, 2 * MLSTM_QK_DIM, MLSTM_V_DIM), F32),
            pltpu.VMEM((MLSTM_HEADS // 2, 1, 2 * MLSTM_QK_DIM), F32),
        ],
        compiler_params=pltpu.CompilerParams(
            dimension_semantics=("arbitrary", "arbitrary"), vmem_limit_bytes=VMEM_LIMIT_BYTES),
        name="mlstm",
    )(rows, cols, qk, v, og)


def _merge_stage_weights(wpa_hbm, wpm_hbm, wo_hbm, hn_ref, wpa_ref, wpm_ref, wo_ref, stage, sem):
    hd = ATTN_HEAD_DIM
    wpa_copies = [pltpu.make_async_copy(wpa_hbm.at[pl.ds(head * hd, hd), :],
                                        stage.at[0, pl.ds(pos * hd, hd), :], sem.at[0])
                  for pos, head in enumerate(ATTN_HEAD_ORDER)]
    wpm_copy = pltpu.make_async_copy(wpm_hbm, stage.at[1], sem.at[1])
    wo_copy = pltpu.make_async_copy(wo_hbm, stage.at[0], sem.at[0])
    for k, cp in enumerate(wpa_copies):
        cp.start(priority=k % 2)
    wpm_copy.start()
    for cp in wpa_copies:
        cp.wait()
    wpa_ref[...] = stage[0].astype(BF16)
    wo_copy.start()
    wpm_copy.wait()
    wpm_ref[...] = (stage[1] * hn_ref[...]).astype(BF16)
    wo_copy.wait()
    wo_ref[...] = stage[0].astype(BF16)


def _merge_kernel(x_ref, a_ref, hm_ref, gate_ref, hn_ref, wpa_hbm, wpm_hbm, wo_hbm, o_ref,
                  wpa_ref, wpm_ref, wo_ref, stage, sem):
    @pl.when(pl.program_id(0) == 0)
    def _():
        _merge_stage_weights(wpa_hbm, wpm_hbm, wo_hbm, hn_ref, wpa_ref, wpm_ref, wo_ref, stage, sem)

    ya = jnp.dot(a_ref[...], wpa_ref[...], preferred_element_type=F32)
    ym = jnp.dot(hm_ref[...], wpm_ref[...], preferred_element_type=F32)
    ga = jax.nn.sigmoid(gate_ref[:, 0:D_MODEL].astype(F32))
    gm = jax.nn.sigmoid(gate_ref[:, D_MODEL:2 * D_MODEL].astype(F32))
    merged = (ga * ya + gm * ym).astype(BF16)
    o_ref[...] = x_ref[...] + jnp.dot(merged, wo_ref[...], preferred_element_type=F32)


def _merge(x, a, hm, gates, hn_col, wpa, wpm, wo):
    n = x.shape[0]
    tm = TM_MERGE
    row = lambda width: pl.BlockSpec((tm, width), lambda i: (i, 0))
    return pl.pallas_call(
        _merge_kernel,
        grid=(n // tm,),
        in_specs=[row(D_MODEL), row(SZ_AQ), row(SZ_MV), row(2 * D_MODEL), _const_spec((SZ_MV, 1)),
                  pl.BlockSpec(memory_space=pl.ANY), pl.BlockSpec(memory_space=pl.ANY),
                  pl.BlockSpec(memory_space=pl.ANY)],
        out_specs=row(D_MODEL),
        out_shape=jax.ShapeDtypeStruct((n, D_MODEL), F32),
        scratch_shapes=[pltpu.VMEM((D_MODEL, D_MODEL), BF16)] * 3 + [
            pltpu.VMEM((2, D_MODEL, D_MODEL), F32), pltpu.SemaphoreType.DMA((2,))],
        compiler_params=pltpu.CompilerParams(
            dimension_semantics=("arbitrary",), vmem_limit_bytes=VMEM_LIMIT_BYTES),
        name="merge",
    )(x, a, hm, gates, hn_col, wpa, wpm, wo)


def kernel(x, ffn1_norm, ffn1_w_gate, ffn1_w_up, ffn1_w_down, mix_norm, w_in, b_in, attn_sinks,
           mlstm_conv, mlstm_head_norm, w_proj_attn, w_proj_mlstm, w_out, ffn2_norm, ffn2_w_gate,
           ffn2_w_up, ffn2_w_down, final_norm):
    batch, seq, d = x.shape
    assert d == D_MODEL and ffn1_norm.shape[0] == 1, "one layer of width D_MODEL"
    assert all(seq % t == 0 for t in (TM_INPROJ, TM_MLSTM, ATTN_BLOCKS * WINDOW))
    assert all((batch * seq) % t == 0 for t in (TM_FFN, TM_MERGE))
    n = batch * seq
    xf = x.reshape(n, d)
    fin = final_norm.reshape(1, d)

    x1 = _ffn(xf, ffn1_norm[0].reshape(1, d), ffn1_w_gate[0], ffn1_w_up[0], ffn1_w_down[0], fin,
              apply_final=False)

    q, kk, avt, mqk, mvt, mo, gates, gt = _inproj(x1, mix_norm[0].reshape(1, d), w_in[0], b_in[0],
                                                  mlstm_conv[0], seq)

    ya = _attention(attn_sinks[0], q, kk, avt, batch, seq)

    rows, cols = _mlstm_gates(gt, seq)
    hm = _mlstm(rows, cols, mqk, mvt, mo, batch, seq)

    x2 = _merge(x1, ya, hm, gates, mlstm_head_norm[0].reshape(SZ_MV, 1), w_proj_attn[0],
                w_proj_mlstm[0], w_out[0])

    out = _ffn(x2, ffn2_norm[0].reshape(1, d), ffn2_w_gate[0], ffn2_w_up[0], ffn2_w_down[0], fin,
               apply_final=True)
    return out.reshape(batch, seq, d)
```
